```python
import math
import jax, jax.numpy as jnp
from jax import lax
import numpy as np

D_MODEL = 1024
BATCH = 1
SEQ = 16384
DEPTH = 1

EPS = 1e-6
SSM_WIDTH = 512
SSM_GROUP = 16
SSM_GROUPS = SSM_WIDTH // SSM_GROUP
SSM_STATE = 64
DT_MIN = 1e-3
DT_MAX = 1e-1
NSA_HEADS = 8
NSA_KV_HEADS = 2
GQA_RATIO = NSA_HEADS // NSA_KV_HEADS
HEAD_DIM = 64
NSA_WIDTH = NSA_HEADS * HEAD_DIM
KV_WIDTH = NSA_KV_HEADS * HEAD_DIM
CMP_LEN = 32
CMP_STRIDE = 16
CMP_HIDDEN = 256
SLC_LEN = 64
N_SEL = 16
N_LOCAL = 2
WINDOW = 512
Q_BLOCK = 128
BIG = 1e4
REL_BUCKETS = 32
REL_MAX_DIST = 128
D_FF = -(-8 * D_MODEL // (3 * 256)) * 256

IN_SIZES = [SSM_WIDTH, NSA_WIDTH] + [KV_WIDTH] * 6 + [3 * NSA_HEADS, 2 * D_MODEL]
IN_COLS = sum(IN_SIZES)
IN_SPLITS = [int(v) for v in np.cumsum(IN_SIZES)[:-1]]

kernel_name = "hybrid_s5_nsa_gated_block"


def rmsnorm(x, g):
    x32 = x.astype(jnp.float32)
    y = x32 * lax.rsqrt(jnp.mean(x32 * x32, axis=-1, keepdims=True) + EPS)
    return (y * g.astype(jnp.float32)).astype(x.dtype)


def masked_softmax(logits, mask):
    l = jnp.where(mask, logits.astype(jnp.float32), -1e30)
    m = jnp.max(l, axis=-1, keepdims=True)
    p = jnp.exp(l - m) * mask
    return p / jnp.maximum(jnp.sum(p, axis=-1, keepdims=True), 1e-30)


def t5_bucket(dist):
    n = jnp.maximum(dist, 0)
    max_exact = REL_BUCKETS // 2
    nf = jnp.maximum(n, 1).astype(jnp.float32)
    large = max_exact + (jnp.log(nf / max_exact) / math.log(REL_MAX_DIST / max_exact)
                         * (REL_BUCKETS - max_exact)).astype(jnp.int32)
    large = jnp.minimum(large, REL_BUCKETS - 1)
    return jnp.where(n < max_exact, n, large)


def head_bias(rel_bias, dist):
    b = rel_bias.astype(jnp.float32)[t5_bucket(dist)]
    b = b.reshape(dist.shape + (NSA_KV_HEADS, GQA_RATIO))
    return jnp.transpose(b, (2, 3, 0, 1))


def s5_mixer(u, a_re, a_im, log_dt, b_re, b_im, c_re, c_im, d, w_glu):
    bsz, s, _ = u.shape
    f32 = jnp.float32
    lam = lax.complex(a_re.astype(f32), a_im.astype(f32))
    dt = jnp.exp(log_dt.astype(f32))[:, None]
    lam_bar = jnp.exp(lam * dt)
    b = lax.complex(b_re.astype(f32), b_im.astype(f32))
    b_bar = ((lam_bar - 1.0) / lam)[..., None] * b
    c = lax.complex(c_re.astype(f32), c_im.astype(f32))
    ug = u.astype(f32).reshape(bsz, s, SSM_GROUPS, SSM_GROUP)
    bu = jnp.einsum("bsgc,gpc->bsgp", ug.astype(jnp.complex64), b_bar)
    a = jnp.broadcast_to(lam_bar, bu.shape)

    def combine(e1, e2):
        a1, x1 = e1
        a2, x2 = e2
        return a1 * a2, a2 * x1 + x2

    _, states = lax.associative_scan(combine, (a, bu), axis=1)
    y = jnp.einsum("bsgp,gcp->bsgc", states, c).real \
        + d.astype(f32).reshape(SSM_GROUPS, SSM_GROUP) * ug
    y = y.reshape(bsz, s, SSM_WIDTH).astype(u.dtype)
    z = jax.nn.gelu(y)
    return z * jax.nn.sigmoid(z @ w_glu)


def compress(k, pos, w1, w2):
    bsz, s = k.shape[0], k.shape[1]
    n_cmp = (s - CMP_LEN) // CMP_STRIDE + 1
    idx = np.arange(n_cmp)[:, None] * CMP_STRIDE + np.arange(CMP_LEN)[None, :]
    blocks = k[:, idx] + pos[None, None, :, None, :]
    blocks = jnp.transpose(blocks, (0, 1, 3, 2, 4)).reshape(bsz, n_cmp, NSA_KV_HEADS, CMP_LEN * HEAD_DIM)
    return jax.nn.gelu(blocks @ w1) @ w2


def nsa_mixer(q, kc, vc, ks, vs, kw, vw, gates, rel_bias):
    bsz, s = q.shape[0], q.shape[1]
    n_cmp = kc.shape[1]
    n_slc = s // SLC_LEN
    n_sel = min(N_SEL, n_slc)
    n_qblk = s // Q_BLOCK
    cmp_end = jnp.asarray(np.arange(n_cmp) * CMP_STRIDE + CMP_LEN - 1, dtype=jnp.int32)
    ratio = SLC_LEN // CMP_STRIDE
    front = CMP_LEN // CMP_STRIDE - 1
    w_ov = [float(v) for v in np.convolve(np.ones(ratio), np.ones(CMP_LEN // CMP_STRIDE))]
    back = ratio * n_slc + len(w_ov) - 1 - front - n_cmp
    ks_t = jnp.transpose(ks.reshape(bsz, n_slc, SLC_LEN, NSA_KV_HEADS, HEAD_DIM), (0, 3, 1, 2, 4))
    vs_t = jnp.transpose(vs.reshape(bsz, n_slc, SLC_LEN, NSA_KV_HEADS, HEAD_DIM), (0, 3, 1, 2, 4))
    pad_w = ((0, 0), (WINDOW, 0), (0, 0), (0, 0))
    kw_pad = jnp.pad(kw, pad_w)
    vw_pad = jnp.pad(vw, pad_w)
    tab_t = rel_bias.astype(jnp.float32).T.reshape(NSA_KV_HEADS, GQA_RATIO, REL_BUCKETS)
    bi = jnp.arange(bsz).reshape(bsz, 1, 1, 1)
    gi = jnp.arange(NSA_KV_HEADS).reshape(1, NSA_KV_HEADS, 1, 1)
    gi6 = jnp.arange(NSA_KV_HEADS).reshape(1, NSA_KV_HEADS, 1, 1, 1, 1)
    ri6 = jnp.arange(GQA_RATIO).reshape(1, 1, GQA_RATIO, 1, 1, 1)
    blk = jnp.arange(n_slc)

    def block(i):
        s0 = i * Q_BLOCK
        t = s0 + jnp.arange(Q_BLOCK)
        qb = lax.dynamic_slice_in_dim(q, s0, Q_BLOCK, axis=1)
        gb = lax.dynamic_slice_in_dim(gates, s0, Q_BLOCK, axis=1)
        dist_c = t[:, None] - cmp_end[None, :]
        logit_c = jnp.einsum("bqgrd,bngd->bgrqn", qb, kc) + head_bias(rel_bias, dist_c)
        p_c = masked_softmax(logit_c, dist_c >= 0)
        o_c = jnp.einsum("bgrqn,bngd->bqgrd", p_c.astype(vc.dtype), vc)
        imp = jnp.pad(p_c.sum(axis=2), ((0, 0), (0, 0), (0, 0), (front, back)))
        p_slc = sum(w_ov[o] * imp[..., o:o + ratio * n_slc:ratio] for o in range(len(w_ov)))
        cur = t // SLC_LEN
        valid = blk[None, :] <= cur[:, None]
        forced = valid & ((blk[None, :] == 0) | (blk[None, :] >= cur[:, None] - (N_LOCAL - 1)))
        score = jnp.where(forced, BIG, jnp.where(valid, p_slc, -BIG))
        _, sel = lax.top_k(score, n_sel)
        ks_g = ks_t[bi, gi, sel]
        vs_g = vs_t[bi, gi, sel]
        pos_s = sel[..., None] * SLC_LEN + jnp.arange(SLC_LEN)
        dist_s = t[None, None, :, None, None] - pos_s
        bias_s = tab_t[gi6, ri6, t5_bucket(dist_s)[:, :, None]]
        logit_s = jnp.einsum("bqgrd,bgqksd->bgrqks", qb, ks_g) + bias_s
        kflat = n_sel * SLC_LEN
        p_s = masked_softmax(logit_s.reshape(bsz, NSA_KV_HEADS, GQA_RATIO, Q_BLOCK, kflat),
                             (dist_s >= 0).reshape(bsz, NSA_KV_HEADS, 1, Q_BLOCK, kflat))
        o_s = jnp.einsum("bgrqk,bgqkd->bqgrd", p_s.astype(vs.dtype),
                         vs_g.reshape(bsz, NSA_KV_HEADS, Q_BLOCK, kflat, HEAD_DIM))
        kwb = lax.dynamic_slice_in_dim(kw_pad, s0, WINDOW + Q_BLOCK, axis=1)
        vwb = lax.dynamic_slice_in_dim(vw_pad, s0, WINDOW + Q_BLOCK, axis=1)
        pos_w = s0 - WINDOW + jnp.arange(WINDOW + Q_BLOCK)
        dist_w = t[:, None] - pos_w[None, :]
        mask_w = (dist_w >= 0) & (dist_w < WINDOW) & (pos_w[None, :] >= 0)
        logit_w = jnp.einsum("bqgrd,bkgd->bgrqk", qb, kwb) + head_bias(rel_bias, dist_w)
        p_w = masked_softmax(logit_w, mask_w)
        o_w = jnp.einsum("bgrqk,bkgd->bqgrd", p_w.astype(vw.dtype), vwb)
        o = gb[..., 0:1] * o_c + gb[..., 1:2] * o_s + gb[..., 2:3] * o_w
        return o.reshape(bsz, Q_BLOCK, NSA_WIDTH)

    out = lax.map(block, jnp.arange(n_qblk))
    return jnp.transpose(out, (1, 0, 2, 3)).reshape(bsz, s, NSA_WIDTH)


def setup_inputs(seed: int = 0) -> dict:
    key = jax.random.key(seed)
    ks = jax.random.split(key, 32)
    nrm = lambda k, shape, scale: jax.random.normal(k, shape, jnp.float32) * scale
    L = DEPTH
    n_idx = jnp.arange(SSM_STATE, dtype=jnp.float32)
    return {
        "x": nrm(ks[0], (BATCH, SEQ, D_MODEL), 1.0),
        "norm_mix_g": 1.0 + nrm(ks[1], (L, D_MODEL), 0.01),
        "w_in": nrm(ks[2], (L, D_MODEL, IN_COLS), D_MODEL ** -0.5),
        "ssm_a_re": -0.5 + nrm(ks[3], (L, SSM_GROUPS, SSM_STATE), 0.01),
        "ssm_a_im": math.pi * n_idx + nrm(ks[4], (L, SSM_GROUPS, SSM_STATE), 0.01),
        "ssm_log_dt": jax.random.uniform(ks[5], (L, SSM_GROUPS), jnp.float32,
                                         math.log(DT_MIN), math.log(DT_MAX)),
        "ssm_b_re": nrm(ks[6], (L, SSM_GROUPS, SSM_STATE, SSM_GROUP), (2 * SSM_GROUP) ** -0.5),
        "ssm_b_im": nrm(ks[7], (L, SSM_GROUPS, SSM_STATE, SSM_GROUP), (2 * SSM_GROUP) ** -0.5),
        "ssm_c_re": nrm(ks[8], (L, SSM_GROUPS, SSM_GROUP, SSM_STATE), SSM_STATE ** -0.5),
        "ssm_c_im": nrm(ks[9], (L, SSM_GROUPS, SSM_GROUP, SSM_STATE), SSM_STATE ** -0.5),
        "ssm_d": nrm(ks[10], (L, SSM_WIDTH), 1.0),
        "ssm_w_glu": nrm(ks[11], (L, SSM_WIDTH, SSM_WIDTH), SSM_WIDTH ** -0.5),
        "w_up_ssm": nrm(ks[12], (L, SSM_WIDTH, D_MODEL), SSM_WIDTH ** -0.5),
        "cmp_pos_k": nrm(ks[13], (L, CMP_LEN, HEAD_DIM), 0.1),
        "cmp_pos_v": nrm(ks[14], (L, CMP_LEN, HEAD_DIM), 0.1),
        "cmp_w1_k": nrm(ks[15], (L, CMP_LEN * HEAD_DIM, CMP_HIDDEN), (CMP_LEN * HEAD_DIM) ** -0.5),
        "cmp_w2_k": nrm(ks[16], (L, CMP_HIDDEN, HEAD_DIM), CMP_HIDDEN ** -0.5),
        "cmp_w1_v": nrm(ks[17], (L, CMP_LEN * HEAD_DIM, CMP_HIDDEN), (CMP_LEN * HEAD_DIM) ** -0.5),
        "cmp_w2_v": nrm(ks[18], (L, CMP_HIDDEN, HEAD_DIM), CMP_HIDDEN ** -0.5),
        "rel_bias": nrm(ks[19], (REL_BUCKETS, NSA_HEADS), 0.5),
        "w_up_nsa": nrm(ks[20], (L, NSA_WIDTH, D_MODEL), NSA_WIDTH ** -0.5),
        "w_out": nrm(ks[21], (L, D_MODEL, D_MODEL), D_MODEL ** -0.5),
        "norm_ffn_g": 1.0 + nrm(ks[22], (L, D_MODEL), 0.01),
        "w_ffn_gate": nrm(ks[23], (L, D_MODEL, D_FF), D_MODEL ** -0.5),
        "w_ffn_up": nrm(ks[24], (L, D_MODEL, D_FF), D_MODEL ** -0.5),
        "w_ffn_down": nrm(ks[25], (L, D_FF, D_MODEL), D_FF ** -0.5),
        "norm_final_g": 1.0 + nrm(ks[26], (D_MODEL,), 0.01),
    }


def reference(x, norm_mix_g, w_in, ssm_a_re, ssm_a_im, ssm_log_dt, ssm_b_re, ssm_b_im,
              ssm_c_re, ssm_c_im, ssm_d, ssm_w_glu, w_up_ssm, cmp_pos_k, cmp_pos_v,
              cmp_w1_k, cmp_w2_k, cmp_w1_v, cmp_w2_v, rel_bias, w_up_nsa, w_out,
              norm_ffn_g, w_ffn_gate, w_ffn_up, w_ffn_down, norm_final_g):
    bsz, s, _ = x.shape
    kv_shape = (bsz, s, NSA_KV_HEADS, HEAD_DIM)
    for l in range(DEPTH):
        h = rmsnorm(x, norm_mix_g[l])
        proj = h @ w_in[l]
        u, q, kc_r, vc_r, ks_r, vs_r, kw_r, vw_r, g_nsa, g_br = jnp.split(proj, IN_SPLITS, axis=-1)
        y_a = s5_mixer(u, ssm_a_re[l], ssm_a_im[l], ssm_log_dt[l], ssm_b_re[l], ssm_b_im[l],
                       ssm_c_re[l], ssm_c_im[l], ssm_d[l], ssm_w_glu[l]) @ w_up_ssm[l]
        qh = q.reshape(bsz, s, NSA_KV_HEADS, GQA_RATIO, HEAD_DIM) * (HEAD_DIM ** -0.5)
        kc = compress(kc_r.reshape(kv_shape), cmp_pos_k[l], cmp_w1_k[l], cmp_w2_k[l])
        vc = compress(vc_r.reshape(kv_shape), cmp_pos_v[l], cmp_w1_v[l], cmp_w2_v[l])
        gates = jax.nn.sigmoid(g_nsa).reshape(bsz, s, NSA_KV_HEADS, GQA_RATIO, 3)
        y_b = nsa_mixer(qh, kc, vc, ks_r.reshape(kv_shape), vs_r.reshape(kv_shape),
                        kw_r.reshape(kv_shape), vw_r.reshape(kv_shape), gates, rel_bias) @ w_up_nsa[l]
        g_a, g_b = jnp.split(jax.nn.sigmoid(g_br), 2, axis=-1)
        x = x + (g_a * y_a + g_b * y_b) @ w_out[l]
        h = rmsnorm(x, norm_ffn_g[l])
        x = x + (jax.nn.silu(h @ w_ffn_gate[l]) * (h @ w_ffn_up[l])) @ w_ffn_down[l]
    return rmsnorm(x, norm_final_g)
```

```python
import functools
import math

import numpy as np
import jax
import jax.numpy as jnp
from jax import lax
from jax.experimental import pallas as pl
from jax.experimental.pallas import tpu as pltpu

F32 = jnp.float32
BF16 = jnp.bfloat16

D_MODEL = 1024
SEQ = 16384
EPS = 1e-6
SSM_WIDTH = 512
SSM_GROUP = 16
SSM_GROUPS = SSM_WIDTH // SSM_GROUP
SSM_STATE = 64
SSM_LANES = SSM_GROUPS * SSM_STATE
NSA_HEADS = 8
NSA_KV_HEADS = 2
GQA = NSA_HEADS // NSA_KV_HEADS
HEAD_DIM = 64
NSA_WIDTH = NSA_HEADS * HEAD_DIM
KV_WIDTH = NSA_KV_HEADS * HEAD_DIM
CMP_LEN = 32
CMP_STRIDE = 16
CMP_HIDDEN = 256
SLC_LEN = 64
N_SEL = 16
N_LOCAL = 2
WINDOW = 512
BIG = 1e4
REL_BUCKETS = 32
REL_MAX_DIST = 128
D_FF = 2816

N_CHUNK = SEQ // CMP_STRIDE
N_CMP = (SEQ - CMP_LEN) // CMP_STRIDE + 1
N_SLC = SEQ // SLC_LEN
TQ = 128
QL = GQA * TQ
CMP_PAD = 128
N_CMP_PAD = CMP_PAD + N_CHUNK
CMP_BAND = 24
NEG = -1e30
M_FLOOR = -1e29
SCAN_ROWS = 8

VMEM_LIMIT = 56 * 1024 * 1024

COL_U = 0
COL_Q = 512
COL_KC = 1024
COL_VC = 1152
COL_KS = 1280
COL_VS = 1408
COL_KW = 1536
COL_VW = 1664
COL_GN = 1792
COL_GB = 1816
COL_END = 3864


def _rms(x, g):
    return x * lax.rsqrt(jnp.mean(x * x, axis=-1, keepdims=True) + EPS) * g


def _dot(a, b):
    return jnp.dot(a, b, preferred_element_type=F32)


def _dot_nt(a, b):
    return lax.dot_general(a, b, (((1,), (1,)), ((), ())), preferred_element_type=F32)


def _inproj_kernel(x_ref, g_ref, wm_ref, wn_ref, wb_ref,
                   u_ref, qT_ref, kcr_ref, vcr_ref, ks_ref, kw_ref, vsT_ref, vwT_ref, gnT_ref, gb_ref):
    h = _rms(x_ref[...], g_ref[...]).astype(BF16)
    pm = _dot(h, wm_ref[...])
    u_ref[...] = pm[:, COL_U:COL_Q]
    qT_ref[...] = (pm[:, COL_Q:COL_KC] * (HEAD_DIM ** -0.5)).T.astype(BF16)
    for g in range(NSA_KV_HEADS):
        lo = g * HEAD_DIM
        kcr_ref[g] = pm[:, COL_KC + lo:COL_KC + lo + HEAD_DIM]
        vcr_ref[g] = pm[:, COL_VC + lo:COL_VC + lo + HEAD_DIM]
        ks_ref[g] = pm[:, COL_KS + lo:COL_KS + lo + HEAD_DIM].astype(BF16)
        kw_ref[g] = pm[:, COL_KW + lo:COL_KW + lo + HEAD_DIM].astype(BF16)
    vsT_ref[...] = pm[:, COL_VS:COL_KW].T.astype(BF16)
    vwT_ref[...] = pm[:, COL_VW:COL_GN].T.astype(BF16)
    gn = jax.nn.sigmoid(_dot(h, wn_ref[...]))
    gnT_ref[...] = gn.T[:32, :]
    gb_ref[...] = jax.nn.sigmoid(_dot(h, wb_ref[...]))


def _inproj(x2, g, wm, wn, wb, tm=512):
    s = x2.shape[0]
    const = lambda i: (0, 0)
    row = lambda i: (i, 0)
    col = lambda i: (0, i)
    return pl.pallas_call(
        _inproj_kernel,
        grid=(s // tm,),
        in_specs=[
            pl.BlockSpec((tm, D_MODEL), row),
            pl.BlockSpec((1, D_MODEL), const),
            pl.BlockSpec(wm.shape, const),
            pl.BlockSpec(wn.shape, const),
            pl.BlockSpec(wb.shape, const),
        ],
        out_specs=[
            pl.BlockSpec((tm, SSM_WIDTH), row),
            pl.BlockSpec((NSA_WIDTH, tm), col),
            pl.BlockSpec((NSA_KV_HEADS, tm, HEAD_DIM), lambda i: (0, i, 0)),
            pl.BlockSpec((NSA_KV_HEADS, tm, HEAD_DIM), lambda i: (0, i, 0)),
            pl.BlockSpec((NSA_KV_HEADS, tm, HEAD_DIM), lambda i: (0, i, 0)),
            pl.BlockSpec((NSA_KV_HEADS, tm, HEAD_DIM), lambda i: (0, i, 0)),
            pl.BlockSpec((KV_WIDTH, tm), col),
            pl.BlockSpec((KV_WIDTH, tm), col),
            pl.BlockSpec((32, tm), col),
            pl.BlockSpec((tm, 2 * D_MODEL), row),
        ],
        out_shape=[
            jax.ShapeDtypeStruct((s, SSM_WIDTH), F32),
            jax.ShapeDtypeStruct((NSA_WIDTH, s), BF16),
            jax.ShapeDtypeStruct((NSA_KV_HEADS, s, HEAD_DIM), F32),
            jax.ShapeDtypeStruct((NSA_KV_HEADS, s, HEAD_DIM), F32),
            jax.ShapeDtypeStruct((NSA_KV_HEADS, s, HEAD_DIM), BF16),
            jax.ShapeDtypeStruct((NSA_KV_HEADS, s, HEAD_DIM), BF16),
            jax.ShapeDtypeStruct((KV_WIDTH, s), BF16),
            jax.ShapeDtypeStruct((KV_WIDTH, s), BF16),
            jax.ShapeDtypeStruct((32, s), F32),
            jax.ShapeDtypeStruct((s, 2 * D_MODEL), F32),
        ],
        compiler_params=pltpu.CompilerParams(
            dimension_semantics=("arbitrary",), vmem_limit_bytes=VMEM_LIMIT),
        name="inproj",
    )(x2, g, wm, wn, wb)


def _s5_kernel(u_ref, bre_ref, bim_ref, cre_ref, cim_ref, sc_ref, d_ref, wglu_ref, wup_ref,
               ya_ref, xre_ref, xim_ref, cre_s, cim_s, *, tt):
    @pl.when(pl.program_id(0) == 0)
    def _():
        cre_s[...] = jnp.zeros_like(cre_s)
        cim_s[...] = jnp.zeros_like(cim_s)

    u = u_ref[...]
    ub = u.astype(BF16)
    xre_ref[...] = _dot(ub, bre_ref[...])
    xim_ref[...] = _dot(ub, bim_ref[...])

    def cmul_add(re, im, are, aim, sre, sim):
        return re + are * sre - aim * sim, im + are * sim + aim * sre

    def slab(i, carry):
        c_re, c_im = carry
        r0 = pl.multiple_of(i * SCAN_ROWS, SCAN_ROWS)
        re = xre_ref[pl.ds(r0, SCAN_ROWS), :]
        im = xim_ref[pl.ds(r0, SCAN_ROWS), :]
        for k, shift in enumerate((1, 2, 4)):
            are = sc_ref[2 * k]
            aim = sc_ref[2 * k + 1]
            sre = pltpu.roll(re, shift, 0)
            sim = pltpu.roll(im, shift, 0)
            re, im = cmul_add(re, im, are, aim, sre, sim)
        re, im = cmul_add(re, im, sc_ref[6], sc_ref[7], c_re, c_im)
        xre_ref[pl.ds(r0, SCAN_ROWS), :] = re
        xim_ref[pl.ds(r0, SCAN_ROWS), :] = im
        return re[SCAN_ROWS - 1:SCAN_ROWS, :], im[SCAN_ROWS - 1:SCAN_ROWS, :]

    c_re, c_im = lax.fori_loop(0, tt // SCAN_ROWS, slab, (cre_s[...], cim_s[...]))
    cre_s[...] = c_re
    cim_s[...] = c_im

    y = (_dot(xre_ref[...].astype(BF16), cre_ref[...]) - _dot(xim_ref[...].astype(BF16), cim_ref[...])
         + d_ref[...] * u)
    z = jax.nn.gelu(y)
    z = z * jax.nn.sigmoid(_dot(z.astype(BF16), wglu_ref[...]))
    ya_ref[...] = _dot(z.astype(BF16), wup_ref[...])


def _s5(u, bre, bim, cre, cim, scan_consts, d, wglu, wup, tt=256):
    s = u.shape[0]
    const2 = lambda i: (0, 0)
    return pl.pallas_call(
        functools.partial(_s5_kernel, tt=tt),
        grid=(s // tt,),
        in_specs=[
            pl.BlockSpec((tt, SSM_WIDTH), lambda i: (i, 0)),
            pl.BlockSpec(bre.shape, const2),
            pl.BlockSpec(bim.shape, const2),
            pl.BlockSpec(cre.shape, const2),
            pl.BlockSpec(cim.shape, const2),
            pl.BlockSpec(scan_consts.shape, lambda i: (0, 0, 0)),
            pl.BlockSpec((1, SSM_WIDTH), const2),
            pl.BlockSpec(wglu.shape, const2),
            pl.BlockSpec(wup.shape, const2),
        ],
        out_specs=pl.BlockSpec((tt, D_MODEL), lambda i: (i, 0)),
        out_shape=jax.ShapeDtypeStruct((s, D_MODEL), F32),
        scratch_shapes=[
            pltpu.VMEM((tt, SSM_LANES), F32),
            pltpu.VMEM((tt, SSM_LANES), F32),
            pltpu.VMEM((1, SSM_LANES), F32),
            pltpu.VMEM((1, SSM_LANES), F32),
        ],
        compiler_params=pltpu.CompilerParams(
            dimension_semantics=("arbitrary",), vmem_limit_bytes=VMEM_LIMIT),
        name="s5",
    )(u, bre, bim, cre, cim, scan_consts, d, wglu, wup)


def _s5_params(a_re, a_im, log_dt, b_re, b_im, c_re, c_im):
    dt = jnp.exp(log_dt.astype(F32))[:, None]
    ar, ai = a_re.astype(F32), a_im.astype(F32)
    mag = jnp.exp(ar * dt)
    lr, li = mag * jnp.cos(ai * dt), mag * jnp.sin(ai * dt)
    den = ar * ar + ai * ai
    kr = ((lr - 1.0) * ar + li * ai) / den
    ki = (li * ar - (lr - 1.0) * ai) / den
    br, bi = b_re.astype(F32), b_im.astype(F32)
    bbr = kr[..., None] * br - ki[..., None] * bi
    bbi = kr[..., None] * bi + ki[..., None] * br
    eye = jnp.eye(SSM_GROUPS, dtype=F32)
    bd_in = lambda b: jnp.einsum("gpc,gh->gchp", b, eye).reshape(SSM_WIDTH, SSM_LANES)
    bd_out = lambda c: jnp.einsum("gcp,gh->gphc", c, eye).reshape(SSM_LANES, SSM_WIDTH)
    lr, li = lr.reshape(1, SSM_LANES), li.reshape(1, SSM_LANES)

    def cm(xr, xi, yr, yi):
        return xr * yr - xi * yi, xr * yi + xi * yr

    pows = [(lr, li)]
    for _ in range(SCAN_ROWS - 1):
        pows.append(cm(*pows[-1], lr, li))
    row = jnp.arange(SCAN_ROWS)[:, None]
    consts = []
    for shift in (1, 2, 4):
        pr, pi = pows[shift - 1]
        consts += [jnp.where(row >= shift, pr, 0.0), jnp.where(row >= shift, pi, 0.0)]
    consts += [jnp.concatenate([p[0] for p in pows], 0), jnp.concatenate([p[1] for p in pows], 0)]
    return (bd_in(bbr).astype(BF16), bd_in(bbi).astype(BF16),
            bd_out(c_re.astype(F32)).astype(BF16), bd_out(c_im.astype(F32)).astype(BF16),
            jnp.stack(consts))


def _compress_kernel(xk_ref, xv_ref, w1k_ref, w2k_ref, pk_ref, w1v_ref, w2vT_ref, pv_ref, kc_ref, vcT_ref):
    half = CMP_STRIDE * HEAD_DIM

    def hidden(x_ref, w1_ref, pos_ref):
        x = x_ref[0].astype(BF16)
        first = _dot(x, w1_ref[:half, :])
        second = _dot(x, w1_ref[half:, :])
        bias = _dot(jnp.broadcast_to(pos_ref[...], (8, 2 * half)).astype(BF16), w1_ref[...])[:1]
        pre = first + pltpu.roll(second, N_CHUNK - 1, 0) + bias
        return jax.nn.gelu(pre).astype(BF16)

    hk = hidden(xk_ref, w1k_ref, pk_ref)
    kc = _dot(hk, w2k_ref[...])
    kc_ref[0] = jnp.concatenate([jnp.zeros((CMP_PAD, HEAD_DIM), F32), kc], axis=0).astype(BF16)
    hv = hidden(xv_ref, w1v_ref, pv_ref)
    vcT = _dot_nt(w2vT_ref[...], hv)
    vcT_ref[0] = jnp.concatenate([jnp.zeros((HEAD_DIM, CMP_PAD), F32), vcT], axis=1).astype(BF16)


def _compress(xk, xv, w1k, w2k, pk, w1v, w2vT, pv):
    const2 = lambda g: (0, 0)
    head = lambda g: (g, 0, 0)
    return pl.pallas_call(
        _compress_kernel,
        grid=(NSA_KV_HEADS,),
        in_specs=[
            pl.BlockSpec((1, N_CHUNK, CMP_STRIDE * HEAD_DIM), head),
            pl.BlockSpec((1, N_CHUNK, CMP_STRIDE * HEAD_DIM), head),
            pl.BlockSpec(w1k.shape, const2),
            pl.BlockSpec(w2k.shape, const2),
            pl.BlockSpec(pk.shape, const2),
            pl.BlockSpec(w1v.shape, const2),
            pl.BlockSpec(w2vT.shape, const2),
            pl.BlockSpec(pv.shape, const2),
        ],
        out_specs=[
            pl.BlockSpec((1, N_CMP_PAD, HEAD_DIM), head),
            pl.BlockSpec((1, HEAD_DIM, N_CMP_PAD), head),
        ],
        out_shape=[
            jax.ShapeDtypeStruct((NSA_KV_HEADS, N_CMP_PAD, HEAD_DIM), BF16),
            jax.ShapeDtypeStruct((NSA_KV_HEADS, HEAD_DIM, N_CMP_PAD), BF16),
        ],
        compiler_params=pltpu.CompilerParams(
            dimension_semantics=("arbitrary",), vmem_limit_bytes=VMEM_LIMIT),
        name="compress",
    )(xk, xv, w1k, w2k, pk, w1v, w2vT, pv)


def _nsa_kernel(qT_ref, kc_ref, vcT_ref, ks_ref, vsT_ref, kw_ref, vwT_ref, bn_ref, bc_ref, mt_ref, gT_ref,
                oT_ref, sc_ref, neg_ref, m_ref, l_ref, acc_ref, tot_ref):
    i = pl.program_id(1)
    s0 = i * TQ
    qT = jnp.concatenate([qT_ref[r * HEAD_DIM:(r + 1) * HEAD_DIM, :] for r in range(GQA)], axis=1)

    def gate_row(branch):
        return jnp.concatenate([gT_ref[r * 3 + branch:r * 3 + branch + 1, :] for r in range(GQA)], axis=1)

    def reset():
        m_ref[...] = jnp.full_like(m_ref, M_FLOOR)
        l_ref[...] = jnp.zeros_like(l_ref)
        acc_ref[...] = jnp.zeros_like(acc_ref)

    def attend(k_ref, vT_ref, start, size, add):
        s = _dot(k_ref[0, pl.ds(start, size), :], qT)
        if add is not None:
            s = s + add
        m_prev = m_ref[...]
        m_new = jnp.maximum(m_prev, jnp.max(s, axis=0, keepdims=True))
        alpha = jnp.exp(m_prev - m_new)
        p = jnp.exp(s - m_new)
        l_ref[...] = alpha * l_ref[...] + jnp.sum(p, axis=0, keepdims=True)
        acc_ref[...] = alpha * acc_ref[...] + _dot(vT_ref[:, pl.ds(start, size)], p.astype(BF16))
        m_ref[...] = m_new

    def finish(branch, first=False):
        o = acc_ref[...] * (gate_row(branch) / jnp.maximum(l_ref[...], 1e-30))
        if first:
            tot_ref[...] = o
        else:
            tot_ref[...] += o

    n0 = i * (TQ // CMP_STRIDE)
    band0 = pl.multiple_of(n0 + CMP_PAD - 16, 8)
    row = lax.broadcasted_iota(jnp.int32, (N_CMP_PAD, QL), 0)
    live = (row >= CMP_PAD) & (row < band0 + CMP_BAND)
    sc_ref[...] = jnp.where(live, _dot(kc_ref[0], qT), NEG)
    sc_ref[pl.ds(band0, CMP_BAND), :] += bc_ref[0]
    sc = sc_ref[...]
    mc = jnp.maximum(jnp.max(sc, axis=0, keepdims=True), M_FLOOR)
    pc = jnp.exp(sc - mc)
    pc = pc * (1.0 / jnp.maximum(jnp.sum(pc, axis=0, keepdims=True), 1e-30))
    tot_ref[...] = _dot(vcT_ref[0], pc.astype(BF16)) * gate_row(0)

    imp = pc[:, 0:TQ]
    for r in range(1, GQA):
        imp = imp + pc[:, r * TQ:(r + 1) * TQ]
    mt = mt_ref[...]
    p_slc = jnp.zeros((N_SLC, TQ), F32)
    rem = imp
    for _ in range(3):
        piece = rem.astype(BF16)
        p_slc = p_slc + _dot(mt, piece)
        rem = rem - piece.astype(F32)
    blk = lax.broadcasted_iota(jnp.int32, (N_SLC, TQ), 0)
    cur = (s0 + lax.broadcasted_iota(jnp.int32, (N_SLC, TQ), 1)) // SLC_LEN
    valid = blk <= cur
    forced = valid & ((blk == 0) | (blk >= cur - (N_LOCAL - 1)))
    score = jnp.where(forced, BIG, jnp.where(valid, p_slc, -BIG))
    blk_f = blk.astype(F32)
    picked = jnp.zeros((N_SLC, TQ), F32)
    for _ in range(N_SEL):
        best = jnp.max(score, axis=0, keepdims=True)
        first = jnp.min(jnp.where(score == best, blk_f, float(N_SLC)), axis=0, keepdims=True)
        hit = blk_f == first
        picked = jnp.where(hit, 1.0, picked)
        score = jnp.where(hit, -jnp.inf, score)
    neg = jnp.where(picked > 0.0, 0.0, NEG)
    neg_ref[...] = jnp.concatenate([neg] * GQA, axis=1)

    def block_mask(j0, nblk):
        return jnp.concatenate(
            [jnp.broadcast_to(neg_ref[pl.ds(j0 + b, 1), :], (SLC_LEN, QL)) for b in range(nblk)], axis=0)

    reset()

    def far_body(c, carry):
        attend(ks_ref, vsT_ref, pl.multiple_of(c * TQ, TQ), TQ, block_mask(c * 2, 2))
        return carry

    lax.fori_loop(0, i - 1, far_body, 0)

    @pl.when(i > 0)
    def _():
        attend(ks_ref, vsT_ref, pl.multiple_of(s0 - TQ, TQ), TQ, bn_ref[0, 0:TQ, :] + block_mask(2 * i - 2, 2))

    attend(ks_ref, vsT_ref, pl.multiple_of(s0, TQ), TQ, bn_ref[0, TQ:2 * TQ, :] + block_mask(2 * i, 2))
    finish(1)

    reset()
    kq = lax.broadcasted_iota(jnp.int32, (TQ, QL), 0)
    iq = lax.broadcasted_iota(jnp.int32, (TQ, QL), 1) % TQ
    oldest = jnp.where(kq > iq, 0.0, NEG)
    n_win = WINDOW // TQ
    adds = [oldest] + [None] * (n_win - 2) + [bn_ref[0, 0:TQ, :], bn_ref[0, TQ:2 * TQ, :]]
    for c in range(n_win + 1):
        def chunk(c=c):
            attend(kw_ref, vwT_ref, pl.multiple_of(s0 - WINDOW + c * TQ, TQ), TQ, adds[c])
        if c == n_win:
            chunk()
        else:
            pl.when(i + c >= n_win)(chunk)
    finish(2)

    tot = tot_ref[...]
    for r in range(GQA):
        oT_ref[r * HEAD_DIM:(r + 1) * HEAD_DIM, :] = tot[:, r * TQ:(r + 1) * TQ]


def _nsa(qT, kc, vcT, ks, vsT, kw, vwT, bias_near, bias_cmp, mt, gnT):
    s = qT.shape[1]
    head3 = lambda g, i: (g, 0, 0)
    return pl.pallas_call(
        _nsa_kernel,
        grid=(NSA_KV_HEADS, s // TQ),
        in_specs=[
            pl.BlockSpec((GQA * HEAD_DIM, TQ), lambda g, i: (g, i)),
            pl.BlockSpec((1, N_CMP_PAD, HEAD_DIM), head3),
            pl.BlockSpec((1, HEAD_DIM, N_CMP_PAD), head3),
            pl.BlockSpec((1, s, HEAD_DIM), head3),
            pl.BlockSpec((HEAD_DIM, s), lambda g, i: (g, 0)),
            pl.BlockSpec((1, s, HEAD_DIM), head3),
            pl.BlockSpec((HEAD_DIM, s), lambda g, i: (g, 0)),
            pl.BlockSpec((1, 2 * TQ, QL), head3),
            pl.BlockSpec((1, CMP_BAND, QL), head3),
            pl.BlockSpec(mt.shape, lambda g, i: (0, 0)),
            pl.BlockSpec((16, TQ), lambda g, i: (g, i)),
        ],
        out_specs=pl.BlockSpec((GQA * HEAD_DIM, TQ), lambda g, i: (g, i)),
        out_shape=jax.ShapeDtypeStruct((NSA_WIDTH, s), F32),
        scratch_shapes=[
            pltpu.VMEM((N_CMP_PAD, QL), F32),
            pltpu.VMEM((N_SLC, QL), F32),
            pltpu.VMEM((1, QL), F32),
            pltpu.VMEM((1, QL), F32),
            pltpu.VMEM((HEAD_DIM, QL), F32),
            pltpu.VMEM((HEAD_DIM, QL), F32),
        ],
        compiler_params=pltpu.CompilerParams(
            dimension_semantics=("arbitrary", "arbitrary"), vmem_limit_bytes=VMEM_LIMIT),
        name="nsa",
    )(qT, kc, vcT, ks, vsT, kw, vwT, bias_near, bias_cmp, mt, gnT)


def _t5_bucket(dist):
    n = jnp.maximum(dist, 0)
    max_exact = REL_BUCKETS // 2
    nf = jnp.maximum(n, 1).astype(F32)
    large = max_exact + (jnp.log(nf / max_exact) / math.log(REL_MAX_DIST / max_exact)
                         * (REL_BUCKETS - max_exact)).astype(jnp.int32)
    large = jnp.minimum(large, REL_BUCKETS - 1)
    return jnp.where(n < max_exact, n, large)


def _bias_tiles(rel_bias):
    n_dist = 2 * TQ
    tab = rel_bias.astype(F32)
    tab = tab[_t5_bucket(jnp.arange(n_dist))] - tab[REL_BUCKETS - 1]
    iq = jnp.arange(TQ)

    def expand(dist):
        b = jnp.where(dist[..., None] >= 0, tab[jnp.clip(dist, 0, n_dist - 1)], NEG)
        b = b.reshape(dist.shape + (NSA_KV_HEADS, GQA))
        return jnp.transpose(b, (2, 0, 3, 1)).reshape(NSA_KV_HEADS, dist.shape[0], QL)

    near = expand(iq[None, :] + TQ - jnp.arange(2 * TQ)[:, None])
    band_end = (jnp.arange(CMP_BAND) - 16) * CMP_STRIDE + CMP_LEN - 1
    cmp_band = expand(iq[None, :] - band_end[:, None])
    return near, cmp_band


def _overlap_matrix():
    ratio = SLC_LEN // CMP_STRIDE
    front = CMP_LEN // CMP_STRIDE - 1
    w_ov = np.convolve(np.ones(ratio), np.ones(CMP_LEN // CMP_STRIDE))
    mt = np.zeros((N_SLC, N_CMP_PAD), np.float32)
    for j in range(N_SLC):
        for o, w in enumerate(w_ov):
            n = ratio * j + o - front
            if 0 <= n < N_CMP:
                mt[j, CMP_PAD + n] = w
    return jnp.asarray(mt, BF16)


def _merge_kernel(x_ref, ya_ref, oT_ref, gb_ref, wup_ref, wout_ref, x1_ref):
    yb = _dot(oT_ref[...].T.astype(BF16), wup_ref[...])
    mix = gb_ref[:, :D_MODEL] * ya_ref[...] + gb_ref[:, D_MODEL:] * yb
    x1_ref[...] = x_ref[...] + _dot(mix.astype(BF16), wout_ref[...])


def _merge(x2, ya, oT, gb, wup, wout, tm=512):
    s = x2.shape[0]
    row = lambda i: (i, 0)
    const = lambda i: (0, 0)
    return pl.pallas_call(
        _merge_kernel,
        grid=(s // tm,),
        in_specs=[
            pl.BlockSpec((tm, D_MODEL), row),
            pl.BlockSpec((tm, D_MODEL), row),
            pl.BlockSpec((NSA_WIDTH, tm), lambda i: (0, i)),
            pl.BlockSpec((tm, 2 * D_MODEL), row),
            pl.BlockSpec(wup.shape, const),
            pl.BlockSpec(wout.shape, const),
        ],
        out_specs=pl.BlockSpec((tm, D_MODEL), row),
        out_shape=jax.ShapeDtypeStruct((s, D_MODEL), F32),
        compiler_params=pltpu.CompilerParams(
            dimension_semantics=("arbitrary",), vmem_limit_bytes=VMEM_LIMIT),
        name="merge",
    )(x2, ya, oT, gb, wup, wout)


def _ffn_kernel(x_ref, g_ref, wg_ref, wu_ref, wd_ref, gf_ref, o_ref):
    x = x_ref[...]
    h = _rms(x, g_ref[...]).astype(BF16)
    f = jax.nn.silu(_dot(h, wg_ref[...])) * _dot(h, wu_ref[...])
    x = x + _dot(f.astype(BF16), wd_ref[...])
    o_ref[...] = _rms(x, gf_ref[...])


def _ffn(x1, g, wg, wu, wd, gf, tm=256):
    s = x1.shape[0]
    row = lambda i: (i, 0)
    const = lambda i: (0, 0)
    return pl.pallas_call(
        _ffn_kernel,
        grid=(s // tm,),
        in_specs=[
            pl.BlockSpec((tm, D_MODEL), row),
            pl.BlockSpec((1, D_MODEL), const),
            pl.BlockSpec(wg.shape, const),
            pl.BlockSpec(wu.shape, const),
            pl.BlockSpec(wd.shape, const),
            pl.BlockSpec((1, D_MODEL), const),
        ],
        out_specs=pl.BlockSpec((tm, D_MODEL), row),
        out_shape=jax.ShapeDtypeStruct((s, D_MODEL), F32),
        compiler_params=pltpu.CompilerParams(
            dimension_semantics=("arbitrary",), vmem_limit_bytes=VMEM_LIMIT),
        name="ffn",
    )(x1, g, wg, wu, wd, gf)


def kernel(x, norm_mix_g, w_in, ssm_a_re, ssm_a_im, ssm_log_dt, ssm_b_re, ssm_b_im, ssm_c_re, ssm_c_im, ssm_d, ssm_w_glu, w_up_ssm, cmp_pos_k, cmp_pos_v, cmp_w1_k, cmp_w2_k, cmp_w1_v, cmp_w2_v, rel_bias, w_up_nsa, w_out, norm_ffn_g, w_ffn_gate, w_ffn_up, w_ffn_down, norm_final_g):
    bsz, s, _ = x.shape
    assert (bsz, s) == (1, SEQ) and w_in.shape[0] == 1
    x2 = x.reshape(s, D_MODEL)
    l = 0
    w = w_in[l].astype(BF16)
    wm = w[:, :COL_GN]
    gates_per_head = GQA * 3
    wn = jnp.zeros((D_MODEL, 128), BF16)
    for g in range(NSA_KV_HEADS):
        wn = wn.at[:, 16 * g:16 * g + gates_per_head].set(
            w[:, COL_GN + g * gates_per_head:COL_GN + (g + 1) * gates_per_head])
    wb = w[:, COL_GB:]
    row = lambda v: v.astype(F32).reshape(1, -1)

    u, qT, kcr, vcr, ks, kw, vsT, vwT, gnT, gb = _inproj(x2, row(norm_mix_g[l]), wm, wn, wb)

    bre, bim, cre, cim, scan_consts = _s5_params(
        ssm_a_re[l], ssm_a_im[l], ssm_log_dt[l], ssm_b_re[l], ssm_b_im[l], ssm_c_re[l], ssm_c_im[l])
    ya = _s5(u, bre, bim, cre, cim, scan_consts, row(ssm_d[l]),
             ssm_w_glu[l].astype(BF16), w_up_ssm[l].astype(BF16))

    chunks = lambda a: a.reshape(NSA_KV_HEADS, N_CHUNK, CMP_STRIDE * HEAD_DIM)
    kc, vcT = _compress(chunks(kcr), chunks(vcr),
                        cmp_w1_k[l].astype(BF16), cmp_w2_k[l].astype(BF16), row(cmp_pos_k[l]),
                        cmp_w1_v[l].astype(BF16), cmp_w2_v[l].T.astype(BF16), row(cmp_pos_v[l]))

    bias_near, bias_cmp = _bias_tiles(rel_bias)
    oT = _nsa(qT, kc, vcT, ks, vsT, kw, vwT, bias_near, bias_cmp, _overlap_matrix(), gnT)

    x1 = _merge(x2, ya, oT, gb, w_up_nsa[l].astype(BF16), w_out[l].astype(BF16))
    out = _ffn(x1, row(norm_ffn_g[l]), w_ffn_gate[l].astype(BF16), w_ffn_up[l].astype(BF16),
               w_ffn_down[l].astype(BF16), row(norm_final_g))
    return out.reshape(bsz, s, D_MODEL)
```

```python
import functools
import math

import numpy as np
import jax
import jax.numpy as jnp
from jax import lax
from jax.experimental import pallas as pl
from jax.experimental.pallas import tpu as pltpu

F32 = jnp.float32
BF16 = jnp.bfloat16

D_MODEL = 1024
SEQ = 16384
EPS = 1e-6
SSM_WIDTH = 512
SSM_GROUP = 16
SSM_GROUPS = SSM_WIDTH // SSM_GROUP
SSM_STATE = 64
SSM_LANES = SSM_GROUPS * SSM_STATE
NSA_HEADS = 8
NSA_KV_HEADS = 2
GQA = NSA_HEADS // NSA_KV_HEADS
HEAD_DIM = 64
NSA_WIDTH = NSA_HEADS * HEAD_DIM
KV_WIDTH = NSA_KV_HEADS * HEAD_DIM
CMP_LEN = 32
CMP_STRIDE = 16
CMP_HIDDEN = 256
SLC_LEN = 64
N_SEL = 16
N_LOCAL = 2
WINDOW = 512
BIG = 1e4
REL_BUCKETS = 32
REL_MAX_DIST = 128
D_FF = 2816

N_CHUNK = SEQ // CMP_STRIDE
N_CMP = (SEQ - CMP_LEN) // CMP_STRIDE + 1
N_SLC = SEQ // SLC_LEN
TQ = 128
QL = GQA * TQ
CMP_PAD = 128
N_CMP_PAD = CMP_PAD + N_CHUNK
CMP_BAND = 24
NEG = -1e30
M_FLOOR = -1e29
SCAN_ROWS = 8
LOG2E = math.log2(math.e)
V_ROWS = HEAD_DIM + 16
FAR_KEYS = 512

VMEM_LIMIT = 56 * 1024 * 1024

COL_U = 0
COL_Q = 512
COL_KC = 1024
COL_VC = 1152
COL_KS = 1280
COL_VS = 1408
COL_KW = 1536
COL_VW = 1664
COL_GN = 1792
COL_GB = 1816
COL_END = 3864


def _rms(x, g):
    return x * lax.rsqrt(jnp.mean(x * x, axis=-1, keepdims=True) + EPS) * g


def _dot(a, b):
    return jnp.dot(a, b, preferred_element_type=F32)


def _dot_nt(a, b):
    return lax.dot_general(a, b, (((1,), (1,)), ((), ())), preferred_element_type=F32)


def _inproj_kernel(x_ref, g_ref, wm_ref, wn_ref, wb_ref,
                   u_ref, qT_ref, kcr_ref, vcr_ref, ks_ref, kw_ref, vsT_ref, vwT_ref, gnT_ref, gb_ref):
    h = _rms(x_ref[...], g_ref[...]).astype(BF16)
    pm = _dot(h, wm_ref[...])
    u_ref[...] = pm[:, COL_U:COL_Q]
    qT_ref[...] = (pm[:, COL_Q:COL_KC] * (HEAD_DIM ** -0.5 * LOG2E)).T.astype(BF16)
    vsT = pm[:, COL_VS:COL_KW].T
    vwT = pm[:, COL_VW:COL_GN].T
    ones = jnp.ones((V_ROWS - HEAD_DIM, vsT.shape[1]), F32)
    for g in range(NSA_KV_HEADS):
        lo = g * HEAD_DIM
        kcr_ref[g] = pm[:, COL_KC + lo:COL_KC + lo + HEAD_DIM]
        vcr_ref[g] = pm[:, COL_VC + lo:COL_VC + lo + HEAD_DIM]
        ks_ref[g] = pm[:, COL_KS + lo:COL_KS + lo + HEAD_DIM].astype(BF16)
        kw_ref[g] = pm[:, COL_KW + lo:COL_KW + lo + HEAD_DIM].astype(BF16)
        vsT_ref[g] = jnp.concatenate([vsT[lo:lo + HEAD_DIM], ones], axis=0).astype(BF16)
        vwT_ref[g] = jnp.concatenate([vwT[lo:lo + HEAD_DIM], ones], axis=0).astype(BF16)
    gn = jax.nn.sigmoid(_dot(h, wn_ref[...]))
    gnT_ref[...] = gn.T[:32, :]
    gb_ref[...] = jax.nn.sigmoid(_dot(h, wb_ref[...]))


def _inproj(x2, g, wm, wn, wb, tm=512):
    s = x2.shape[0]
    const = lambda i: (0, 0)
    row = lambda i: (i, 0)
    col = lambda i: (0, i)
    return pl.pallas_call(
        _inproj_kernel,
        grid=(s // tm,),
        in_specs=[
            pl.BlockSpec((tm, D_MODEL), row),
            pl.BlockSpec((1, D_MODEL), const),
            pl.BlockSpec(wm.shape, const),
            pl.BlockSpec(wn.shape, const),
            pl.BlockSpec(wb.shape, const),
        ],
        out_specs=[
            pl.BlockSpec((tm, SSM_WIDTH), row),
            pl.BlockSpec((NSA_WIDTH, tm), col),
            pl.BlockSpec((NSA_KV_HEADS, tm, HEAD_DIM), lambda i: (0, i, 0)),
            pl.BlockSpec((NSA_KV_HEADS, tm, HEAD_DIM), lambda i: (0, i, 0)),
            pl.BlockSpec((NSA_KV_HEADS, tm, HEAD_DIM), lambda i: (0, i, 0)),
            pl.BlockSpec((NSA_KV_HEADS, tm, HEAD_DIM), lambda i: (0, i, 0)),
            pl.BlockSpec((NSA_KV_HEADS, V_ROWS, tm), lambda i: (0, 0, i)),
            pl.BlockSpec((NSA_KV_HEADS, V_ROWS, tm), lambda i: (0, 0, i)),
            pl.BlockSpec((32, tm), col),
            pl.BlockSpec((tm, 2 * D_MODEL), row),
        ],
        out_shape=[
            jax.ShapeDtypeStruct((s, SSM_WIDTH), F32),
            jax.ShapeDtypeStruct((NSA_WIDTH, s), BF16),
            jax.ShapeDtypeStruct((NSA_KV_HEADS, s, HEAD_DIM), F32),
            jax.ShapeDtypeStruct((NSA_KV_HEADS, s, HEAD_DIM), F32),
            jax.ShapeDtypeStruct((NSA_KV_HEADS, s, HEAD_DIM), BF16),
            jax.ShapeDtypeStruct((NSA_KV_HEADS, s, HEAD_DIM), BF16),
            jax.ShapeDtypeStruct((NSA_KV_HEADS, V_ROWS, s), BF16),
            jax.ShapeDtypeStruct((NSA_KV_HEADS, V_ROWS, s), BF16),
            jax.ShapeDtypeStruct((32, s), F32),
            jax.ShapeDtypeStruct((s, 2 * D_MODEL), F32),
        ],
        compiler_params=pltpu.CompilerParams(
            dimension_semantics=("arbitrary",), vmem_limit_bytes=VMEM_LIMIT),
        name="inproj",
    )(x2, g, wm, wn, wb)


def _s5_kernel(u_ref, bre_ref, bim_ref, cre_ref, cim_ref, sc_ref, d_ref, wglu_ref, wup_ref,
               ya_ref, xre_ref, xim_ref, cre_s, cim_s, *, tt):
    @pl.when(pl.program_id(0) == 0)
    def _():
        cre_s[...] = jnp.zeros_like(cre_s)
        cim_s[...] = jnp.zeros_like(cim_s)

    u = u_ref[...]
    ub = u.astype(BF16)
    xre_ref[...] = _dot(ub, bre_ref[...])
    xim_ref[...] = _dot(ub, bim_ref[...])

    def cmul_add(re, im, are, aim, sre, sim):
        return re + are * sre - aim * sim, im + are * sim + aim * sre

    def slab(i, carry):
        c_re, c_im = carry
        r0 = pl.multiple_of(i * SCAN_ROWS, SCAN_ROWS)
        re = xre_ref[pl.ds(r0, SCAN_ROWS), :]
        im = xim_ref[pl.ds(r0, SCAN_ROWS), :]
        for k, shift in enumerate((1, 2, 4)):
            are = sc_ref[2 * k]
            aim = sc_ref[2 * k + 1]
            sre = pltpu.roll(re, shift, 0)
            sim = pltpu.roll(im, shift, 0)
            re, im = cmul_add(re, im, are, aim, sre, sim)
        re, im = cmul_add(re, im, sc_ref[6], sc_ref[7], c_re, c_im)
        xre_ref[pl.ds(r0, SCAN_ROWS), :] = re
        xim_ref[pl.ds(r0, SCAN_ROWS), :] = im
        return re[SCAN_ROWS - 1:SCAN_ROWS, :], im[SCAN_ROWS - 1:SCAN_ROWS, :]

    c_re, c_im = lax.fori_loop(0, tt // SCAN_ROWS, slab, (cre_s[...], cim_s[...]))
    cre_s[...] = c_re
    cim_s[...] = c_im

    y = (_dot(xre_ref[...].astype(BF16), cre_ref[...]) - _dot(xim_ref[...].astype(BF16), cim_ref[...])
         + d_ref[...] * u)
    z = jax.nn.gelu(y)
    z = z * jax.nn.sigmoid(_dot(z.astype(BF16), wglu_ref[...]))
    ya_ref[...] = _dot(z.astype(BF16), wup_ref[...])


def _s5(u, bre, bim, cre, cim, scan_consts, d, wglu, wup, tt=256):
    s = u.shape[0]
    const2 = lambda i: (0, 0)
    return pl.pallas_call(
        functools.partial(_s5_kernel, tt=tt),
        grid=(s // tt,),
        in_specs=[
            pl.BlockSpec((tt, SSM_WIDTH), lambda i: (i, 0)),
            pl.BlockSpec(bre.shape, const2),
            pl.BlockSpec(bim.shape, const2),
            pl.BlockSpec(cre.shape, const2),
            pl.BlockSpec(cim.shape, const2),
            pl.BlockSpec(scan_consts.shape, lambda i: (0, 0, 0)),
            pl.BlockSpec((1, SSM_WIDTH), const2),
            pl.BlockSpec(wglu.shape, const2),
            pl.BlockSpec(wup.shape, const2),
        ],
        out_specs=pl.BlockSpec((tt, D_MODEL), lambda i: (i, 0)),
        out_shape=jax.ShapeDtypeStruct((s, D_MODEL), F32),
        scratch_shapes=[
            pltpu.VMEM((tt, SSM_LANES), F32),
            pltpu.VMEM((tt, SSM_LANES), F32),
            pltpu.VMEM((1, SSM_LANES), F32),
            pltpu.VMEM((1, SSM_LANES), F32),
        ],
        compiler_params=pltpu.CompilerParams(
            dimension_semantics=("arbitrary",), vmem_limit_bytes=VMEM_LIMIT),
        name="s5",
    )(u, bre, bim, cre, cim, scan_consts, d, wglu, wup)


def _s5_params(a_re, a_im, log_dt, b_re, b_im, c_re, c_im):
    dt = jnp.exp(log_dt.astype(F32))[:, None]
    ar, ai = a_re.astype(F32), a_im.astype(F32)
    mag = jnp.exp(ar * dt)
    lr, li = mag * jnp.cos(ai * dt), mag * jnp.sin(ai * dt)
    den = ar * ar + ai * ai
    kr = ((lr - 1.0) * ar + li * ai) / den
    ki = (li * ar - (lr - 1.0) * ai) / den
    br, bi = b_re.astype(F32), b_im.astype(F32)
    bbr = kr[..., None] * br - ki[..., None] * bi
    bbi = kr[..., None] * bi + ki[..., None] * br
    eye = jnp.eye(SSM_GROUPS, dtype=F32)
    bd_in = lambda b: jnp.einsum("gpc,gh->gchp", b, eye).reshape(SSM_WIDTH, SSM_LANES)
    bd_out = lambda c: jnp.einsum("gcp,gh->gphc", c, eye).reshape(SSM_LANES, SSM_WIDTH)
    lr, li = lr.reshape(1, SSM_LANES), li.reshape(1, SSM_LANES)

    def cm(xr, xi, yr, yi):
        return xr * yr - xi * yi, xr * yi + xi * yr

    pows = [(lr, li)]
    for _ in range(SCAN_ROWS - 1):
        pows.append(cm(*pows[-1], lr, li))
    row = jnp.arange(SCAN_ROWS)[:, None]
    consts = []
    for shift in (1, 2, 4):
        pr, pi = pows[shift - 1]
        consts += [jnp.where(row >= shift, pr, 0.0), jnp.where(row >= shift, pi, 0.0)]
    consts += [jnp.concatenate([p[0] for p in pows], 0), jnp.concatenate([p[1] for p in pows], 0)]
    return (bd_in(bbr).astype(BF16), bd_in(bbi).astype(BF16),
            bd_out(c_re.astype(F32)).astype(BF16), bd_out(c_im.astype(F32)).astype(BF16),
            jnp.stack(consts))


def _compress_kernel(xk_ref, xv_ref, w1k_ref, w2k_ref, pk_ref, w1v_ref, w2vT_ref, pv_ref, kc_ref, vcT_ref):
    half = CMP_STRIDE * HEAD_DIM

    def hidden(x_ref, w1_ref, pos_ref):
        x = x_ref[0].astype(BF16)
        first = _dot(x, w1_ref[:half, :])
        second = _dot(x, w1_ref[half:, :])
        bias = _dot(jnp.broadcast_to(pos_ref[...], (8, 2 * half)).astype(BF16), w1_ref[...])[:1]
        pre = first + pltpu.roll(second, N_CHUNK - 1, 0) + bias
        return jax.nn.gelu(pre).astype(BF16)

    hk = hidden(xk_ref, w1k_ref, pk_ref)
    kc = _dot(hk, w2k_ref[...])
    kc_ref[0] = jnp.concatenate([jnp.zeros((CMP_PAD, HEAD_DIM), F32), kc], axis=0).astype(BF16)
    hv = hidden(xv_ref, w1v_ref, pv_ref)
    vcT = _dot_nt(w2vT_ref[...], hv)
    vcT_ref[0] = jnp.concatenate([jnp.zeros((HEAD_DIM, CMP_PAD), F32), vcT], axis=1).astype(BF16)


def _compress(xk, xv, w1k, w2k, pk, w1v, w2vT, pv):
    const2 = lambda g: (0, 0)
    head = lambda g: (g, 0, 0)
    return pl.pallas_call(
        _compress_kernel,
        grid=(NSA_KV_HEADS,),
        in_specs=[
            pl.BlockSpec((1, N_CHUNK, CMP_STRIDE * HEAD_DIM), head),
            pl.BlockSpec((1, N_CHUNK, CMP_STRIDE * HEAD_DIM), head),
            pl.BlockSpec(w1k.shape, const2),
            pl.BlockSpec(w2k.shape, const2),
            pl.BlockSpec(pk.shape, const2),
            pl.BlockSpec(w1v.shape, const2),
            pl.BlockSpec(w2vT.shape, const2),
            pl.BlockSpec(pv.shape, const2),
        ],
        out_specs=[
            pl.BlockSpec((1, N_CMP_PAD, HEAD_DIM), head),
            pl.BlockSpec((1, HEAD_DIM, N_CMP_PAD), head),
        ],
        out_shape=[
            jax.ShapeDtypeStruct((NSA_KV_HEADS, N_CMP_PAD, HEAD_DIM), BF16),
            jax.ShapeDtypeStruct((NSA_KV_HEADS, HEAD_DIM, N_CMP_PAD), BF16),
        ],
        compiler_params=pltpu.CompilerParams(
            dimension_semantics=("arbitrary",), vmem_limit_bytes=VMEM_LIMIT),
        name="compress",
    )(xk, xv, w1k, w2k, pk, w1v, w2vT, pv)


def _nsa_kernel(qT_ref, kc_ref, vcT_ref, ks_ref, vsT_ref, kw_ref, vwT_ref, bn_ref, bw_ref, bc_ref, mt_ref, gT_ref,
                oT_ref, sc_ref, neg_ref, negfar_ref, m_ref, acc_ref, tot_ref, sbuf_ref, mloc_ref):
    i = pl.program_id(1)
    s0 = i * TQ
    qT = jnp.concatenate([qT_ref[r * HEAD_DIM:(r + 1) * HEAD_DIM, :] for r in range(GQA)], axis=1)

    def gate_row(branch):
        return jnp.concatenate([gT_ref[r * 3 + branch:r * 3 + branch + 1, :] for r in range(GQA)], axis=1)

    def reset():
        m_ref[...] = jnp.full_like(m_ref, M_FLOOR)
        acc_ref[...] = jnp.zeros_like(acc_ref)

    def attend(k_ref, vT_ref, start, size, add):
        s = _dot(k_ref[0, pl.ds(start, size), :], qT)
        if add is not None:
            s = s + add
        m_prev = m_ref[...]
        m_new = jnp.maximum(m_prev, jnp.max(s, axis=0, keepdims=True))
        alpha = jnp.exp2(m_prev - m_new)
        p = jnp.exp2(s - m_new).astype(BF16)
        acc_ref[...] = alpha * acc_ref[...] + _dot(vT_ref[0, :, pl.ds(start, size)], p)
        m_ref[...] = m_new

    def finish(branch):
        acc = acc_ref[...]
        scale = gate_row(branch) / jnp.maximum(acc[HEAD_DIM:HEAD_DIM + 1, :], 1e-30)
        tot_ref[...] += acc[:HEAD_DIM, :] * scale

    n0 = i * (TQ // CMP_STRIDE)
    band0 = pl.multiple_of(n0 + CMP_PAD - 16, 8)
    row = lax.broadcasted_iota(jnp.int32, (N_CMP_PAD, QL), 0)
    live = (row >= CMP_PAD) & (row < band0 + CMP_BAND)
    sc_ref[...] = jnp.where(live, _dot(kc_ref[0], qT), NEG)
    sc_ref[pl.ds(band0, CMP_BAND), :] += bc_ref[0]
    sc = sc_ref[...]
    mc = jnp.maximum(jnp.max(sc, axis=0, keepdims=True), M_FLOOR)
    pc = jnp.exp2(sc - mc)
    pc = pc * (1.0 / jnp.maximum(jnp.sum(pc, axis=0, keepdims=True), 1e-30))
    tot_ref[...] = _dot(vcT_ref[0], pc.astype(BF16)) * gate_row(0)

    imp = pc[:, 0:TQ]
    for r in range(1, GQA):
        imp = imp + pc[:, r * TQ:(r + 1) * TQ]
    mt = mt_ref[...]
    p_slc = jnp.zeros((N_SLC, TQ), F32)
    rem = imp
    for _ in range(2):
        piece = rem.astype(BF16)
        p_slc = p_slc + _dot(mt, piece)
        rem = rem - piece.astype(F32)
    blk = lax.broadcasted_iota(jnp.int32, (N_SLC, TQ), 0)
    cur = (s0 + lax.broadcasted_iota(jnp.int32, (N_SLC, TQ), 1)) // SLC_LEN
    valid = blk <= cur
    forced = valid & ((blk == 0) | (blk >= cur - (N_LOCAL - 1)))
    score = jnp.where(forced, BIG, jnp.where(valid, p_slc, -BIG))
    blk_f = blk.astype(F32)
    picked = jnp.zeros((N_SLC, TQ), F32)
    for _ in range(N_SEL):
        best = jnp.max(score, axis=0, keepdims=True)
        first = jnp.min(jnp.where(score == best, blk_f, float(N_SLC)), axis=0, keepdims=True)
        hit = blk_f == first
        picked = jnp.where(hit, 1.0, picked)
        score = jnp.where(hit, -jnp.inf, score)
    neg = jnp.where(picked > 0.0, 0.0, NEG)
    neg_ref[...] = jnp.concatenate([neg] * GQA, axis=1)
    near_blk = 2 * i - 2
    negfar_ref[...] = jnp.concatenate([jnp.where(blk >= near_blk, NEG, neg)] * GQA, axis=1)

    def block_mask(ref, j0, nblk):
        return jnp.concatenate(
            [jnp.broadcast_to(ref[pl.ds(j0 + b, 1), :], (SLC_LEN, QL)) for b in range(nblk)], axis=0)

    reset()
    far_blocks = FAR_KEYS // SLC_LEN
    n_far = (i - 1 + FAR_KEYS // TQ - 1) // (FAR_KEYS // TQ)

    def far_logits(c):
        start = pl.multiple_of(c * FAR_KEYS, FAR_KEYS)
        s = _dot(ks_ref[0, pl.ds(start, FAR_KEYS), :], qT) + block_mask(negfar_ref, c * far_blocks, far_blocks)
        sbuf_ref[c % 2] = s
        mloc_ref[c % 2] = jnp.max(s, axis=0, keepdims=True)

    def far_consume(c):
        start = pl.multiple_of(c * FAR_KEYS, FAR_KEYS)
        m_prev = m_ref[...]
        m_new = jnp.maximum(m_prev, mloc_ref[c % 2])
        alpha = jnp.exp2(m_prev - m_new)
        p = jnp.exp2(sbuf_ref[c % 2] - m_new).astype(BF16)
        acc_ref[...] = alpha * acc_ref[...] + _dot(vsT_ref[0, :, pl.ds(start, FAR_KEYS)], p)
        m_ref[...] = m_new

    def near_attend():
        attend(ks_ref, vsT_ref, pl.multiple_of(s0 - TQ, TQ), 2 * TQ,
               bn_ref[0] + block_mask(neg_ref, near_blk, 4))

    @pl.when(i >= 2)
    def _():
        far_logits(0)

        def far_body(c, carry):
            far_consume(c)
            far_logits(c + 1)
            return carry

        lax.fori_loop(0, n_far - 1, far_body, 0)
        far_consume(n_far - 1)
        near_attend()

    @pl.when(i == 1)
    def _():
        near_attend()

    @pl.when(i == 0)
    def _():
        attend(ks_ref, vsT_ref, 0, TQ, bn_ref[0, TQ:2 * TQ, :] + block_mask(neg_ref, 0, 2))

    finish(1)

    reset()
    n_win = WINDOW // TQ

    @pl.when(i >= n_win)
    def _():
        attend(kw_ref, vwT_ref, pl.multiple_of(s0 - WINDOW, TQ), WINDOW + TQ, bw_ref[0])

    @pl.when(i < n_win)
    def _():
        for c in range(n_win + 1):
            def chunk(c=c):
                attend(kw_ref, vwT_ref, pl.multiple_of(s0 - WINDOW + c * TQ, TQ), TQ,
                       bw_ref[0, c * TQ:(c + 1) * TQ, :])
            if c == n_win:
                chunk()
            else:
                pl.when(i + c >= n_win)(chunk)

    finish(2)

    tot = tot_ref[...]
    for r in range(GQA):
        oT_ref[r * HEAD_DIM:(r + 1) * HEAD_DIM, :] = tot[:, r * TQ:(r + 1) * TQ]


def _nsa(qT, kc, vcT, ks, vsT, kw, vwT, bias_near, bias_win, bias_cmp, mt, gnT):
    s = qT.shape[1]
    head3 = lambda g, i: (g, 0, 0)
    return pl.pallas_call(
        _nsa_kernel,
        grid=(NSA_KV_HEADS, s // TQ),
        in_specs=[
            pl.BlockSpec((GQA * HEAD_DIM, TQ), lambda g, i: (g, i)),
            pl.BlockSpec((1, N_CMP_PAD, HEAD_DIM), head3),
            pl.BlockSpec((1, HEAD_DIM, N_CMP_PAD), head3),
            pl.BlockSpec((1, s, HEAD_DIM), head3),
            pl.BlockSpec((1, V_ROWS, s), head3),
            pl.BlockSpec((1, s, HEAD_DIM), head3),
            pl.BlockSpec((1, V_ROWS, s), head3),
            pl.BlockSpec((1, 2 * TQ, QL), head3),
            pl.BlockSpec((1, WINDOW + TQ, QL), head3),
            pl.BlockSpec((1, CMP_BAND, QL), head3),
            pl.BlockSpec(mt.shape, lambda g, i: (0, 0)),
            pl.BlockSpec((16, TQ), lambda g, i: (g, i)),
        ],
        out_specs=pl.BlockSpec((GQA * HEAD_DIM, TQ), lambda g, i: (g, i)),
        out_shape=jax.ShapeDtypeStruct((NSA_WIDTH, s), F32),
        scratch_shapes=[
            pltpu.VMEM((N_CMP_PAD, QL), F32),
            pltpu.VMEM((N_SLC, QL), F32),
            pltpu.VMEM((N_SLC, QL), F32),
            pltpu.VMEM((1, QL), F32),
            pltpu.VMEM((V_ROWS, QL), F32),
            pltpu.VMEM((HEAD_DIM, QL), F32),
            pltpu.VMEM((2, FAR_KEYS, QL), F32),
            pltpu.VMEM((2, 1, QL), F32),
        ],
        compiler_params=pltpu.CompilerParams(
            dimension_semantics=("arbitrary", "arbitrary"), vmem_limit_bytes=VMEM_LIMIT),
        name="nsa",
    )(qT, kc, vcT, ks, vsT, kw, vwT, bias_near, bias_win, bias_cmp, mt, gnT)


def _t5_bucket(dist):
    n = jnp.maximum(dist, 0)
    max_exact = REL_BUCKETS // 2
    nf = jnp.maximum(n, 1).astype(F32)
    large = max_exact + (jnp.log(nf / max_exact) / math.log(REL_MAX_DIST / max_exact)
                         * (REL_BUCKETS - max_exact)).astype(jnp.int32)
    large = jnp.minimum(large, REL_BUCKETS - 1)
    return jnp.where(n < max_exact, n, large)


def _bias_tiles(rel_bias):
    n_dist = 2 * TQ
    tab = rel_bias.astype(F32)
    tab = (tab[_t5_bucket(jnp.arange(n_dist))] - tab[REL_BUCKETS - 1]).T * LOG2E

    def by_distance(d):
        return jnp.where(d >= 0, tab[:, jnp.clip(d, 0, n_dist - 1)], NEG)

    def toeplitz(c, nk):
        n = nk + TQ
        w = by_distance(jnp.arange(n) - (nk - 1) + c)
        a = jnp.tile(w, (1, nk + 1))[:, :nk * (n + 1)].reshape(NSA_HEADS, nk, n + 1)[:, ::-1, :TQ]
        a = a.reshape(NSA_KV_HEADS, GQA, nk, TQ)
        return jnp.transpose(a, (0, 2, 1, 3)).reshape(NSA_KV_HEADS, nk, QL)

    near = toeplitz(TQ, 2 * TQ)
    cmp_band = toeplitz(16 * CMP_STRIDE - (CMP_LEN - 1), CMP_BAND * CMP_STRIDE)[:, ::CMP_STRIDE]
    kq, iq = np.arange(TQ)[:, None], np.arange(TQ)[None, :]
    oldest = np.tile(np.where(kq > iq, 0.0, NEG).astype(np.float32), (1, GQA))
    window = jnp.concatenate([
        jnp.broadcast_to(jnp.asarray(oldest), (NSA_KV_HEADS, TQ, QL)),
        jnp.zeros((NSA_KV_HEADS, WINDOW - 2 * TQ, QL), F32),
        near], axis=1)
    return near, window, cmp_band


def _overlap_matrix():
    ratio = SLC_LEN // CMP_STRIDE
    front = CMP_LEN // CMP_STRIDE - 1
    w_ov = np.convolve(np.ones(ratio), np.ones(CMP_LEN // CMP_STRIDE))
    mt = np.zeros((N_SLC, N_CMP_PAD), np.float32)
    for j in range(N_SLC):
        for o, w in enumerate(w_ov):
            n = ratio * j + o - front
            if 0 <= n < N_CMP:
                mt[j, CMP_PAD + n] = w
    return jnp.asarray(mt, BF16)


def _merge_kernel(x_ref, ya_ref, oT_ref, gb_ref, wup_ref, wout_ref, x1_ref):
    yb = _dot(oT_ref[...].T.astype(BF16), wup_ref[...])
    mix = gb_ref[:, :D_MODEL] * ya_ref[...] + gb_ref[:, D_MODEL:] * yb
    x1_ref[...] = x_ref[...] + _dot(mix.astype(BF16), wout_ref[...])


def _merge(x2, ya, oT, gb, wup, wout, tm=512):
    s = x2.shape[0]
    row = lambda i: (i, 0)
    const = lambda i: (0, 0)
    return pl.pallas_call(
        _merge_kernel,
        grid=(s // tm,),
        in_specs=[
            pl.BlockSpec((tm, D_MODEL), row),
            pl.BlockSpec((tm, D_MODEL), row),
            pl.BlockSpec((NSA_WIDTH, tm), lambda i: (0, i)),
            pl.BlockSpec((tm, 2 * D_MODEL), row),
            pl.BlockSpec(wup.shape, const),
            pl.BlockSpec(wout.shape, const),
        ],
        out_specs=pl.BlockSpec((tm, D_MODEL), row),
        out_shape=jax.ShapeDtypeStruct((s, D_MODEL), F32),
        compiler_params=pltpu.CompilerParams(
            dimension_semantics=("arbitrary",), vmem_limit_bytes=VMEM_LIMIT),
        name="merge",
    )(x2, ya, oT, gb, wup, wout)


def _ffn_kernel(x_ref, g_ref, wg_ref, wu_ref, wd_ref, gf_ref, o_ref):
    x = x_ref[...]
    h = _rms(x, g_ref[...]).astype(BF16)
    f = jax.nn.silu(_dot(h, wg_ref[...])) * _dot(h, wu_ref[...])
    x = x + _dot(f.astype(BF16), wd_ref[...])
    o_ref[...] = _rms(x, gf_ref[...])


def _ffn(x1, g, wg, wu, wd, gf, tm=256):
    s = x1.shape[0]
    row = lambda i: (i, 0)
    const = lambda i: (0, 0)
    return pl.pallas_call(
        _ffn_kernel,
        grid=(s // tm,),
        in_specs=[
            pl.BlockSpec((tm, D_MODEL), row),
            pl.BlockSpec((1, D_MODEL), const),
            pl.BlockSpec(wg.shape, const),
            pl.BlockSpec(wu.shape, const),
            pl.BlockSpec(wd.shape, const),
            pl.BlockSpec((1, D_MODEL), const),
        ],
        out_specs=pl.BlockSpec((tm, D_MODEL), row),
        out_shape=jax.ShapeDtypeStruct((s, D_MODEL), F32),
        compiler_params=pltpu.CompilerParams(
            dimension_semantics=("arbitrary",), vmem_limit_bytes=VMEM_LIMIT),
        name="ffn",
    )(x1, g, wg, wu, wd, gf)


def kernel(x, norm_mix_g, w_in, ssm_a_re, ssm_a_im, ssm_log_dt, ssm_b_re, ssm_b_im, ssm_c_re, ssm_c_im, ssm_d, ssm_w_glu, w_up_ssm, cmp_pos_k, cmp_pos_v, cmp_w1_k, cmp_w2_k, cmp_w1_v, cmp_w2_v, rel_bias, w_up_nsa, w_out, norm_ffn_g, w_ffn_gate, w_ffn_up, w_ffn_down, norm_final_g):
    bsz, s, _ = x.shape
    assert (bsz, s) == (1, SEQ) and w_in.shape[0] == 1
    x2 = x.reshape(s, D_MODEL)
    l = 0
    w = w_in[l].astype(BF16)
    wm = w[:, :COL_GN]
    gates_per_head = GQA * 3
    wn = jnp.zeros((D_MODEL, 128), BF16)
    for g in range(NSA_KV_HEADS):
        wn = wn.at[:, 16 * g:16 * g + gates_per_head].set(
            w[:, COL_GN + g * gates_per_head:COL_GN + (g + 1) * gates_per_head])
    wb = w[:, COL_GB:]
    row = lambda v: v.astype(F32).reshape(1, -1)

    u, qT, kcr, vcr, ks, kw, vsT, vwT, gnT, gb = _inproj(x2, row(norm_mix_g[l]), wm, wn, wb)

    bre, bim, cre, cim, scan_consts = _s5_params(
        ssm_a_re[l], ssm_a_im[l], ssm_log_dt[l], ssm_b_re[l], ssm_b_im[l], ssm_c_re[l], ssm_c_im[l])
    ya = _s5(u, bre, bim, cre, cim, scan_consts, row(ssm_d[l]),
             ssm_w_glu[l].astype(BF16), w_up_ssm[l].astype(BF16))

    chunks = lambda a: a.reshape(NSA_KV_HEADS, N_CHUNK, CMP_STRIDE * HEAD_DIM)
    kc, vcT = _compress(chunks(kcr), chunks(vcr),
                        cmp_w1_k[l].astype(BF16), cmp_w2_k[l].astype(BF16), row(cmp_pos_k[l]),
                        cmp_w1_v[l].astype(BF16), cmp_w2_v[l].T.astype(BF16), row(cmp_pos_v[l]))

    bias_near, bias_win, bias_cmp = _bias_tiles(rel_bias)
    oT = _nsa(qT, kc, vcT, ks, vsT, kw, vwT, bias_near, bias_win, bias_cmp, _overlap_matrix(), gnT)

    x1 = _merge(x2, ya, oT, gb, w_up_nsa[l].astype(BF16), w_out[l].astype(BF16))
    out = _ffn(x1, row(norm_ffn_g[l]), w_ffn_gate[l].astype(BF16), w_ffn_up[l].astype(BF16),
               w_ffn_down[l].astype(BF16), row(norm_final_g))
    return out.reshape(bsz, s, D_MODEL)
```

```python
import functools
import math

import numpy as np
import jax
import jax.numpy as jnp
from jax import lax
from jax.experimental import pallas as pl
from jax.experimental.pallas import tpu as pltpu

F32 = jnp.float32
BF16 = jnp.bfloat16

D_MODEL = 1024
SEQ = 16384
EPS = 1e-6
SSM_WIDTH = 512
SSM_GROUP = 16
SSM_GROUPS = SSM_WIDTH // SSM_GROUP
SSM_STATE = 64
SSM_LANES = SSM_GROUPS * SSM_STATE
NSA_HEADS = 8
NSA_KV_HEADS = 2
GQA = NSA_HEADS // NSA_KV_HEADS
HEAD_DIM = 64
NSA_WIDTH = NSA_HEADS * HEAD_DIM
KV_WIDTH = NSA_KV_HEADS * HEAD_DIM
CMP_LEN = 32
CMP_STRIDE = 16
CMP_HIDDEN = 256
SLC_LEN = 64
N_SEL = 16
N_LOCAL = 2
WINDOW = 512
BIG = 1e4
REL_BUCKETS = 32
REL_MAX_DIST = 128
D_FF = 2816

N_CHUNK = SEQ // CMP_STRIDE
N_CMP = (SEQ - CMP_LEN) // CMP_STRIDE + 1
N_SLC = SEQ // SLC_LEN
TQ = 128
QL = GQA * TQ
CMP_PAD = 128
N_CMP_PAD = CMP_PAD + N_CHUNK
CMP_BAND = 24
NEG = -1e30
M_FLOOR = -1e29
SCAN_ROWS = 8
LOG2E = math.log2(math.e)
V_ROWS = HEAD_DIM + 16
FAR_KEYS = 512
FAR_BLOCKS = FAR_KEYS // SLC_LEN
K_COLS = HEAD_DIM + 16
CMP_EXTENTS = (384, 640, 896, N_CMP_PAD)

VMEM_LIMIT = 56 * 1024 * 1024

COL_U = 0
COL_Q = 512
COL_KC = 1024
COL_VC = 1152
COL_KS = 1280
COL_VS = 1408
COL_KW = 1536
COL_VW = 1664
COL_GN = 1792
COL_GB = 1816
COL_END = 3864


def _rms(x, g):
    return x * lax.rsqrt(jnp.mean(x * x, axis=-1, keepdims=True) + EPS) * g


def _dot(a, b):
    return jnp.dot(a, b, preferred_element_type=F32)


def _dot_nt(a, b):
    return lax.dot_general(a, b, (((1,), (1,)), ((), ())), preferred_element_type=F32)


def _inproj_kernel(x_ref, g_ref, wm_ref, wn_ref, wb_ref,
                   u_ref, qT_ref, kcr_ref, vcr_ref, ks_ref, kw_ref, vsT_ref, vwT_ref, gnT_ref, gb_ref):
    h = _rms(x_ref[...], g_ref[...]).astype(BF16)
    pm = _dot(h, wm_ref[...])
    u_ref[...] = pm[:, COL_U:COL_Q]
    qT_ref[...] = (pm[:, COL_Q:COL_KC] * (HEAD_DIM ** -0.5 * LOG2E)).T.astype(BF16)
    vsT = pm[:, COL_VS:COL_KW].T
    vwT = pm[:, COL_VW:COL_GN].T
    tm = vsT.shape[1]
    ones = jnp.ones((V_ROWS - HEAD_DIM, tm), F32)
    tok = pl.program_id(0) * tm + lax.broadcasted_iota(jnp.int32, (tm, K_COLS - HEAD_DIM), 0)
    col = lax.broadcasted_iota(jnp.int32, (tm, K_COLS - HEAD_DIM), 1)
    blk_onehot = jnp.where((tok // SLC_LEN) % FAR_BLOCKS == col, 1.0, 0.0)
    for g in range(NSA_KV_HEADS):
        lo = g * HEAD_DIM
        kcr_ref[g] = pm[:, COL_KC + lo:COL_KC + lo + HEAD_DIM]
        vcr_ref[g] = pm[:, COL_VC + lo:COL_VC + lo + HEAD_DIM]
        ks_ref[g] = jnp.concatenate(
            [pm[:, COL_KS + lo:COL_KS + lo + HEAD_DIM], blk_onehot], axis=1).astype(BF16)
        kw_ref[g] = pm[:, COL_KW + lo:COL_KW + lo + HEAD_DIM].astype(BF16)
        vsT_ref[g] = jnp.concatenate([vsT[lo:lo + HEAD_DIM], ones], axis=0).astype(BF16)
        vwT_ref[g] = jnp.concatenate([vwT[lo:lo + HEAD_DIM], ones], axis=0).astype(BF16)
    gn = jax.nn.sigmoid(_dot(h, wn_ref[...]))
    gnT_ref[...] = gn.T[:32, :]
    gb_ref[...] = jax.nn.sigmoid(_dot(h, wb_ref[...]))


def _inproj(x2, g, wm, wn, wb, tm=512):
    s = x2.shape[0]
    const = lambda i: (0, 0)
    row = lambda i: (i, 0)
    col = lambda i: (0, i)
    return pl.pallas_call(
        _inproj_kernel,
        grid=(s // tm,),
        in_specs=[
            pl.BlockSpec((tm, D_MODEL), row),
            pl.BlockSpec((1, D_MODEL), const),
            pl.BlockSpec(wm.shape, const),
            pl.BlockSpec(wn.shape, const),
            pl.BlockSpec(wb.shape, const),
        ],
        out_specs=[
            pl.BlockSpec((tm, SSM_WIDTH), row),
            pl.BlockSpec((NSA_WIDTH, tm), col),
            pl.BlockSpec((NSA_KV_HEADS, tm, HEAD_DIM), lambda i: (0, i, 0)),
            pl.BlockSpec((NSA_KV_HEADS, tm, HEAD_DIM), lambda i: (0, i, 0)),
            pl.BlockSpec((NSA_KV_HEADS, tm, K_COLS), lambda i: (0, i, 0)),
            pl.BlockSpec((NSA_KV_HEADS, tm, HEAD_DIM), lambda i: (0, i, 0)),
            pl.BlockSpec((NSA_KV_HEADS, V_ROWS, tm), lambda i: (0, 0, i)),
            pl.BlockSpec((NSA_KV_HEADS, V_ROWS, tm), lambda i: (0, 0, i)),
            pl.BlockSpec((32, tm), col),
            pl.BlockSpec((tm, 2 * D_MODEL), row),
        ],
        out_shape=[
            jax.ShapeDtypeStruct((s, SSM_WIDTH), F32),
            jax.ShapeDtypeStruct((NSA_WIDTH, s), BF16),
            jax.ShapeDtypeStruct((NSA_KV_HEADS, s, HEAD_DIM), F32),
            jax.ShapeDtypeStruct((NSA_KV_HEADS, s, HEAD_DIM), F32),
            jax.ShapeDtypeStruct((NSA_KV_HEADS, s, K_COLS), BF16),
            jax.ShapeDtypeStruct((NSA_KV_HEADS, s, HEAD_DIM), BF16),
            jax.ShapeDtypeStruct((NSA_KV_HEADS, V_ROWS, s), BF16),
            jax.ShapeDtypeStruct((NSA_KV_HEADS, V_ROWS, s), BF16),
            jax.ShapeDtypeStruct((32, s), F32),
            jax.ShapeDtypeStruct((s, 2 * D_MODEL), F32),
        ],
        compiler_params=pltpu.CompilerParams(
            dimension_semantics=("arbitrary",), vmem_limit_bytes=VMEM_LIMIT),
        name="inproj",
    )(x2, g, wm, wn, wb)


def _s5_kernel(u_ref, bre_ref, bim_ref, cre_ref, cim_ref, sc_ref, d_ref, wglu_ref, wup_ref,
               ya_ref, xre_ref, xim_ref, cre_s, cim_s, *, tt):
    @pl.when(pl.program_id(0) == 0)
    def _():
        cre_s[...] = jnp.zeros_like(cre_s)
        cim_s[...] = jnp.zeros_like(cim_s)

    u = u_ref[...]
    ub = u.astype(BF16)
    xre_ref[...] = _dot(ub, bre_ref[...])
    xim_ref[...] = _dot(ub, bim_ref[...])

    def cmul_add(re, im, are, aim, sre, sim):
        return re + are * sre - aim * sim, im + are * sim + aim * sre

    def slab(i, carry):
        c_re, c_im = carry
        r0 = pl.multiple_of(i * SCAN_ROWS, SCAN_ROWS)
        re = xre_ref[pl.ds(r0, SCAN_ROWS), :]
        im = xim_ref[pl.ds(r0, SCAN_ROWS), :]
        for k, shift in enumerate((1, 2, 4)):
            are = sc_ref[2 * k]
            aim = sc_ref[2 * k + 1]
            sre = pltpu.roll(re, shift, 0)
            sim = pltpu.roll(im, shift, 0)
            re, im = cmul_add(re, im, are, aim, sre, sim)
        re, im = cmul_add(re, im, sc_ref[6], sc_ref[7], c_re, c_im)
        xre_ref[pl.ds(r0, SCAN_ROWS), :] = re
        xim_ref[pl.ds(r0, SCAN_ROWS), :] = im
        return re[SCAN_ROWS - 1:SCAN_ROWS, :], im[SCAN_ROWS - 1:SCAN_ROWS, :]

    c_re, c_im = lax.fori_loop(0, tt // SCAN_ROWS, slab, (cre_s[...], cim_s[...]))
    cre_s[...] = c_re
    cim_s[...] = c_im

    y = (_dot(xre_ref[...].astype(BF16), cre_ref[...]) - _dot(xim_ref[...].astype(BF16), cim_ref[...])
         + d_ref[...] * u)
    z = jax.nn.gelu(y)
    z = z * jax.nn.sigmoid(_dot(z.astype(BF16), wglu_ref[...]))
    ya_ref[...] = _dot(z.astype(BF16), wup_ref[...])


def _s5(u, bre, bim, cre, cim, scan_consts, d, wglu, wup, tt=256):
    s = u.shape[0]
    const2 = lambda i: (0, 0)
    return pl.pallas_call(
        functools.partial(_s5_kernel, tt=tt),
        grid=(s // tt,),
        in_specs=[
            pl.BlockSpec((tt, SSM_WIDTH), lambda i: (i, 0)),
            pl.BlockSpec(bre.shape, const2),
            pl.BlockSpec(bim.shape, const2),
            pl.BlockSpec(cre.shape, const2),
            pl.BlockSpec(cim.shape, const2),
            pl.BlockSpec(scan_consts.shape, lambda i: (0, 0, 0)),
            pl.BlockSpec((1, SSM_WIDTH), const2),
            pl.BlockSpec(wglu.shape, const2),
            pl.BlockSpec(wup.shape, const2),
        ],
        out_specs=pl.BlockSpec((tt, D_MODEL), lambda i: (i, 0)),
        out_shape=jax.ShapeDtypeStruct((s, D_MODEL), F32),
        scratch_shapes=[
            pltpu.VMEM((tt, SSM_LANES), F32),
            pltpu.VMEM((tt, SSM_LANES), F32),
            pltpu.VMEM((1, SSM_LANES), F32),
            pltpu.VMEM((1, SSM_LANES), F32),
        ],
        compiler_params=pltpu.CompilerParams(
            dimension_semantics=("arbitrary",), vmem_limit_bytes=VMEM_LIMIT),
        name="s5",
    )(u, bre, bim, cre, cim, scan_consts, d, wglu, wup)


def _s5_params(a_re, a_im, log_dt, b_re, b_im, c_re, c_im):
    dt = jnp.exp(log_dt.astype(F32))[:, None]
    ar, ai = a_re.astype(F32), a_im.astype(F32)
    mag = jnp.exp(ar * dt)
    lr, li = mag * jnp.cos(ai * dt), mag * jnp.sin(ai * dt)
    den = ar * ar + ai * ai
    kr = ((lr - 1.0) * ar + li * ai) / den
    ki = (li * ar - (lr - 1.0) * ai) / den
    br, bi = b_re.astype(F32), b_im.astype(F32)
    bbr = kr[..., None] * br - ki[..., None] * bi
    bbi = kr[..., None] * bi + ki[..., None] * br
    eye = jnp.eye(SSM_GROUPS, dtype=F32)
    bd_in = lambda b: jnp.einsum("gpc,gh->gchp", b, eye).reshape(SSM_WIDTH, SSM_LANES)
    bd_out = lambda c: jnp.einsum("gcp,gh->gphc", c, eye).reshape(SSM_LANES, SSM_WIDTH)
    lr, li = lr.reshape(1, SSM_LANES), li.reshape(1, SSM_LANES)

    def cm(xr, xi, yr, yi):
        return xr * yr - xi * yi, xr * yi + xi * yr

    pows = [(lr, li)]
    for _ in range(SCAN_ROWS - 1):
        pows.append(cm(*pows[-1], lr, li))
    row = jnp.arange(SCAN_ROWS)[:, None]
    consts = []
    for shift in (1, 2, 4):
        pr, pi = pows[shift - 1]
        consts += [jnp.where(row >= shift, pr, 0.0), jnp.where(row >= shift, pi, 0.0)]
    consts += [jnp.concatenate([p[0] for p in pows], 0), jnp.concatenate([p[1] for p in pows], 0)]
    return (bd_in(bbr).astype(BF16), bd_in(bbi).astype(BF16),
            bd_out(c_re.astype(F32)).astype(BF16), bd_out(c_im.astype(F32)).astype(BF16),
            jnp.stack(consts))


def _compress_kernel(xk_ref, xv_ref, w1k_ref, w2k_ref, pk_ref, w1v_ref, w2vT_ref, pv_ref, kc_ref, vcT_ref):
    half = CMP_STRIDE * HEAD_DIM

    def hidden(x_ref, w1_ref, pos_ref):
        x = x_ref[0].astype(BF16)
        first = _dot(x, w1_ref[:half, :])
        second = _dot(x, w1_ref[half:, :])
        bias = _dot(jnp.broadcast_to(pos_ref[...], (8, 2 * half)).astype(BF16), w1_ref[...])[:1]
        pre = first + pltpu.roll(second, N_CHUNK - 1, 0) + bias
        return jax.nn.gelu(pre).astype(BF16)

    hk = hidden(xk_ref, w1k_ref, pk_ref)
    kc = _dot(hk, w2k_ref[...])
    kc_ref[0] = jnp.concatenate([jnp.zeros((CMP_PAD, HEAD_DIM), F32), kc], axis=0).astype(BF16)
    hv = hidden(xv_ref, w1v_ref, pv_ref)
    vcT = _dot_nt(w2vT_ref[...], hv)
    vcT_ref[0] = jnp.concatenate([jnp.zeros((HEAD_DIM, CMP_PAD), F32), vcT], axis=1).astype(BF16)


def _compress(xk, xv, w1k, w2k, pk, w1v, w2vT, pv):
    const2 = lambda g: (0, 0)
    head = lambda g: (g, 0, 0)
    return pl.pallas_call(
        _compress_kernel,
        grid=(NSA_KV_HEADS,),
        in_specs=[
            pl.BlockSpec((1, N_CHUNK, CMP_STRIDE * HEAD_DIM), head),
            pl.BlockSpec((1, N_CHUNK, CMP_STRIDE * HEAD_DIM), head),
            pl.BlockSpec(w1k.shape, const2),
            pl.BlockSpec(w2k.shape, const2),
            pl.BlockSpec(pk.shape, const2),
            pl.BlockSpec(w1v.shape, const2),
            pl.BlockSpec(w2vT.shape, const2),
            pl.BlockSpec(pv.shape, const2),
        ],
        out_specs=[
            pl.BlockSpec((1, N_CMP_PAD, HEAD_DIM), head),
            pl.BlockSpec((1, HEAD_DIM, N_CMP_PAD), head),
        ],
        out_shape=[
            jax.ShapeDtypeStruct((NSA_KV_HEADS, N_CMP_PAD, HEAD_DIM), BF16),
            jax.ShapeDtypeStruct((NSA_KV_HEADS, HEAD_DIM, N_CMP_PAD), BF16),
        ],
        compiler_params=pltpu.CompilerParams(
            dimension_semantics=("arbitrary",), vmem_limit_bytes=VMEM_LIMIT),
        name="compress",
    )(xk, xv, w1k, w2k, pk, w1v, w2vT, pv)


def _nsa_kernel(qT_ref, kc_ref, vcT_ref, ks_ref, vsT_ref, kw_ref, vwT_ref, bn_ref, bw_ref, bc_ref, mt_ref, gT_ref,
                oT_ref, sc_ref, neg_ref, negfar_ref, m_ref, acc_ref, tot_ref, sbuf0_ref, sbuf1_ref, mloc_ref, pslc_ref):
    i = pl.program_id(1)
    s0 = i * TQ
    qT = jnp.concatenate([qT_ref[r * HEAD_DIM:(r + 1) * HEAD_DIM, :] for r in range(GQA)], axis=1)

    qT_nomask = jnp.concatenate([qT, jnp.zeros((K_COLS - HEAD_DIM, QL), BF16)], axis=0)

    def gate_row(branch):
        return jnp.concatenate([gT_ref[r * 3 + branch:r * 3 + branch + 1, :] for r in range(GQA)], axis=1)

    def reset():
        m_ref[...] = jnp.full_like(m_ref, M_FLOOR)
        acc_ref[...] = jnp.zeros_like(acc_ref)

    def attend(k_ref, vT_ref, start, size, add):
        k = k_ref[0, pl.ds(start, size), :]
        s = _dot(k, qT if k.shape[1] == HEAD_DIM else qT_nomask) + add
        m_prev = m_ref[...]
        m_new = jnp.maximum(m_prev, jnp.max(s, axis=0, keepdims=True))
        alpha = jnp.exp2(m_prev - m_new)
        p = jnp.exp2(s - m_new).astype(BF16)
        acc_ref[...] = alpha * acc_ref[...] + _dot(vT_ref[0, :, pl.ds(start, size)], p)
        m_ref[...] = m_new

    def finish(branch):
        acc = acc_ref[...]
        scale = gate_row(branch) / jnp.maximum(acc[HEAD_DIM:HEAD_DIM + 1, :], 1e-30)
        tot_ref[...] += acc[:HEAD_DIM, :] * scale

    n0 = i * (TQ // CMP_STRIDE)
    band0 = pl.multiple_of(n0 + CMP_PAD - 16, 8)

    def cmp_branch(nrows):
        row = lax.broadcasted_iota(jnp.int32, (nrows, QL), 0)
        live = (row >= CMP_PAD) & (row < band0 + CMP_BAND)
        sc_ref[0:nrows, :] = jnp.where(live, _dot(kc_ref[0, 0:nrows, :], qT), NEG)
        sc_ref[pl.ds(band0, CMP_BAND), :] += bc_ref[0]
        sc = sc_ref[0:nrows, :]
        mc = jnp.maximum(jnp.max(sc, axis=0, keepdims=True), M_FLOOR)
        pc = jnp.exp2(sc - mc)
        pc = pc * (1.0 / jnp.maximum(jnp.sum(pc, axis=0, keepdims=True), 1e-30))
        tot_ref[...] = _dot(vcT_ref[0, :, 0:nrows], pc.astype(BF16)) * gate_row(0)
        imp = pc[:, 0:TQ]
        for r in range(1, GQA):
            imp = imp + pc[:, r * TQ:(r + 1) * TQ]
        mt = mt_ref[:, 0:nrows]
        p_slc = jnp.zeros((N_SLC, TQ), F32)
        rem = imp
        for _ in range(2):
            piece = rem.astype(BF16)
            p_slc = p_slc + _dot(mt, piece)
            rem = rem - piece.astype(F32)
        pslc_ref[...] = p_slc

    prev_rows = 0
    for nrows in CMP_EXTENTS:
        lo, hi = prev_rows, nrows
        pl.when((band0 + CMP_BAND > lo) & (band0 + CMP_BAND <= hi))(functools.partial(cmp_branch, nrows))
        prev_rows = nrows

    p_slc = pslc_ref[...]
    blk = lax.broadcasted_iota(jnp.int32, (N_SLC, TQ), 0)
    cur = (s0 + lax.broadcasted_iota(jnp.int32, (N_SLC, TQ), 1)) // SLC_LEN
    valid = blk <= cur
    forced = valid & ((blk == 0) | (blk >= cur - (N_LOCAL - 1)))
    score = jnp.where(forced, BIG, jnp.where(valid, p_slc, -BIG))
    blk_f = blk.astype(F32)
    for _ in range(N_SEL):
        best = jnp.max(score, axis=0, keepdims=True)
        first = jnp.min(jnp.where(score == best, blk_f, float(N_SLC)), axis=0, keepdims=True)
        score = jnp.where(blk_f == first, -jnp.inf, score)
    neg = jnp.where(score == -jnp.inf, 0.0, NEG)
    neg_ref[...] = jnp.concatenate([neg] * GQA, axis=1)
    near_blk = 2 * i - 2
    negfar_ref[...] = jnp.concatenate([jnp.where(blk >= near_blk, NEG, neg)] * GQA, axis=1)

    def block_mask(ref, j0, nblk):
        return jnp.concatenate(
            [jnp.broadcast_to(ref[pl.ds(j0 + b, 1), :], (SLC_LEN, QL)) for b in range(nblk)], axis=0)

    reset()
    n_far =(i - 1 + FAR_KEYS // TQ - 1) // (FAR_KEYS // TQ)

    sbufs = (sbuf0_ref, sbuf1_ref)

    def far_logits(c, slot):
        start = pl.multiple_of(c * FAR_KEYS, FAR_KEYS)
        mask_rows = negfar_ref[pl.ds(pl.multiple_of(c * FAR_BLOCKS, FAR_BLOCKS), FAR_BLOCKS), :]
        extra = jnp.concatenate([mask_rows, jnp.zeros((K_COLS - HEAD_DIM - FAR_BLOCKS, QL), F32)], axis=0)
        q_masked = jnp.concatenate([qT, extra.astype(BF16)], axis=0)
        s = _dot(ks_ref[0, pl.ds(start, FAR_KEYS), :], q_masked)
        sbufs[slot][...] = s
        mloc_ref[slot] = jnp.max(s, axis=0, keepdims=True)

    def far_consume(c, slot):
        start = pl.multiple_of(c * FAR_KEYS, FAR_KEYS)
        m_prev = m_ref[...]
        m_new = jnp.maximum(m_prev, mloc_ref[slot])
        alpha = jnp.exp2(m_prev - m_new)
        p = jnp.exp2(sbufs[slot][...] - m_new).astype(BF16)
        acc_ref[...] = alpha * acc_ref[...] + _dot(vsT_ref[0, :, pl.ds(start, FAR_KEYS)], p)
        m_ref[...] = m_new

    def near_attend():
        attend(ks_ref, vsT_ref, pl.multiple_of(s0 - TQ, TQ), 2 * TQ,
               bn_ref[0] + block_mask(neg_ref, near_blk, 4))

    @pl.when(i >= 2)
    def _():
        n_pairs = (n_far + 1) // 2
        far_logits(0, 0)

        def far_body(p, carry):
            far_logits(2 * p + 1, 1)
            far_consume(2 * p, 0)
            far_logits(2 * p + 2, 0)
            far_consume(2 * p + 1, 1)
            return carry

        lax.fori_loop(0, n_pairs - 1, far_body, 0)
        last = 2 * (n_pairs - 1)
        far_logits(last + 1, 1)
        far_consume(last, 0)
        far_consume(last + 1, 1)
        near_attend()

    @pl.when(i == 1)
    def _():
        near_attend()

    @pl.when(i == 0)
    def _():
        attend(ks_ref, vsT_ref, 0, TQ, bn_ref[0, TQ:2 * TQ, :] + block_mask(neg_ref, 0, 2))

    finish(1)

    reset()
    n_win = WINDOW // TQ

    @pl.when(i >= n_win)
    def _():
        attend(kw_ref, vwT_ref, pl.multiple_of(s0 - WINDOW, TQ), WINDOW + TQ, bw_ref[0])

    @pl.when(i < n_win)
    def _():
        for c in range(n_win + 1):
            def chunk(c=c):
                attend(kw_ref, vwT_ref, pl.multiple_of(s0 - WINDOW + c * TQ, TQ), TQ,
                       bw_ref[0, c * TQ:(c + 1) * TQ, :])
            if c == n_win:
                chunk()
            else:
                pl.when(i + c >= n_win)(chunk)

    finish(2)

    tot = tot_ref[...]
    for r in range(GQA):
        oT_ref[r * HEAD_DIM:(r + 1) * HEAD_DIM, :] = tot[:, r * TQ:(r + 1) * TQ]


def _nsa(qT, kc, vcT, ks, vsT, kw, vwT, bias_near, bias_win, bias_cmp, mt, gnT):
    s = qT.shape[1]
    head3 = lambda g, i: (g, 0, 0)
    return pl.pallas_call(
        _nsa_kernel,
        grid=(NSA_KV_HEADS, s // TQ),
        in_specs=[
            pl.BlockSpec((GQA * HEAD_DIM, TQ), lambda g, i: (g, i)),
            pl.BlockSpec((1, N_CMP_PAD, HEAD_DIM), head3),
            pl.BlockSpec((1, HEAD_DIM, N_CMP_PAD), head3),
            pl.BlockSpec((1, s, K_COLS), head3),
            pl.BlockSpec((1, V_ROWS, s), head3),
            pl.BlockSpec((1, s, HEAD_DIM), head3),
            pl.BlockSpec((1, V_ROWS, s), head3),
            pl.BlockSpec((1, 2 * TQ, QL), head3),
            pl.BlockSpec((1, WINDOW + TQ, QL), head3),
            pl.BlockSpec((1, CMP_BAND, QL), head3),
            pl.BlockSpec(mt.shape, lambda g, i: (0, 0)),
            pl.BlockSpec((16, TQ), lambda g, i: (g, i)),
        ],
        out_specs=pl.BlockSpec((GQA * HEAD_DIM, TQ), lambda g, i: (g, i)),
        out_shape=jax.ShapeDtypeStruct((NSA_WIDTH, s), F32),
        scratch_shapes=[
            pltpu.VMEM((N_CMP_PAD, QL), F32),
            pltpu.VMEM((N_SLC, QL), F32),
            pltpu.VMEM((N_SLC, QL), F32),
            pltpu.VMEM((1, QL), F32),
            pltpu.VMEM((V_ROWS, QL), F32),
            pltpu.VMEM((HEAD_DIM, QL), F32),
            pltpu.VMEM((FAR_KEYS, QL), F32),
            pltpu.VMEM((FAR_KEYS, QL), F32),
            pltpu.VMEM((2, 1, QL), F32),
            pltpu.VMEM((N_SLC, TQ), F32),
        ],
        compiler_params=pltpu.CompilerParams(
            dimension_semantics=("arbitrary", "arbitrary"), vmem_limit_bytes=VMEM_LIMIT),
        name="nsa",
    )(qT, kc, vcT, ks, vsT, kw, vwT, bias_near, bias_win, bias_cmp, mt, gnT)


def _t5_bucket(dist):
    n = jnp.maximum(dist, 0)
    max_exact = REL_BUCKETS // 2
    nf = jnp.maximum(n, 1).astype(F32)
    large = max_exact + (jnp.log(nf / max_exact) / math.log(REL_MAX_DIST / max_exact)
                         * (REL_BUCKETS - max_exact)).astype(jnp.int32)
    large = jnp.minimum(large, REL_BUCKETS - 1)
    return jnp.where(n < max_exact, n, large)


def _bias_tiles(rel_bias):
    n_dist = 2 * TQ
    tab = rel_bias.astype(F32)
    tab = (tab[_t5_bucket(jnp.arange(n_dist))] - tab[REL_BUCKETS - 1]).T * LOG2E

    def by_distance(d):
        return jnp.where(d >= 0, tab[:, jnp.clip(d, 0, n_dist - 1)], NEG)

    def toeplitz(c, nk):
        n = nk + TQ
        w = by_distance(jnp.arange(n) - (nk - 1) + c)
        a = jnp.tile(w, (1, nk + 1))[:, :nk * (n + 1)].reshape(NSA_HEADS, nk, n + 1)[:, ::-1, :TQ]
        a = a.reshape(NSA_KV_HEADS, GQA, nk, TQ)
        return jnp.transpose(a, (0, 2, 1, 3)).reshape(NSA_KV_HEADS, nk, QL)

    near = toeplitz(TQ, 2 * TQ)
    cmp_band = toeplitz(16 * CMP_STRIDE - (CMP_LEN - 1), CMP_BAND * CMP_STRIDE)[:, ::CMP_STRIDE]
    kq, iq = np.arange(TQ)[:, None], np.arange(TQ)[None, :]
    oldest = np.tile(np.where(kq > iq, 0.0, NEG).astype(np.float32), (1, GQA))
    window = jnp.concatenate([
        jnp.broadcast_to(jnp.asarray(oldest), (NSA_KV_HEADS, TQ, QL)),
        jnp.zeros((NSA_KV_HEADS, WINDOW - 2 * TQ, QL), F32),
        near], axis=1)
    return near, window, cmp_band


def _overlap_matrix():
    ratio = SLC_LEN // CMP_STRIDE
    front = CMP_LEN // CMP_STRIDE - 1
    w_ov = np.convolve(np.ones(ratio), np.ones(CMP_LEN // CMP_STRIDE))
    mt = np.zeros((N_SLC, N_CMP_PAD), np.float32)
    for j in range(N_SLC):
        for o, w in enumerate(w_ov):
            n = ratio * j + o - front
            if 0 <= n < N_CMP:
                mt[j, CMP_PAD + n] = w
    return jnp.asarray(mt, BF16)


def _merge_kernel(x_ref, ya_ref, oT_ref, gb_ref, wup_ref, wout_ref, x1_ref):
    yb = _dot(oT_ref[...].T.astype(BF16), wup_ref[...])
    mix = gb_ref[:, :D_MODEL] * ya_ref[...] + gb_ref[:, D_MODEL:] * yb
    x1_ref[...] = x_ref[...] + _dot(mix.astype(BF16), wout_ref[...])


def _merge(x2, ya, oT, gb, wup, wout, tm=512):
    s = x2.shape[0]
    row = lambda i: (i, 0)
    const = lambda i: (0, 0)
    return pl.pallas_call(
        _merge_kernel,
        grid=(s // tm,),
        in_specs=[
            pl.BlockSpec((tm, D_MODEL), row),
            pl.BlockSpec((tm, D_MODEL), row),
            pl.BlockSpec((NSA_WIDTH, tm), lambda i: (0, i)),
            pl.BlockSpec((tm, 2 * D_MODEL), row),
            pl.BlockSpec(wup.shape, const),
            pl.BlockSpec(wout.shape, const),
        ],
        out_specs=pl.BlockSpec((tm, D_MODEL), row),
        out_shape=jax.ShapeDtypeStruct((s, D_MODEL), F32),
        compiler_params=pltpu.CompilerParams(
            dimension_semantics=("arbitrary",), vmem_limit_bytes=VMEM_LIMIT),
        name="merge",
    )(x2, ya, oT, gb, wup, wout)


def _ffn_kernel(x_ref, g_ref, wg_ref, wu_ref, wd_ref, gf_ref, o_ref):
    x = x_ref[...]
    h = _rms(x, g_ref[...]).astype(BF16)
    f = jax.nn.silu(_dot(h, wg_ref[...])) * _dot(h, wu_ref[...])
    x = x + _dot(f.astype(BF16), wd_ref[...])
    o_ref[...] = _rms(x, gf_ref[...])


def _ffn(x1, g, wg, wu, wd, gf, tm=256):
    s = x1.shape[0]
    row = lambda i: (i, 0)
    const = lambda i: (0, 0)
    return pl.pallas_call(
        _ffn_kernel,
        grid=(s // tm,),
        in_specs=[
            pl.BlockSpec((tm, D_MODEL), row),
            pl.BlockSpec((1, D_MODEL), const),
            pl.BlockSpec(wg.shape, const),
            pl.BlockSpec(wu.shape, const),
            pl.BlockSpec(wd.shape, const),
            pl.BlockSpec((1, D_MODEL), const),
        ],
        out_specs=pl.BlockSpec((tm, D_MODEL), row),
        out_shape=jax.ShapeDtypeStruct((s, D_MODEL), F32),
        compiler_params=pltpu.CompilerParams(
            dimension_semantics=("arbitrary",), vmem_limit_bytes=VMEM_LIMIT),
        name="ffn",
    )(x1, g, wg, wu, wd, gf)


def kernel(x, norm_mix_g, w_in, ssm_a_re, ssm_a_im, ssm_log_dt, ssm_b_re, ssm_b_im, ssm_c_re, ssm_c_im, ssm_d, ssm_w_glu, w_up_ssm, cmp_pos_k, cmp_pos_v, cmp_w1_k, cmp_w2_k, cmp_w1_v, cmp_w2_v, rel_bias, w_up_nsa, w_out, norm_ffn_g, w_ffn_gate, w_ffn_up, w_ffn_down, norm_final_g):
    bsz, s, _ = x.shape
    assert (bsz, s) == (1, SEQ) and w_in.shape[0] == 1
    x2 = x.reshape(s, D_MODEL)
    l = 0
    w = w_in[l].astype(BF16)
    wm = w[:, :COL_GN]
    gates_per_head = GQA * 3
    wn = jnp.zeros((D_MODEL, 128), BF16)
    for g in range(NSA_KV_HEADS):
        wn = wn.at[:, 16 * g:16 * g + gates_per_head].set(
            w[:, COL_GN + g * gates_per_head:COL_GN + (g + 1) * gates_per_head])
    wb = w[:, COL_GB:]
    row = lambda v: v.astype(F32).reshape(1, -1)

    u, qT, kcr, vcr, ks, kw, vsT, vwT, gnT, gb = _inproj(x2, row(norm_mix_g[l]), wm, wn, wb)

    bre, bim, cre, cim, scan_consts = _s5_params(
        ssm_a_re[l], ssm_a_im[l], ssm_log_dt[l], ssm_b_re[l], ssm_b_im[l], ssm_c_re[l], ssm_c_im[l])
    ya = _s5(u, bre, bim, cre, cim, scan_consts, row(ssm_d[l]),
             ssm_w_glu[l].astype(BF16), w_up_ssm[l].astype(BF16))

    chunks = lambda a: a.reshape(NSA_KV_HEADS, N_CHUNK, CMP_STRIDE * HEAD_DIM)
    kc, vcT = _compress(chunks(kcr), chunks(vcr),
                        cmp_w1_k[l].astype(BF16), cmp_w2_k[l].astype(BF16), row(cmp_pos_k[l]),
                        cmp_w1_v[l].astype(BF16), cmp_w2_v[l].T.astype(BF16), row(cmp_pos_v[l]))

    bias_near, bias_win, bias_cmp = _bias_tiles(rel_bias)
    oT = _nsa(qT, kc, vcT, ks, vsT, kw, vwT, bias_near, bias_win, bias_cmp, _overlap_matrix(), gnT)

    x1 = _merge(x2, ya, oT, gb, w_up_nsa[l].astype(BF16), w_out[l].astype(BF16))
    out = _ffn(x1, row(norm_ffn_g[l]), w_ffn_gate[l].astype(BF16), w_ffn_up[l].astype(BF16),
               w_ffn_down[l].astype(BF16), row(norm_final_g))
    return out.reshape(bsz, s, D_MODEL)
```

```python
import functools
import math

import numpy as np
import jax
import jax.numpy as jnp
from jax import lax
from jax.experimental import pallas as pl
from jax.experimental.pallas import tpu as pltpu

F32 = jnp.float32
BF16 = jnp.bfloat16

D_MODEL = 1024
SEQ = 16384
EPS = 1e-6
SSM_WIDTH = 512
SSM_GROUP = 16
SSM_GROUPS = SSM_WIDTH // SSM_GROUP
SSM_STATE = 64
SSM_LANES = SSM_GROUPS * SSM_STATE
NSA_HEADS = 8
NSA_KV_HEADS = 2
GQA = NSA_HEADS // NSA_KV_HEADS
HEAD_DIM = 64
NSA_WIDTH = NSA_HEADS * HEAD_DIM
KV_WIDTH = NSA_KV_HEADS * HEAD_DIM
CMP_LEN = 32
CMP_STRIDE = 16
CMP_HIDDEN = 256
SLC_LEN = 64
N_SEL = 16
N_LOCAL = 2
WINDOW = 512
BIG = 1e4
REL_BUCKETS = 32
REL_MAX_DIST = 128
D_FF = 2816

N_CHUNK = SEQ // CMP_STRIDE
N_CMP = (SEQ - CMP_LEN) // CMP_STRIDE + 1
N_SLC = SEQ // SLC_LEN
TQ = 128
QL = GQA * TQ
CMP_PAD = 128
N_CMP_PAD = CMP_PAD + N_CHUNK
CMP_BAND = 24
NEG = -1e30
M_FLOOR = -1e29
SCAN_ROWS = 8
S5_TT = 256
SEG = S5_TT // SCAN_ROWS
SSM_SUPER = 2
SG_CH = SSM_WIDTH // SSM_SUPER
SG_LANES = SSM_LANES // SSM_SUPER
LOG2E = math.log2(math.e)
V_ROWS = HEAD_DIM + 16
FAR_KEYS = 512
FAR_BLOCKS = FAR_KEYS // SLC_LEN
K_COLS = HEAD_DIM + 16
CMP_EXTENTS = (384, 640, 896, N_CMP_PAD)

VMEM_LIMIT = 56 * 1024 * 1024

COL_U = 0
COL_Q = 512
COL_KC = 1024
COL_VC = 1152
COL_KS = 1280
COL_VS = 1408
COL_KW = 1536
COL_VW = 1664
COL_GN = 1792
COL_GB = 1816
COL_END = 3864


def _rms(x, g):
    return x * lax.rsqrt(jnp.mean(x * x, axis=-1, keepdims=True) + EPS) * g


def _dot(a, b):
    return jnp.dot(a, b, preferred_element_type=F32)


def _dot_nt(a, b):
    return lax.dot_general(a, b, (((1,), (1,)), ((), ())), preferred_element_type=F32)


def _inproj_kernel(x_ref, g_ref, wm_ref, wn_ref, wb_ref,
                   u_ref, qT_ref, kcr_ref, vcr_ref, ks_ref, kw_ref, vsT_ref, vwT_ref, gnT_ref, gb_ref):
    h = _rms(x_ref[...], g_ref[...]).astype(BF16)
    pm = _dot(h, wm_ref[...])
    u_ref[...] = pm[:, COL_U:COL_Q]
    qT_ref[...] = (pm[:, COL_Q:COL_KC] * (HEAD_DIM ** -0.5 * LOG2E)).T.astype(BF16)
    vsT = pm[:, COL_VS:COL_KW].T
    vwT = pm[:, COL_VW:COL_GN].T
    tm = vsT.shape[1]
    ones = jnp.ones((V_ROWS - HEAD_DIM, tm), F32)
    tok = pl.program_id(0) * tm + lax.broadcasted_iota(jnp.int32, (tm, K_COLS - HEAD_DIM), 0)
    col = lax.broadcasted_iota(jnp.int32, (tm, K_COLS - HEAD_DIM), 1)
    blk_onehot = jnp.where((tok // SLC_LEN) % FAR_BLOCKS == col, 1.0, 0.0)
    for g in range(NSA_KV_HEADS):
        lo = g * HEAD_DIM
        kcr_ref[g] = pm[:, COL_KC + lo:COL_KC + lo + HEAD_DIM]
        vcr_ref[g] = pm[:, COL_VC + lo:COL_VC + lo + HEAD_DIM]
        ks_ref[g] = jnp.concatenate(
            [pm[:, COL_KS + lo:COL_KS + lo + HEAD_DIM], blk_onehot], axis=1).astype(BF16)
        kw_ref[g] = pm[:, COL_KW + lo:COL_KW + lo + HEAD_DIM].astype(BF16)
        vsT_ref[g] = jnp.concatenate([vsT[lo:lo + HEAD_DIM], ones], axis=0).astype(BF16)
        vwT_ref[g] = jnp.concatenate([vwT[lo:lo + HEAD_DIM], ones], axis=0).astype(BF16)
    gn = jax.nn.sigmoid(_dot(h, wn_ref[...]))
    gnT_ref[...] = gn.T[:32, :]
    gb_ref[...] = jax.nn.sigmoid(_dot(h, wb_ref[...])).astype(BF16)


def _inproj(x2, g, wm, wn, wb, tm=512):
    s = x2.shape[0]
    const = lambda i: (0, 0)
    row = lambda i: (i, 0)
    col = lambda i: (0, i)
    return pl.pallas_call(
        _inproj_kernel,
        grid=(s // tm,),
        in_specs=[
            pl.BlockSpec((tm, D_MODEL), row),
            pl.BlockSpec((1, D_MODEL), const),
            pl.BlockSpec(wm.shape, const),
            pl.BlockSpec(wn.shape, const),
            pl.BlockSpec(wb.shape, const),
        ],
        out_specs=[
            pl.BlockSpec((tm, SSM_WIDTH), row),
            pl.BlockSpec((NSA_WIDTH, tm), col),
            pl.BlockSpec((NSA_KV_HEADS, tm, HEAD_DIM), lambda i: (0, i, 0)),
            pl.BlockSpec((NSA_KV_HEADS, tm, HEAD_DIM), lambda i: (0, i, 0)),
            pl.BlockSpec((NSA_KV_HEADS, tm, K_COLS), lambda i: (0, i, 0)),
            pl.BlockSpec((NSA_KV_HEADS, tm, HEAD_DIM), lambda i: (0, i, 0)),
            pl.BlockSpec((NSA_KV_HEADS, V_ROWS, tm), lambda i: (0, 0, i)),
            pl.BlockSpec((NSA_KV_HEADS, V_ROWS, tm), lambda i: (0, 0, i)),
            pl.BlockSpec((32, tm), col),
            pl.BlockSpec((tm, 2 * D_MODEL), row),
        ],
        out_shape=[
            jax.ShapeDtypeStruct((s, SSM_WIDTH), F32),
            jax.ShapeDtypeStruct((NSA_WIDTH, s), BF16),
            jax.ShapeDtypeStruct((NSA_KV_HEADS, s, HEAD_DIM), F32),
            jax.ShapeDtypeStruct((NSA_KV_HEADS, s, HEAD_DIM), F32),
            jax.ShapeDtypeStruct((NSA_KV_HEADS, s, K_COLS), BF16),
            jax.ShapeDtypeStruct((NSA_KV_HEADS, s, HEAD_DIM), BF16),
            jax.ShapeDtypeStruct((NSA_KV_HEADS, V_ROWS, s), BF16),
            jax.ShapeDtypeStruct((NSA_KV_HEADS, V_ROWS, s), BF16),
            jax.ShapeDtypeStruct((32, s), F32),
            jax.ShapeDtypeStruct((s, 2 * D_MODEL), BF16),
        ],
        compiler_params=pltpu.CompilerParams(
            dimension_semantics=("arbitrary",), vmem_limit_bytes=VMEM_LIMIT),
        name="inproj",
    )(x2, g, wm, wn, wb)


def _s5_kernel(u_ref, perm_ref, permT_ref, b_ref, c_ref, pw_ref, seg_ref, d_ref, wglu_ref, wup_ref,
               ya_ref, xre_ref, xim_ref, st_ref, cre_s, cim_s):
    @pl.when(pl.program_id(0) == 0)
    def _():
        cre_s[...] = jnp.zeros_like(cre_s)
        cim_s[...] = jnp.zeros_like(cim_s)

    u = u_ref[...]
    ub = _dot(perm_ref[...], u.astype(BF16)).astype(BF16)
    for sg in range(SSM_SUPER):
        bu = _dot(ub[:, sg * SG_CH:(sg + 1) * SG_CH], b_ref[sg])
        xre_ref[:, sg * SG_LANES:(sg + 1) * SG_LANES] = bu[:, :SG_LANES]
        xim_ref[:, sg * SG_LANES:(sg + 1) * SG_LANES] = bu[:, SG_LANES:]

    def cmul_add(re, im, are, aim, sre, sim):
        return re + are * sre - aim * sim, im + are * sim + aim * sre

    for sg in range(SSM_SUPER):
        lanes = slice(sg * SG_LANES, (sg + 1) * SG_LANES)

        lam_re, lam_im = pw_ref[0, 0:SCAN_ROWS, lanes], pw_ref[1, 0:SCAN_ROWS, lanes]

        def local(j, carry):
            r0 = pl.multiple_of(j * SCAN_ROWS, SCAN_ROWS)
            re, im = cmul_add(xre_ref[pl.ds(r0, SCAN_ROWS), lanes], xim_ref[pl.ds(r0, SCAN_ROWS), lanes],
                              lam_re, lam_im, *carry)
            xre_ref[pl.ds(r0, SCAN_ROWS), lanes] = re
            xim_ref[pl.ds(r0, SCAN_ROWS), lanes] = im
            return re, im

        zero = jnp.zeros((SCAN_ROWS, SG_LANES), F32)
        re, im = lax.fori_loop(0, SEG, local, (zero, zero), unroll=True)

        for k, shift in enumerate((1, 2, 4)):
            re, im = cmul_add(re, im, seg_ref[2 * k, :, lanes], seg_ref[2 * k + 1, :, lanes],
                              pltpu.roll(re, shift, 0), pltpu.roll(im, shift, 0))
        cin_re, cin_im = cre_s[:, lanes], cim_s[:, lanes]
        re, im = cmul_add(re, im, seg_ref[6, :, lanes], seg_ref[7, :, lanes], cin_re, cin_im)
        cre_s[:, lanes] = re[SCAN_ROWS - 1:SCAN_ROWS, :]
        cim_s[:, lanes] = im[SCAN_ROWS - 1:SCAN_ROWS, :]
        first = lax.broadcasted_iota(jnp.int32, (SCAN_ROWS, SG_LANES), 0) == 0
        start_re = jnp.where(first, cin_re, pltpu.roll(re, 1, 0))
        start_im = jnp.where(first, cin_im, pltpu.roll(im, 1, 0))
        start_re = jnp.concatenate([start_re, start_re], axis=0)
        start_im = jnp.concatenate([start_im, start_im], axis=0)

        def fix(jj, carry):
            r0 = pl.multiple_of(jj * 2 * SCAN_ROWS, 2 * SCAN_ROWS)
            rows = pl.ds(r0, 2 * SCAN_ROWS)
            re, im = cmul_add(xre_ref[rows, lanes], xim_ref[rows, lanes],
                              pw_ref[0, rows, lanes], pw_ref[1, rows, lanes], start_re, start_im)
            st_ref[rows, 2 * sg * SG_LANES:(2 * sg + 1) * SG_LANES] = re.astype(BF16)
            st_ref[rows, (2 * sg + 1) * SG_LANES:(2 * sg + 2) * SG_LANES] = im.astype(BF16)
            return carry

        lax.fori_loop(0, SEG // 2, fix, 0, unroll=True)
    y_perm = jnp.concatenate(
        [_dot(st_ref[:, 2 * sg * SG_LANES:(2 * sg + 2) * SG_LANES], c_ref[sg])
         for sg in range(SSM_SUPER)], axis=1)
    y_hi = y_perm.astype(BF16)
    y_lo = (y_perm - y_hi.astype(F32)).astype(BF16)
    y = _dot(permT_ref[...], y_hi) + _dot(permT_ref[...], y_lo) + d_ref[...] * u
    z = jax.nn.gelu(y)
    z = z * jax.nn.sigmoid(_dot(z.astype(BF16), wglu_ref[...]))
    ya_ref[...] = _dot(z.astype(BF16), wup_ref[...])


def _s5(u, b, c, pw, seg, d, wglu, wup):
    s = u.shape[0]
    t = np.arange(S5_TT)
    perm = np.zeros((S5_TT, S5_TT), np.float32)
    perm[SCAN_ROWS * (t % SEG) + t // SEG, t] = 1.0
    permT = jnp.asarray(perm.T, BF16)
    perm = jnp.asarray(perm, BF16)
    const2 = lambda i: (0, 0)
    const3 = lambda i: (0, 0, 0)
    return pl.pallas_call(
        _s5_kernel,
        grid=(s // S5_TT,),
        in_specs=[
            pl.BlockSpec((S5_TT, SSM_WIDTH), lambda i: (i, 0)),
            pl.BlockSpec(perm.shape, const2),
            pl.BlockSpec(permT.shape, const2),
            pl.BlockSpec(b.shape, const3),
            pl.BlockSpec(c.shape, const3),
            pl.BlockSpec(pw.shape, const3),
            pl.BlockSpec(seg.shape, const3),
            pl.BlockSpec((1, SSM_WIDTH), const2),
            pl.BlockSpec(wglu.shape, const2),
            pl.BlockSpec(wup.shape, const2),
        ],
        out_specs=pl.BlockSpec((S5_TT, D_MODEL), lambda i: (i, 0)),
        out_shape=jax.ShapeDtypeStruct((s, D_MODEL), F32),
        scratch_shapes=[
            pltpu.VMEM((S5_TT, SSM_LANES), F32),
            pltpu.VMEM((S5_TT, SSM_LANES), F32),
            pltpu.VMEM((S5_TT, 2 * SSM_LANES), BF16),
            pltpu.VMEM((1, SSM_LANES), F32),
            pltpu.VMEM((1, SSM_LANES), F32),
        ],
        compiler_params=pltpu.CompilerParams(
            dimension_semantics=("arbitrary",), vmem_limit_bytes=VMEM_LIMIT),
        name="s5",
    )(u, perm, permT, b, c, pw, seg, d, wglu, wup)


def _s5_params(a_re, a_im, log_dt, b_re, b_im, c_re, c_im):
    dt = jnp.exp(log_dt.astype(F32))[:, None]
    ar, ai = a_re.astype(F32), a_im.astype(F32)
    zr, zi = ar * dt, ai * dt

    def power(n):
        mag = jnp.exp(n * zr)
        return mag * jnp.cos(n * zi), mag * jnp.sin(n * zi)

    lr, li = power(1.0)
    den = ar * ar + ai * ai
    kr = ((lr - 1.0) * ar + li * ai) / den
    ki = (li * ar - (lr - 1.0) * ai) / den
    br, bi = b_re.astype(F32), b_im.astype(F32)
    bbr = kr[..., None] * br - ki[..., None] * bi
    bbi = kr[..., None] * bi + ki[..., None] * br
    eye = jnp.eye(SSM_GROUPS, dtype=F32)
    bd_in = lambda b: jnp.einsum("gpc,gh->gchp", b, eye).reshape(SSM_WIDTH, SSM_LANES)
    bd_out = lambda c: jnp.einsum("gcp,gh->gphc", c, eye).reshape(SSM_LANES, SSM_WIDTH)
    zr, zi = zr.reshape(1, SSM_LANES), zi.reshape(1, SSM_LANES)
    pw = jnp.stack(power(jnp.repeat(jnp.arange(1, SEG + 1, dtype=F32), SCAN_ROWS)[:, None]))
    row = jnp.arange(SCAN_ROWS)[:, None]
    seg = []
    for shift in (1, 2, 4):
        pr, pi = power(float(SEG * shift))
        seg += [jnp.where(row >= shift, pr, 0.0), jnp.where(row >= shift, pi, 0.0)]
    seg += list(power(SEG * (row + 1).astype(F32)))
    b_re_bd, b_im_bd = bd_in(bbr), bd_in(bbi)
    c_re_bd, c_im_bd = bd_out(c_re.astype(F32)), bd_out(c_im.astype(F32))
    b_sg, c_sg = [], []
    for sg in range(SSM_SUPER):
        ch = slice(sg * SG_CH, (sg + 1) * SG_CH)
        st = slice(sg * SG_LANES, (sg + 1) * SG_LANES)
        b_sg.append(jnp.concatenate([b_re_bd[ch, st], b_im_bd[ch, st]], axis=1))
        c_sg.append(jnp.concatenate([c_re_bd[st, ch], -c_im_bd[st, ch]], axis=0))
    return jnp.stack(b_sg).astype(BF16), jnp.stack(c_sg).astype(BF16), pw, jnp.stack(seg)


def _compress_kernel(xk_ref, xv_ref, w1k_ref, w2k_ref, pk_ref, w1v_ref, w2vT_ref, pv_ref, kc_ref, vcT_ref):
    half = CMP_STRIDE * HEAD_DIM

    def hidden(x_ref, w1_ref, pos_ref):
        x = x_ref[0].astype(BF16)
        first = _dot(x, w1_ref[:half, :])
        second = _dot(x, w1_ref[half:, :])
        bias = _dot(jnp.broadcast_to(pos_ref[...], (8, 2 * half)).astype(BF16), w1_ref[...])[:1]
        pre = first + pltpu.roll(second, N_CHUNK - 1, 0) + bias
        return jax.nn.gelu(pre).astype(BF16)

    hk = hidden(xk_ref, w1k_ref, pk_ref)
    kc = _dot(hk, w2k_ref[...])
    kc_ref[0] = jnp.concatenate([jnp.zeros((CMP_PAD, HEAD_DIM), F32), kc], axis=0).astype(BF16)
    hv = hidden(xv_ref, w1v_ref, pv_ref)
    vcT = _dot_nt(w2vT_ref[...], hv)
    vcT_ref[0] = jnp.concatenate([jnp.zeros((HEAD_DIM, CMP_PAD), F32), vcT], axis=1).astype(BF16)


def _compress(xk, xv, w1k, w2k, pk, w1v, w2vT, pv):
    const2 = lambda g: (0, 0)
    head = lambda g: (g, 0, 0)
    return pl.pallas_call(
        _compress_kernel,
        grid=(NSA_KV_HEADS,),
        in_specs=[
            pl.BlockSpec((1, N_CHUNK, CMP_STRIDE * HEAD_DIM), head),
            pl.BlockSpec((1, N_CHUNK, CMP_STRIDE * HEAD_DIM), head),
            pl.BlockSpec(w1k.shape, const2),
            pl.BlockSpec(w2k.shape, const2),
            pl.BlockSpec(pk.shape, const2),
            pl.BlockSpec(w1v.shape, const2),
            pl.BlockSpec(w2vT.shape, const2),
            pl.BlockSpec(pv.shape, const2),
        ],
        out_specs=[
            pl.BlockSpec((1, N_CMP_PAD, HEAD_DIM), head),
            pl.BlockSpec((1, HEAD_DIM, N_CMP_PAD), head),
        ],
        out_shape=[
            jax.ShapeDtypeStruct((NSA_KV_HEADS, N_CMP_PAD, HEAD_DIM), BF16),
            jax.ShapeDtypeStruct((NSA_KV_HEADS, HEAD_DIM, N_CMP_PAD), BF16),
        ],
        compiler_params=pltpu.CompilerParams(
            dimension_semantics=("arbitrary",), vmem_limit_bytes=VMEM_LIMIT),
        name="compress",
    )(xk, xv, w1k, w2k, pk, w1v, w2vT, pv)


def _nsa_kernel(qT_ref, kc_ref, vcT_ref, ks_ref, vsT_ref, kw_ref, vwT_ref, bn_ref, bw_ref, bc_ref, mt_ref, gT_ref,
                oT_ref, sc_ref, neg_ref, negfar_ref, m_ref, acc_ref, tot_ref, sbuf0_ref, sbuf1_ref, mloc_ref, pslc_ref):
    i = pl.program_id(1)
    s0 = i * TQ
    qT = jnp.concatenate([qT_ref[r * HEAD_DIM:(r + 1) * HEAD_DIM, :] for r in range(GQA)], axis=1)

    qT_nomask = jnp.concatenate([qT, jnp.zeros((K_COLS - HEAD_DIM, QL), BF16)], axis=0)

    def gate_row(branch):
        return jnp.concatenate([gT_ref[r * 3 + branch:r * 3 + branch + 1, :] for r in range(GQA)], axis=1)

    def reset():
        m_ref[...] = jnp.full_like(m_ref, M_FLOOR)
        acc_ref[...] = jnp.zeros_like(acc_ref)

    def attend(k_ref, vT_ref, start, size, add):
        k = k_ref[0, pl.ds(start, size), :]
        s = _dot(k, qT if k.shape[1] == HEAD_DIM else qT_nomask) + add
        m_prev = m_ref[...]
        m_new = jnp.maximum(m_prev, jnp.max(s, axis=0, keepdims=True))
        alpha = jnp.exp2(m_prev - m_new)
        p = jnp.exp2(s - m_new).astype(BF16)
        acc_ref[...] = alpha * acc_ref[...] + _dot(vT_ref[0, :, pl.ds(start, size)], p)
        m_ref[...] = m_new

    def finish(branch):
        acc = acc_ref[...]
        scale = gate_row(branch) / jnp.maximum(acc[HEAD_DIM:HEAD_DIM + 1, :], 1e-30)
        tot_ref[...] += acc[:HEAD_DIM, :] * scale

    n0 = i * (TQ // CMP_STRIDE)
    band0 = pl.multiple_of(n0 + CMP_PAD - 16, 8)

    def cmp_branch(nrows):
        row = lax.broadcasted_iota(jnp.int32, (nrows, QL), 0)
        live = (row >= CMP_PAD) & (row < band0 + CMP_BAND)
        sc_ref[0:nrows, :] = jnp.where(live, _dot(kc_ref[0, 0:nrows, :], qT), NEG)
        sc_ref[pl.ds(band0, CMP_BAND), :] += bc_ref[0]
        sc = sc_ref[0:nrows, :]
        mc = jnp.maximum(jnp.max(sc, axis=0, keepdims=True), M_FLOOR)
        pc = jnp.exp2(sc - mc)
        pc = pc * (1.0 / jnp.maximum(jnp.sum(pc, axis=0, keepdims=True), 1e-30))
        tot_ref[...] = _dot(vcT_ref[0, :, 0:nrows], pc.astype(BF16)) * gate_row(0)
        imp = pc[:, 0:TQ]
        for r in range(1, GQA):
            imp = imp + pc[:, r * TQ:(r + 1) * TQ]
        mt = mt_ref[:, 0:nrows]
        p_slc = jnp.zeros((N_SLC, TQ), F32)
        rem = imp
        for _ in range(2):
            piece = rem.astype(BF16)
            p_slc = p_slc + _dot(mt, piece)
            rem = rem - piece.astype(F32)
        pslc_ref[...] = p_slc

    prev_rows = 0
    for nrows in CMP_EXTENTS:
        lo, hi = prev_rows, nrows
        pl.when((band0 + CMP_BAND > lo) & (band0 + CMP_BAND <= hi))(functools.partial(cmp_branch, nrows))
        prev_rows = nrows

    p_slc = pslc_ref[...]
    blk = lax.broadcasted_iota(jnp.int32, (N_SLC, TQ), 0)
    cur = (s0 + lax.broadcasted_iota(jnp.int32, (N_SLC, TQ), 1)) // SLC_LEN
    valid = blk <= cur
    forced = valid & ((blk == 0) | (blk >= cur - (N_LOCAL - 1)))
    score = jnp.where(forced, BIG, jnp.where(valid, p_slc, -BIG))
    blk_f = blk.astype(F32)
    for _ in range(N_SEL):
        best = jnp.max(score, axis=0, keepdims=True)
        first = jnp.min(jnp.where(score == best, blk_f, float(N_SLC)), axis=0, keepdims=True)
        score = jnp.where(blk_f == first, -jnp.inf, score)
    neg = jnp.where(score == -jnp.inf, 0.0, NEG)
    neg_ref[...] = jnp.concatenate([neg] * GQA, axis=1)
    near_blk = 2 * i - 2
    negfar_ref[...] = jnp.concatenate([jnp.where(blk >= near_blk, NEG, neg)] * GQA, axis=1)

    def block_mask(ref, j0, nblk):
        return jnp.concatenate(
            [jnp.broadcast_to(ref[pl.ds(j0 + b, 1), :], (SLC_LEN, QL)) for b in range(nblk)], axis=0)

    reset()
    n_far =(i - 1 + FAR_KEYS // TQ - 1) // (FAR_KEYS // TQ)

    sbufs = (sbuf0_ref, sbuf1_ref)

    def far_logits(c, slot):
        start = pl.multiple_of(c * FAR_KEYS, FAR_KEYS)
        mask_rows = negfar_ref[pl.ds(pl.multiple_of(c * FAR_BLOCKS, FAR_BLOCKS), FAR_BLOCKS), :]
        extra = jnp.concatenate([mask_rows, jnp.zeros((K_COLS - HEAD_DIM - FAR_BLOCKS, QL), F32)], axis=0)
        q_masked = jnp.concatenate([qT, extra.astype(BF16)], axis=0)
        s = _dot(ks_ref[0, pl.ds(start, FAR_KEYS), :], q_masked)
        sbufs[slot][...] = s
        mloc_ref[slot] = jnp.max(s, axis=0, keepdims=True)

    def far_consume(c, slot):
        start = pl.multiple_of(c * FAR_KEYS, FAR_KEYS)
        m_prev = m_ref[...]
        m_new = jnp.maximum(m_prev, mloc_ref[slot])
        alpha = jnp.exp2(m_prev - m_new)
        p = jnp.exp2(sbufs[slot][...] - m_new).astype(BF16)
        acc_ref[...] = alpha * acc_ref[...] + _dot(vsT_ref[0, :, pl.ds(start, FAR_KEYS)], p)
        m_ref[...] = m_new

    def near_attend():
        attend(ks_ref, vsT_ref, pl.multiple_of(s0 - TQ, TQ), 2 * TQ,
               bn_ref[0] + block_mask(neg_ref, near_blk, 4))

    @pl.when(i >= 2)
    def _():
        n_pairs = (n_far + 1) // 2
        far_logits(0, 0)

        def far_body(p, carry):
            far_logits(2 * p + 1, 1)
            far_consume(2 * p, 0)
            far_logits(2 * p + 2, 0)
            far_consume(2 * p + 1, 1)
            return carry

        lax.fori_loop(0, n_pairs - 1, far_body, 0)
        last = 2 * (n_pairs - 1)
        far_logits(last + 1, 1)
        far_consume(last, 0)
        far_consume(last + 1, 1)
        near_attend()

    @pl.when(i == 1)
    def _():
        near_attend()

    @pl.when(i == 0)
    def _():
        attend(ks_ref, vsT_ref, 0, TQ, bn_ref[0, TQ:2 * TQ, :] + block_mask(neg_ref, 0, 2))

    finish(1)

    reset()
    n_win = WINDOW // TQ

    @pl.when(i >= n_win)
    def _():
        attend(kw_ref, vwT_ref, pl.multiple_of(s0 - WINDOW, TQ), WINDOW + TQ, bw_ref[0])

    @pl.when(i < n_win)
    def _():
        for c in range(n_win + 1):
            def chunk(c=c):
                attend(kw_ref, vwT_ref, pl.multiple_of(s0 - WINDOW + c * TQ, TQ), TQ,
                       bw_ref[0, c * TQ:(c + 1) * TQ, :])
            if c == n_win:
                chunk()
            else:
                pl.when(i + c >= n_win)(chunk)

    finish(2)

    tot = tot_ref[...]
    for r in range(GQA):
        oT_ref[r * HEAD_DIM:(r + 1) * HEAD_DIM, :] = tot[:, r * TQ:(r + 1) * TQ]


def _nsa(qT, kc, vcT, ks, vsT, kw, vwT, bias_near, bias_win, bias_cmp, mt, gnT):
    s = qT.shape[1]
    head3 = lambda g, i: (g, 0, 0)
    return pl.pallas_call(
        _nsa_kernel,
        grid=(NSA_KV_HEADS, s // TQ),
        in_specs=[
            pl.BlockSpec((GQA * HEAD_DIM, TQ), lambda g, i: (g, i)),
            pl.BlockSpec((1, N_CMP_PAD, HEAD_DIM), head3),
            pl.BlockSpec((1, HEAD_DIM, N_CMP_PAD), head3),
            pl.BlockSpec((1, s, K_COLS), head3),
            pl.BlockSpec((1, V_ROWS, s), head3),
            pl.BlockSpec((1, s, HEAD_DIM), head3),
            pl.BlockSpec((1, V_ROWS, s), head3),
            pl.BlockSpec((1, 2 * TQ, QL), head3),
            pl.BlockSpec((1, WINDOW + TQ, QL), head3),
            pl.BlockSpec((1, CMP_BAND, QL), head3),
            pl.BlockSpec(mt.shape, lambda g, i: (0, 0)),
            pl.BlockSpec((16, TQ), lambda g, i: (g, i)),
        ],
        out_specs=pl.BlockSpec((GQA * HEAD_DIM, TQ), lambda g, i: (g, i)),
        out_shape=jax.ShapeDtypeStruct((NSA_WIDTH, s), F32),
        scratch_shapes=[
            pltpu.VMEM((N_CMP_PAD, QL), F32),
            pltpu.VMEM((N_SLC, QL), F32),
            pltpu.VMEM((N_SLC, QL), F32),
            pltpu.VMEM((1, QL), F32),
            pltpu.VMEM((V_ROWS, QL), F32),
            pltpu.VMEM((HEAD_DIM, QL), F32),
            pltpu.VMEM((FAR_KEYS, QL), F32),
            pltpu.VMEM((FAR_KEYS, QL), F32),
            pltpu.VMEM((2, 1, QL), F32),
            pltpu.VMEM((N_SLC, TQ), F32),
        ],
        compiler_params=pltpu.CompilerParams(
            dimension_semantics=("arbitrary", "arbitrary"), vmem_limit_bytes=VMEM_LIMIT),
        name="nsa",
    )(qT, kc, vcT, ks, vsT, kw, vwT, bias_near, bias_win, bias_cmp, mt, gnT)


def _t5_bucket(dist):
    n = jnp.maximum(dist, 0)
    max_exact = REL_BUCKETS // 2
    nf = jnp.maximum(n, 1).astype(F32)
    large = max_exact + (jnp.log(nf / max_exact) / math.log(REL_MAX_DIST / max_exact)
                         * (REL_BUCKETS - max_exact)).astype(jnp.int32)
    large = jnp.minimum(large, REL_BUCKETS - 1)
    return jnp.where(n < max_exact, n, large)


def _bias_tiles(rel_bias):
    n_dist = 2 * TQ
    tab = rel_bias.astype(F32)
    tab = (tab[_t5_bucket(jnp.arange(n_dist))] - tab[REL_BUCKETS - 1]).T * LOG2E

    def by_distance(d):
        return jnp.where(d >= 0, tab[:, jnp.clip(d, 0, n_dist - 1)], NEG)

    def toeplitz(c, nk):
        n = nk + TQ
        w = by_distance(jnp.arange(n) - (nk - 1) + c)
        a = jnp.tile(w, (1, nk + 1))[:, :nk * (n + 1)].reshape(NSA_HEADS, nk, n + 1)[:, ::-1, :TQ]
        a = a.reshape(NSA_KV_HEADS, GQA, nk, TQ)
        return jnp.transpose(a, (0, 2, 1, 3)).reshape(NSA_KV_HEADS, nk, QL)

    near = toeplitz(TQ, 2 * TQ)
    cmp_band = toeplitz(16 * CMP_STRIDE - (CMP_LEN - 1), CMP_BAND * CMP_STRIDE)[:, ::CMP_STRIDE]
    kq, iq = np.arange(TQ)[:, None], np.arange(TQ)[None, :]
    oldest = np.tile(np.where(kq > iq, 0.0, NEG).astype(np.float32), (1, GQA))
    window = jnp.concatenate([
        jnp.broadcast_to(jnp.asarray(oldest), (NSA_KV_HEADS, TQ, QL)),
        jnp.zeros((NSA_KV_HEADS, WINDOW - 2 * TQ, QL), F32),
        near], axis=1)
    return near, window, cmp_band


def _overlap_matrix():
    ratio = SLC_LEN // CMP_STRIDE
    front = CMP_LEN // CMP_STRIDE - 1
    w_ov = np.convolve(np.ones(ratio), np.ones(CMP_LEN // CMP_STRIDE))
    mt = np.zeros((N_SLC, N_CMP_PAD), np.float32)
    for j in range(N_SLC):
        for o, w in enumerate(w_ov):
            n = ratio * j + o - front
            if 0 <= n < N_CMP:
                mt[j, CMP_PAD + n] = w
    return jnp.asarray(mt, BF16)


def _tail_kernel(x_ref, ya_ref, oT_ref, gb_ref, wup_ref, wout_ref, g_ref, wg_ref, wu_ref, wd_ref, gf_ref, o_ref):
    yb = _dot(oT_ref[...].T.astype(BF16), wup_ref[...])
    mix = gb_ref[:, :D_MODEL].astype(F32) * ya_ref[...] + gb_ref[:, D_MODEL:].astype(F32) * yb
    x = x_ref[...] + _dot(mix.astype(BF16), wout_ref[...])
    h = _rms(x, g_ref[...]).astype(BF16)
    f = jax.nn.silu(_dot(h, wg_ref[...])) * _dot(h, wu_ref[...])
    x = x + _dot(f.astype(BF16), wd_ref[...])
    o_ref[...] = _rms(x, gf_ref[...])


def _tail(x2, ya, oT, gb, wup, wout, g, wg, wu, wd, gf, tm=256):
    s = x2.shape[0]
    row = lambda i: (i, 0)
    const = lambda i: (0, 0)
    return pl.pallas_call(
        _tail_kernel,
        grid=(s // tm,),
        in_specs=[
            pl.BlockSpec((tm, D_MODEL), row),
            pl.BlockSpec((tm, D_MODEL), row),
            pl.BlockSpec((NSA_WIDTH, tm), lambda i: (0, i)),
            pl.BlockSpec((tm, 2 * D_MODEL), row),
            pl.BlockSpec(wup.shape, const),
            pl.BlockSpec(wout.shape, const),
            pl.BlockSpec((1, D_MODEL), const),
            pl.BlockSpec(wg.shape, const),
            pl.BlockSpec(wu.shape, const),
            pl.BlockSpec(wd.shape, const),
            pl.BlockSpec((1, D_MODEL), const),
        ],
        out_specs=pl.BlockSpec((tm, D_MODEL), row),
        out_shape=jax.ShapeDtypeStruct((s, D_MODEL), F32),
        compiler_params=pltpu.CompilerParams(
            dimension_semantics=("arbitrary",), vmem_limit_bytes=VMEM_LIMIT),
        name="tail",
    )(x2, ya, oT, gb, wup, wout, g, wg, wu, wd, gf)


def kernel(x, norm_mix_g, w_in, ssm_a_re, ssm_a_im, ssm_log_dt, ssm_b_re, ssm_b_im, ssm_c_re, ssm_c_im, ssm_d, ssm_w_glu, w_up_ssm, cmp_pos_k, cmp_pos_v, cmp_w1_k, cmp_w2_k, cmp_w1_v, cmp_w2_v, rel_bias, w_up_nsa, w_out, norm_ffn_g, w_ffn_gate, w_ffn_up, w_ffn_down, norm_final_g):
    bsz, s, _ = x.shape
    assert (bsz, s) == (1, SEQ) and w_in.shape[0] == 1
    x2 = x.reshape(s, D_MODEL)
    l = 0
    w = w_in[l].astype(BF16)
    wm = w[:, :COL_GN]
    gates_per_head = GQA * 3
    wn = jnp.zeros((D_MODEL, 128), BF16)
    for g in range(NSA_KV_HEADS):
        wn = wn.at[:, 16 * g:16 * g + gates_per_head].set(
            w[:, COL_GN + g * gates_per_head:COL_GN + (g + 1) * gates_per_head])
    wb = w[:, COL_GB:]
    row = lambda v: v.astype(F32).reshape(1, -1)

    u, qT, kcr, vcr, ks, kw, vsT, vwT, gnT, gb = _inproj(x2, row(norm_mix_g[l]), wm, wn, wb)

    b_sg, c_sg, pw, seg = _s5_params(
        ssm_a_re[l], ssm_a_im[l], ssm_log_dt[l], ssm_b_re[l], ssm_b_im[l], ssm_c_re[l], ssm_c_im[l])
    ya = _s5(u, b_sg, c_sg, pw, seg, row(ssm_d[l]), ssm_w_glu[l].astype(BF16), w_up_ssm[l].astype(BF16))

    chunks = lambda a: a.reshape(NSA_KV_HEADS, N_CHUNK, CMP_STRIDE * HEAD_DIM)
    kc, vcT = _compress(chunks(kcr), chunks(vcr),
                        cmp_w1_k[l].astype(BF16), cmp_w2_k[l].astype(BF16), row(cmp_pos_k[l]),
                        cmp_w1_v[l].astype(BF16), cmp_w2_v[l].T.astype(BF16), row(cmp_pos_v[l]))

    bias_near, bias_win, bias_cmp = _bias_tiles(rel_bias)
    oT = _nsa(qT, kc, vcT, ks, vsT, kw, vwT, bias_near, bias_win, bias_cmp, _overlap_matrix(), gnT)

    out = _tail(x2, ya, oT, gb, w_up_nsa[l].astype(BF16), w_out[l].astype(BF16),
                row(norm_ffn_g[l]), w_ffn_gate[l].astype(BF16), w_ffn_up[l].astype(BF16),
                w_ffn_down[l].astype(BF16), row(norm_final_g))
    return out.reshape(bsz, s, D_MODEL)
```

```python
import functools
import math

import numpy as np
import jax
import jax.numpy as jnp
from jax import lax
from jax.experimental import pallas as pl
from jax.experimental.pallas import tpu as pltpu

F32 = jnp.float32
BF16 = jnp.bfloat16

D_MODEL = 1024
SEQ = 16384
EPS = 1e-6
SSM_WIDTH = 512
SSM_GROUP = 16
SSM_GROUPS = SSM_WIDTH // SSM_GROUP
SSM_STATE = 64
SSM_LANES = SSM_GROUPS * SSM_STATE
NSA_HEADS = 8
NSA_KV_HEADS = 2
GQA = NSA_HEADS // NSA_KV_HEADS
HEAD_DIM = 64
NSA_WIDTH = NSA_HEADS * HEAD_DIM
KV_WIDTH = NSA_KV_HEADS * HEAD_DIM
CMP_LEN = 32
CMP_STRIDE = 16
CMP_HIDDEN = 256
SLC_LEN = 64
N_SEL = 16
N_LOCAL = 2
WINDOW = 512
BIG = 1e4
REL_BUCKETS = 32
REL_MAX_DIST = 128
D_FF = 2816

N_CHUNK = SEQ // CMP_STRIDE
N_CMP = (SEQ - CMP_LEN) // CMP_STRIDE + 1
N_SLC = SEQ // SLC_LEN
TQ = 256
QL = GQA * TQ
NEAR = 128
assert TQ > NEAR and TQ % NEAR == 0
CMP_PAD = 128
N_CMP_PAD = CMP_PAD + N_CHUNK
CMP_BAND = 16 + TQ // CMP_STRIDE
NEG = -1e30
M_FLOOR = -1e29
SCAN_ROWS = 8
S5_TT = 256
SEG = S5_TT // SCAN_ROWS
SSM_SUPER = 2
SG_CH = SSM_WIDTH // SSM_SUPER
SG_LANES = SSM_LANES // SSM_SUPER
LOG2E = math.log2(math.e)
V_ROWS = HEAD_DIM + 16
FAR_KEYS = 512
FAR_BLOCKS = FAR_KEYS // SLC_LEN
K_COLS = HEAD_DIM + 16
CMP_EXTENTS = (384, 640, 896, N_CMP_PAD)

VMEM_LIMIT = 56 * 1024 * 1024

COL_U = 0
COL_Q = 512
COL_KC = 1024
COL_VC = 1152
COL_KS = 1280
COL_VS = 1408
COL_KW = 1536
COL_VW = 1664
COL_GN = 1792
COL_GB = 1816
COL_END = 3864


def _rms(x, g):
    return x * lax.rsqrt(jnp.mean(x * x, axis=-1, keepdims=True) + EPS) * g


def _dot(a, b):
    return jnp.dot(a, b, preferred_element_type=F32)


def _dot_nt(a, b):
    return lax.dot_general(a, b, (((1,), (1,)), ((), ())), preferred_element_type=F32)


def _inproj_kernel(x_ref, g_ref, wm_ref, wn_ref, wb_ref,
                   u_ref, qT_ref, kcr_ref, vcr_ref, ks_ref, kw_ref, vsT_ref, vwT_ref, gnT_ref, gb_ref):
    h = _rms(x_ref[...], g_ref[...]).astype(BF16)
    pm = _dot(h, wm_ref[...])
    u_ref[...] = pm[:, COL_U:COL_Q]
    qT_ref[...] = (pm[:, COL_Q:COL_KC] * (HEAD_DIM ** -0.5 * LOG2E)).T.astype(BF16)
    vsT = pm[:, COL_VS:COL_KW].T
    vwT = pm[:, COL_VW:COL_GN].T
    tm = vsT.shape[1]
    ones = jnp.ones((V_ROWS - HEAD_DIM, tm), F32)
    tok = pl.program_id(0) * tm + lax.broadcasted_iota(jnp.int32, (tm, K_COLS - HEAD_DIM), 0)
    col = lax.broadcasted_iota(jnp.int32, (tm, K_COLS - HEAD_DIM), 1)
    blk_onehot = jnp.where((tok // SLC_LEN) % FAR_BLOCKS == col, 1.0, 0.0)
    for g in range(NSA_KV_HEADS):
        lo = g * HEAD_DIM
        kcr_ref[g] = pm[:, COL_KC + lo:COL_KC + lo + HEAD_DIM]
        vcr_ref[g] = pm[:, COL_VC + lo:COL_VC + lo + HEAD_DIM]
        ks_ref[g] = jnp.concatenate(
            [pm[:, COL_KS + lo:COL_KS + lo + HEAD_DIM], blk_onehot], axis=1).astype(BF16)
        kw_ref[g] = pm[:, COL_KW + lo:COL_KW + lo + HEAD_DIM].astype(BF16)
        vsT_ref[g] = jnp.concatenate([vsT[lo:lo + HEAD_DIM], ones], axis=0).astype(BF16)
        vwT_ref[g] = jnp.concatenate([vwT[lo:lo + HEAD_DIM], ones], axis=0).astype(BF16)
    gn = jax.nn.sigmoid(_dot(h, wn_ref[...]))
    gnT_ref[...] = gn.T[:32, :]
    gb_ref[...] = jax.nn.sigmoid(_dot(h, wb_ref[...])).astype(BF16)


def _inproj(x2, g, wm, wn, wb, tm=512):
    s = x2.shape[0]
    const = lambda i: (0, 0)
    row = lambda i: (i, 0)
    col = lambda i: (0, i)
    return pl.pallas_call(
        _inproj_kernel,
        grid=(s // tm,),
        in_specs=[
            pl.BlockSpec((tm, D_MODEL), row),
            pl.BlockSpec((1, D_MODEL), const),
            pl.BlockSpec(wm.shape, const),
            pl.BlockSpec(wn.shape, const),
            pl.BlockSpec(wb.shape, const),
        ],
        out_specs=[
            pl.BlockSpec((tm, SSM_WIDTH), row),
            pl.BlockSpec((NSA_WIDTH, tm), col),
            pl.BlockSpec((NSA_KV_HEADS, tm, HEAD_DIM), lambda i: (0, i, 0)),
            pl.BlockSpec((NSA_KV_HEADS, tm, HEAD_DIM), lambda i: (0, i, 0)),
            pl.BlockSpec((NSA_KV_HEADS, tm, K_COLS), lambda i: (0, i, 0)),
            pl.BlockSpec((NSA_KV_HEADS, tm, HEAD_DIM), lambda i: (0, i, 0)),
            pl.BlockSpec((NSA_KV_HEADS, V_ROWS, tm), lambda i: (0, 0, i)),
            pl.BlockSpec((NSA_KV_HEADS, V_ROWS, tm), lambda i: (0, 0, i)),
            pl.BlockSpec((32, tm), col),
            pl.BlockSpec((tm, 2 * D_MODEL), row),
        ],
        out_shape=[
            jax.ShapeDtypeStruct((s, SSM_WIDTH), F32),
            jax.ShapeDtypeStruct((NSA_WIDTH, s), BF16),
            jax.ShapeDtypeStruct((NSA_KV_HEADS, s, HEAD_DIM), F32),
            jax.ShapeDtypeStruct((NSA_KV_HEADS, s, HEAD_DIM), F32),
            jax.ShapeDtypeStruct((NSA_KV_HEADS, s, K_COLS), BF16),
            jax.ShapeDtypeStruct((NSA_KV_HEADS, s, HEAD_DIM), BF16),
            jax.ShapeDtypeStruct((NSA_KV_HEADS, V_ROWS, s), BF16),
            jax.ShapeDtypeStruct((NSA_KV_HEADS, V_ROWS, s), BF16),
            jax.ShapeDtypeStruct((32, s), F32),
            jax.ShapeDtypeStruct((s, 2 * D_MODEL), BF16),
        ],
        compiler_params=pltpu.CompilerParams(
            dimension_semantics=("arbitrary",), vmem_limit_bytes=VMEM_LIMIT),
        name="inproj",
    )(x2, g, wm, wn, wb)


def _s5_kernel(u_ref, perm_ref, permT_ref, b_ref, c_ref, pw_ref, seg_ref, d_ref, wglu_ref, wup_ref,
               ya_ref, xre_ref, xim_ref, st_ref, cre_s, cim_s):
    @pl.when(pl.program_id(0) == 0)
    def _():
        cre_s[...] = jnp.zeros_like(cre_s)
        cim_s[...] = jnp.zeros_like(cim_s)

    u = u_ref[...]
    ub = _dot(perm_ref[...], u.astype(BF16)).astype(BF16)
    for sg in range(SSM_SUPER):
        bu = _dot(ub[:, sg * SG_CH:(sg + 1) * SG_CH], b_ref[sg])
        xre_ref[:, sg * SG_LANES:(sg + 1) * SG_LANES] = bu[:, :SG_LANES]
        xim_ref[:, sg * SG_LANES:(sg + 1) * SG_LANES] = bu[:, SG_LANES:]

    def cmul_add(re, im, are, aim, sre, sim):
        return re + are * sre - aim * sim, im + are * sim + aim * sre

    for sg in range(SSM_SUPER):
        lanes = slice(sg * SG_LANES, (sg + 1) * SG_LANES)

        lam_re, lam_im = pw_ref[0, 0:SCAN_ROWS, lanes], pw_ref[1, 0:SCAN_ROWS, lanes]

        def local(j, carry):
            r0 = pl.multiple_of(j * SCAN_ROWS, SCAN_ROWS)
            re, im = cmul_add(xre_ref[pl.ds(r0, SCAN_ROWS), lanes], xim_ref[pl.ds(r0, SCAN_ROWS), lanes],
                              lam_re, lam_im, *carry)
            xre_ref[pl.ds(r0, SCAN_ROWS), lanes] = re
            xim_ref[pl.ds(r0, SCAN_ROWS), lanes] = im
            return re, im

        zero = jnp.zeros((SCAN_ROWS, SG_LANES), F32)
        re, im = lax.fori_loop(0, SEG, local, (zero, zero), unroll=True)

        for k, shift in enumerate((1, 2, 4)):
            re, im = cmul_add(re, im, seg_ref[2 * k, :, lanes], seg_ref[2 * k + 1, :, lanes],
                              pltpu.roll(re, shift, 0), pltpu.roll(im, shift, 0))
        cin_re, cin_im = cre_s[:, lanes], cim_s[:, lanes]
        re, im = cmul_add(re, im, seg_ref[6, :, lanes], seg_ref[7, :, lanes], cin_re, cin_im)
        cre_s[:, lanes] = re[SCAN_ROWS - 1:SCAN_ROWS, :]
        cim_s[:, lanes] = im[SCAN_ROWS - 1:SCAN_ROWS, :]
        first = lax.broadcasted_iota(jnp.int32, (SCAN_ROWS, SG_LANES), 0) == 0
        start_re = jnp.where(first, cin_re, pltpu.roll(re, 1, 0))
        start_im = jnp.where(first, cin_im, pltpu.roll(im, 1, 0))
        start_re = jnp.concatenate([start_re, start_re], axis=0)
        start_im = jnp.concatenate([start_im, start_im], axis=0)

        def fix(jj, carry):
            r0 = pl.multiple_of(jj * 2 * SCAN_ROWS, 2 * SCAN_ROWS)
            rows = pl.ds(r0, 2 * SCAN_ROWS)
            re, im = cmul_add(xre_ref[rows, lanes], xim_ref[rows, lanes],
                              pw_ref[0, rows, lanes], pw_ref[1, rows, lanes], start_re, start_im)
            st_ref[rows, 2 * sg * SG_LANES:(2 * sg + 1) * SG_LANES] = re.astype(BF16)
            st_ref[rows, (2 * sg + 1) * SG_LANES:(2 * sg + 2) * SG_LANES] = im.astype(BF16)
            return carry

        lax.fori_loop(0, SEG // 2, fix, 0, unroll=True)
    y_perm = jnp.concatenate(
        [_dot(st_ref[:, 2 * sg * SG_LANES:(2 * sg + 2) * SG_LANES], c_ref[sg])
         for sg in range(SSM_SUPER)], axis=1)
    y_hi = y_perm.astype(BF16)
    y_lo = (y_perm - y_hi.astype(F32)).astype(BF16)
    y = _dot(permT_ref[...], y_hi) + _dot(permT_ref[...], y_lo) + d_ref[...] * u
    z = jax.nn.gelu(y)
    z = z * jax.nn.sigmoid(_dot(z.astype(BF16), wglu_ref[...]))
    ya_ref[...] = _dot(z.astype(BF16), wup_ref[...])


def _s5(u, b, c, pw, seg, d, wglu, wup):
    s = u.shape[0]
    t = np.arange(S5_TT)
    perm = np.zeros((S5_TT, S5_TT), np.float32)
    perm[SCAN_ROWS * (t % SEG) + t // SEG, t] = 1.0
    permT = jnp.asarray(perm.T, BF16)
    perm = jnp.asarray(perm, BF16)
    const2 = lambda i: (0, 0)
    const3 = lambda i: (0, 0, 0)
    return pl.pallas_call(
        _s5_kernel,
        grid=(s // S5_TT,),
        in_specs=[
            pl.BlockSpec((S5_TT, SSM_WIDTH), lambda i: (i, 0)),
            pl.BlockSpec(perm.shape, const2),
            pl.BlockSpec(permT.shape, const2),
            pl.BlockSpec(b.shape, const3),
            pl.BlockSpec(c.shape, const3),
            pl.BlockSpec(pw.shape, const3),
            pl.BlockSpec(seg.shape, const3),
            pl.BlockSpec((1, SSM_WIDTH), const2),
            pl.BlockSpec(wglu.shape, const2),
            pl.BlockSpec(wup.shape, const2),
        ],
        out_specs=pl.BlockSpec((S5_TT, D_MODEL), lambda i: (i, 0)),
        out_shape=jax.ShapeDtypeStruct((s, D_MODEL), F32),
        scratch_shapes=[
            pltpu.VMEM((S5_TT, SSM_LANES), F32),
            pltpu.VMEM((S5_TT, SSM_LANES), F32),
            pltpu.VMEM((S5_TT, 2 * SSM_LANES), BF16),
            pltpu.VMEM((1, SSM_LANES), F32),
            pltpu.VMEM((1, SSM_LANES), F32),
        ],
        compiler_params=pltpu.CompilerParams(
            dimension_semantics=("arbitrary",), vmem_limit_bytes=VMEM_LIMIT),
        name="s5",
    )(u, perm, permT, b, c, pw, seg, d, wglu, wup)


def _s5_params(a_re, a_im, log_dt, b_re, b_im, c_re, c_im):
    dt = jnp.exp(log_dt.astype(F32))[:, None]
    ar, ai = a_re.astype(F32), a_im.astype(F32)
    zr, zi = ar * dt, ai * dt

    def power(n):
        mag = jnp.exp(n * zr)
        return mag * jnp.cos(n * zi), mag * jnp.sin(n * zi)

    lr, li = power(1.0)
    den = ar * ar + ai * ai
    kr = ((lr - 1.0) * ar + li * ai) / den
    ki = (li * ar - (lr - 1.0) * ai) / den
    br, bi = b_re.astype(F32), b_im.astype(F32)
    bbr = kr[..., None] * br - ki[..., None] * bi
    bbi = kr[..., None] * bi + ki[..., None] * br
    eye = jnp.eye(SSM_GROUPS, dtype=F32)
    bd_in = lambda b: jnp.einsum("gpc,gh->gchp", b, eye).reshape(SSM_WIDTH, SSM_LANES)
    bd_out = lambda c: jnp.einsum("gcp,gh->gphc", c, eye).reshape(SSM_LANES, SSM_WIDTH)
    zr, zi = zr.reshape(1, SSM_LANES), zi.reshape(1, SSM_LANES)
    pw = jnp.stack(power(jnp.repeat(jnp.arange(1, SEG + 1, dtype=F32), SCAN_ROWS)[:, None]))
    row = jnp.arange(SCAN_ROWS)[:, None]
    seg = []
    for shift in (1, 2, 4):
        pr, pi = power(float(SEG * shift))
        seg += [jnp.where(row >= shift, pr, 0.0), jnp.where(row >= shift, pi, 0.0)]
    seg += list(power(SEG * (row + 1).astype(F32)))
    b_re_bd, b_im_bd = bd_in(bbr), bd_in(bbi)
    c_re_bd, c_im_bd = bd_out(c_re.astype(F32)), bd_out(c_im.astype(F32))
    b_sg, c_sg = [], []
    for sg in range(SSM_SUPER):
        ch = slice(sg * SG_CH, (sg + 1) * SG_CH)
        st = slice(sg * SG_LANES, (sg + 1) * SG_LANES)
        b_sg.append(jnp.concatenate([b_re_bd[ch, st], b_im_bd[ch, st]], axis=1))
        c_sg.append(jnp.concatenate([c_re_bd[st, ch], -c_im_bd[st, ch]], axis=0))
    return jnp.stack(b_sg).astype(BF16), jnp.stack(c_sg).astype(BF16), pw, jnp.stack(seg)


def _compress_kernel(xk_ref, xv_ref, w1k_ref, w2k_ref, pk_ref, w1v_ref, w2vT_ref, pv_ref, kc_ref, vcT_ref):
    half = CMP_STRIDE * HEAD_DIM

    def hidden(x_ref, w1_ref, pos_ref):
        x = x_ref[0].astype(BF16)
        first = _dot(x, w1_ref[:half, :])
        second = _dot(x, w1_ref[half:, :])
        bias = _dot(jnp.broadcast_to(pos_ref[...], (8, 2 * half)).astype(BF16), w1_ref[...])[:1]
        pre = first + pltpu.roll(second, N_CHUNK - 1, 0) + bias
        return jax.nn.gelu(pre).astype(BF16)

    hk = hidden(xk_ref, w1k_ref, pk_ref)
    kc = _dot(hk, w2k_ref[...])
    kc_ref[0] = jnp.concatenate([jnp.zeros((CMP_PAD, HEAD_DIM), F32), kc], axis=0).astype(BF16)
    hv = hidden(xv_ref, w1v_ref, pv_ref)
    vcT = _dot_nt(w2vT_ref[...], hv)
    vcT_ref[0] = jnp.concatenate([jnp.zeros((HEAD_DIM, CMP_PAD), F32), vcT], axis=1).astype(BF16)


def _compress(xk, xv, w1k, w2k, pk, w1v, w2vT, pv):
    const2 = lambda g: (0, 0)
    head = lambda g: (g, 0, 0)
    return pl.pallas_call(
        _compress_kernel,
        grid=(NSA_KV_HEADS,),
        in_specs=[
            pl.BlockSpec((1, N_CHUNK, CMP_STRIDE * HEAD_DIM), head),
            pl.BlockSpec((1, N_CHUNK, CMP_STRIDE * HEAD_DIM), head),
            pl.BlockSpec(w1k.shape, const2),
            pl.BlockSpec(w2k.shape, const2),
            pl.BlockSpec(pk.shape, const2),
            pl.BlockSpec(w1v.shape, const2),
            pl.BlockSpec(w2vT.shape, const2),
            pl.BlockSpec(pv.shape, const2),
        ],
        out_specs=[
            pl.BlockSpec((1, N_CMP_PAD, HEAD_DIM), head),
            pl.BlockSpec((1, HEAD_DIM, N_CMP_PAD), head),
        ],
        out_shape=[
            jax.ShapeDtypeStruct((NSA_KV_HEADS, N_CMP_PAD, HEAD_DIM), BF16),
            jax.ShapeDtypeStruct((NSA_KV_HEADS, HEAD_DIM, N_CMP_PAD), BF16),
        ],
        compiler_params=pltpu.CompilerParams(
            dimension_semantics=("arbitrary",), vmem_limit_bytes=VMEM_LIMIT),
        name="compress",
    )(xk, xv, w1k, w2k, pk, w1v, w2vT, pv)


def _nsa_kernel(qT_ref, kc_ref, vcT_ref, ks_ref, vsT_ref, kw_ref, vwT_ref, bn_ref, bw_ref, bc_ref, mt_ref, gT_ref,
                oT_ref, sc_ref, neg_ref, negfar_ref, m_ref, acc_ref, tot_ref, sbuf0_ref, sbuf1_ref, mloc_ref, pslc_ref):
    i = pl.program_id(1)
    s0 = i * TQ
    qT = jnp.concatenate([qT_ref[r * HEAD_DIM:(r + 1) * HEAD_DIM, :] for r in range(GQA)], axis=1)

    qT_nomask = jnp.concatenate([qT, jnp.zeros((K_COLS - HEAD_DIM, QL), BF16)], axis=0)

    def gate_row(branch):
        return jnp.concatenate([gT_ref[r * 3 + branch:r * 3 + branch + 1, :] for r in range(GQA)], axis=1)

    def reset():
        m_ref[...] = jnp.full_like(m_ref, M_FLOOR)
        acc_ref[...] = jnp.zeros_like(acc_ref)

    def attend(k_ref, vT_ref, start, size, add):
        k = k_ref[0, pl.ds(start, size), :]
        s = _dot(k, qT if k.shape[1] == HEAD_DIM else qT_nomask) + add
        m_prev = m_ref[...]
        m_new = jnp.maximum(m_prev, jnp.max(s, axis=0, keepdims=True))
        alpha = jnp.exp2(m_prev - m_new)
        p = jnp.exp2(s - m_new).astype(BF16)
        acc_ref[...] = alpha * acc_ref[...] + _dot(vT_ref[0, :, pl.ds(start, size)], p)
        m_ref[...] = m_new

    def finish(branch):
        acc = acc_ref[...]
        scale = gate_row(branch) / jnp.maximum(acc[HEAD_DIM:HEAD_DIM + 1, :], 1e-30)
        tot_ref[...] += acc[:HEAD_DIM, :] * scale

    n0 = i * (TQ // CMP_STRIDE)
    band0 = pl.multiple_of(n0 + CMP_PAD - 16, 8)

    def cmp_branch(nrows):
        row = lax.broadcasted_iota(jnp.int32, (nrows, QL), 0)
        live = (row >= CMP_PAD) & (row < band0 + CMP_BAND)
        sc_ref[0:nrows, :] = jnp.where(live, _dot(kc_ref[0, 0:nrows, :], qT), NEG)
        sc_ref[pl.ds(band0, CMP_BAND), :] += bc_ref[0]
        sc = sc_ref[0:nrows, :]
        mc = jnp.maximum(jnp.max(sc, axis=0, keepdims=True), M_FLOOR)
        pc = jnp.exp2(sc - mc)
        pc = pc * (1.0 / jnp.maximum(jnp.sum(pc, axis=0, keepdims=True), 1e-30))
        tot_ref[...] = _dot(vcT_ref[0, :, 0:nrows], pc.astype(BF16)) * gate_row(0)
        imp = pc[:, 0:TQ]
        for r in range(1, GQA):
            imp = imp + pc[:, r * TQ:(r + 1) * TQ]
        mt = mt_ref[:, 0:nrows]
        p_slc = jnp.zeros((N_SLC, TQ), F32)
        rem = imp
        for _ in range(2):
            piece = rem.astype(BF16)
            p_slc = p_slc + _dot(mt, piece)
            rem = rem - piece.astype(F32)
        pslc_ref[...] = p_slc

    prev_rows = 0
    for nrows in CMP_EXTENTS:
        lo, hi = prev_rows, nrows
        pl.when((band0 + CMP_BAND > lo) & (band0 + CMP_BAND <= hi))(functools.partial(cmp_branch, nrows))
        prev_rows = nrows

    p_slc = pslc_ref[...]
    blk = lax.broadcasted_iota(jnp.int32, (N_SLC, TQ), 0)
    cur = (s0 + lax.broadcasted_iota(jnp.int32, (N_SLC, TQ), 1)) // SLC_LEN
    valid = blk <= cur
    forced = valid & ((blk == 0) | (blk >= cur - (N_LOCAL - 1)))
    score = jnp.where(forced, BIG, jnp.where(valid, p_slc, -BIG))
    blk_f = blk.astype(F32)
    for _ in range(N_SEL):
        best = jnp.max(score, axis=0, keepdims=True)
        first = jnp.min(jnp.where(score == best, blk_f, float(N_SLC)), axis=0, keepdims=True)
        score = jnp.where(blk_f == first, -jnp.inf, score)
    neg = jnp.where(score == -jnp.inf, 0.0, NEG)
    neg_ref[...] = jnp.concatenate([neg] * GQA, axis=1)
    near_blk = (s0 - NEAR) // SLC_LEN
    negfar_ref[...] = jnp.concatenate([jnp.where(blk >= near_blk, NEG, neg)] * GQA, axis=1)

    def block_mask(ref, j0, nblk):
        return jnp.concatenate(
            [jnp.broadcast_to(ref[pl.ds(j0 + b, 1), :], (SLC_LEN, QL)) for b in range(nblk)], axis=0)

    reset()
    n_far = (s0 - NEAR + FAR_KEYS - 1) // FAR_KEYS

    sbufs = (sbuf0_ref, sbuf1_ref)

    def far_logits(c, slot):
        start = pl.multiple_of(c * FAR_KEYS, FAR_KEYS)
        mask_rows = negfar_ref[pl.ds(pl.multiple_of(c * FAR_BLOCKS, FAR_BLOCKS), FAR_BLOCKS), :]
        extra = jnp.concatenate([mask_rows, jnp.zeros((K_COLS - HEAD_DIM - FAR_BLOCKS, QL), F32)], axis=0)
        q_masked = jnp.concatenate([qT, extra.astype(BF16)], axis=0)
        s = _dot(ks_ref[0, pl.ds(start, FAR_KEYS), :], q_masked)
        sbufs[slot][...] = s
        mloc_ref[slot] = jnp.max(s, axis=0, keepdims=True)

    def far_consume(c, slot):
        start = pl.multiple_of(c * FAR_KEYS, FAR_KEYS)
        m_prev = m_ref[...]
        m_new = jnp.maximum(m_prev, mloc_ref[slot])
        alpha = jnp.exp2(m_prev - m_new)
        p = jnp.exp2(sbufs[slot][...] - m_new).astype(BF16)
        acc_ref[...] = alpha * acc_ref[...] + _dot(vsT_ref[0, :, pl.ds(start, FAR_KEYS)], p)
        m_ref[...] = m_new

    @pl.when(i >= 1)
    def _():
        n_pairs = (n_far + 1) // 2
        far_logits(0, 0)

        def far_body(p, carry):
            far_logits(2 * p + 1, 1)
            far_consume(2 * p, 0)
            far_logits(2 * p + 2, 0)
            far_consume(2 * p + 1, 1)
            return carry

        lax.fori_loop(0, n_pairs - 1, far_body, 0)
        last = 2 * (n_pairs - 1)
        far_logits(last + 1, 1)
        far_consume(last, 0)
        far_consume(last + 1, 1)
        attend(ks_ref, vsT_ref, pl.multiple_of(s0 - NEAR, NEAR), NEAR + TQ,
               bn_ref[0] + block_mask(neg_ref, near_blk, (NEAR + TQ) // SLC_LEN))

    @pl.when(i == 0)
    def _():
        attend(ks_ref, vsT_ref, 0, TQ, bn_ref[0, NEAR:NEAR + TQ, :] + block_mask(neg_ref, 0, TQ // SLC_LEN))

    finish(1)

    reset()

    @pl.when(s0 >= WINDOW)
    def _():
        attend(kw_ref, vwT_ref, pl.multiple_of(s0 - WINDOW, NEAR), WINDOW + TQ, bw_ref[0])

    @pl.when(s0 < WINDOW)
    def _():
        for c in range((WINDOW + TQ) // NEAR):
            def chunk(c=c):
                attend(kw_ref, vwT_ref, pl.multiple_of(s0 - WINDOW + c * NEAR, NEAR), NEAR,
                       bw_ref[0, c * NEAR:(c + 1) * NEAR, :])
            if c * NEAR >= WINDOW:
                chunk()
            else:
                pl.when(s0 - WINDOW + c * NEAR >= 0)(chunk)

    finish(2)

    tot = tot_ref[...]
    for r in range(GQA):
        oT_ref[r * HEAD_DIM:(r + 1) * HEAD_DIM, :] = tot[:, r * TQ:(r + 1) * TQ]


def _nsa(qT, kc, vcT, ks, vsT, kw, vwT, bias_near, bias_win, bias_cmp, mt, gnT):
    s = qT.shape[1]
    head3 = lambda g, i: (g, 0, 0)
    return pl.pallas_call(
        _nsa_kernel,
        grid=(NSA_KV_HEADS, s // TQ),
        in_specs=[
            pl.BlockSpec((GQA * HEAD_DIM, TQ), lambda g, i: (g, i)),
            pl.BlockSpec((1, N_CMP_PAD, HEAD_DIM), head3),
            pl.BlockSpec((1, HEAD_DIM, N_CMP_PAD), head3),
            pl.BlockSpec((1, s, K_COLS), head3),
            pl.BlockSpec((1, V_ROWS, s), head3),
            pl.BlockSpec((1, s, HEAD_DIM), head3),
            pl.BlockSpec((1, V_ROWS, s), head3),
            pl.BlockSpec((1, NEAR + TQ, QL), head3),
            pl.BlockSpec((1, WINDOW + TQ, QL), head3),
            pl.BlockSpec((1, CMP_BAND, QL), head3),
            pl.BlockSpec(mt.shape, lambda g, i: (0, 0)),
            pl.BlockSpec((16, TQ), lambda g, i: (g, i)),
        ],
        out_specs=pl.BlockSpec((GQA * HEAD_DIM, TQ), lambda g, i: (g, i)),
        out_shape=jax.ShapeDtypeStruct((NSA_WIDTH, s), F32),
        scratch_shapes=[
            pltpu.VMEM((N_CMP_PAD, QL), F32),
            pltpu.VMEM((N_SLC, QL), F32),
            pltpu.VMEM((N_SLC, QL), F32),
            pltpu.VMEM((1, QL), F32),
            pltpu.VMEM((V_ROWS, QL), F32),
            pltpu.VMEM((HEAD_DIM, QL), F32),
            pltpu.VMEM((FAR_KEYS, QL), F32),
            pltpu.VMEM((FAR_KEYS, QL), F32),
            pltpu.VMEM((2, 1, QL), F32),
            pltpu.VMEM((N_SLC, TQ), F32),
        ],
        compiler_params=pltpu.CompilerParams(
            dimension_semantics=("arbitrary", "arbitrary"), vmem_limit_bytes=VMEM_LIMIT),
        name="nsa",
    )(qT, kc, vcT, ks, vsT, kw, vwT, bias_near, bias_win, bias_cmp, mt, gnT)


def _t5_bucket(dist):
    n = jnp.maximum(dist, 0)
    max_exact = REL_BUCKETS // 2
    nf = jnp.maximum(n, 1).astype(F32)
    large = max_exact + (jnp.log(nf / max_exact) / math.log(REL_MAX_DIST / max_exact)
                         * (REL_BUCKETS - max_exact)).astype(jnp.int32)
    large = jnp.minimum(large, REL_BUCKETS - 1)
    return jnp.where(n < max_exact, n, large)


def _bias_tiles(rel_bias):
    tab = rel_bias.astype(F32)
    tab = (tab[_t5_bucket(jnp.arange(NEAR))] - tab[REL_BUCKETS - 1]).T * LOG2E
    tab = jnp.concatenate([tab, jnp.zeros((NSA_HEADS, 1), F32)], axis=1)

    def by_distance(d):
        return jnp.where(d >= 0, tab[:, jnp.clip(d, 0, NEAR)], NEG)

    def toeplitz(c, nk, nq, step=1):
        n = step * (nk - 1) + nq
        w = by_distance(jnp.arange(n) + c - step * (nk - 1))
        reps = -(-nk * (n + step) // n)
        return jnp.tile(w, (1, reps))[:, :nk * (n + step)].reshape(NSA_HEADS, nk, n + step)[:, ::-1, :nq]

    def per_kv_head(a):
        a = a.reshape(NSA_KV_HEADS, GQA, a.shape[1], TQ)
        return jnp.transpose(a, (0, 2, 1, 3)).reshape(NSA_KV_HEADS, a.shape[2], QL)

    nb = TQ // NEAR
    diag = {0: toeplitz(0, NEAR, NEAR), 1: toeplitz(NEAR, NEAR, NEAR)}
    zero = jnp.zeros((NSA_HEADS, NEAR, NEAR), F32)
    future = jnp.full((NSA_HEADS, NEAR, NEAR), NEG, F32)
    block = lambda d: diag.get(d, zero if d > 0 else future)
    near = jnp.concatenate(
        [jnp.concatenate([block(b - a + 1) for b in range(nb)], axis=2) for a in range(nb + 1)], axis=1)
    cmp_band = toeplitz(16 * CMP_STRIDE - (CMP_LEN - 1), CMP_BAND, TQ, step=CMP_STRIDE)
    kq, iq = np.arange(TQ)[:, None], np.arange(TQ)[None, :]
    oldest = jnp.broadcast_to(jnp.asarray(np.where(kq > iq, 0.0, NEG), F32), (NSA_HEADS, TQ, TQ))
    window = jnp.concatenate([oldest, jnp.zeros((NSA_HEADS, WINDOW - TQ - NEAR, TQ), F32), near], axis=1)
    return per_kv_head(near), per_kv_head(window), per_kv_head(cmp_band)


def _overlap_matrix():
    ratio = SLC_LEN // CMP_STRIDE
    front = CMP_LEN // CMP_STRIDE - 1
    w_ov = np.convolve(np.ones(ratio), np.ones(CMP_LEN // CMP_STRIDE))
    mt = np.zeros((N_SLC, N_CMP_PAD), np.float32)
    for j in range(N_SLC):
        for o, w in enumerate(w_ov):
            n = ratio * j + o - front
            if 0 <= n < N_CMP:
                mt[j, CMP_PAD + n] = w
    return jnp.asarray(mt, BF16)


def _tail_kernel(x_ref, ya_ref, oT_ref, gb_ref, wup_ref, wout_ref, g_ref, wg_ref, wu_ref, wd_ref, gf_ref, o_ref):
    yb = _dot(oT_ref[...].T.astype(BF16), wup_ref[...])
    mix = gb_ref[:, :D_MODEL].astype(F32) * ya_ref[...] + gb_ref[:, D_MODEL:].astype(F32) * yb
    x = x_ref[...] + _dot(mix.astype(BF16), wout_ref[...])
    h = _rms(x, g_ref[...]).astype(BF16)
    f = jax.nn.silu(_dot(h, wg_ref[...])) * _dot(h, wu_ref[...])
    x = x + _dot(f.astype(BF16), wd_ref[...])
    o_ref[...] = _rms(x, gf_ref[...])


def _tail(x2, ya, oT, gb, wup, wout, g, wg, wu, wd, gf, tm=256):
    s = x2.shape[0]
    row = lambda i: (i, 0)
    const = lambda i: (0, 0)
    return pl.pallas_call(
        _tail_kernel,
        grid=(s // tm,),
        in_specs=[
            pl.BlockSpec((tm, D_MODEL), row),
            pl.BlockSpec((tm, D_MODEL), row),
            pl.BlockSpec((NSA_WIDTH, tm), lambda i: (0, i)),
            pl.BlockSpec((tm, 2 * D_MODEL), row),
            pl.BlockSpec(wup.shape, const),
            pl.BlockSpec(wout.shape, const),
            pl.BlockSpec((1, D_MODEL), const),
            pl.BlockSpec(wg.shape, const),
            pl.BlockSpec(wu.shape, const),
            pl.BlockSpec(wd.shape, const),
            pl.BlockSpec((1, D_MODEL), const),
        ],
        out_specs=pl.BlockSpec((tm, D_MODEL), row),
        out_shape=jax.ShapeDtypeStruct((s, D_MODEL), F32),
        compiler_params=pltpu.CompilerParams(
            dimension_semantics=("arbitrary",), vmem_limit_bytes=VMEM_LIMIT),
        name="tail",
    )(x2, ya, oT, gb, wup, wout, g, wg, wu, wd, gf)


def kernel(x, norm_mix_g, w_in, ssm_a_re, ssm_a_im, ssm_log_dt, ssm_b_re, ssm_b_im, ssm_c_re, ssm_c_im, ssm_d, ssm_w_glu, w_up_ssm, cmp_pos_k, cmp_pos_v, cmp_w1_k, cmp_w2_k, cmp_w1_v, cmp_w2_v, rel_bias, w_up_nsa, w_out, norm_ffn_g, w_ffn_gate, w_ffn_up, w_ffn_down, norm_final_g):
    bsz, s, _ = x.shape
    assert (bsz, s) == (1, SEQ) and w_in.shape[0] == 1
    x2 = x.reshape(s, D_MODEL)
    l = 0
    w = w_in[l].astype(BF16)
    wm = w[:, :COL_GN]
    gates_per_head = GQA * 3
    wn = jnp.zeros((D_MODEL, 128), BF16)
    for g in range(NSA_KV_HEADS):
        wn = wn.at[:, 16 * g:16 * g + gates_per_head].set(
            w[:, COL_GN + g * gates_per_head:COL_GN + (g + 1) * gates_per_head])
    wb = w[:, COL_GB:]
    row = lambda v: v.astype(F32).reshape(1, -1)

    u, qT, kcr, vcr, ks, kw, vsT, vwT, gnT, gb = _inproj(x2, row(norm_mix_g[l]), wm, wn, wb)

    b_sg, c_sg, pw, seg = _s5_params(
        ssm_a_re[l], ssm_a_im[l], ssm_log_dt[l], ssm_b_re[l], ssm_b_im[l], ssm_c_re[l], ssm_c_im[l])
    ya = _s5(u, b_sg, c_sg, pw, seg, row(ssm_d[l]), ssm_w_glu[l].astype(BF16), w_up_ssm[l].astype(BF16))

    chunks = lambda a: a.reshape(NSA_KV_HEADS, N_CHUNK, CMP_STRIDE * HEAD_DIM)
    kc, vcT = _compress(chunks(kcr), chunks(vcr),
                        cmp_w1_k[l].astype(BF16), cmp_w2_k[l].astype(BF16), row(cmp_pos_k[l]),
                        cmp_w1_v[l].astype(BF16), cmp_w2_v[l].T.astype(BF16), row(cmp_pos_v[l]))

    bias_near, bias_win, bias_cmp = _bias_tiles(rel_bias)
    oT = _nsa(qT, kc, vcT, ks, vsT, kw, vwT, bias_near, bias_win, bias_cmp, _overlap_matrix(), gnT)

    out = _tail(x2, ya, oT, gb, w_up_nsa[l].astype(BF16), w_out[l].astype(BF16),
                row(norm_ffn_g[l]), w_ffn_gate[l].astype(BF16), w_ffn_up[l].astype(BF16),
                w_ffn_down[l].astype(BF16), row(norm_final_g))
    return out.reshape(bsz, s, D_MODEL)
```

```python
import functools
import math

import numpy as np
import jax
import jax.numpy as jnp
from jax import lax
from jax.experimental import pallas as pl
from jax.experimental.pallas import tpu as pltpu

F32 = jnp.float32
BF16 = jnp.bfloat16

D_MODEL = 1024
SEQ = 16384
EPS = 1e-6
SSM_WIDTH = 512
SSM_GROUP = 16
SSM_GROUPS = SSM_WIDTH // SSM_GROUP
SSM_STATE = 64
SSM_LANES = SSM_GROUPS * SSM_STATE
NSA_HEADS = 8
NSA_KV_HEADS = 2
GQA = NSA_HEADS // NSA_KV_HEADS
HEAD_DIM = 64
NSA_WIDTH = NSA_HEADS * HEAD_DIM
KV_WIDTH = NSA_KV_HEADS * HEAD_DIM
CMP_LEN = 32
CMP_STRIDE = 16
CMP_HIDDEN = 256
SLC_LEN = 64
N_SEL = 16
N_LOCAL = 2
WINDOW = 512
BIG = 1e4
REL_BUCKETS = 32
REL_MAX_DIST = 128
D_FF = 2816

N_CHUNK = SEQ // CMP_STRIDE
N_CMP = (SEQ - CMP_LEN) // CMP_STRIDE + 1
N_SLC = SEQ // SLC_LEN
TQ = 256
QL = GQA * TQ
NEAR = 128
assert TQ > NEAR and TQ % NEAR == 0
CMP_PAD = 16
CMP_ROW_STEP = 128
N_CMP_PAD = -(-(CMP_PAD + N_CHUNK) // CMP_ROW_STEP) * CMP_ROW_STEP
CMP_BAND = 16 + TQ // CMP_STRIDE
NEG = -1e30
M_FLOOR = -1e29
SCAN_ROWS = 8
S5_TT = 256
SEG = S5_TT // SCAN_ROWS
SSM_SUPER = 2
SG_CH = SSM_WIDTH // SSM_SUPER
SG_LANES = SSM_LANES // SSM_SUPER
LOG2E = math.log2(math.e)
V_ROWS = HEAD_DIM + 16
FAR_KEYS = 512
FAR_BLOCKS = FAR_KEYS // SLC_LEN
K_COLS = HEAD_DIM + 16
CMP_EXTENTS = tuple(range(2 * CMP_ROW_STEP, N_CMP_PAD + 1, CMP_ROW_STEP))

VMEM_LIMIT = 56 * 1024 * 1024

COL_U = 0
COL_Q = 512
COL_KC = 1024
COL_VC = 1152
COL_KS = 1280
COL_VS = 1408
COL_KW = 1536
COL_VW = 1664
COL_GN = 1792
COL_GB = 1816
COL_END = 3864


def _rms(x, g):
    return x * lax.rsqrt(jnp.mean(x * x, axis=-1, keepdims=True) + EPS) * g


def _dot(a, b):
    return jnp.dot(a, b, preferred_element_type=F32)


def _dot_nt(a, b):
    return lax.dot_general(a, b, (((1,), (1,)), ((), ())), preferred_element_type=F32)


def _inproj_kernel(x_ref, g_ref, wm_ref, wn_ref, wb_ref,
                   u_ref, qT_ref, kcr_ref, vcr_ref, ks_ref, kw_ref, vsT_ref, vwT_ref, gnT_ref, gb_ref):
    h = _rms(x_ref[...], g_ref[...]).astype(BF16)
    pm = _dot(h, wm_ref[...])
    u_ref[...] = pm[:, COL_U:COL_Q]
    qT_ref[...] = (pm[:, COL_Q:COL_KC] * (HEAD_DIM ** -0.5 * LOG2E)).T.astype(BF16)
    vsT = pm[:, COL_VS:COL_KW].T
    vwT = pm[:, COL_VW:COL_GN].T
    tm = vsT.shape[1]
    ones = jnp.ones((V_ROWS - HEAD_DIM, tm), F32)
    tok = pl.program_id(0) * tm + lax.broadcasted_iota(jnp.int32, (tm, K_COLS - HEAD_DIM), 0)
    col = lax.broadcasted_iota(jnp.int32, (tm, K_COLS - HEAD_DIM), 1)
    blk_onehot = jnp.where((tok // SLC_LEN) % FAR_BLOCKS == col, 1.0, 0.0)
    for g in range(NSA_KV_HEADS):
        lo = g * HEAD_DIM
        kcr_ref[g] = pm[:, COL_KC + lo:COL_KC + lo + HEAD_DIM]
        vcr_ref[g] = pm[:, COL_VC + lo:COL_VC + lo + HEAD_DIM]
        ks_ref[g] = jnp.concatenate(
            [pm[:, COL_KS + lo:COL_KS + lo + HEAD_DIM], blk_onehot], axis=1).astype(BF16)
        kw_ref[g] = pm[:, COL_KW + lo:COL_KW + lo + HEAD_DIM].astype(BF16)
        vsT_ref[g] = jnp.concatenate([vsT[lo:lo + HEAD_DIM], ones], axis=0).astype(BF16)
        vwT_ref[g] = jnp.concatenate([vwT[lo:lo + HEAD_DIM], ones], axis=0).astype(BF16)
    gn = jax.nn.sigmoid(_dot(h, wn_ref[...]))
    gnT_ref[...] = gn.T[:32, :]
    gb_ref[...] = jax.nn.sigmoid(_dot(h, wb_ref[...])).astype(BF16)


def _inproj(x2, g, wm, wn, wb, tm=512):
    s = x2.shape[0]
    const = lambda i: (0, 0)
    row = lambda i: (i, 0)
    col = lambda i: (0, i)
    return pl.pallas_call(
        _inproj_kernel,
        grid=(s // tm,),
        in_specs=[
            pl.BlockSpec((tm, D_MODEL), row),
            pl.BlockSpec((1, D_MODEL), const),
            pl.BlockSpec(wm.shape, const),
            pl.BlockSpec(wn.shape, const),
            pl.BlockSpec(wb.shape, const),
        ],
        out_specs=[
            pl.BlockSpec((tm, SSM_WIDTH), row),
            pl.BlockSpec((NSA_WIDTH, tm), col),
            pl.BlockSpec((NSA_KV_HEADS, tm, HEAD_DIM), lambda i: (0, i, 0)),
            pl.BlockSpec((NSA_KV_HEADS, tm, HEAD_DIM), lambda i: (0, i, 0)),
            pl.BlockSpec((NSA_KV_HEADS, tm, K_COLS), lambda i: (0, i, 0)),
            pl.BlockSpec((NSA_KV_HEADS, tm, HEAD_DIM), lambda i: (0, i, 0)),
            pl.BlockSpec((NSA_KV_HEADS, V_ROWS, tm), lambda i: (0, 0, i)),
            pl.BlockSpec((NSA_KV_HEADS, V_ROWS, tm), lambda i: (0, 0, i)),
            pl.BlockSpec((32, tm), col),
            pl.BlockSpec((tm, 2 * D_MODEL), row),
        ],
        out_shape=[
            jax.ShapeDtypeStruct((s, SSM_WIDTH), F32),
            jax.ShapeDtypeStruct((NSA_WIDTH, s), BF16),
            jax.ShapeDtypeStruct((NSA_KV_HEADS, s, HEAD_DIM), F32),
            jax.ShapeDtypeStruct((NSA_KV_HEADS, s, HEAD_DIM), F32),
            jax.ShapeDtypeStruct((NSA_KV_HEADS, s, K_COLS), BF16),
            jax.ShapeDtypeStruct((NSA_KV_HEADS, s, HEAD_DIM), BF16),
            jax.ShapeDtypeStruct((NSA_KV_HEADS, V_ROWS, s), BF16),
            jax.ShapeDtypeStruct((NSA_KV_HEADS, V_ROWS, s), BF16),
            jax.ShapeDtypeStruct((32, s), F32),
            jax.ShapeDtypeStruct((s, 2 * D_MODEL), BF16),
        ],
        compiler_params=pltpu.CompilerParams(
            dimension_semantics=("arbitrary",), vmem_limit_bytes=VMEM_LIMIT),
        name="inproj",
    )(x2, g, wm, wn, wb)


def _s5_kernel(u_ref, perm_ref, permT_ref, b_ref, c_ref, pw_ref, seg_ref, d_ref, wglu_ref, wup_ref,
               ya_ref, xre_ref, xim_ref, st_ref, cre_s, cim_s):
    @pl.when(pl.program_id(0) == 0)
    def _():
        cre_s[...] = jnp.zeros_like(cre_s)
        cim_s[...] = jnp.zeros_like(cim_s)

    u = u_ref[...]
    ub = _dot(perm_ref[...], u.astype(BF16)).astype(BF16)
    for sg in range(SSM_SUPER):
        bu = _dot(ub[:, sg * SG_CH:(sg + 1) * SG_CH], b_ref[sg])
        xre_ref[:, sg * SG_LANES:(sg + 1) * SG_LANES] = bu[:, :SG_LANES]
        xim_ref[:, sg * SG_LANES:(sg + 1) * SG_LANES] = bu[:, SG_LANES:]

    def cmul_add(re, im, are, aim, sre, sim):
        return re + are * sre - aim * sim, im + are * sim + aim * sre

    for sg in range(SSM_SUPER):
        lanes = slice(sg * SG_LANES, (sg + 1) * SG_LANES)

        lam_re, lam_im = pw_ref[0, 0:SCAN_ROWS, lanes], pw_ref[1, 0:SCAN_ROWS, lanes]

        def local(j, carry):
            r0 = pl.multiple_of(j * SCAN_ROWS, SCAN_ROWS)
            re, im = cmul_add(xre_ref[pl.ds(r0, SCAN_ROWS), lanes], xim_ref[pl.ds(r0, SCAN_ROWS), lanes],
                              lam_re, lam_im, *carry)
            xre_ref[pl.ds(r0, SCAN_ROWS), lanes] = re
            xim_ref[pl.ds(r0, SCAN_ROWS), lanes] = im
            return re, im

        zero = jnp.zeros((SCAN_ROWS, SG_LANES), F32)
        re, im = lax.fori_loop(0, SEG, local, (zero, zero), unroll=True)

        for k, shift in enumerate((1, 2, 4)):
            re, im = cmul_add(re, im, seg_ref[2 * k, :, lanes], seg_ref[2 * k + 1, :, lanes],
                              pltpu.roll(re, shift, 0), pltpu.roll(im, shift, 0))
        cin_re, cin_im = cre_s[:, lanes], cim_s[:, lanes]
        re, im = cmul_add(re, im, seg_ref[6, :, lanes], seg_ref[7, :, lanes], cin_re, cin_im)
        cre_s[:, lanes] = re[SCAN_ROWS - 1:SCAN_ROWS, :]
        cim_s[:, lanes] = im[SCAN_ROWS - 1:SCAN_ROWS, :]
        first = lax.broadcasted_iota(jnp.int32, (SCAN_ROWS, SG_LANES), 0) == 0
        start_re = jnp.where(first, cin_re, pltpu.roll(re, 1, 0))
        start_im = jnp.where(first, cin_im, pltpu.roll(im, 1, 0))
        start_re = jnp.concatenate([start_re, start_re], axis=0)
        start_im = jnp.concatenate([start_im, start_im], axis=0)

        def fix(jj, carry):
            r0 = pl.multiple_of(jj * 2 * SCAN_ROWS, 2 * SCAN_ROWS)
            rows = pl.ds(r0, 2 * SCAN_ROWS)
            re, im = cmul_add(xre_ref[rows, lanes], xim_ref[rows, lanes],
                              pw_ref[0, rows, lanes], pw_ref[1, rows, lanes], start_re, start_im)
            st_ref[rows, 2 * sg * SG_LANES:(2 * sg + 1) * SG_LANES] = re.astype(BF16)
            st_ref[rows, (2 * sg + 1) * SG_LANES:(2 * sg + 2) * SG_LANES] = im.astype(BF16)
            return carry

        lax.fori_loop(0, SEG // 2, fix, 0, unroll=True)
    y_perm = jnp.concatenate(
        [_dot(st_ref[:, 2 * sg * SG_LANES:(2 * sg + 2) * SG_LANES], c_ref[sg])
         for sg in range(SSM_SUPER)], axis=1)
    y_hi = y_perm.astype(BF16)
    y_lo = (y_perm - y_hi.astype(F32)).astype(BF16)
    y = _dot(permT_ref[...], y_hi) + _dot(permT_ref[...], y_lo) + d_ref[...] * u
    z = jax.nn.gelu(y)
    z = z * jax.nn.sigmoid(_dot(z.astype(BF16), wglu_ref[...]))
    ya_ref[...] = _dot(z.astype(BF16), wup_ref[...])


def _s5(u, b, c, pw, seg, d, wglu, wup):
    s = u.shape[0]
    t = np.arange(S5_TT)
    perm = np.zeros((S5_TT, S5_TT), np.float32)
    perm[SCAN_ROWS * (t % SEG) + t // SEG, t] = 1.0
    permT = jnp.asarray(perm.T, BF16)
    perm = jnp.asarray(perm, BF16)
    const2 = lambda i: (0, 0)
    const3 = lambda i: (0, 0, 0)
    return pl.pallas_call(
        _s5_kernel,
        grid=(s // S5_TT,),
        in_specs=[
            pl.BlockSpec((S5_TT, SSM_WIDTH), lambda i: (i, 0)),
            pl.BlockSpec(perm.shape, const2),
            pl.BlockSpec(permT.shape, const2),
            pl.BlockSpec(b.shape, const3),
            pl.BlockSpec(c.shape, const3),
            pl.BlockSpec(pw.shape, const3),
            pl.BlockSpec(seg.shape, const3),
            pl.BlockSpec((1, SSM_WIDTH), const2),
            pl.BlockSpec(wglu.shape, const2),
            pl.BlockSpec(wup.shape, const2),
        ],
        out_specs=pl.BlockSpec((S5_TT, D_MODEL), lambda i: (i, 0)),
        out_shape=jax.ShapeDtypeStruct((s, D_MODEL), F32),
        scratch_shapes=[
            pltpu.VMEM((S5_TT, SSM_LANES), F32),
            pltpu.VMEM((S5_TT, SSM_LANES), F32),
            pltpu.VMEM((S5_TT, 2 * SSM_LANES), BF16),
            pltpu.VMEM((1, SSM_LANES), F32),
            pltpu.VMEM((1, SSM_LANES), F32),
        ],
        compiler_params=pltpu.CompilerParams(
            dimension_semantics=("arbitrary",), vmem_limit_bytes=VMEM_LIMIT),
        name="s5",
    )(u, perm, permT, b, c, pw, seg, d, wglu, wup)


def _s5_params(a_re, a_im, log_dt, b_re, b_im, c_re, c_im):
    dt = jnp.exp(log_dt.astype(F32))[:, None]
    ar, ai = a_re.astype(F32), a_im.astype(F32)
    zr, zi = ar * dt, ai * dt

    def power(n):
        mag = jnp.exp(n * zr)
        return mag * jnp.cos(n * zi), mag * jnp.sin(n * zi)

    lr, li = power(1.0)
    den = ar * ar + ai * ai
    kr = ((lr - 1.0) * ar + li * ai) / den
    ki = (li * ar - (lr - 1.0) * ai) / den
    br, bi = b_re.astype(F32), b_im.astype(F32)
    bbr = kr[..., None] * br - ki[..., None] * bi
    bbi = kr[..., None] * bi + ki[..., None] * br
    eye = jnp.eye(SSM_GROUPS, dtype=F32)
    bd_in = lambda b: jnp.einsum("gpc,gh->gchp", b, eye).reshape(SSM_WIDTH, SSM_LANES)
    bd_out = lambda c: jnp.einsum("gcp,gh->gphc", c, eye).reshape(SSM_LANES, SSM_WIDTH)
    zr, zi = zr.reshape(1, SSM_LANES), zi.reshape(1, SSM_LANES)
    pw = jnp.stack(power(jnp.repeat(jnp.arange(1, SEG + 1, dtype=F32), SCAN_ROWS)[:, None]))
    row = jnp.arange(SCAN_ROWS)[:, None]
    seg = []
    for shift in (1, 2, 4):
        pr, pi = power(float(SEG * shift))
        seg += [jnp.where(row >= shift, pr, 0.0), jnp.where(row >= shift, pi, 0.0)]
    seg += list(power(SEG * (row + 1).astype(F32)))
    b_re_bd, b_im_bd = bd_in(bbr), bd_in(bbi)
    c_re_bd, c_im_bd = bd_out(c_re.astype(F32)), bd_out(c_im.astype(F32))
    b_sg, c_sg = [], []
    for sg in range(SSM_SUPER):
        ch = slice(sg * SG_CH, (sg + 1) * SG_CH)
        st = slice(sg * SG_LANES, (sg + 1) * SG_LANES)
        b_sg.append(jnp.concatenate([b_re_bd[ch, st], b_im_bd[ch, st]], axis=1))
        c_sg.append(jnp.concatenate([c_re_bd[st, ch], -c_im_bd[st, ch]], axis=0))
    return jnp.stack(b_sg).astype(BF16), jnp.stack(c_sg).astype(BF16), pw, jnp.stack(seg)


def _compress_kernel(xk_ref, xv_ref, w1k_ref, w2k_ref, pk_ref, w1v_ref, w2vT_ref, pv_ref, kc_ref, vcT_ref):
    half = CMP_STRIDE * HEAD_DIM

    def hidden(x_ref, w1_ref, pos_ref):
        x = x_ref[0].astype(BF16)
        first = _dot(x, w1_ref[:half, :])
        second = _dot(x, w1_ref[half:, :])
        bias = _dot(jnp.broadcast_to(pos_ref[...], (8, 2 * half)).astype(BF16), w1_ref[...])[:1]
        pre = first + pltpu.roll(second, N_CHUNK - 1, 0) + bias
        return jax.nn.gelu(pre).astype(BF16)

    hk = hidden(xk_ref, w1k_ref, pk_ref)
    kc = _dot(hk, w2k_ref[...])
    tail_pad = N_CMP_PAD - CMP_PAD - N_CHUNK
    kc_ref[0] = jnp.concatenate([jnp.zeros((CMP_PAD, HEAD_DIM), F32), kc,
                                 jnp.zeros((tail_pad, HEAD_DIM), F32)], axis=0).astype(BF16)
    hv = hidden(xv_ref, w1v_ref, pv_ref)
    vcT = _dot_nt(w2vT_ref[...], hv)
    vcT_ref[0] = jnp.concatenate([jnp.zeros((HEAD_DIM, CMP_PAD), F32), vcT,
                                  jnp.zeros((HEAD_DIM, tail_pad), F32)], axis=1).astype(BF16)


def _compress(xk, xv, w1k, w2k, pk, w1v, w2vT, pv):
    const2 = lambda g: (0, 0)
    head = lambda g: (g, 0, 0)
    return pl.pallas_call(
        _compress_kernel,
        grid=(NSA_KV_HEADS,),
        in_specs=[
            pl.BlockSpec((1, N_CHUNK, CMP_STRIDE * HEAD_DIM), head),
            pl.BlockSpec((1, N_CHUNK, CMP_STRIDE * HEAD_DIM), head),
            pl.BlockSpec(w1k.shape, const2),
            pl.BlockSpec(w2k.shape, const2),
            pl.BlockSpec(pk.shape, const2),
            pl.BlockSpec(w1v.shape, const2),
            pl.BlockSpec(w2vT.shape, const2),
            pl.BlockSpec(pv.shape, const2),
        ],
        out_specs=[
            pl.BlockSpec((1, N_CMP_PAD, HEAD_DIM), head),
            pl.BlockSpec((1, HEAD_DIM, N_CMP_PAD), head),
        ],
        out_shape=[
            jax.ShapeDtypeStruct((NSA_KV_HEADS, N_CMP_PAD, HEAD_DIM), BF16),
            jax.ShapeDtypeStruct((NSA_KV_HEADS, HEAD_DIM, N_CMP_PAD), BF16),
        ],
        compiler_params=pltpu.CompilerParams(
            dimension_semantics=("arbitrary",), vmem_limit_bytes=VMEM_LIMIT),
        name="compress",
    )(xk, xv, w1k, w2k, pk, w1v, w2vT, pv)


def _nsa_kernel(qT_ref, kc_ref, vcT_ref, ks_ref, vsT_ref, kw_ref, vwT_ref, bn_ref, bw_ref, bc_ref, mt_ref, gT_ref,
                oT_ref, sc_ref, neg_ref, negfar_ref, m_ref, acc_ref, tot_ref, sbuf0_ref, sbuf1_ref, mloc_ref, pslc_ref):
    i = pl.program_id(1)
    s0 = i * TQ
    qT = jnp.concatenate([qT_ref[r * HEAD_DIM:(r + 1) * HEAD_DIM, :] for r in range(GQA)], axis=1)

    qT_nomask = jnp.concatenate([qT, jnp.zeros((K_COLS - HEAD_DIM, QL), BF16)], axis=0)

    def gate_row(branch):
        return jnp.concatenate([gT_ref[r * 3 + branch:r * 3 + branch + 1, :] for r in range(GQA)], axis=1)

    def reset():
        m_ref[...] = jnp.full_like(m_ref, M_FLOOR)
        acc_ref[...] = jnp.zeros_like(acc_ref)

    def attend(k_ref, vT_ref, start, size, add):
        k = k_ref[0, pl.ds(start, size), :]
        s = _dot(k, qT if k.shape[1] == HEAD_DIM else qT_nomask) + add
        m_prev = m_ref[...]
        m_new = jnp.maximum(m_prev, jnp.max(s, axis=0, keepdims=True))
        alpha = jnp.exp2(m_prev - m_new)
        p = jnp.exp2(s - m_new).astype(BF16)
        acc_ref[...] = alpha * acc_ref[...] + _dot(vT_ref[0, :, pl.ds(start, size)], p)
        m_ref[...] = m_new

    def finish(branch):
        acc = acc_ref[...]
        scale = gate_row(branch) / jnp.maximum(acc[HEAD_DIM:HEAD_DIM + 1, :], 1e-30)
        tot_ref[...] += acc[:HEAD_DIM, :] * scale

    n0 = i * (TQ // CMP_STRIDE)
    band0 = pl.multiple_of(n0 + CMP_PAD - 16, 8)

    def cmp_branch(nrows):
        row = lax.broadcasted_iota(jnp.int32, (nrows, QL), 0)
        live = (row >= CMP_PAD) & (row < band0 + CMP_BAND)
        sc_ref[0:nrows, :] = jnp.where(live, _dot(kc_ref[0, 0:nrows, :], qT), NEG)
        sc_ref[pl.ds(band0, CMP_BAND), :] += bc_ref[0]
        sc = sc_ref[0:nrows, :]
        mc = jnp.maximum(jnp.max(sc, axis=0, keepdims=True), M_FLOOR)
        pc = jnp.exp2(sc - mc)
        pc = pc * (1.0 / jnp.maximum(jnp.sum(pc, axis=0, keepdims=True), 1e-30))
        tot_ref[...] = _dot(vcT_ref[0, :, 0:nrows], pc.astype(BF16)) * gate_row(0)
        imp = pc[:, 0:TQ]
        for r in range(1, GQA):
            imp = imp + pc[:, r * TQ:(r + 1) * TQ]
        mt = mt_ref[:, 0:nrows]
        p_slc = jnp.zeros((N_SLC, TQ), F32)
        rem = imp
        for _ in range(2):
            piece = rem.astype(BF16)
            p_slc = p_slc + _dot(mt, piece)
            rem = rem - piece.astype(F32)
        pslc_ref[...] = p_slc

    prev_rows = 0
    for nrows in CMP_EXTENTS:
        lo, hi = prev_rows, nrows
        pl.when((band0 + CMP_BAND > lo) & (band0 + CMP_BAND <= hi))(functools.partial(cmp_branch, nrows))
        prev_rows = nrows

    reset()
    attend(kw_ref, vwT_ref, pl.multiple_of(jnp.maximum(s0 - WINDOW, 0), NEAR), WINDOW + TQ, bw_ref[0, 0])
    finish(2)

    p_slc = pslc_ref[...]
    blk = lax.broadcasted_iota(jnp.int32, (N_SLC, TQ), 0)
    cur = (s0 + lax.broadcasted_iota(jnp.int32, (N_SLC, TQ), 1)) // SLC_LEN
    valid = blk <= cur
    forced = valid & ((blk == 0) | (blk >= cur - (N_LOCAL - 1)))
    score = jnp.where(forced, -jnp.inf, jnp.where(valid, p_slc, -BIG))
    blk_f = blk.astype(F32)
    for _ in range(N_SEL - (N_LOCAL + 1)):
        best = jnp.max(score, axis=0, keepdims=True)
        first = jnp.min(jnp.where(score == best, blk_f, float(N_SLC)), axis=0, keepdims=True)
        score = jnp.where(blk_f == first, -jnp.inf, score)
    neg = jnp.where(score == -jnp.inf, 0.0, NEG)
    neg_ref[...] = jnp.concatenate([neg] * GQA, axis=1)
    near_blk = (s0 - NEAR) // SLC_LEN
    negfar_ref[...] = jnp.concatenate([jnp.where(blk >= near_blk, NEG, neg)] * GQA, axis=1)

    def block_mask(ref, j0, nblk):
        return jnp.concatenate(
            [jnp.broadcast_to(ref[pl.ds(j0 + b, 1), :], (SLC_LEN, QL)) for b in range(nblk)], axis=0)

    reset()
    n_far = (s0 - NEAR + FAR_KEYS - 1) // FAR_KEYS

    sbufs = (sbuf0_ref, sbuf1_ref)

    def far_logits(c, slot):
        start = pl.multiple_of(c * FAR_KEYS, FAR_KEYS)
        mask_rows = negfar_ref[pl.ds(pl.multiple_of(c * FAR_BLOCKS, FAR_BLOCKS), FAR_BLOCKS), :]
        extra = jnp.concatenate([mask_rows, jnp.zeros((K_COLS - HEAD_DIM - FAR_BLOCKS, QL), F32)], axis=0)
        q_masked = jnp.concatenate([qT, extra.astype(BF16)], axis=0)
        s = _dot(ks_ref[0, pl.ds(start, FAR_KEYS), :], q_masked)
        sbufs[slot][...] = s
        mloc_ref[slot] = jnp.max(s, axis=0, keepdims=True)

    def far_consume(c, slot):
        start = pl.multiple_of(c * FAR_KEYS, FAR_KEYS)
        m_prev = m_ref[...]
        m_new = jnp.maximum(m_prev, mloc_ref[slot])
        alpha = jnp.exp2(m_prev - m_new)
        p = jnp.exp2(sbufs[slot][...] - m_new).astype(BF16)
        acc_ref[...] = alpha * acc_ref[...] + _dot(vsT_ref[0, :, pl.ds(start, FAR_KEYS)], p)
        m_ref[...] = m_new

    @pl.when(i >= 1)
    def _():
        n_pairs = (n_far + 1) // 2
        far_logits(0, 0)
        attend(ks_ref, vsT_ref, pl.multiple_of(s0 - NEAR, NEAR), NEAR + TQ,
               bn_ref[0] + block_mask(neg_ref, near_blk, (NEAR + TQ) // SLC_LEN))

        def far_body(p, carry):
            far_logits(2 * p + 1, 1)
            far_consume(2 * p, 0)
            far_logits(2 * p + 2, 0)
            far_consume(2 * p + 1, 1)
            return carry

        lax.fori_loop(0, n_pairs - 1, far_body, 0)
        last = 2 * (n_pairs - 1)

        @pl.when(n_far % 2 == 0)
        def _():
            far_logits(last + 1, 1)
            far_consume(last, 0)
            far_consume(last + 1, 1)

        @pl.when(n_far % 2 == 1)
        def _():
            far_consume(last, 0)

    @pl.when(i == 0)
    def _():
        attend(ks_ref, vsT_ref, 0, TQ, bn_ref[0, NEAR:NEAR + TQ, :] + block_mask(neg_ref, 0, TQ // SLC_LEN))

    finish(1)

    tot = tot_ref[...]
    for r in range(GQA):
        oT_ref[r * HEAD_DIM:(r + 1) * HEAD_DIM, :] = tot[:, r * TQ:(r + 1) * TQ]


def _nsa(qT, kc, vcT, ks, vsT, kw, vwT, bias_near, bias_win, bias_cmp, mt, gnT):
    s = qT.shape[1]
    head3 = lambda g, i: (g, 0, 0)
    return pl.pallas_call(
        _nsa_kernel,
        grid=(NSA_KV_HEADS, s // TQ),
        in_specs=[
            pl.BlockSpec((GQA * HEAD_DIM, TQ), lambda g, i: (g, i)),
            pl.BlockSpec((1, N_CMP_PAD, HEAD_DIM), head3),
            pl.BlockSpec((1, HEAD_DIM, N_CMP_PAD), head3),
            pl.BlockSpec((1, s, K_COLS), head3),
            pl.BlockSpec((1, V_ROWS, s), head3),
            pl.BlockSpec((1, s, HEAD_DIM), head3),
            pl.BlockSpec((1, V_ROWS, s), head3),
            pl.BlockSpec((1, NEAR + TQ, QL), head3),
            pl.BlockSpec((1, 1, WINDOW + TQ, QL), lambda g, i: (g, jnp.minimum(i, WINDOW // TQ), 0, 0)),
            pl.BlockSpec((1, CMP_BAND, QL), head3),
            pl.BlockSpec(mt.shape, lambda g, i: (0, 0)),
            pl.BlockSpec((16, TQ), lambda g, i: (g, i)),
        ],
        out_specs=pl.BlockSpec((GQA * HEAD_DIM, TQ), lambda g, i: (g, i)),
        out_shape=jax.ShapeDtypeStruct((NSA_WIDTH, s), F32),
        scratch_shapes=[
            pltpu.VMEM((N_CMP_PAD, QL), F32),
            pltpu.VMEM((N_SLC, QL), F32),
            pltpu.VMEM((N_SLC, QL), F32),
            pltpu.VMEM((1, QL), F32),
            pltpu.VMEM((V_ROWS, QL), F32),
            pltpu.VMEM((HEAD_DIM, QL), F32),
            pltpu.VMEM((FAR_KEYS, QL), F32),
            pltpu.VMEM((FAR_KEYS, QL), F32),
            pltpu.VMEM((2, 1, QL), F32),
            pltpu.VMEM((N_SLC, TQ), F32),
        ],
        compiler_params=pltpu.CompilerParams(
            dimension_semantics=("arbitrary", "arbitrary"), vmem_limit_bytes=VMEM_LIMIT),
        name="nsa",
    )(qT, kc, vcT, ks, vsT, kw, vwT, bias_near, bias_win, bias_cmp, mt, gnT)


def _t5_bucket(dist):
    n = jnp.maximum(dist, 0)
    max_exact = REL_BUCKETS // 2
    nf = jnp.maximum(n, 1).astype(F32)
    large = max_exact + (jnp.log(nf / max_exact) / math.log(REL_MAX_DIST / max_exact)
                         * (REL_BUCKETS - max_exact)).astype(jnp.int32)
    large = jnp.minimum(large, REL_BUCKETS - 1)
    return jnp.where(n < max_exact, n, large)


def _bias_tiles(rel_bias):
    tab = rel_bias.astype(F32)
    tab = (tab[_t5_bucket(jnp.arange(NEAR))] - tab[REL_BUCKETS - 1]).T * LOG2E
    tab = jnp.concatenate([tab, jnp.zeros((NSA_HEADS, 1), F32)], axis=1)

    def by_distance(d):
        return jnp.where(d >= 0, tab[:, jnp.clip(d, 0, NEAR)], NEG)

    def toeplitz(c, nk, nq, step=1):
        n = step * (nk - 1) + nq
        w = by_distance(jnp.arange(n) + c - step * (nk - 1))
        reps = -(-nk * (n + step) // n)
        return jnp.tile(w, (1, reps))[:, :nk * (n + step)].reshape(NSA_HEADS, nk, n + step)[:, ::-1, :nq]

    def per_kv_head(a):
        a = a.reshape(NSA_KV_HEADS, GQA, a.shape[1], TQ)
        return jnp.transpose(a, (0, 2, 1, 3)).reshape(NSA_KV_HEADS, a.shape[2], QL)

    k2, q2 = np.arange(NEAR)[:, None], np.arange(NEAR)[None, :]
    const = lambda a: jnp.broadcast_to(jnp.asarray(a, F32), (NSA_HEADS, NEAR, NEAR))
    edge = WINDOW // NEAR

    def block(d):
        if d < 0:
            return const(NEG)
        if d < 2:
            return toeplitz(d * NEAR, NEAR, NEAR)
        if d < edge:
            return const(0.0)
        return const(np.where(k2 > q2, 0.0, NEG) if d == edge else NEG)

    def tile(first_d, n_key_blocks):
        return jnp.concatenate(
            [jnp.concatenate([block(b - a + first_d) for b in range(TQ // NEAR)], axis=2)
             for a in range(n_key_blocks)], axis=1)

    near = tile(1, (NEAR + TQ) // NEAR)
    window = jnp.stack([tile(j * TQ // NEAR, (WINDOW + TQ) // NEAR) for j in range(WINDOW // TQ + 1)], axis=1)
    window = jnp.stack([per_kv_head(window[:, j]) for j in range(window.shape[1])], axis=1)
    cmp_band = toeplitz(16 * CMP_STRIDE - (CMP_LEN - 1), CMP_BAND, TQ, step=CMP_STRIDE)
    return per_kv_head(near), window, per_kv_head(cmp_band)


def _overlap_matrix():
    ratio = SLC_LEN // CMP_STRIDE
    front = CMP_LEN // CMP_STRIDE - 1
    w_ov = np.convolve(np.ones(ratio), np.ones(CMP_LEN // CMP_STRIDE))
    mt = np.zeros((N_SLC, N_CMP_PAD), np.float32)
    for j in range(N_SLC):
        for o, w in enumerate(w_ov):
            n = ratio * j + o - front
            if 0 <= n < N_CMP:
                mt[j, CMP_PAD + n] = w
    return jnp.asarray(mt, BF16)


def _tail_kernel(x_ref, ya_ref, oT_ref, gb_ref, wup_ref, wout_ref, g_ref, wg_ref, wu_ref, wd_ref, gf_ref, o_ref):
    yb = _dot(oT_ref[...].T.astype(BF16), wup_ref[...])
    mix = gb_ref[:, :D_MODEL].astype(F32) * ya_ref[...] + gb_ref[:, D_MODEL:].astype(F32) * yb
    x = x_ref[...] + _dot(mix.astype(BF16), wout_ref[...])
    h = _rms(x, g_ref[...]).astype(BF16)
    f = jax.nn.silu(_dot(h, wg_ref[...])) * _dot(h, wu_ref[...])
    x = x + _dot(f.astype(BF16), wd_ref[...])
    o_ref[...] = _rms(x, gf_ref[...])


def _tail(x2, ya, oT, gb, wup, wout, g, wg, wu, wd, gf, tm=256):
    s = x2.shape[0]
    row = lambda i: (i, 0)
    const = lambda i: (0, 0)
    return pl.pallas_call(
        _tail_kernel,
        grid=(s // tm,),
        in_specs=[
            pl.BlockSpec((tm, D_MODEL), row),
            pl.BlockSpec((tm, D_MODEL), row),
            pl.BlockSpec((NSA_WIDTH, tm), lambda i: (0, i)),
            pl.BlockSpec((tm, 2 * D_MODEL), row),
            pl.BlockSpec(wup.shape, const),
            pl.BlockSpec(wout.shape, const),
            pl.BlockSpec((1, D_MODEL), const),
            pl.BlockSpec(wg.shape, const),
            pl.BlockSpec(wu.shape, const),
            pl.BlockSpec(wd.shape, const),
            pl.BlockSpec((1, D_MODEL), const),
        ],
        out_specs=pl.BlockSpec((tm, D_MODEL), row),
        out_shape=jax.ShapeDtypeStruct((s, D_MODEL), F32),
        compiler_params=pltpu.CompilerParams(
            dimension_semantics=("arbitrary",), vmem_limit_bytes=VMEM_LIMIT),
        name="tail",
    )(x2, ya, oT, gb, wup, wout, g, wg, wu, wd, gf)


def kernel(x, norm_mix_g, w_in, ssm_a_re, ssm_a_im, ssm_log_dt, ssm_b_re, ssm_b_im, ssm_c_re, ssm_c_im, ssm_d, ssm_w_glu, w_up_ssm, cmp_pos_k, cmp_pos_v, cmp_w1_k, cmp_w2_k, cmp_w1_v, cmp_w2_v, rel_bias, w_up_nsa, w_out, norm_ffn_g, w_ffn_gate, w_ffn_up, w_ffn_down, norm_final_g):
    bsz, s, _ = x.shape
    assert (bsz, s) == (1, SEQ) and w_in.shape[0] == 1
    x2 = x.reshape(s, D_MODEL)
    l = 0
    w = w_in[l].astype(BF16)
    wm = w[:, :COL_GN]
    gates_per_head = GQA * 3
    wn = jnp.zeros((D_MODEL, 128), BF16)
    for g in range(NSA_KV_HEADS):
        wn = wn.at[:, 16 * g:16 * g + gates_per_head].set(
            w[:, COL_GN + g * gates_per_head:COL_GN + (g + 1) * gates_per_head])
    wb = w[:, COL_GB:]
    row = lambda v: v.astype(F32).reshape(1, -1)

    u, qT, kcr, vcr, ks, kw, vsT, vwT, gnT, gb = _inproj(x2, row(norm_mix_g[l]), wm, wn, wb)

    b_sg, c_sg, pw, seg = _s5_params(
        ssm_a_re[l], ssm_a_im[l], ssm_log_dt[l], ssm_b_re[l], ssm_b_im[l], ssm_c_re[l], ssm_c_im[l])
    ya = _s5(u, b_sg, c_sg, pw, seg, row(ssm_d[l]), ssm_w_glu[l].astype(BF16), w_up_ssm[l].astype(BF16))

    chunks = lambda a: a.reshape(NSA_KV_HEADS, N_CHUNK, CMP_STRIDE * HEAD_DIM)
    kc, vcT = _compress(chunks(kcr), chunks(vcr),
                        cmp_w1_k[l].astype(BF16), cmp_w2_k[l].astype(BF16), row(cmp_pos_k[l]),
                        cmp_w1_v[l].astype(BF16), cmp_w2_v[l].T.astype(BF16), row(cmp_pos_v[l]))

    bias_near, bias_win, bias_cmp = _bias_tiles(rel_bias)
    oT = _nsa(qT, kc, vcT, ks, vsT, kw, vwT, bias_near, bias_win, bias_cmp, _overlap_matrix(), gnT)

    out = _tail(x2, ya, oT, gb, w_up_nsa[l].astype(BF16), w_out[l].astype(BF16),
                row(norm_ffn_g[l]), w_ffn_gate[l].astype(BF16), w_ffn_up[l].astype(BF16),
                w_ffn_down[l].astype(BF16), row(norm_final_g))
    return out.reshape(bsz, s, D_MODEL)
```

```python
import functools
import math

import numpy as np
import jax
import jax.numpy as jnp
from jax import lax
from jax.experimental import pallas as pl
from jax.experimental.pallas import tpu as pltpu

F32 = jnp.float32
BF16 = jnp.bfloat16

D_MODEL = 1024
SEQ = 16384
EPS = 1e-6
SSM_WIDTH = 512
SSM_GROUP = 16
SSM_GROUPS = SSM_WIDTH // SSM_GROUP
SSM_STATE = 64
SSM_LANES = SSM_GROUPS * SSM_STATE
NSA_HEADS = 8
NSA_KV_HEADS = 2
GQA = NSA_HEADS // NSA_KV_HEADS
HEAD_DIM = 64
NSA_WIDTH = NSA_HEADS * HEAD_DIM
KV_WIDTH = NSA_KV_HEADS * HEAD_DIM
CMP_LEN = 32
CMP_STRIDE = 16
CMP_HIDDEN = 256
SLC_LEN = 64
N_SEL = 16
N_LOCAL = 2
WINDOW = 512
BIG = 1e4
REL_BUCKETS = 32
REL_MAX_DIST = 128
D_FF = 2816

N_CHUNK = SEQ // CMP_STRIDE
N_CMP = (SEQ - CMP_LEN) // CMP_STRIDE + 1
N_SLC = SEQ // SLC_LEN
TQ = 256
QL = GQA * TQ
NEAR = 128
assert TQ > NEAR and TQ % NEAR == 0
CMP_PAD = 16
CMP_ROW_STEP = 128
N_CMP_PAD = -(-(CMP_PAD + N_CHUNK) // CMP_ROW_STEP) * CMP_ROW_STEP
CMP_BAND = 16 + TQ // CMP_STRIDE
NEG = -1e30
M_FLOOR = -1e29
SCAN_ROWS = 8
S5_TT = 256
SEG = S5_TT // SCAN_ROWS
SSM_SUPER = 2
SG_CH = SSM_WIDTH // SSM_SUPER
SG_LANES = SSM_LANES // SSM_SUPER
LOG2E = math.log2(math.e)
V_ROWS = HEAD_DIM + 16
FAR_KEYS = 512
FAR_BLOCKS = FAR_KEYS // SLC_LEN
K_COLS = HEAD_DIM + 16
CMP_EXTENTS = tuple(range(2 * CMP_ROW_STEP, N_CMP_PAD + 1, CMP_ROW_STEP))

VMEM_LIMIT = 56 * 1024 * 1024

COL_U = 0
COL_Q = 512
COL_KC = 1024
COL_VC = 1152
COL_KS = 1280
COL_VS = 1408
COL_KW = 1536
COL_VW = 1664
COL_GN = 1792
COL_GB = 1816
COL_END = 3864
GATE_COLS = 128
GATE_ROWS = 32
GATES_PER_KV_HEAD = GQA * 3


def _rms(x, g):
    return x * lax.rsqrt(jnp.mean(x * x, axis=-1, keepdims=True) + EPS) * g


def _dot(a, b):
    return jnp.dot(a, b, preferred_element_type=F32)


def _dot_nt(a, b):
    return lax.dot_general(a, b, (((1,), (1,)), ((), ())), preferred_element_type=F32)


def _inproj_kernel(x_ref, g_ref, w_ref,
                   u_ref, qT_ref, kcr_ref, vcr_ref, ks_ref, kw_ref, vsT_ref, vwT_ref, gnT_ref, gb_ref,
                   wm_ref, wn_ref, wb_ref):
    @pl.when(pl.program_id(0) == 0)
    def _():
        wm_ref[...] = w_ref[:, :COL_GN].astype(BF16)
        wn_ref[...] = w_ref[:, COL_GN:COL_GN + GATE_COLS].astype(BF16)
        wb_ref[...] = w_ref[:, COL_GB:].astype(BF16)

    h = _rms(x_ref[...], g_ref[...]).astype(BF16)
    pm = _dot(h, wm_ref[...])
    u_ref[...] = pm[:, COL_U:COL_Q]
    qT_ref[...] = (pm[:, COL_Q:COL_KC] * (HEAD_DIM ** -0.5 * LOG2E)).T.astype(BF16)
    vsT = pm[:, COL_VS:COL_KW].T
    vwT = pm[:, COL_VW:COL_GN].T
    tm = vsT.shape[1]
    ones = jnp.ones((V_ROWS - HEAD_DIM, tm), F32)
    tok = pl.program_id(0) * tm + lax.broadcasted_iota(jnp.int32, (tm, K_COLS - HEAD_DIM), 0)
    col = lax.broadcasted_iota(jnp.int32, (tm, K_COLS - HEAD_DIM), 1)
    blk_onehot = jnp.where((tok // SLC_LEN) % FAR_BLOCKS == col, 1.0, 0.0)
    for g in range(NSA_KV_HEADS):
        lo = g * HEAD_DIM
        kcr_ref[g] = pm[:, COL_KC + lo:COL_KC + lo + HEAD_DIM]
        vcr_ref[g] = pm[:, COL_VC + lo:COL_VC + lo + HEAD_DIM]
        ks_ref[g] = jnp.concatenate(
            [pm[:, COL_KS + lo:COL_KS + lo + HEAD_DIM], blk_onehot], axis=1).astype(BF16)
        kw_ref[g] = pm[:, COL_KW + lo:COL_KW + lo + HEAD_DIM].astype(BF16)
        vsT_ref[g] = jnp.concatenate([vsT[lo:lo + HEAD_DIM], ones], axis=0).astype(BF16)
        vwT_ref[g] = jnp.concatenate([vwT[lo:lo + HEAD_DIM], ones], axis=0).astype(BF16)
    gn = jax.nn.sigmoid(_dot(h, wn_ref[...]))
    gnT_ref[...] = gn.T[:GATE_ROWS, :]
    gb_ref[...] = jax.nn.sigmoid(_dot(h, wb_ref[...])).astype(BF16)


def _inproj(x2, g, w, tm=512):
    s = x2.shape[0]
    const = lambda i: (0, 0)
    row = lambda i: (i, 0)
    col = lambda i: (0, i)
    return pl.pallas_call(
        _inproj_kernel,
        grid=(s // tm,),
        in_specs=[
            pl.BlockSpec((tm, D_MODEL), row),
            pl.BlockSpec((1, D_MODEL), const),
            pl.BlockSpec(w.shape, const),
        ],
        out_specs=[
            pl.BlockSpec((tm, SSM_WIDTH), row),
            pl.BlockSpec((NSA_WIDTH, tm), col),
            pl.BlockSpec((NSA_KV_HEADS, tm, HEAD_DIM), lambda i: (0, i, 0)),
            pl.BlockSpec((NSA_KV_HEADS, tm, HEAD_DIM), lambda i: (0, i, 0)),
            pl.BlockSpec((NSA_KV_HEADS, tm, K_COLS), lambda i: (0, i, 0)),
            pl.BlockSpec((NSA_KV_HEADS, tm, HEAD_DIM), lambda i: (0, i, 0)),
            pl.BlockSpec((NSA_KV_HEADS, V_ROWS, tm), lambda i: (0, 0, i)),
            pl.BlockSpec((NSA_KV_HEADS, V_ROWS, tm), lambda i: (0, 0, i)),
            pl.BlockSpec((GATE_ROWS, tm), col),
            pl.BlockSpec((tm, 2 * D_MODEL), row),
        ],
        out_shape=[
            jax.ShapeDtypeStruct((s, SSM_WIDTH), F32),
            jax.ShapeDtypeStruct((NSA_WIDTH, s), BF16),
            jax.ShapeDtypeStruct((NSA_KV_HEADS, s, HEAD_DIM), F32),
            jax.ShapeDtypeStruct((NSA_KV_HEADS, s, HEAD_DIM), F32),
            jax.ShapeDtypeStruct((NSA_KV_HEADS, s, K_COLS), BF16),
            jax.ShapeDtypeStruct((NSA_KV_HEADS, s, HEAD_DIM), BF16),
            jax.ShapeDtypeStruct((NSA_KV_HEADS, V_ROWS, s), BF16),
            jax.ShapeDtypeStruct((NSA_KV_HEADS, V_ROWS, s), BF16),
            jax.ShapeDtypeStruct((GATE_ROWS, s), F32),
            jax.ShapeDtypeStruct((s, 2 * D_MODEL), BF16),
        ],
        scratch_shapes=[
            pltpu.VMEM((D_MODEL, COL_GN), BF16),
            pltpu.VMEM((D_MODEL, GATE_COLS), BF16),
            pltpu.VMEM((D_MODEL, COL_END - COL_GB), BF16),
        ],
        compiler_params=pltpu.CompilerParams(
            dimension_semantics=("arbitrary",), vmem_limit_bytes=VMEM_LIMIT),
        name="inproj",
    )(x2, g, w)


def _s5_kernel(u_ref, perm_ref, permT_ref, b_ref, c_ref, pw_ref, seg_ref, d_ref, wglu_ref, wup_ref,
               ya_ref, xre_ref, xim_ref, st_ref, cre_s, cim_s):
    @pl.when(pl.program_id(0) == 0)
    def _():
        cre_s[...] = jnp.zeros_like(cre_s)
        cim_s[...] = jnp.zeros_like(cim_s)

    u = u_ref[...]
    ub = _dot(perm_ref[...], u.astype(BF16)).astype(BF16)
    for sg in range(SSM_SUPER):
        bu = _dot(ub[:, sg * SG_CH:(sg + 1) * SG_CH], b_ref[sg])
        xre_ref[:, sg * SG_LANES:(sg + 1) * SG_LANES] = bu[:, :SG_LANES]
        xim_ref[:, sg * SG_LANES:(sg + 1) * SG_LANES] = bu[:, SG_LANES:]

    def cmul_add(re, im, are, aim, sre, sim):
        return re + are * sre - aim * sim, im + are * sim + aim * sre

    for sg in range(SSM_SUPER):
        lanes = slice(sg * SG_LANES, (sg + 1) * SG_LANES)

        lam_re, lam_im = pw_ref[0, 0:SCAN_ROWS, lanes], pw_ref[1, 0:SCAN_ROWS, lanes]

        def local(j, carry):
            r0 = pl.multiple_of(j * SCAN_ROWS, SCAN_ROWS)
            re, im = cmul_add(xre_ref[pl.ds(r0, SCAN_ROWS), lanes], xim_ref[pl.ds(r0, SCAN_ROWS), lanes],
                              lam_re, lam_im, *carry)
            xre_ref[pl.ds(r0, SCAN_ROWS), lanes] = re
            xim_ref[pl.ds(r0, SCAN_ROWS), lanes] = im
            return re, im

        zero = jnp.zeros((SCAN_ROWS, SG_LANES), F32)
        re, im = lax.fori_loop(0, SEG, local, (zero, zero), unroll=True)

        for k, shift in enumerate((1, 2, 4)):
            re, im = cmul_add(re, im, seg_ref[2 * k, :, lanes], seg_ref[2 * k + 1, :, lanes],
                              pltpu.roll(re, shift, 0), pltpu.roll(im, shift, 0))
        cin_re, cin_im = cre_s[:, lanes], cim_s[:, lanes]
        re, im = cmul_add(re, im, seg_ref[6, :, lanes], seg_ref[7, :, lanes], cin_re, cin_im)
        cre_s[:, lanes] = re[SCAN_ROWS - 1:SCAN_ROWS, :]
        cim_s[:, lanes] = im[SCAN_ROWS - 1:SCAN_ROWS, :]
        first = lax.broadcasted_iota(jnp.int32, (SCAN_ROWS, SG_LANES), 0) == 0
        start_re = jnp.where(first, cin_re, pltpu.roll(re, 1, 0))
        start_im = jnp.where(first, cin_im, pltpu.roll(im, 1, 0))
        start_re = jnp.concatenate([start_re, start_re], axis=0)
        start_im = jnp.concatenate([start_im, start_im], axis=0)

        def fix(jj, carry):
            r0 = pl.multiple_of(jj * 2 * SCAN_ROWS, 2 * SCAN_ROWS)
            rows = pl.ds(r0, 2 * SCAN_ROWS)
            re, im = cmul_add(xre_ref[rows, lanes], xim_ref[rows, lanes],
                              pw_ref[0, rows, lanes], pw_ref[1, rows, lanes], start_re, start_im)
            st_ref[rows, 2 * sg * SG_LANES:(2 * sg + 1) * SG_LANES] = re.astype(BF16)
            st_ref[rows, (2 * sg + 1) * SG_LANES:(2 * sg + 2) * SG_LANES] = im.astype(BF16)
            return carry

        lax.fori_loop(0, SEG // 2, fix, 0, unroll=True)
    y_perm = jnp.concatenate(
        [_dot(st_ref[:, 2 * sg * SG_LANES:(2 * sg + 2) * SG_LANES], c_ref[sg])
         for sg in range(SSM_SUPER)], axis=1)
    y_hi = y_perm.astype(BF16)
    y_lo = (y_perm - y_hi.astype(F32)).astype(BF16)
    y = _dot(permT_ref[...], y_hi) + _dot(permT_ref[...], y_lo) + d_ref[...] * u
    z = jax.nn.gelu(y)
    z = z * jax.nn.sigmoid(_dot(z.astype(BF16), wglu_ref[...]))
    ya_ref[...] = _dot(z.astype(BF16), wup_ref[...])


def _s5(u, b, c, pw, seg, d, wglu, wup):
    s = u.shape[0]
    t = np.arange(S5_TT)
    perm = np.zeros((S5_TT, S5_TT), np.float32)
    perm[SCAN_ROWS * (t % SEG) + t // SEG, t] = 1.0
    permT = jnp.asarray(perm.T, BF16)
    perm = jnp.asarray(perm, BF16)
    const2 = lambda i: (0, 0)
    const3 = lambda i: (0, 0, 0)
    return pl.pallas_call(
        _s5_kernel,
        grid=(s // S5_TT,),
        in_specs=[
            pl.BlockSpec((S5_TT, SSM_WIDTH), lambda i: (i, 0)),
            pl.BlockSpec(perm.shape, const2),
            pl.BlockSpec(permT.shape, const2),
            pl.BlockSpec(b.shape, const3),
            pl.BlockSpec(c.shape, const3),
            pl.BlockSpec(pw.shape, const3),
            pl.BlockSpec(seg.shape, const3),
            pl.BlockSpec((1, SSM_WIDTH), const2),
            pl.BlockSpec(wglu.shape, const2),
            pl.BlockSpec(wup.shape, const2),
        ],
        out_specs=pl.BlockSpec((S5_TT, D_MODEL), lambda i: (i, 0)),
        out_shape=jax.ShapeDtypeStruct((s, D_MODEL), F32),
        scratch_shapes=[
            pltpu.VMEM((S5_TT, SSM_LANES), F32),
            pltpu.VMEM((S5_TT, SSM_LANES), F32),
            pltpu.VMEM((S5_TT, 2 * SSM_LANES), BF16),
            pltpu.VMEM((1, SSM_LANES), F32),
            pltpu.VMEM((1, SSM_LANES), F32),
        ],
        compiler_params=pltpu.CompilerParams(
            dimension_semantics=("arbitrary",), vmem_limit_bytes=VMEM_LIMIT),
        name="s5",
    )(u, perm, permT, b, c, pw, seg, d, wglu, wup)


def _s5_params(a_re, a_im, log_dt, b_re, b_im, c_re, c_im):
    dt = jnp.exp(log_dt.astype(F32))[:, None]
    ar, ai = a_re.astype(F32), a_im.astype(F32)
    zr, zi = ar * dt, ai * dt

    def power(n):
        mag = jnp.exp(n * zr)
        return mag * jnp.cos(n * zi), mag * jnp.sin(n * zi)

    lr, li = power(1.0)
    den = ar * ar + ai * ai
    kr = ((lr - 1.0) * ar + li * ai) / den
    ki = (li * ar - (lr - 1.0) * ai) / den
    br, bi = b_re.astype(F32), b_im.astype(F32)
    bbr = kr[..., None] * br - ki[..., None] * bi
    bbi = kr[..., None] * bi + ki[..., None] * br
    eye = jnp.eye(SSM_GROUPS, dtype=F32)
    bd_in = lambda b: jnp.einsum("gpc,gh->gchp", b, eye).reshape(SSM_WIDTH, SSM_LANES)
    bd_out = lambda c: jnp.einsum("gcp,gh->gphc", c, eye).reshape(SSM_LANES, SSM_WIDTH)
    zr, zi = zr.reshape(1, SSM_LANES), zi.reshape(1, SSM_LANES)
    pw = jnp.stack(power(jnp.repeat(jnp.arange(1, SEG + 1, dtype=F32), SCAN_ROWS)[:, None]))
    row = jnp.arange(SCAN_ROWS)[:, None]
    seg = []
    for shift in (1, 2, 4):
        pr, pi = power(float(SEG * shift))
        seg += [jnp.where(row >= shift, pr, 0.0), jnp.where(row >= shift, pi, 0.0)]
    seg += list(power(SEG * (row + 1).astype(F32)))
    b_re_bd, b_im_bd = bd_in(bbr), bd_in(bbi)
    c_re_bd, c_im_bd = bd_out(c_re.astype(F32)), bd_out(c_im.astype(F32))
    b_sg, c_sg = [], []
    for sg in range(SSM_SUPER):
        ch = slice(sg * SG_CH, (sg + 1) * SG_CH)
        st = slice(sg * SG_LANES, (sg + 1) * SG_LANES)
        b_sg.append(jnp.concatenate([b_re_bd[ch, st], b_im_bd[ch, st]], axis=1))
        c_sg.append(jnp.concatenate([c_re_bd[st, ch], -c_im_bd[st, ch]], axis=0))
    return jnp.stack(b_sg).astype(BF16), jnp.stack(c_sg).astype(BF16), pw, jnp.stack(seg)


def _compress_kernel(xk_ref, xv_ref, w1k_ref, w2k_ref, pk_ref, w1v_ref, w2vT_ref, pv_ref, kc_ref, vcT_ref):
    half = CMP_STRIDE * HEAD_DIM

    def hidden(x_ref, w1_ref, pos_ref):
        x = x_ref[0].astype(BF16)
        first = _dot(x, w1_ref[:half, :])
        second = _dot(x, w1_ref[half:, :])
        bias = _dot(jnp.broadcast_to(pos_ref[...], (8, 2 * half)).astype(BF16), w1_ref[...])[:1]
        pre = first + pltpu.roll(second, N_CHUNK - 1, 0) + bias
        return jax.nn.gelu(pre).astype(BF16)

    hk = hidden(xk_ref, w1k_ref, pk_ref)
    kc = _dot(hk, w2k_ref[...])
    tail_pad = N_CMP_PAD - CMP_PAD - N_CHUNK
    kc_ref[0] = jnp.concatenate([jnp.zeros((CMP_PAD, HEAD_DIM), F32), kc,
                                 jnp.zeros((tail_pad, HEAD_DIM), F32)], axis=0).astype(BF16)
    hv = hidden(xv_ref, w1v_ref, pv_ref)
    vcT = _dot_nt(w2vT_ref[...], hv)
    vcT_ref[0] = jnp.concatenate([jnp.zeros((HEAD_DIM, CMP_PAD), F32), vcT,
                                  jnp.zeros((HEAD_DIM, tail_pad), F32)], axis=1).astype(BF16)


def _compress(xk, xv, w1k, w2k, pk, w1v, w2vT, pv):
    const2 = lambda g: (0, 0)
    head = lambda g: (g, 0, 0)
    return pl.pallas_call(
        _compress_kernel,
        grid=(NSA_KV_HEADS,),
        in_specs=[
            pl.BlockSpec((1, N_CHUNK, CMP_STRIDE * HEAD_DIM), head),
            pl.BlockSpec((1, N_CHUNK, CMP_STRIDE * HEAD_DIM), head),
            pl.BlockSpec(w1k.shape, const2),
            pl.BlockSpec(w2k.shape, const2),
            pl.BlockSpec(pk.shape, const2),
            pl.BlockSpec(w1v.shape, const2),
            pl.BlockSpec(w2vT.shape, const2),
            pl.BlockSpec(pv.shape, const2),
        ],
        out_specs=[
            pl.BlockSpec((1, N_CMP_PAD, HEAD_DIM), head),
            pl.BlockSpec((1, HEAD_DIM, N_CMP_PAD), head),
        ],
        out_shape=[
            jax.ShapeDtypeStruct((NSA_KV_HEADS, N_CMP_PAD, HEAD_DIM), BF16),
            jax.ShapeDtypeStruct((NSA_KV_HEADS, HEAD_DIM, N_CMP_PAD), BF16),
        ],
        compiler_params=pltpu.CompilerParams(
            dimension_semantics=("arbitrary",), vmem_limit_bytes=VMEM_LIMIT),
        name="compress",
    )(xk, xv, w1k, w2k, pk, w1v, w2vT, pv)


def _nsa_kernel(qT_ref, kc_ref, vcT_ref, ks_ref, vsT_ref, kw_ref, vwT_ref, bn_ref, bw_ref, bc_ref, mt_ref, gT_ref,
                oT_ref, sc_ref, neg_ref, negfar_ref, m_ref, acc_ref, tot_ref, sbuf0_ref, sbuf1_ref, mloc_ref, pslc_ref):
    i = pl.program_id(1)
    s0 = i * TQ
    qT = jnp.concatenate([qT_ref[r * HEAD_DIM:(r + 1) * HEAD_DIM, :] for r in range(GQA)], axis=1)

    qT_nomask = jnp.concatenate([qT, jnp.zeros((K_COLS - HEAD_DIM, QL), BF16)], axis=0)

    gate0 = pl.program_id(0) * GATES_PER_KV_HEAD

    def gate_row(branch):
        return jnp.concatenate([gT_ref[pl.ds(gate0 + r * 3 + branch, 1), :] for r in range(GQA)], axis=1)

    def reset():
        m_ref[...] = jnp.full_like(m_ref, M_FLOOR)
        acc_ref[...] = jnp.zeros_like(acc_ref)

    def attend(k_ref, vT_ref, start, size, add):
        k = k_ref[0, pl.ds(start, size), :]
        s = _dot(k, qT if k.shape[1] == HEAD_DIM else qT_nomask) + add
        m_prev = m_ref[...]
        m_new = jnp.maximum(m_prev, jnp.max(s, axis=0, keepdims=True))
        alpha = jnp.exp2(m_prev - m_new)
        p = jnp.exp2(s - m_new).astype(BF16)
        acc_ref[...] = alpha * acc_ref[...] + _dot(vT_ref[0, :, pl.ds(start, size)], p)
        m_ref[...] = m_new

    def finish(branch):
        acc = acc_ref[...]
        scale = gate_row(branch) / jnp.maximum(acc[HEAD_DIM:HEAD_DIM + 1, :], 1e-30)
        tot_ref[...] += acc[:HEAD_DIM, :] * scale

    n0 = i * (TQ // CMP_STRIDE)
    band0 = pl.multiple_of(n0 + CMP_PAD - 16, 8)

    def cmp_branch(nrows):
        row = lax.broadcasted_iota(jnp.int32, (nrows, QL), 0)
        live = (row >= CMP_PAD) & (row < band0 + CMP_BAND)
        sc_ref[0:nrows, :] = jnp.where(live, _dot(kc_ref[0, 0:nrows, :], qT), NEG)
        sc_ref[pl.ds(band0, CMP_BAND), :] += bc_ref[0]
        sc = sc_ref[0:nrows, :]
        mc = jnp.maximum(jnp.max(sc, axis=0, keepdims=True), M_FLOOR)
        pc = jnp.exp2(sc - mc)
        pc = pc * (1.0 / jnp.maximum(jnp.sum(pc, axis=0, keepdims=True), 1e-30))
        tot_ref[...] = _dot(vcT_ref[0, :, 0:nrows], pc.astype(BF16)) * gate_row(0)
        imp = pc[:, 0:TQ]
        for r in range(1, GQA):
            imp = imp + pc[:, r * TQ:(r + 1) * TQ]
        mt = mt_ref[:, 0:nrows]
        p_slc = jnp.zeros((N_SLC, TQ), F32)
        rem = imp
        for _ in range(2):
            piece = rem.astype(BF16)
            p_slc = p_slc + _dot(mt, piece)
            rem = rem - piece.astype(F32)
        pslc_ref[...] = p_slc

    prev_rows = 0
    for nrows in CMP_EXTENTS:
        lo, hi = prev_rows, nrows
        pl.when((band0 + CMP_BAND > lo) & (band0 + CMP_BAND <= hi))(functools.partial(cmp_branch, nrows))
        prev_rows = nrows

    reset()
    attend(kw_ref, vwT_ref, pl.multiple_of(jnp.maximum(s0 - WINDOW, 0), NEAR), WINDOW + TQ, bw_ref[0, 0])
    finish(2)

    p_slc = pslc_ref[...]
    blk = lax.broadcasted_iota(jnp.int32, (N_SLC, TQ), 0)
    cur = (s0 + lax.broadcasted_iota(jnp.int32, (N_SLC, TQ), 1)) // SLC_LEN
    valid = blk <= cur
    forced = valid & ((blk == 0) | (blk >= cur - (N_LOCAL - 1)))
    score = jnp.where(forced, -jnp.inf, jnp.where(valid, p_slc, -BIG))
    blk_f = blk.astype(F32)
    for _ in range(N_SEL - (N_LOCAL + 1)):
        best = jnp.max(score, axis=0, keepdims=True)
        first = jnp.min(jnp.where(score == best, blk_f, float(N_SLC)), axis=0, keepdims=True)
        score = jnp.where(blk_f == first, -jnp.inf, score)
    neg = jnp.where(score == -jnp.inf, 0.0, NEG)
    neg_ref[...] = jnp.concatenate([neg] * GQA, axis=1)
    near_blk = (s0 - NEAR) // SLC_LEN
    negfar_ref[...] = jnp.concatenate([jnp.where(blk >= near_blk, NEG, neg)] * GQA, axis=1)

    def block_mask(ref, j0, nblk):
        return jnp.concatenate(
            [jnp.broadcast_to(ref[pl.ds(j0 + b, 1), :], (SLC_LEN, QL)) for b in range(nblk)], axis=0)

    reset()
    n_far = (s0 - NEAR + FAR_KEYS - 1) // FAR_KEYS

    sbufs = (sbuf0_ref, sbuf1_ref)

    def far_logits(c, slot):
        start = pl.multiple_of(c * FAR_KEYS, FAR_KEYS)
        mask_rows = negfar_ref[pl.ds(pl.multiple_of(c * FAR_BLOCKS, FAR_BLOCKS), FAR_BLOCKS), :]
        extra = jnp.concatenate([mask_rows, jnp.zeros((K_COLS - HEAD_DIM - FAR_BLOCKS, QL), F32)], axis=0)
        q_masked = jnp.concatenate([qT, extra.astype(BF16)], axis=0)
        s = _dot(ks_ref[0, pl.ds(start, FAR_KEYS), :], q_masked)
        sbufs[slot][...] = s
        mloc_ref[slot] = jnp.max(s, axis=0, keepdims=True)

    def far_consume(c, slot):
        start = pl.multiple_of(c * FAR_KEYS, FAR_KEYS)
        m_prev = m_ref[...]
        m_new = jnp.maximum(m_prev, mloc_ref[slot])
        alpha = jnp.exp2(m_prev - m_new)
        p = jnp.exp2(sbufs[slot][...] - m_new).astype(BF16)
        acc_ref[...] = alpha * acc_ref[...] + _dot(vsT_ref[0, :, pl.ds(start, FAR_KEYS)], p)
        m_ref[...] = m_new

    @pl.when(i >= 1)
    def _():
        n_pairs = (n_far + 1) // 2
        far_logits(0, 0)
        attend(ks_ref, vsT_ref, pl.multiple_of(s0 - NEAR, NEAR), NEAR + TQ,
               bn_ref[0] + block_mask(neg_ref, near_blk, (NEAR + TQ) // SLC_LEN))

        def far_body(p, carry):
            far_logits(2 * p + 1, 1)
            far_consume(2 * p, 0)
            far_logits(2 * p + 2, 0)
            far_consume(2 * p + 1, 1)
            return carry

        lax.fori_loop(0, n_pairs - 1, far_body, 0)
        last = 2 * (n_pairs - 1)

        @pl.when(n_far % 2 == 0)
        def _():
            far_logits(last + 1, 1)
            far_consume(last, 0)
            far_consume(last + 1, 1)

        @pl.when(n_far % 2 == 1)
        def _():
            far_consume(last, 0)

    @pl.when(i == 0)
    def _():
        attend(ks_ref, vsT_ref, 0, TQ, bn_ref[0, NEAR:NEAR + TQ, :] + block_mask(neg_ref, 0, TQ // SLC_LEN))

    finish(1)

    tot = tot_ref[...]
    for r in range(GQA):
        oT_ref[r * HEAD_DIM:(r + 1) * HEAD_DIM, :] = tot[:, r * TQ:(r + 1) * TQ]


def _nsa(qT, kc, vcT, ks, vsT, kw, vwT, bias_near, bias_win, bias_cmp, mt, gnT):
    s = qT.shape[1]
    head3 = lambda g, i: (g, 0, 0)
    return pl.pallas_call(
        _nsa_kernel,
        grid=(NSA_KV_HEADS, s // TQ),
        in_specs=[
            pl.BlockSpec((GQA * HEAD_DIM, TQ), lambda g, i: (g, i)),
            pl.BlockSpec((1, N_CMP_PAD, HEAD_DIM), head3),
            pl.BlockSpec((1, HEAD_DIM, N_CMP_PAD), head3),
            pl.BlockSpec((1, s, K_COLS), head3, pipeline_mode=pl.Buffered(1)),
            pl.BlockSpec((1, V_ROWS, s), head3, pipeline_mode=pl.Buffered(1)),
            pl.BlockSpec((1, s, HEAD_DIM), head3, pipeline_mode=pl.Buffered(1)),
            pl.BlockSpec((1, V_ROWS, s), head3, pipeline_mode=pl.Buffered(1)),
            pl.BlockSpec((1, NEAR + TQ, QL), head3),
            pl.BlockSpec((1, 1, WINDOW + TQ, QL), lambda g, i: (g, jnp.minimum(i, WINDOW // TQ), 0, 0)),
            pl.BlockSpec((1, CMP_BAND, QL), head3),
            pl.BlockSpec(mt.shape, lambda g, i: (0, 0)),
            pl.BlockSpec((GATE_ROWS, TQ), lambda g, i: (0, i)),
        ],
        out_specs=pl.BlockSpec((GQA * HEAD_DIM, TQ), lambda g, i: (g, i)),
        out_shape=jax.ShapeDtypeStruct((NSA_WIDTH, s), F32),
        scratch_shapes=[
            pltpu.VMEM((N_CMP_PAD, QL), F32),
            pltpu.VMEM((N_SLC, QL), F32),
            pltpu.VMEM((N_SLC, QL), F32),
            pltpu.VMEM((1, QL), F32),
            pltpu.VMEM((V_ROWS, QL), F32),
            pltpu.VMEM((HEAD_DIM, QL), F32),
            pltpu.VMEM((FAR_KEYS, QL), F32),
            pltpu.VMEM((FAR_KEYS, QL), F32),
            pltpu.VMEM((2, 1, QL), F32),
            pltpu.VMEM((N_SLC, TQ), F32),
        ],
        compiler_params=pltpu.CompilerParams(
            dimension_semantics=("arbitrary", "arbitrary"), vmem_limit_bytes=VMEM_LIMIT),
        name="nsa",
    )(qT, kc, vcT, ks, vsT, kw, vwT, bias_near, bias_win, bias_cmp, mt, gnT)


def _t5_bucket(dist):
    n = jnp.maximum(dist, 0)
    max_exact = REL_BUCKETS // 2
    nf = jnp.maximum(n, 1).astype(F32)
    large = max_exact + (jnp.log(nf / max_exact) / math.log(REL_MAX_DIST / max_exact)
                         * (REL_BUCKETS - max_exact)).astype(jnp.int32)
    large = jnp.minimum(large, REL_BUCKETS - 1)
    return jnp.where(n < max_exact, n, large)


def _bias_tiles(rel_bias):
    tab = rel_bias.astype(F32)
    tab = (tab[_t5_bucket(jnp.arange(NEAR))] - tab[REL_BUCKETS - 1]).T * LOG2E
    tab = jnp.concatenate([tab, jnp.zeros((NSA_HEADS, 1), F32)], axis=1)

    def by_distance(d):
        return jnp.where(d >= 0, tab[:, jnp.clip(d, 0, NEAR)], NEG)

    def toeplitz(c, nk, nq, step=1):
        n = step * (nk - 1) + nq
        w = by_distance(jnp.arange(n) + c - step * (nk - 1))
        reps = -(-nk * (n + step) // n)
        return jnp.tile(w, (1, reps))[:, :nk * (n + step)].reshape(NSA_HEADS, nk, n + step)[:, ::-1, :nq]

    def per_kv_head(a):
        a = a.reshape(NSA_KV_HEADS, GQA, a.shape[1], TQ)
        return jnp.transpose(a, (0, 2, 1, 3)).reshape(NSA_KV_HEADS, a.shape[2], QL)

    k2, q2 = np.arange(NEAR)[:, None], np.arange(NEAR)[None, :]
    const = lambda a: jnp.broadcast_to(jnp.asarray(a, F32), (NSA_HEADS, NEAR, NEAR))
    edge = WINDOW // NEAR
    nb = TQ // NEAR
    blocks, slabs = {}, {}

    def block(d):
        d = min(max(d, -1), edge + 1)
        if d not in blocks:
            if d < 0:
                b = const(NEG)
            elif d < 2:
                b = toeplitz(d * NEAR, NEAR, NEAR)
            elif d < edge:
                b = const(0.0)
            else:
                b = const(np.where(k2 > q2, 0.0, NEG) if d == edge else NEG)
            blocks[d] = jnp.transpose(b.reshape(NSA_KV_HEADS, GQA, NEAR, NEAR), (0, 2, 1, 3))
        return blocks[d]

    def slab(dd):
        dd = min(max(dd, -nb), edge + 1)
        if dd not in slabs:
            s = jnp.stack([block(dd + b) for b in range(nb)], axis=3)
            slabs[dd] = s.reshape(NSA_KV_HEADS, NEAR, QL)
        return slabs[dd]

    def tile(first_d, n_key_blocks):
        return jnp.concatenate([slab(first_d - a) for a in range(n_key_blocks)], axis=1)

    near = tile(1, (NEAR + TQ) // NEAR)
    window = jnp.stack([tile(j * nb, (WINDOW + TQ) // NEAR) for j in range(WINDOW // TQ + 1)], axis=1)
    cmp_band = toeplitz(16 * CMP_STRIDE - (CMP_LEN - 1), CMP_BAND, TQ, step=CMP_STRIDE)
    return near, window, per_kv_head(cmp_band)


def _overlap_matrix():
    ratio = SLC_LEN // CMP_STRIDE
    front = CMP_LEN // CMP_STRIDE - 1
    w_ov = np.convolve(np.ones(ratio), np.ones(CMP_LEN // CMP_STRIDE))
    mt = np.zeros((N_SLC, N_CMP_PAD), np.float32)
    for j in range(N_SLC):
        for o, w in enumerate(w_ov):
            n = ratio * j + o - front
            if 0 <= n < N_CMP:
                mt[j, CMP_PAD + n] = w
    return jnp.asarray(mt, BF16)


def _tail_kernel(x_ref, ya_ref, oT_ref, gb_ref, wup_ref, wout_ref, g_ref, wg_ref, wu_ref, wd_ref, gf_ref, o_ref):
    yb = _dot(oT_ref[...].T.astype(BF16), wup_ref[...])
    mix = gb_ref[:, :D_MODEL].astype(F32) * ya_ref[...] + gb_ref[:, D_MODEL:].astype(F32) * yb
    x = x_ref[...] + _dot(mix.astype(BF16), wout_ref[...])
    h = _rms(x, g_ref[...]).astype(BF16)
    f = jax.nn.silu(_dot(h, wg_ref[...])) * _dot(h, wu_ref[...])
    x = x + _dot(f.astype(BF16), wd_ref[...])
    o_ref[...] = _rms(x, gf_ref[...])


def _tail(x2, ya, oT, gb, wup, wout, g, wg, wu, wd, gf, tm=256):
    s = x2.shape[0]
    row = lambda i: (i, 0)
    const = lambda i: (0, 0)
    return pl.pallas_call(
        _tail_kernel,
        grid=(s // tm,),
        in_specs=[
            pl.BlockSpec((tm, D_MODEL), row),
            pl.BlockSpec((tm, D_MODEL), row),
            pl.BlockSpec((NSA_WIDTH, tm), lambda i: (0, i)),
            pl.BlockSpec((tm, 2 * D_MODEL), row),
            pl.BlockSpec(wup.shape, const),
            pl.BlockSpec(wout.shape, const),
            pl.BlockSpec((1, D_MODEL), const),
            pl.BlockSpec(wg.shape, const),
            pl.BlockSpec(wu.shape, const),
            pl.BlockSpec(wd.shape, const),
            pl.BlockSpec((1, D_MODEL), const),
        ],
        out_specs=pl.BlockSpec((tm, D_MODEL), row),
        out_shape=jax.ShapeDtypeStruct((s, D_MODEL), F32),
        compiler_params=pltpu.CompilerParams(
            dimension_semantics=("arbitrary",), vmem_limit_bytes=VMEM_LIMIT),
        name="tail",
    )(x2, ya, oT, gb, wup, wout, g, wg, wu, wd, gf)


def kernel(x, norm_mix_g, w_in, ssm_a_re, ssm_a_im, ssm_log_dt, ssm_b_re, ssm_b_im, ssm_c_re, ssm_c_im, ssm_d, ssm_w_glu, w_up_ssm, cmp_pos_k, cmp_pos_v, cmp_w1_k, cmp_w2_k, cmp_w1_v, cmp_w2_v, rel_bias, w_up_nsa, w_out, norm_ffn_g, w_ffn_gate, w_ffn_up, w_ffn_down, norm_final_g):
    bsz, s, _ = x.shape
    assert (bsz, s) == (1, SEQ) and w_in.shape[0] == 1
    x2 = x.reshape(s, D_MODEL)
    l = 0
    row = lambda v: v.astype(F32).reshape(1, -1)

    u, qT, kcr, vcr, ks, kw, vsT, vwT, gnT, gb = _inproj(x2, row(norm_mix_g[l]), w_in[l].astype(F32))

    b_sg, c_sg, pw, seg = _s5_params(
        ssm_a_re[l], ssm_a_im[l], ssm_log_dt[l], ssm_b_re[l], ssm_b_im[l], ssm_c_re[l], ssm_c_im[l])
    ya = _s5(u, b_sg, c_sg, pw, seg, row(ssm_d[l]), ssm_w_glu[l].astype(BF16), w_up_ssm[l].astype(BF16))

    chunks = lambda a: a.reshape(NSA_KV_HEADS, N_CHUNK, CMP_STRIDE * HEAD_DIM)
    kc, vcT = _compress(chunks(kcr), chunks(vcr),
                        cmp_w1_k[l].astype(BF16), cmp_w2_k[l].astype(BF16), row(cmp_pos_k[l]),
                        cmp_w1_v[l].astype(BF16), cmp_w2_v[l].T.astype(BF16), row(cmp_pos_v[l]))

    bias_near, bias_win, bias_cmp = _bias_tiles(rel_bias)
    oT = _nsa(qT, kc, vcT, ks, vsT, kw, vwT, bias_near, bias_win, bias_cmp, _overlap_matrix(), gnT)

    out = _tail(x2, ya, oT, gb, w_up_nsa[l].astype(BF16), w_out[l].astype(BF16),
                row(norm_ffn_g[l]), w_ffn_gate[l].astype(BF16), w_ffn_up[l].astype(BF16),
                w_ffn_down[l].astype(BF16), row(norm_final_g))
    return out.reshape(bsz, s, D_MODEL)
```

```python
import functools
import math

import numpy as np
import jax
import jax.numpy as jnp
from jax import lax
from jax.experimental import pallas as pl
from jax.experimental.pallas import tpu as pltpu

F32 = jnp.float32
BF16 = jnp.bfloat16

D_MODEL = 1024
SEQ = 16384
EPS = 1e-6
SSM_WIDTH = 512
SSM_GROUP = 16
SSM_GROUPS = SSM_WIDTH // SSM_GROUP
SSM_STATE = 64
SSM_LANES = SSM_GROUPS * SSM_STATE
NSA_HEADS = 8
NSA_KV_HEADS = 2
GQA = NSA_HEADS // NSA_KV_HEADS
HEAD_DIM = 64
NSA_WIDTH = NSA_HEADS * HEAD_DIM
KV_WIDTH = NSA_KV_HEADS * HEAD_DIM
CMP_LEN = 32
CMP_STRIDE = 16
CMP_HIDDEN = 256
SLC_LEN = 64
N_SEL = 16
N_LOCAL = 2
WINDOW = 512
BIG = 1e4
REL_BUCKETS = 32
REL_MAX_DIST = 128
D_FF = 2816

N_CHUNK = SEQ // CMP_STRIDE
N_CMP = (SEQ - CMP_LEN) // CMP_STRIDE + 1
N_SLC = SEQ // SLC_LEN
TQ = 256
QL = GQA * TQ
NEAR = 128
assert TQ > NEAR and TQ % NEAR == 0
CMP_PAD = 16
CMP_ROW_STEP = 128
N_CMP_PAD = -(-(CMP_PAD + N_CHUNK) // CMP_ROW_STEP) * CMP_ROW_STEP
CMP_BAND = 16 + TQ // CMP_STRIDE
NEG = -1e30
M_FLOOR = -1e29
SCAN_ROWS = 8
S5_TT = 256
SEG = S5_TT // SCAN_ROWS
SSM_SUPER = 2
SG_CH = SSM_WIDTH // SSM_SUPER
SG_LANES = SSM_LANES // SSM_SUPER
LOG2E = math.log2(math.e)
V_ROWS = HEAD_DIM + 16
FAR_KEYS = 512
FAR_BLOCKS = FAR_KEYS // SLC_LEN
K_COLS = HEAD_DIM + 16
CMP_EXTENTS = tuple(range(2 * CMP_ROW_STEP, N_CMP_PAD + 1, CMP_ROW_STEP))

VMEM_LIMIT = 56 * 1024 * 1024

COL_U = 0
COL_Q = 512
COL_KC = 1024
COL_VC = 1152
COL_KS = 1280
COL_VS = 1408
COL_KW = 1536
COL_VW = 1664
COL_GN = 1792
COL_GB = 1816
COL_END = 3864
GATE_COLS = 128
GATE_ROWS = 32
GATES_PER_KV_HEAD = GQA * 3


def _rms(x, g):
    return x * lax.rsqrt(jnp.mean(x * x, axis=-1, keepdims=True) + EPS) * g


def _dot(a, b):
    return jnp.dot(a, b, preferred_element_type=F32)


def _dot_nt(a, b):
    return lax.dot_general(a, b, (((1,), (1,)), ((), ())), preferred_element_type=F32)


def _inproj_kernel(x_ref, g_ref, w_ref,
                   u_ref, qT_ref, kcr_ref, vcr_ref, ks_ref, kw_ref, vsT_ref, vwT_ref, gnT_ref, gb_ref,
                   wm_ref, wn_ref, wb_ref, kstage_ref, vstage_ref):
    @pl.when(pl.program_id(0) == 0)
    def _():
        wm_ref[...] = w_ref[:, :COL_GN].astype(BF16)
        wn_ref[...] = w_ref[:, COL_GN:COL_GN + GATE_COLS].astype(BF16)
        wb_ref[...] = w_ref[:, COL_GB:].astype(BF16)

    h = _rms(x_ref[...], g_ref[...]).astype(BF16)
    pm = _dot(h, wm_ref[...])
    u_ref[...] = pm[:, COL_U:COL_Q]
    qT_ref[...] = (pm[:, COL_Q:COL_KC] * (HEAD_DIM ** -0.5 * LOG2E)).T.astype(BF16)
    vsT = pm[:, COL_VS:COL_KW].T
    vwT = pm[:, COL_VW:COL_GN].T
    tm = vsT.shape[1]
    ones = jnp.ones((V_ROWS - HEAD_DIM, tm), F32)
    tok = pl.program_id(0) * tm + lax.broadcasted_iota(jnp.int32, (tm, K_COLS - HEAD_DIM), 0)
    col = lax.broadcasted_iota(jnp.int32, (tm, K_COLS - HEAD_DIM), 1)
    blk_onehot = jnp.where((tok // SLC_LEN) % FAR_BLOCKS == col, 1.0, 0.0)
    for g in range(NSA_KV_HEADS):
        lo = g * HEAD_DIM
        ks_ref[g] = jnp.concatenate(
            [pm[:, COL_KS + lo:COL_KS + lo + HEAD_DIM], blk_onehot], axis=1).astype(BF16)
        kw_ref[g] = pm[:, COL_KW + lo:COL_KW + lo + HEAD_DIM].astype(BF16)
        vsT_ref[g] = jnp.concatenate([vsT[lo:lo + HEAD_DIM], ones], axis=0).astype(BF16)
        vwT_ref[g] = jnp.concatenate([vwT[lo:lo + HEAD_DIM], ones], axis=0).astype(BF16)
    for src_col, stage_ref, dst_ref in ((COL_KC, kstage_ref, kcr_ref), (COL_VC, vstage_ref, vcr_ref)):
        stage_ref[...] = pm[:, src_col:src_col + KV_WIDTH]
        for t in range(CMP_STRIDE):
            rows = stage_ref[pl.ds(t, tm // CMP_STRIDE, stride=CMP_STRIDE), :]
            for g in range(NSA_KV_HEADS):
                dst_ref[g, :, t * HEAD_DIM:(t + 1) * HEAD_DIM] = rows[:, g * HEAD_DIM:(g + 1) * HEAD_DIM]
    gn = jax.nn.sigmoid(_dot(h, wn_ref[...]))
    gnT_ref[...] = gn.T[:GATE_ROWS, :]
    gb_ref[...] = jax.nn.sigmoid(_dot(h, wb_ref[...])).astype(BF16)


def _inproj(x2, g, w, tm=512):
    s = x2.shape[0]
    const = lambda i: (0, 0)
    row = lambda i: (i, 0)
    col = lambda i: (0, i)
    return pl.pallas_call(
        _inproj_kernel,
        grid=(s // tm,),
        in_specs=[
            pl.BlockSpec((tm, D_MODEL), row),
            pl.BlockSpec((1, D_MODEL), const),
            pl.BlockSpec(w.shape, const),
        ],
        out_specs=[
            pl.BlockSpec((tm, SSM_WIDTH), row),
            pl.BlockSpec((NSA_WIDTH, tm), col),
            pl.BlockSpec((NSA_KV_HEADS, tm // CMP_STRIDE, CMP_STRIDE * HEAD_DIM), lambda i: (0, i, 0)),
            pl.BlockSpec((NSA_KV_HEADS, tm // CMP_STRIDE, CMP_STRIDE * HEAD_DIM), lambda i: (0, i, 0)),
            pl.BlockSpec((NSA_KV_HEADS, tm, K_COLS), lambda i: (0, i, 0)),
            pl.BlockSpec((NSA_KV_HEADS, tm, HEAD_DIM), lambda i: (0, i, 0)),
            pl.BlockSpec((NSA_KV_HEADS, V_ROWS, tm), lambda i: (0, 0, i)),
            pl.BlockSpec((NSA_KV_HEADS, V_ROWS, tm), lambda i: (0, 0, i)),
            pl.BlockSpec((GATE_ROWS, tm), col),
            pl.BlockSpec((tm, 2 * D_MODEL), row),
        ],
        out_shape=[
            jax.ShapeDtypeStruct((s, SSM_WIDTH), F32),
            jax.ShapeDtypeStruct((NSA_WIDTH, s), BF16),
            jax.ShapeDtypeStruct((NSA_KV_HEADS, s // CMP_STRIDE, CMP_STRIDE * HEAD_DIM), F32),
            jax.ShapeDtypeStruct((NSA_KV_HEADS, s // CMP_STRIDE, CMP_STRIDE * HEAD_DIM), F32),
            jax.ShapeDtypeStruct((NSA_KV_HEADS, s, K_COLS), BF16),
            jax.ShapeDtypeStruct((NSA_KV_HEADS, s, HEAD_DIM), BF16),
            jax.ShapeDtypeStruct((NSA_KV_HEADS, V_ROWS, s), BF16),
            jax.ShapeDtypeStruct((NSA_KV_HEADS, V_ROWS, s), BF16),
            jax.ShapeDtypeStruct((GATE_ROWS, s), F32),
            jax.ShapeDtypeStruct((s, 2 * D_MODEL), BF16),
        ],
        scratch_shapes=[
            pltpu.VMEM((D_MODEL, COL_GN), BF16),
            pltpu.VMEM((D_MODEL, GATE_COLS), BF16),
            pltpu.VMEM((D_MODEL, COL_END - COL_GB), BF16),
            pltpu.VMEM((tm, KV_WIDTH), F32),
            pltpu.VMEM((tm, KV_WIDTH), F32),
        ],
        compiler_params=pltpu.CompilerParams(
            dimension_semantics=("arbitrary",), vmem_limit_bytes=VMEM_LIMIT),
        name="inproj",
    )(x2, g, w)


def _s5_kernel(u_ref, perm_ref, permT_ref, b_ref, c_ref, pw_ref, seg_ref, d_ref, wglu_ref, wup_ref,
               ya_ref, xre_ref, xim_ref, st_ref, cre_s, cim_s):
    @pl.when(pl.program_id(0) == 0)
    def _():
        cre_s[...] = jnp.zeros_like(cre_s)
        cim_s[...] = jnp.zeros_like(cim_s)

    u = u_ref[...]
    ub = _dot(perm_ref[...], u.astype(BF16)).astype(BF16)
    for sg in range(SSM_SUPER):
        bu = _dot(ub[:, sg * SG_CH:(sg + 1) * SG_CH], b_ref[sg])
        xre_ref[:, sg * SG_LANES:(sg + 1) * SG_LANES] = bu[:, :SG_LANES]
        xim_ref[:, sg * SG_LANES:(sg + 1) * SG_LANES] = bu[:, SG_LANES:]

    def cmul_add(re, im, are, aim, sre, sim):
        return re + are * sre - aim * sim, im + are * sim + aim * sre

    for sg in range(SSM_SUPER):
        lanes = slice(sg * SG_LANES, (sg + 1) * SG_LANES)

        lam_re, lam_im = pw_ref[0, 0:SCAN_ROWS, lanes], pw_ref[1, 0:SCAN_ROWS, lanes]

        def local(j, carry):
            r0 = pl.multiple_of(j * SCAN_ROWS, SCAN_ROWS)
            re, im = cmul_add(xre_ref[pl.ds(r0, SCAN_ROWS), lanes], xim_ref[pl.ds(r0, SCAN_ROWS), lanes],
                              lam_re, lam_im, *carry)
            xre_ref[pl.ds(r0, SCAN_ROWS), lanes] = re
            xim_ref[pl.ds(r0, SCAN_ROWS), lanes] = im
            return re, im

        zero = jnp.zeros((SCAN_ROWS, SG_LANES), F32)
        re, im = lax.fori_loop(0, SEG, local, (zero, zero), unroll=True)

        for k, shift in enumerate((1, 2, 4)):
            re, im = cmul_add(re, im, seg_ref[2 * k, :, lanes], seg_ref[2 * k + 1, :, lanes],
                              pltpu.roll(re, shift, 0), pltpu.roll(im, shift, 0))
        cin_re, cin_im = cre_s[:, lanes], cim_s[:, lanes]
        re, im = cmul_add(re, im, seg_ref[6, :, lanes], seg_ref[7, :, lanes], cin_re, cin_im)
        cre_s[:, lanes] = re[SCAN_ROWS - 1:SCAN_ROWS, :]
        cim_s[:, lanes] = im[SCAN_ROWS - 1:SCAN_ROWS, :]
        first = lax.broadcasted_iota(jnp.int32, (SCAN_ROWS, SG_LANES), 0) == 0
        start_re = jnp.where(first, cin_re, pltpu.roll(re, 1, 0))
        start_im = jnp.where(first, cin_im, pltpu.roll(im, 1, 0))
        start_re = jnp.concatenate([start_re, start_re], axis=0)
        start_im = jnp.concatenate([start_im, start_im], axis=0)

        def fix(jj, carry):
            r0 = pl.multiple_of(jj * 2 * SCAN_ROWS, 2 * SCAN_ROWS)
            rows = pl.ds(r0, 2 * SCAN_ROWS)
            re, im = cmul_add(xre_ref[rows, lanes], xim_ref[rows, lanes],
                              pw_ref[0, rows, lanes], pw_ref[1, rows, lanes], start_re, start_im)
            st_ref[rows, 2 * sg * SG_LANES:(2 * sg + 1) * SG_LANES] = re.astype(BF16)
            st_ref[rows, (2 * sg + 1) * SG_LANES:(2 * sg + 2) * SG_LANES] = im.astype(BF16)
            return carry

        lax.fori_loop(0, SEG // 2, fix, 0, unroll=True)
    y_perm = jnp.concatenate(
        [_dot(st_ref[:, 2 * sg * SG_LANES:(2 * sg + 2) * SG_LANES], c_ref[sg])
         for sg in range(SSM_SUPER)], axis=1)
    y_hi = y_perm.astype(BF16)
    y_lo = (y_perm - y_hi.astype(F32)).astype(BF16)
    y = _dot(permT_ref[...], y_hi) + _dot(permT_ref[...], y_lo) + d_ref[...] * u
    z = jax.nn.gelu(y)
    z = z * jax.nn.sigmoid(_dot(z.astype(BF16), wglu_ref[...]))
    ya_ref[...] = _dot(z.astype(BF16), wup_ref[...])


def _s5(u, b, c, pw, seg, d, wglu, wup):
    s = u.shape[0]
    t = np.arange(S5_TT)
    perm = np.zeros((S5_TT, S5_TT), np.float32)
    perm[SCAN_ROWS * (t % SEG) + t // SEG, t] = 1.0
    permT = jnp.asarray(perm.T, BF16)
    perm = jnp.asarray(perm, BF16)
    const2 = lambda i: (0, 0)
    const3 = lambda i: (0, 0, 0)
    return pl.pallas_call(
        _s5_kernel,
        grid=(s // S5_TT,),
        in_specs=[
            pl.BlockSpec((S5_TT, SSM_WIDTH), lambda i: (i, 0)),
            pl.BlockSpec(perm.shape, const2),
            pl.BlockSpec(permT.shape, const2),
            pl.BlockSpec(b.shape, const3),
            pl.BlockSpec(c.shape, const3),
            pl.BlockSpec(pw.shape, const3),
            pl.BlockSpec(seg.shape, const3),
            pl.BlockSpec((1, SSM_WIDTH), const2),
            pl.BlockSpec(wglu.shape, const2),
            pl.BlockSpec(wup.shape, const2),
        ],
        out_specs=pl.BlockSpec((S5_TT, D_MODEL), lambda i: (i, 0)),
        out_shape=jax.ShapeDtypeStruct((s, D_MODEL), F32),
        scratch_shapes=[
            pltpu.VMEM((S5_TT, SSM_LANES), F32),
            pltpu.VMEM((S5_TT, SSM_LANES), F32),
            pltpu.VMEM((S5_TT, 2 * SSM_LANES), BF16),
            pltpu.VMEM((1, SSM_LANES), F32),
            pltpu.VMEM((1, SSM_LANES), F32),
        ],
        compiler_params=pltpu.CompilerParams(
            dimension_semantics=("arbitrary",), vmem_limit_bytes=VMEM_LIMIT),
        name="s5",
    )(u, perm, permT, b, c, pw, seg, d, wglu, wup)


def _s5_params(a_re, a_im, log_dt, b_re, b_im, c_re, c_im):
    dt = jnp.exp(log_dt.astype(F32))[:, None]
    ar, ai = a_re.astype(F32), a_im.astype(F32)
    zr, zi = ar * dt, ai * dt

    def power(n):
        mag = jnp.exp(n * zr)
        return mag * jnp.cos(n * zi), mag * jnp.sin(n * zi)

    lr, li = power(1.0)
    den = ar * ar + ai * ai
    kr = ((lr - 1.0) * ar + li * ai) / den
    ki = (li * ar - (lr - 1.0) * ai) / den
    br, bi = b_re.astype(F32), b_im.astype(F32)
    bbr = kr[..., None] * br - ki[..., None] * bi
    bbi = kr[..., None] * bi + ki[..., None] * br
    eye = jnp.eye(SSM_GROUPS, dtype=F32)
    bd_in = lambda b: jnp.einsum("gpc,gh->gchp", b, eye).reshape(SSM_WIDTH, SSM_LANES)
    bd_out = lambda c: jnp.einsum("gcp,gh->gphc", c, eye).reshape(SSM_LANES, SSM_WIDTH)
    zr, zi = zr.reshape(1, SSM_LANES), zi.reshape(1, SSM_LANES)
    pw = jnp.stack(power(jnp.repeat(jnp.arange(1, SEG + 1, dtype=F32), SCAN_ROWS)[:, None]))
    row = jnp.arange(SCAN_ROWS)[:, None]
    seg = []
    for shift in (1, 2, 4):
        pr, pi = power(float(SEG * shift))
        seg += [jnp.where(row >= shift, pr, 0.0), jnp.where(row >= shift, pi, 0.0)]
    seg += list(power(SEG * (row + 1).astype(F32)))
    b_re_bd, b_im_bd = bd_in(bbr), bd_in(bbi)
    c_re_bd, c_im_bd = bd_out(c_re.astype(F32)), bd_out(c_im.astype(F32))
    b_sg, c_sg = [], []
    for sg in range(SSM_SUPER):
        ch = slice(sg * SG_CH, (sg + 1) * SG_CH)
        st = slice(sg * SG_LANES, (sg + 1) * SG_LANES)
        b_sg.append(jnp.concatenate([b_re_bd[ch, st], b_im_bd[ch, st]], axis=1))
        c_sg.append(jnp.concatenate([c_re_bd[st, ch], -c_im_bd[st, ch]], axis=0))
    return jnp.stack(b_sg).astype(BF16), jnp.stack(c_sg).astype(BF16), pw, jnp.stack(seg)


def _compress_kernel(xk_ref, xv_ref, w1k_ref, w2k_ref, pk_ref, w1v_ref, w2vT_ref, pv_ref, kc_ref, vcT_ref):
    half = CMP_STRIDE * HEAD_DIM

    def hidden(x_ref, w1_ref, pos_ref):
        x = x_ref[0].astype(BF16)
        first = _dot(x, w1_ref[:half, :])
        second = _dot(x, w1_ref[half:, :])
        bias = _dot(jnp.broadcast_to(pos_ref[...], (8, 2 * half)).astype(BF16), w1_ref[...])[:1]
        pre = first + pltpu.roll(second, N_CHUNK - 1, 0) + bias
        return jax.nn.gelu(pre).astype(BF16)

    hk = hidden(xk_ref, w1k_ref, pk_ref)
    kc = _dot(hk, w2k_ref[...])
    tail_pad = N_CMP_PAD - CMP_PAD - N_CHUNK
    kc_ref[0] = jnp.concatenate([jnp.zeros((CMP_PAD, HEAD_DIM), F32), kc,
                                 jnp.zeros((tail_pad, HEAD_DIM), F32)], axis=0).astype(BF16)
    hv = hidden(xv_ref, w1v_ref, pv_ref)
    vcT = _dot_nt(w2vT_ref[...], hv)
    vcT_ref[0] = jnp.concatenate([jnp.zeros((HEAD_DIM, CMP_PAD), F32), vcT,
                                  jnp.zeros((HEAD_DIM, tail_pad), F32)], axis=1).astype(BF16)


def _compress(xk, xv, w1k, w2k, pk, w1v, w2vT, pv):
    const2 = lambda g: (0, 0)
    head = lambda g: (g, 0, 0)
    return pl.pallas_call(
        _compress_kernel,
        grid=(NSA_KV_HEADS,),
        in_specs=[
            pl.BlockSpec((1, N_CHUNK, CMP_STRIDE * HEAD_DIM), head),
            pl.BlockSpec((1, N_CHUNK, CMP_STRIDE * HEAD_DIM), head),
            pl.BlockSpec(w1k.shape, const2),
            pl.BlockSpec(w2k.shape, const2),
            pl.BlockSpec(pk.shape, const2),
            pl.BlockSpec(w1v.shape, const2),
            pl.BlockSpec(w2vT.shape, const2),
            pl.BlockSpec(pv.shape, const2),
        ],
        out_specs=[
            pl.BlockSpec((1, N_CMP_PAD, HEAD_DIM), head),
            pl.BlockSpec((1, HEAD_DIM, N_CMP_PAD), head),
        ],
        out_shape=[
            jax.ShapeDtypeStruct((NSA_KV_HEADS, N_CMP_PAD, HEAD_DIM), BF16),
            jax.ShapeDtypeStruct((NSA_KV_HEADS, HEAD_DIM, N_CMP_PAD), BF16),
        ],
        compiler_params=pltpu.CompilerParams(
            dimension_semantics=("arbitrary",), vmem_limit_bytes=VMEM_LIMIT),
        name="compress",
    )(xk, xv, w1k, w2k, pk, w1v, w2vT, pv)


def _nsa_kernel(qT_ref, kc_ref, vcT_ref, ks_ref, vsT_ref, kw_ref, vwT_ref, bias_ref, bc_ref, mt_ref, gT_ref,
                oT_ref, sc_ref, neg_ref, negfar_ref, m_ref, acc_ref, tot_ref, sbuf0_ref, sbuf1_ref, mloc_ref, pslc_ref):
    i = pl.program_id(1)
    s0 = i * TQ
    qT = jnp.concatenate([qT_ref[r * HEAD_DIM:(r + 1) * HEAD_DIM, :] for r in range(GQA)], axis=1)

    qT_nomask = jnp.concatenate([qT, jnp.zeros((K_COLS - HEAD_DIM, QL), BF16)], axis=0)

    gate0 = pl.program_id(0) * GATES_PER_KV_HEAD

    def gate_row(branch):
        return jnp.concatenate([gT_ref[pl.ds(gate0 + r * 3 + branch, 1), :] for r in range(GQA)], axis=1)

    def reset():
        m_ref[...] = jnp.full_like(m_ref, M_FLOOR)
        acc_ref[...] = jnp.zeros_like(acc_ref)

    def attend(k_ref, vT_ref, start, size, add):
        k = k_ref[0, pl.ds(start, size), :]
        s = _dot(k, qT if k.shape[1] == HEAD_DIM else qT_nomask) + add
        m_prev = m_ref[...]
        m_new = jnp.maximum(m_prev, jnp.max(s, axis=0, keepdims=True))
        alpha = jnp.exp2(m_prev - m_new)
        p = jnp.exp2(s - m_new).astype(BF16)
        acc_ref[...] = alpha * acc_ref[...] + _dot(vT_ref[0, :, pl.ds(start, size)], p)
        m_ref[...] = m_new

    def finish(branch):
        acc = acc_ref[...]
        scale = gate_row(branch) / jnp.maximum(acc[HEAD_DIM:HEAD_DIM + 1, :], 1e-30)
        tot_ref[...] += acc[:HEAD_DIM, :] * scale

    n0 = i * (TQ // CMP_STRIDE)
    band0 = pl.multiple_of(n0 + CMP_PAD - 16, 8)

    def cmp_branch(nrows):
        row = lax.broadcasted_iota(jnp.int32, (nrows, QL), 0)
        live = (row >= CMP_PAD) & (row < band0 + CMP_BAND)
        sc_ref[0:nrows, :] = jnp.where(live, _dot(kc_ref[0, 0:nrows, :], qT), NEG)
        sc_ref[pl.ds(band0, CMP_BAND), :] += bc_ref[0]
        sc = sc_ref[0:nrows, :]
        mc = jnp.maximum(jnp.max(sc, axis=0, keepdims=True), M_FLOOR)
        pc = jnp.exp2(sc - mc)
        pc = pc * (1.0 / jnp.maximum(jnp.sum(pc, axis=0, keepdims=True), 1e-30))
        tot_ref[...] = _dot(vcT_ref[0, :, 0:nrows], pc.astype(BF16)) * gate_row(0)
        imp = pc[:, 0:TQ]
        for r in range(1, GQA):
            imp = imp + pc[:, r * TQ:(r + 1) * TQ]
        mt = mt_ref[:, 0:nrows]
        p_slc = jnp.zeros((N_SLC, TQ), F32)
        rem = imp
        for _ in range(2):
            piece = rem.astype(BF16)
            p_slc = p_slc + _dot(mt, piece)
            rem = rem - piece.astype(F32)
        pslc_ref[...] = p_slc

    prev_rows = 0
    for nrows in CMP_EXTENTS:
        lo, hi = prev_rows, nrows
        pl.when((band0 + CMP_BAND > lo) & (band0 + CMP_BAND <= hi))(functools.partial(cmp_branch, nrows))
        prev_rows = nrows

    reset()
    win_start = jnp.maximum(s0 - WINDOW, 0)
    win_rows = pl.ds(pl.multiple_of(WINDOW - (s0 - win_start), NEAR), WINDOW + TQ)
    attend(kw_ref, vwT_ref, pl.multiple_of(win_start, NEAR), WINDOW + TQ, bias_ref[0, win_rows, :])
    finish(2)

    p_slc = pslc_ref[...]
    blk = lax.broadcasted_iota(jnp.int32, (N_SLC, TQ), 0)
    cur = (s0 + lax.broadcasted_iota(jnp.int32, (N_SLC, TQ), 1)) // SLC_LEN
    valid = blk <= cur
    forced = valid & ((blk == 0) | (blk >= cur - (N_LOCAL - 1)))
    score = jnp.where(forced, -jnp.inf, jnp.where(valid, p_slc, -BIG))
    blk_f = blk.astype(F32)
    for _ in range(N_SEL - (N_LOCAL + 1)):
        best = jnp.max(score, axis=0, keepdims=True)
        first = jnp.min(jnp.where(score == best, blk_f, float(N_SLC)), axis=0, keepdims=True)
        score = jnp.where(blk_f == first, -jnp.inf, score)
    neg = jnp.where(score == -jnp.inf, 0.0, NEG)
    neg_ref[...] = jnp.concatenate([neg] * GQA, axis=1)
    near_blk = (s0 - NEAR) // SLC_LEN
    negfar_ref[...] = jnp.concatenate([jnp.where(blk >= near_blk, NEG, neg)] * GQA, axis=1)

    def block_mask(ref, j0, nblk):
        return jnp.concatenate(
            [jnp.broadcast_to(ref[pl.ds(j0 + b, 1), :], (SLC_LEN, QL)) for b in range(nblk)], axis=0)

    reset()
    n_far = (s0 - NEAR + FAR_KEYS - 1) // FAR_KEYS

    sbufs = (sbuf0_ref, sbuf1_ref)

    def far_logits(c, slot):
        start = pl.multiple_of(c * FAR_KEYS, FAR_KEYS)
        mask_rows = negfar_ref[pl.ds(pl.multiple_of(c * FAR_BLOCKS, FAR_BLOCKS), FAR_BLOCKS), :]
        extra = jnp.concatenate([mask_rows, jnp.zeros((K_COLS - HEAD_DIM - FAR_BLOCKS, QL), F32)], axis=0)
        q_masked = jnp.concatenate([qT, extra.astype(BF16)], axis=0)
        s = _dot(ks_ref[0, pl.ds(start, FAR_KEYS), :], q_masked)
        sbufs[slot][...] = s
        mloc_ref[slot] = jnp.max(s, axis=0, keepdims=True)

    def far_consume(c, slot):
        start = pl.multiple_of(c * FAR_KEYS, FAR_KEYS)
        m_prev = m_ref[...]
        m_new = jnp.maximum(m_prev, mloc_ref[slot])
        alpha = jnp.exp2(m_prev - m_new)
        p = jnp.exp2(sbufs[slot][...] - m_new).astype(BF16)
        acc_ref[...] = alpha * acc_ref[...] + _dot(vsT_ref[0, :, pl.ds(start, FAR_KEYS)], p)
        m_ref[...] = m_new

    @pl.when(i >= 1)
    def _():
        n_pairs = (n_far + 1) // 2
        far_logits(0, 0)
        attend(ks_ref, vsT_ref, pl.multiple_of(s0 - NEAR, NEAR), NEAR + TQ,
               bias_ref[0, WINDOW - NEAR:WINDOW + TQ, :] + block_mask(neg_ref, near_blk, (NEAR + TQ) // SLC_LEN))

        def far_body(p, carry):
            far_logits(2 * p + 1, 1)
            far_consume(2 * p, 0)
            far_logits(2 * p + 2, 0)
            far_consume(2 * p + 1, 1)
            return carry

        lax.fori_loop(0, n_pairs - 1, far_body, 0)
        last = 2 * (n_pairs - 1)

        @pl.when(n_far % 2 == 0)
        def _():
            far_logits(last + 1, 1)
            far_consume(last, 0)
            far_consume(last + 1, 1)

        @pl.when(n_far % 2 == 1)
        def _():
            far_consume(last, 0)

    @pl.when(i == 0)
    def _():
        attend(ks_ref, vsT_ref, 0, TQ, bias_ref[0, WINDOW:WINDOW + TQ, :] + block_mask(neg_ref, 0, TQ // SLC_LEN))

    finish(1)

    tot = tot_ref[...]
    for r in range(GQA):
        oT_ref[r * HEAD_DIM:(r + 1) * HEAD_DIM, :] = tot[:, r * TQ:(r + 1) * TQ]


def _nsa(qT, kc, vcT, ks, vsT, kw, vwT, bias_tile, bias_cmp, mt, gnT):
    s = qT.shape[1]
    head3 = lambda g, i: (g, 0, 0)
    return pl.pallas_call(
        _nsa_kernel,
        grid=(NSA_KV_HEADS, s // TQ),
        in_specs=[
            pl.BlockSpec((GQA * HEAD_DIM, TQ), lambda g, i: (g, i)),
            pl.BlockSpec((1, N_CMP_PAD, HEAD_DIM), head3),
            pl.BlockSpec((1, HEAD_DIM, N_CMP_PAD), head3),
            pl.BlockSpec((1, s, K_COLS), head3),
            pl.BlockSpec((1, V_ROWS, s), head3),
            pl.BlockSpec((1, s, HEAD_DIM), head3),
            pl.BlockSpec((1, V_ROWS, s), head3),
            pl.BlockSpec((1, 2 * WINDOW + TQ, QL), head3, pipeline_mode=pl.Buffered(1)),
            pl.BlockSpec((1, CMP_BAND, QL), head3),
            pl.BlockSpec(mt.shape, lambda g, i: (0, 0)),
            pl.BlockSpec((GATE_ROWS, TQ), lambda g, i: (0, i)),
        ],
        out_specs=pl.BlockSpec((GQA * HEAD_DIM, TQ), lambda g, i: (g, i)),
        out_shape=jax.ShapeDtypeStruct((NSA_WIDTH, s), F32),
        scratch_shapes=[
            pltpu.VMEM((N_CMP_PAD, QL), F32),
            pltpu.VMEM((N_SLC, QL), F32),
            pltpu.VMEM((N_SLC, QL), F32),
            pltpu.VMEM((1, QL), F32),
            pltpu.VMEM((V_ROWS, QL), F32),
            pltpu.VMEM((HEAD_DIM, QL), F32),
            pltpu.VMEM((FAR_KEYS, QL), F32),
            pltpu.VMEM((FAR_KEYS, QL), F32),
            pltpu.VMEM((2, 1, QL), F32),
            pltpu.VMEM((N_SLC, TQ), F32),
        ],
        compiler_params=pltpu.CompilerParams(
            dimension_semantics=("arbitrary", "arbitrary"), vmem_limit_bytes=VMEM_LIMIT),
        name="nsa",
    )(qT, kc, vcT, ks, vsT, kw, vwT, bias_tile, bias_cmp, mt, gnT)


def _t5_bucket(dist):
    n = jnp.maximum(dist, 0)
    max_exact = REL_BUCKETS // 2
    nf = jnp.maximum(n, 1).astype(F32)
    large = max_exact + (jnp.log(nf / max_exact) / math.log(REL_MAX_DIST / max_exact)
                         * (REL_BUCKETS - max_exact)).astype(jnp.int32)
    large = jnp.minimum(large, REL_BUCKETS - 1)
    return jnp.where(n < max_exact, n, large)


def _bias_tiles(rel_bias):
    tab = rel_bias.astype(F32)
    tab = (tab[_t5_bucket(jnp.arange(NEAR))] - tab[REL_BUCKETS - 1]).T * LOG2E
    tab = jnp.concatenate([tab, jnp.zeros((NSA_HEADS, 1), F32)], axis=1)

    def by_distance(d):
        return jnp.where(d >= 0, tab[:, jnp.clip(d, 0, NEAR)], NEG)

    def toeplitz(c, nk, nq, step=1):
        n = step * (nk - 1) + nq
        w = by_distance(jnp.arange(n) + c - step * (nk - 1))
        reps = -(-nk * (n + step) // n)
        return jnp.tile(w, (1, reps))[:, :nk * (n + step)].reshape(NSA_HEADS, nk, n + step)[:, ::-1, :nq]

    def per_kv_head(a):
        a = a.reshape(NSA_KV_HEADS, GQA, a.shape[1], TQ)
        return jnp.transpose(a, (0, 2, 1, 3)).reshape(NSA_KV_HEADS, a.shape[2], QL)

    k2, q2 = np.arange(NEAR)[:, None], np.arange(NEAR)[None, :]
    const = lambda a: jnp.broadcast_to(jnp.asarray(a, F32), (NSA_HEADS, NEAR, NEAR))
    edge = WINDOW // NEAR
    nb = TQ // NEAR
    blocks, slabs = {}, {}

    def block(d):
        d = min(max(d, -1), edge + 1)
        if d not in blocks:
            if d < 0:
                b = const(NEG)
            elif d < 2:
                b = toeplitz(d * NEAR, NEAR, NEAR)
            elif d < edge:
                b = const(0.0)
            else:
                b = const(np.where(k2 > q2, 0.0, NEG) if d == edge else NEG)
            blocks[d] = jnp.transpose(b.reshape(NSA_KV_HEADS, GQA, NEAR, NEAR), (0, 2, 1, 3))
        return blocks[d]

    def slab(dd):
        dd = min(max(dd, -nb), edge + 1)
        if dd not in slabs:
            s = jnp.stack([block(dd + b) for b in range(nb)], axis=3)
            slabs[dd] = s.reshape(NSA_KV_HEADS, NEAR, QL)
        return slabs[dd]

    tile = jnp.concatenate([slab(edge - a) for a in range((2 * WINDOW + TQ) // NEAR)], axis=1)
    cmp_band = toeplitz(16 * CMP_STRIDE - (CMP_LEN - 1), CMP_BAND, TQ, step=CMP_STRIDE)
    return tile, per_kv_head(cmp_band)


def _overlap_matrix():
    ratio = SLC_LEN // CMP_STRIDE
    front = CMP_LEN // CMP_STRIDE - 1
    w_ov = np.convolve(np.ones(ratio), np.ones(CMP_LEN // CMP_STRIDE))
    mt = np.zeros((N_SLC, N_CMP_PAD), np.float32)
    for j in range(N_SLC):
        for o, w in enumerate(w_ov):
            n = ratio * j + o - front
            if 0 <= n < N_CMP:
                mt[j, CMP_PAD + n] = w
    return jnp.asarray(mt, BF16)


def _tail_kernel(x_ref, ya_ref, oT_ref, gb_ref, wup_ref, wout_ref, g_ref, wg_ref, wu_ref, wd_ref, gf_ref, o_ref):
    yb = _dot(oT_ref[...].T.astype(BF16), wup_ref[...])
    mix = gb_ref[:, :D_MODEL].astype(F32) * ya_ref[...] + gb_ref[:, D_MODEL:].astype(F32) * yb
    x = x_ref[...] + _dot(mix.astype(BF16), wout_ref[...])
    h = _rms(x, g_ref[...]).astype(BF16)
    f = jax.nn.silu(_dot(h, wg_ref[...])) * _dot(h, wu_ref[...])
    x = x + _dot(f.astype(BF16), wd_ref[...])
    o_ref[...] = _rms(x, gf_ref[...])


def _tail(x2, ya, oT, gb, wup, wout, g, wg, wu, wd, gf, tm=256):
    s = x2.shape[0]
    row = lambda i: (i, 0)
    const = lambda i: (0, 0)
    return pl.pallas_call(
        _tail_kernel,
        grid=(s // tm,),
        in_specs=[
            pl.BlockSpec((tm, D_MODEL), row),
            pl.BlockSpec((tm, D_MODEL), row),
            pl.BlockSpec((NSA_WIDTH, tm), lambda i: (0, i)),
            pl.BlockSpec((tm, 2 * D_MODEL), row),
            pl.BlockSpec(wup.shape, const),
            pl.BlockSpec(wout.shape, const),
            pl.BlockSpec((1, D_MODEL), const),
            pl.BlockSpec(wg.shape, const),
            pl.BlockSpec(wu.shape, const),
            pl.BlockSpec(wd.shape, const),
            pl.BlockSpec((1, D_MODEL), const),
        ],
        out_specs=pl.BlockSpec((tm, D_MODEL), row),
        out_shape=jax.ShapeDtypeStruct((s, D_MODEL), F32),
        compiler_params=pltpu.CompilerParams(
            dimension_semantics=("arbitrary",), vmem_limit_bytes=VMEM_LIMIT),
        name="tail",
    )(x2, ya, oT, gb, wup, wout, g, wg, wu, wd, gf)


def kernel(x, norm_mix_g, w_in, ssm_a_re, ssm_a_im, ssm_log_dt, ssm_b_re, ssm_b_im, ssm_c_re, ssm_c_im, ssm_d, ssm_w_glu, w_up_ssm, cmp_pos_k, cmp_pos_v, cmp_w1_k, cmp_w2_k, cmp_w1_v, cmp_w2_v, rel_bias, w_up_nsa, w_out, norm_ffn_g, w_ffn_gate, w_ffn_up, w_ffn_down, norm_final_g):
    bsz, s, _ = x.shape
    assert (bsz, s) == (1, SEQ) and w_in.shape[0] == 1
    x2 = x.reshape(s, D_MODEL)
    l = 0
    row = lambda v: v.astype(F32).reshape(1, -1)

    u, qT, kcr, vcr, ks, kw, vsT, vwT, gnT, gb = _inproj(x2, row(norm_mix_g[l]), w_in[l].astype(F32))

    b_sg, c_sg, pw, seg = _s5_params(
        ssm_a_re[l], ssm_a_im[l], ssm_log_dt[l], ssm_b_re[l], ssm_b_im[l], ssm_c_re[l], ssm_c_im[l])
    ya = _s5(u, b_sg, c_sg, pw, seg, row(ssm_d[l]), ssm_w_glu[l].astype(BF16), w_up_ssm[l].astype(BF16))

    kc, vcT = _compress(kcr, vcr,
                        cmp_w1_k[l].astype(BF16), cmp_w2_k[l].astype(BF16), row(cmp_pos_k[l]),
                        cmp_w1_v[l].astype(BF16), cmp_w2_v[l].T.astype(BF16), row(cmp_pos_v[l]))

    bias_tile, bias_cmp = _bias_tiles(rel_bias)
    oT = _nsa(qT, kc, vcT, ks, vsT, kw, vwT, bias_tile, bias_cmp, _overlap_matrix(), gnT)

    out = _tail(x2, ya, oT, gb, w_up_nsa[l].astype(BF16), w_out[l].astype(BF16),
                row(norm_ffn_g[l]), w_ffn_gate[l].astype(BF16), w_ffn_up[l].astype(BF16),
                w_ffn_down[l].astype(BF16), row(norm_final_g))
    return out.reshape(bsz, s, D_MODEL)
```

```python
import functools
import math

import numpy as np
import jax
import jax.numpy as jnp
from jax import lax
from jax.experimental import pallas as pl
from jax.experimental.pallas import tpu as pltpu

F32 = jnp.float32
BF16 = jnp.bfloat16

D_MODEL = 1024
SEQ = 16384
EPS = 1e-6
SSM_WIDTH = 512
SSM_GROUP = 16
SSM_GROUPS = SSM_WIDTH // SSM_GROUP
SSM_STATE = 64
SSM_LANES = SSM_GROUPS * SSM_STATE
NSA_HEADS = 8
NSA_KV_HEADS = 2
GQA = NSA_HEADS // NSA_KV_HEADS
HEAD_DIM = 64
NSA_WIDTH = NSA_HEADS * HEAD_DIM
KV_WIDTH = NSA_KV_HEADS * HEAD_DIM
CMP_LEN = 32
CMP_STRIDE = 16
CMP_HIDDEN = 256
SLC_LEN = 64
N_SEL = 16
N_LOCAL = 2
WINDOW = 512
BIG = 1e4
REL_BUCKETS = 32
REL_MAX_DIST = 128
D_FF = 2816

N_CHUNK = SEQ // CMP_STRIDE
N_CMP = (SEQ - CMP_LEN) // CMP_STRIDE + 1
N_SLC = SEQ // SLC_LEN
TQ = 256
QL = GQA * TQ
NEAR = 128
assert TQ > NEAR and TQ % NEAR == 0
CMP_PAD = 16
CMP_ROW_STEP = 128
N_CMP_PAD = -(-(CMP_PAD + N_CHUNK) // CMP_ROW_STEP) * CMP_ROW_STEP
CMP_BAND = 16 + TQ // CMP_STRIDE
NEG = -1e30
M_FLOOR = -1e29
SCAN_ROWS = 8
S5_TT = 256
SEG = S5_TT // SCAN_ROWS
SSM_SUPER = 2
SG_CH = SSM_WIDTH // SSM_SUPER
SG_LANES = SSM_LANES // SSM_SUPER
LOG2E = math.log2(math.e)
V_ROWS = HEAD_DIM + 16
FAR_KEYS = 512
FAR_BLOCKS = FAR_KEYS // SLC_LEN
K_COLS = HEAD_DIM + 16
CMP_EXTENTS = tuple(range(2 * CMP_ROW_STEP, N_CMP_PAD + 1, CMP_ROW_STEP))

VMEM_LIMIT = 56 * 1024 * 1024

COL_U = 0
COL_Q = 512
COL_KC = 1024
COL_VC = 1152
COL_KS = 1280
COL_VS = 1408
COL_KW = 1536
COL_VW = 1664
COL_GN = 1792
COL_GB = 1816
COL_END = 3864
GATE_COLS = 128
GATE_ROWS = 32
GATES_PER_KV_HEAD = GQA * 3


def _rms(x, g):
    return x * lax.rsqrt(jnp.mean(x * x, axis=-1, keepdims=True) + EPS) * g


def _dot(a, b):
    return jnp.dot(a, b, preferred_element_type=F32)


def _dot_nt(a, b):
    return lax.dot_general(a, b, (((1,), (1,)), ((), ())), preferred_element_type=F32)


def _inproj_kernel(x_ref, g_ref, w_ref,
                   u_ref, qT_ref, kcr_ref, vcr_ref, ks_ref, kw_ref, vsT_ref, vwT_ref, gnT_ref, gb_ref,
                   wm_ref, wn_ref, wb_ref, kstage_ref, vstage_ref):
    @pl.when(pl.program_id(0) == 0)
    def _():
        wm_ref[...] = w_ref[:COL_GN, :].astype(BF16)
        wn_ref[...] = w_ref[COL_GN:COL_GN + GATE_COLS, :].astype(BF16)
        wb_ref[...] = w_ref[COL_GB:, :].astype(BF16)

    h = _rms(x_ref[...], g_ref[...]).astype(BF16)
    tm = h.shape[0]
    proj = lambda lo, hi: _dot_nt(h, wm_ref[lo:hi, :])
    projT = lambda lo, hi: _dot_nt(wm_ref[lo:hi, :], h)
    u_ref[...] = proj(COL_U, COL_Q)
    qT_ref[...] = (projT(COL_Q, COL_KC) * (HEAD_DIM ** -0.5 * LOG2E)).astype(BF16)
    pm = proj(COL_KC, COL_VS)
    pm_kw = proj(COL_KW, COL_VW)
    vsT = projT(COL_VS, COL_KW)
    vwT = projT(COL_VW, COL_GN)
    ones = jnp.ones((V_ROWS - HEAD_DIM, tm), F32)
    tok = pl.program_id(0) * tm + lax.broadcasted_iota(jnp.int32, (tm, K_COLS - HEAD_DIM), 0)
    col = lax.broadcasted_iota(jnp.int32, (tm, K_COLS - HEAD_DIM), 1)
    blk_onehot = jnp.where((tok // SLC_LEN) % FAR_BLOCKS == col, 1.0, 0.0)
    for g in range(NSA_KV_HEADS):
        lo = g * HEAD_DIM
        ks_ref[g] = jnp.concatenate(
            [pm[:, COL_KS - COL_KC + lo:COL_KS - COL_KC + lo + HEAD_DIM], blk_onehot], axis=1).astype(BF16)
        kw_ref[g] = pm_kw[:, lo:lo + HEAD_DIM].astype(BF16)
        vsT_ref[g] = jnp.concatenate([vsT[lo:lo + HEAD_DIM], ones], axis=0).astype(BF16)
        vwT_ref[g] = jnp.concatenate([vwT[lo:lo + HEAD_DIM], ones], axis=0).astype(BF16)
    for src_col, stage_ref, dst_ref in ((COL_KC, kstage_ref, kcr_ref), (COL_VC, vstage_ref, vcr_ref)):
        stage_ref[...] = pm[:, src_col - COL_KC:src_col - COL_KC + KV_WIDTH]
        for t in range(CMP_STRIDE):
            rows = stage_ref[pl.ds(t, tm // CMP_STRIDE, stride=CMP_STRIDE), :]
            for g in range(NSA_KV_HEADS):
                dst_ref[g, :, t * HEAD_DIM:(t + 1) * HEAD_DIM] = rows[:, g * HEAD_DIM:(g + 1) * HEAD_DIM]
    gnT_ref[...] = jax.nn.sigmoid(_dot_nt(wn_ref[0:GATE_ROWS, :], h))
    gb_ref[...] = jax.nn.sigmoid(_dot_nt(h, wb_ref[...])).astype(BF16)


def _inproj(x2, g, w, tm=512):
    s = x2.shape[0]
    const = lambda i: (0, 0)
    row = lambda i: (i, 0)
    col = lambda i: (0, i)
    return pl.pallas_call(
        _inproj_kernel,
        grid=(s // tm,),
        in_specs=[
            pl.BlockSpec((tm, D_MODEL), row),
            pl.BlockSpec((1, D_MODEL), const),
            pl.BlockSpec(w.shape, const),
        ],
        out_specs=[
            pl.BlockSpec((tm, SSM_WIDTH), row),
            pl.BlockSpec((NSA_WIDTH, tm), col),
            pl.BlockSpec((NSA_KV_HEADS, tm // CMP_STRIDE, CMP_STRIDE * HEAD_DIM), lambda i: (0, i, 0)),
            pl.BlockSpec((NSA_KV_HEADS, tm // CMP_STRIDE, CMP_STRIDE * HEAD_DIM), lambda i: (0, i, 0)),
            pl.BlockSpec((NSA_KV_HEADS, tm, K_COLS), lambda i: (0, i, 0)),
            pl.BlockSpec((NSA_KV_HEADS, tm, HEAD_DIM), lambda i: (0, i, 0)),
            pl.BlockSpec((NSA_KV_HEADS, V_ROWS, tm), lambda i: (0, 0, i)),
            pl.BlockSpec((NSA_KV_HEADS, V_ROWS, tm), lambda i: (0, 0, i)),
            pl.BlockSpec((GATE_ROWS, tm), col),
            pl.BlockSpec((tm, 2 * D_MODEL), row),
        ],
        out_shape=[
            jax.ShapeDtypeStruct((s, SSM_WIDTH), F32),
            jax.ShapeDtypeStruct((NSA_WIDTH, s), BF16),
            jax.ShapeDtypeStruct((NSA_KV_HEADS, s // CMP_STRIDE, CMP_STRIDE * HEAD_DIM), F32),
            jax.ShapeDtypeStruct((NSA_KV_HEADS, s // CMP_STRIDE, CMP_STRIDE * HEAD_DIM), F32),
            jax.ShapeDtypeStruct((NSA_KV_HEADS, s, K_COLS), BF16),
            jax.ShapeDtypeStruct((NSA_KV_HEADS, s, HEAD_DIM), BF16),
            jax.ShapeDtypeStruct((NSA_KV_HEADS, V_ROWS, s), BF16),
            jax.ShapeDtypeStruct((NSA_KV_HEADS, V_ROWS, s), BF16),
            jax.ShapeDtypeStruct((GATE_ROWS, s), F32),
            jax.ShapeDtypeStruct((s, 2 * D_MODEL), BF16),
        ],
        scratch_shapes=[
            pltpu.VMEM((COL_GN, D_MODEL), BF16),
            pltpu.VMEM((GATE_COLS, D_MODEL), BF16),
            pltpu.VMEM((COL_END - COL_GB, D_MODEL), BF16),
            pltpu.VMEM((tm, KV_WIDTH), F32),
            pltpu.VMEM((tm, KV_WIDTH), F32),
        ],
        compiler_params=pltpu.CompilerParams(
            dimension_semantics=("arbitrary",), vmem_limit_bytes=VMEM_LIMIT),
        name="inproj",
    )(x2, g, w)


def _s5_kernel(u_ref, perm_ref, permT_ref, b_ref, c_ref, pw_ref, seg_ref, d_ref, wglu_ref, wup_ref,
               ya_ref, xre_ref, xim_ref, st_ref, cre_s, cim_s):
    @pl.when(pl.program_id(0) == 0)
    def _():
        cre_s[...] = jnp.zeros_like(cre_s)
        cim_s[...] = jnp.zeros_like(cim_s)

    u = u_ref[...]
    ub = _dot(perm_ref[...], u.astype(BF16)).astype(BF16)
    for sg in range(SSM_SUPER):
        bu = _dot(ub[:, sg * SG_CH:(sg + 1) * SG_CH], b_ref[sg])
        xre_ref[:, sg * SG_LANES:(sg + 1) * SG_LANES] = bu[:, :SG_LANES]
        xim_ref[:, sg * SG_LANES:(sg + 1) * SG_LANES] = bu[:, SG_LANES:]

    def cmul_add(re, im, are, aim, sre, sim):
        return re + are * sre - aim * sim, im + are * sim + aim * sre

    for sg in range(SSM_SUPER):
        lanes = slice(sg * SG_LANES, (sg + 1) * SG_LANES)

        lam_re, lam_im = pw_ref[0, 0:SCAN_ROWS, lanes], pw_ref[1, 0:SCAN_ROWS, lanes]

        def local(j, carry):
            r0 = pl.multiple_of(j * SCAN_ROWS, SCAN_ROWS)
            re, im = cmul_add(xre_ref[pl.ds(r0, SCAN_ROWS), lanes], xim_ref[pl.ds(r0, SCAN_ROWS), lanes],
                              lam_re, lam_im, *carry)
            xre_ref[pl.ds(r0, SCAN_ROWS), lanes] = re
            xim_ref[pl.ds(r0, SCAN_ROWS), lanes] = im
            return re, im

        zero = jnp.zeros((SCAN_ROWS, SG_LANES), F32)
        re, im = lax.fori_loop(0, SEG, local, (zero, zero), unroll=True)

        for k, shift in enumerate((1, 2, 4)):
            re, im = cmul_add(re, im, seg_ref[2 * k, :, lanes], seg_ref[2 * k + 1, :, lanes],
                              pltpu.roll(re, shift, 0), pltpu.roll(im, shift, 0))
        cin_re, cin_im = cre_s[:, lanes], cim_s[:, lanes]
        re, im = cmul_add(re, im, seg_ref[6, :, lanes], seg_ref[7, :, lanes], cin_re, cin_im)
        cre_s[:, lanes] = re[SCAN_ROWS - 1:SCAN_ROWS, :]
        cim_s[:, lanes] = im[SCAN_ROWS - 1:SCAN_ROWS, :]
        first = lax.broadcasted_iota(jnp.int32, (SCAN_ROWS, SG_LANES), 0) == 0
        start_re = jnp.where(first, cin_re, pltpu.roll(re, 1, 0))
        start_im = jnp.where(first, cin_im, pltpu.roll(im, 1, 0))
        start_re = jnp.concatenate([start_re, start_re], axis=0)
        start_im = jnp.concatenate([start_im, start_im], axis=0)

        def fix(jj, carry):
            r0 = pl.multiple_of(jj * 2 * SCAN_ROWS, 2 * SCAN_ROWS)
            rows = pl.ds(r0, 2 * SCAN_ROWS)
            re, im = cmul_add(xre_ref[rows, lanes], xim_ref[rows, lanes],
                              pw_ref[0, rows, lanes], pw_ref[1, rows, lanes], start_re, start_im)
            st_ref[rows, 2 * sg * SG_LANES:(2 * sg + 1) * SG_LANES] = re.astype(BF16)
            st_ref[rows, (2 * sg + 1) * SG_LANES:(2 * sg + 2) * SG_LANES] = im.astype(BF16)
            return carry

        lax.fori_loop(0, SEG // 2, fix, 0, unroll=True)
    y_perm = jnp.concatenate(
        [_dot(st_ref[:, 2 * sg * SG_LANES:(2 * sg + 2) * SG_LANES], c_ref[sg])
         for sg in range(SSM_SUPER)], axis=1)
    y_hi = y_perm.astype(BF16)
    y_lo = (y_perm - y_hi.astype(F32)).astype(BF16)
    y = _dot(permT_ref[...], y_hi) + _dot(permT_ref[...], y_lo) + d_ref[...] * u
    z = jax.nn.gelu(y)
    z = z * jax.nn.sigmoid(_dot(z.astype(BF16), wglu_ref[...]))
    ya_ref[...] = _dot(z.astype(BF16), wup_ref[...])


def _s5(u, b, c, pw, seg, d, wglu, wup):
    s = u.shape[0]
    t = np.arange(S5_TT)
    perm = np.zeros((S5_TT, S5_TT), np.float32)
    perm[SCAN_ROWS * (t % SEG) + t // SEG, t] = 1.0
    permT = jnp.asarray(perm.T, BF16)
    perm = jnp.asarray(perm, BF16)
    const2 = lambda i: (0, 0)
    const3 = lambda i: (0, 0, 0)
    return pl.pallas_call(
        _s5_kernel,
        grid=(s // S5_TT,),
        in_specs=[
            pl.BlockSpec((S5_TT, SSM_WIDTH), lambda i: (i, 0)),
            pl.BlockSpec(perm.shape, const2),
            pl.BlockSpec(permT.shape, const2),
            pl.BlockSpec(b.shape, const3),
            pl.BlockSpec(c.shape, const3),
            pl.BlockSpec(pw.shape, const3),
            pl.BlockSpec(seg.shape, const3),
            pl.BlockSpec((1, SSM_WIDTH), const2),
            pl.BlockSpec(wglu.shape, const2),
            pl.BlockSpec(wup.shape, const2),
        ],
        out_specs=pl.BlockSpec((S5_TT, D_MODEL), lambda i: (i, 0)),
        out_shape=jax.ShapeDtypeStruct((s, D_MODEL), F32),
        scratch_shapes=[
            pltpu.VMEM((S5_TT, SSM_LANES), F32),
            pltpu.VMEM((S5_TT, SSM_LANES), F32),
            pltpu.VMEM((S5_TT, 2 * SSM_LANES), BF16),
            pltpu.VMEM((1, SSM_LANES), F32),
            pltpu.VMEM((1, SSM_LANES), F32),
        ],
        compiler_params=pltpu.CompilerParams(
            dimension_semantics=("arbitrary",), vmem_limit_bytes=VMEM_LIMIT),
        name="s5",
    )(u, perm, permT, b, c, pw, seg, d, wglu, wup)


def _s5_params(a_re, a_im, log_dt, b_re, b_im, c_re, c_im):
    dt = jnp.exp(log_dt.astype(F32))[:, None]
    ar, ai = a_re.astype(F32), a_im.astype(F32)
    zr, zi = ar * dt, ai * dt

    def power(n):
        mag = jnp.exp(n * zr)
        return mag * jnp.cos(n * zi), mag * jnp.sin(n * zi)

    lr, li = power(1.0)
    den = ar * ar + ai * ai
    kr = ((lr - 1.0) * ar + li * ai) / den
    ki = (li * ar - (lr - 1.0) * ai) / den
    br, bi = b_re.astype(F32), b_im.astype(F32)
    bbr = kr[..., None] * br - ki[..., None] * bi
    bbi = kr[..., None] * bi + ki[..., None] * br
    groups_per_super = SSM_GROUPS // SSM_SUPER

    def super_blocks(w, per_group_rows, per_group_cols):
        rows = w.reshape(SSM_SUPER, groups_per_super * per_group_rows, per_group_cols)
        tiled = jnp.tile(rows, (1, 1, groups_per_super))
        same = (lax.broadcasted_iota(jnp.int32, tiled.shape, 1) // per_group_rows
                == lax.broadcasted_iota(jnp.int32, tiled.shape, 2) // per_group_cols)
        return jnp.where(same, tiled, 0.0)

    b_in = lambda b: super_blocks(jnp.transpose(b, (0, 2, 1)), SSM_GROUP, SSM_STATE)
    c_out = lambda c: super_blocks(jnp.transpose(c, (0, 2, 1)), SSM_STATE, SSM_GROUP)
    zr, zi = zr.reshape(1, SSM_LANES), zi.reshape(1, SSM_LANES)
    pw = jnp.stack(power(jnp.arange(1, SEG + 1, dtype=F32)[:, None]))
    pw = jnp.broadcast_to(pw[:, :, None, :], (2, SEG, SCAN_ROWS, SSM_LANES)).reshape(2, S5_TT, SSM_LANES)
    row = jnp.arange(SCAN_ROWS)[:, None]
    seg = []
    for shift in (1, 2, 4):
        pr, pi = power(float(SEG * shift))
        seg += [jnp.where(row >= shift, pr, 0.0), jnp.where(row >= shift, pi, 0.0)]
    seg += list(power(SEG * (row + 1).astype(F32)))
    b_sg = jnp.concatenate([b_in(bbr), b_in(bbi)], axis=2)
    c_sg = jnp.concatenate([c_out(c_re.astype(F32)), -c_out(c_im.astype(F32))], axis=1)
    return b_sg.astype(BF16), c_sg.astype(BF16), pw, jnp.stack(seg)


def _compress_kernel(xk_ref, xv_ref, w1k_ref, w2k_ref, pk_ref, w1v_ref, w2vT_ref, pv_ref, kc_ref, vcT_ref):
    half = CMP_STRIDE * HEAD_DIM

    def hidden(x_ref, w1_ref, pos_ref):
        x = x_ref[0].astype(BF16)
        first = _dot(x, w1_ref[:half, :])
        second = _dot(x, w1_ref[half:, :])
        bias = _dot(jnp.broadcast_to(pos_ref[...], (8, 2 * half)).astype(BF16), w1_ref[...])[:1]
        pre = first + pltpu.roll(second, N_CHUNK - 1, 0) + bias
        return jax.nn.gelu(pre).astype(BF16)

    hk = hidden(xk_ref, w1k_ref, pk_ref)
    kc = _dot(hk, w2k_ref[...])
    tail_pad = N_CMP_PAD - CMP_PAD - N_CHUNK
    kc_ref[0] = jnp.concatenate([jnp.zeros((CMP_PAD, HEAD_DIM), F32), kc,
                                 jnp.zeros((tail_pad, HEAD_DIM), F32)], axis=0).astype(BF16)
    hv = hidden(xv_ref, w1v_ref, pv_ref)
    vcT = _dot_nt(w2vT_ref[...], hv)
    vcT_ref[0] = jnp.concatenate([jnp.zeros((HEAD_DIM, CMP_PAD), F32), vcT,
                                  jnp.zeros((HEAD_DIM, tail_pad), F32)], axis=1).astype(BF16)


def _compress(xk, xv, w1k, w2k, pk, w1v, w2vT, pv):
    const2 = lambda g: (0, 0)
    head = lambda g: (g, 0, 0)
    return pl.pallas_call(
        _compress_kernel,
        grid=(NSA_KV_HEADS,),
        in_specs=[
            pl.BlockSpec((1, N_CHUNK, CMP_STRIDE * HEAD_DIM), head),
            pl.BlockSpec((1, N_CHUNK, CMP_STRIDE * HEAD_DIM), head),
            pl.BlockSpec(w1k.shape, const2),
            pl.BlockSpec(w2k.shape, const2),
            pl.BlockSpec(pk.shape, const2),
            pl.BlockSpec(w1v.shape, const2),
            pl.BlockSpec(w2vT.shape, const2),
            pl.BlockSpec(pv.shape, const2),
        ],
        out_specs=[
            pl.BlockSpec((1, N_CMP_PAD, HEAD_DIM), head),
            pl.BlockSpec((1, HEAD_DIM, N_CMP_PAD), head),
        ],
        out_shape=[
            jax.ShapeDtypeStruct((NSA_KV_HEADS, N_CMP_PAD, HEAD_DIM), BF16),
            jax.ShapeDtypeStruct((NSA_KV_HEADS, HEAD_DIM, N_CMP_PAD), BF16),
        ],
        compiler_params=pltpu.CompilerParams(
            dimension_semantics=("arbitrary",), vmem_limit_bytes=VMEM_LIMIT),
        name="compress",
    )(xk, xv, w1k, w2k, pk, w1v, w2vT, pv)


def _nsa_kernel(qT_ref, kc_ref, vcT_ref, ks_ref, vsT_ref, kw_ref, vwT_ref, bias_ref, bc_ref, mt_ref, gT_ref,
                oT_ref, sc_ref, neg_ref, negfar_ref, m_ref, acc_ref, tot_ref, sbuf0_ref, sbuf1_ref, mloc_ref, pslc_ref):
    i = pl.program_id(1)
    s0 = i * TQ
    qT = jnp.concatenate([qT_ref[r * HEAD_DIM:(r + 1) * HEAD_DIM, :] for r in range(GQA)], axis=1)

    qT_nomask = jnp.concatenate([qT, jnp.zeros((K_COLS - HEAD_DIM, QL), BF16)], axis=0)

    gate0 = pl.program_id(0) * GATES_PER_KV_HEAD

    def gate_row(branch):
        return jnp.concatenate([gT_ref[pl.ds(gate0 + r * 3 + branch, 1), :] for r in range(GQA)], axis=1)

    def reset():
        m_ref[...] = jnp.full_like(m_ref, M_FLOOR)
        acc_ref[...] = jnp.zeros_like(acc_ref)

    def attend(k_ref, vT_ref, start, size, add):
        k = k_ref[0, pl.ds(start, size), :]
        s = _dot(k, qT if k.shape[1] == HEAD_DIM else qT_nomask) + add
        m_prev = m_ref[...]
        m_new = jnp.maximum(m_prev, jnp.max(s, axis=0, keepdims=True))
        alpha = jnp.exp2(m_prev - m_new)
        p = jnp.exp2(s - m_new).astype(BF16)
        acc_ref[...] = alpha * acc_ref[...] + _dot(vT_ref[0, :, pl.ds(start, size)], p)
        m_ref[...] = m_new

    def finish(branch):
        acc = acc_ref[...]
        scale = gate_row(branch) / jnp.maximum(acc[HEAD_DIM:HEAD_DIM + 1, :], 1e-30)
        tot_ref[...] += acc[:HEAD_DIM, :] * scale

    n0 = i * (TQ // CMP_STRIDE)
    band0 = pl.multiple_of(n0 + CMP_PAD - 16, 8)

    def cmp_branch(nrows):
        row = lax.broadcasted_iota(jnp.int32, (nrows, QL), 0)
        live = (row >= CMP_PAD) & (row < band0 + CMP_BAND)
        sc_ref[0:nrows, :] = jnp.where(live, _dot(kc_ref[0, 0:nrows, :], qT), NEG)
        sc_ref[pl.ds(band0, CMP_BAND), :] += bc_ref[0]
        sc = sc_ref[0:nrows, :]
        mc = jnp.maximum(jnp.max(sc, axis=0, keepdims=True), M_FLOOR)
        pc = jnp.exp2(sc - mc)
        pc = pc * (1.0 / jnp.maximum(jnp.sum(pc, axis=0, keepdims=True), 1e-30))
        tot_ref[...] = _dot(vcT_ref[0, :, 0:nrows], pc.astype(BF16)) * gate_row(0)
        imp = pc[:, 0:TQ]
        for r in range(1, GQA):
            imp = imp + pc[:, r * TQ:(r + 1) * TQ]
        mt = mt_ref[:, 0:nrows]
        p_slc = jnp.zeros((N_SLC, TQ), F32)
        rem = imp
        for _ in range(2):
            piece = rem.astype(BF16)
            p_slc = p_slc + _dot(mt, piece)
            rem = rem - piece.astype(F32)
        pslc_ref[...] = p_slc

    prev_rows = 0
    for nrows in CMP_EXTENTS:
        lo, hi = prev_rows, nrows
        pl.when((band0 + CMP_BAND > lo) & (band0 + CMP_BAND <= hi))(functools.partial(cmp_branch, nrows))
        prev_rows = nrows

    reset()
    win_start = jnp.maximum(s0 - WINDOW, 0)
    win_rows = pl.ds(pl.multiple_of(WINDOW - (s0 - win_start), NEAR), WINDOW + TQ)
    attend(kw_ref, vwT_ref, pl.multiple_of(win_start, NEAR), WINDOW + TQ, bias_ref[0, win_rows, :])
    finish(2)

    p_slc = pslc_ref[...]
    blk = lax.broadcasted_iota(jnp.int32, (N_SLC, TQ), 0)
    cur = (s0 + lax.broadcasted_iota(jnp.int32, (N_SLC, TQ), 1)) // SLC_LEN
    valid = blk <= cur
    forced = valid & ((blk == 0) | (blk >= cur - (N_LOCAL - 1)))
    score = jnp.where(forced, -jnp.inf, jnp.where(valid, p_slc, -BIG))
    blk_f = blk.astype(F32)
    for _ in range(N_SEL - (N_LOCAL + 1)):
        best = jnp.max(score, axis=0, keepdims=True)
        first = jnp.min(jnp.where(score == best, blk_f, float(N_SLC)), axis=0, keepdims=True)
        score = jnp.where(blk_f == first, -jnp.inf, score)
    neg = jnp.where(score == -jnp.inf, 0.0, NEG)
    neg_ref[...] = jnp.concatenate([neg] * GQA, axis=1)
    near_blk = (s0 - NEAR) // SLC_LEN
    negfar_ref[...] = jnp.concatenate([jnp.where(blk >= near_blk, NEG, neg)] * GQA, axis=1)

    def block_mask(ref, j0, nblk):
        return jnp.concatenate(
            [jnp.broadcast_to(ref[pl.ds(j0 + b, 1), :], (SLC_LEN, QL)) for b in range(nblk)], axis=0)

    reset()
    n_far = (s0 - NEAR + FAR_KEYS - 1) // FAR_KEYS

    sbufs = (sbuf0_ref, sbuf1_ref)

    def far_logits(c, slot):
        start = pl.multiple_of(c * FAR_KEYS, FAR_KEYS)
        mask_rows = negfar_ref[pl.ds(pl.multiple_of(c * FAR_BLOCKS, FAR_BLOCKS), FAR_BLOCKS), :]
        extra = jnp.concatenate([mask_rows, jnp.zeros((K_COLS - HEAD_DIM - FAR_BLOCKS, QL), F32)], axis=0)
        q_masked = jnp.concatenate([qT, extra.astype(BF16)], axis=0)
        s = _dot(ks_ref[0, pl.ds(start, FAR_KEYS), :], q_masked)
        sbufs[slot][...] = s
        mloc_ref[slot] = jnp.max(s, axis=0, keepdims=True)

    def far_consume(c, slot):
        start = pl.multiple_of(c * FAR_KEYS, FAR_KEYS)
        m_prev = m_ref[...]
        m_new = jnp.maximum(m_prev, mloc_ref[slot])
        alpha = jnp.exp2(m_prev - m_new)
        p = jnp.exp2(sbufs[slot][...] - m_new).astype(BF16)
        acc_ref[...] = alpha * acc_ref[...] + _dot(vsT_ref[0, :, pl.ds(start, FAR_KEYS)], p)
        m_ref[...] = m_new

    @pl.when(i >= 1)
    def _():
        n_pairs = (n_far + 1) // 2
        far_logits(0, 0)
        attend(ks_ref, vsT_ref, pl.multiple_of(s0 - NEAR, NEAR), NEAR + TQ,
               bias_ref[0, WINDOW - NEAR:WINDOW + TQ, :] + block_mask(neg_ref, near_blk, (NEAR + TQ) // SLC_LEN))

        def far_body(p, carry):
            far_logits(2 * p + 1, 1)
            far_consume(2 * p, 0)
            far_logits(2 * p + 2, 0)
            far_consume(2 * p + 1, 1)
            return carry

        lax.fori_loop(0, n_pairs - 1, far_body, 0)
        last = 2 * (n_pairs - 1)

        @pl.when(n_far % 2 == 0)
        def _():
            far_logits(last + 1, 1)
            far_consume(last, 0)
            far_consume(last + 1, 1)

        @pl.when(n_far % 2 == 1)
        def _():
            far_consume(last, 0)

    @pl.when(i == 0)
    def _():
        attend(ks_ref, vsT_ref, 0, TQ, bias_ref[0, WINDOW:WINDOW + TQ, :] + block_mask(neg_ref, 0, TQ // SLC_LEN))

    finish(1)

    tot = tot_ref[...]
    for r in range(GQA):
        oT_ref[r * HEAD_DIM:(r + 1) * HEAD_DIM, :] = tot[:, r * TQ:(r + 1) * TQ]


def _nsa(qT, kc, vcT, ks, vsT, kw, vwT, bias_tile, bias_cmp, mt, gnT):
    s = qT.shape[1]
    head3 = lambda g, i: (g, 0, 0)
    return pl.pallas_call(
        _nsa_kernel,
        grid=(NSA_KV_HEADS, s // TQ),
        in_specs=[
            pl.BlockSpec((GQA * HEAD_DIM, TQ), lambda g, i: (g, i)),
            pl.BlockSpec((1, N_CMP_PAD, HEAD_DIM), head3),
            pl.BlockSpec((1, HEAD_DIM, N_CMP_PAD), head3),
            pl.BlockSpec((1, s, K_COLS), head3),
            pl.BlockSpec((1, V_ROWS, s), head3),
            pl.BlockSpec((1, s, HEAD_DIM), head3),
            pl.BlockSpec((1, V_ROWS, s), head3),
            pl.BlockSpec((1, 2 * WINDOW + TQ, QL), head3, pipeline_mode=pl.Buffered(1)),
            pl.BlockSpec((1, CMP_BAND, QL), head3),
            pl.BlockSpec(mt.shape, lambda g, i: (0, 0)),
            pl.BlockSpec((GATE_ROWS, TQ), lambda g, i: (0, i)),
        ],
        out_specs=pl.BlockSpec((GQA * HEAD_DIM, TQ), lambda g, i: (g, i)),
        out_shape=jax.ShapeDtypeStruct((NSA_WIDTH, s), F32),
        scratch_shapes=[
            pltpu.VMEM((N_CMP_PAD, QL), F32),
            pltpu.VMEM((N_SLC, QL), F32),
            pltpu.VMEM((N_SLC, QL), F32),
            pltpu.VMEM((1, QL), F32),
            pltpu.VMEM((V_ROWS, QL), F32),
            pltpu.VMEM((HEAD_DIM, QL), F32),
            pltpu.VMEM((FAR_KEYS, QL), F32),
            pltpu.VMEM((FAR_KEYS, QL), F32),
            pltpu.VMEM((2, 1, QL), F32),
            pltpu.VMEM((N_SLC, TQ), F32),
        ],
        compiler_params=pltpu.CompilerParams(
            dimension_semantics=("arbitrary", "arbitrary"), vmem_limit_bytes=VMEM_LIMIT),
        name="nsa",
    )(qT, kc, vcT, ks, vsT, kw, vwT, bias_tile, bias_cmp, mt, gnT)


def _t5_bucket(dist):
    n = jnp.maximum(dist, 0)
    max_exact = REL_BUCKETS // 2
    nf = jnp.maximum(n, 1).astype(F32)
    large = max_exact + (jnp.log(nf / max_exact) / math.log(REL_MAX_DIST / max_exact)
                         * (REL_BUCKETS - max_exact)).astype(jnp.int32)
    large = jnp.minimum(large, REL_BUCKETS - 1)
    return jnp.where(n < max_exact, n, large)


def _bias_tiles(rel_bias):
    tab = rel_bias.astype(F32)
    tab = (tab[_t5_bucket(jnp.arange(NEAR))] - tab[REL_BUCKETS - 1]).T * LOG2E
    tab = jnp.concatenate([tab, jnp.zeros((NSA_HEADS, 1), F32)], axis=1)

    def by_distance(d):
        return jnp.where(d >= 0, tab[:, jnp.clip(d, 0, NEAR)], NEG)

    def toeplitz(c, nk, nq, step=1):
        n = step * (nk - 1) + nq
        w = by_distance(jnp.arange(n) + c - step * (nk - 1))
        reps = -(-nk * (n + step) // n)
        return jnp.tile(w, (1, reps))[:, :nk * (n + step)].reshape(NSA_HEADS, nk, n + step)[:, ::-1, :nq]

    def per_kv_head(a):
        a = a.reshape(NSA_KV_HEADS, GQA, a.shape[1], TQ)
        return jnp.transpose(a, (0, 2, 1, 3)).reshape(NSA_KV_HEADS, a.shape[2], QL)

    k2, q2 = np.arange(NEAR)[:, None], np.arange(NEAR)[None, :]
    const = lambda a: jnp.broadcast_to(jnp.asarray(a, F32), (NSA_HEADS, NEAR, NEAR))
    edge = WINDOW // NEAR
    nb = TQ // NEAR
    blocks, slabs = {}, {}

    def block(d):
        d = min(max(d, -1), edge + 1)
        if d not in blocks:
            if d < 0:
                b = const(NEG)
            elif d < 2:
                b = toeplitz(d * NEAR, NEAR, NEAR)
            elif d < edge:
                b = const(0.0)
            else:
                b = const(np.where(k2 > q2, 0.0, NEG) if d == edge else NEG)
            blocks[d] = jnp.transpose(b.reshape(NSA_KV_HEADS, GQA, NEAR, NEAR), (0, 2, 1, 3))
        return blocks[d]

    def slab(dd):
        dd = min(max(dd, -nb), edge + 1)
        if dd not in slabs:
            s = jnp.stack([block(dd + b) for b in range(nb)], axis=3)
            slabs[dd] = s.reshape(NSA_KV_HEADS, NEAR, QL)
        return slabs[dd]

    tile = jnp.concatenate([slab(edge - a) for a in range((2 * WINDOW + TQ) // NEAR)], axis=1)
    cmp_band = toeplitz(16 * CMP_STRIDE - (CMP_LEN - 1), CMP_BAND, TQ, step=CMP_STRIDE)
    return tile, per_kv_head(cmp_band)


def _overlap_matrix():
    ratio = SLC_LEN // CMP_STRIDE
    front = CMP_LEN // CMP_STRIDE - 1
    w_ov = np.convolve(np.ones(ratio), np.ones(CMP_LEN // CMP_STRIDE))
    mt = np.zeros((N_SLC, N_CMP_PAD), np.float32)
    for j in range(N_SLC):
        for o, w in enumerate(w_ov):
            n = ratio * j + o - front
            if 0 <= n < N_CMP:
                mt[j, CMP_PAD + n] = w
    return jnp.asarray(mt, BF16)


def _tail_kernel(x_ref, ya_ref, oT_ref, gb_ref, wup_ref, wout_ref, g_ref, wg_ref, wu_ref, wd_ref, gf_ref, o_ref):
    yb = _dot(oT_ref[...].T.astype(BF16), wup_ref[...])
    mix = gb_ref[:, :D_MODEL].astype(F32) * ya_ref[...] + gb_ref[:, D_MODEL:].astype(F32) * yb
    x = x_ref[...] + _dot(mix.astype(BF16), wout_ref[...])
    h = _rms(x, g_ref[...]).astype(BF16)
    f = jax.nn.silu(_dot(h, wg_ref[...])) * _dot(h, wu_ref[...])
    x = x + _dot(f.astype(BF16), wd_ref[...])
    o_ref[...] = _rms(x, gf_ref[...])


def _tail(x2, ya, oT, gb, wup, wout, g, wg, wu, wd, gf, tm=256):
    s = x2.shape[0]
    row = lambda i: (i, 0)
    const = lambda i: (0, 0)
    return pl.pallas_call(
        _tail_kernel,
        grid=(s // tm,),
        in_specs=[
            pl.BlockSpec((tm, D_MODEL), row),
            pl.BlockSpec((tm, D_MODEL), row),
            pl.BlockSpec((NSA_WIDTH, tm), lambda i: (0, i)),
            pl.BlockSpec((tm, 2 * D_MODEL), row),
            pl.BlockSpec(wup.shape, const),
            pl.BlockSpec(wout.shape, const),
            pl.BlockSpec((1, D_MODEL), const),
            pl.BlockSpec(wg.shape, const),
            pl.BlockSpec(wu.shape, const),
            pl.BlockSpec(wd.shape, const),
            pl.BlockSpec((1, D_MODEL), const),
        ],
        out_specs=pl.BlockSpec((tm, D_MODEL), row),
        out_shape=jax.ShapeDtypeStruct((s, D_MODEL), F32),
        compiler_params=pltpu.CompilerParams(
            dimension_semantics=("arbitrary",), vmem_limit_bytes=VMEM_LIMIT),
        name="tail",
    )(x2, ya, oT, gb, wup, wout, g, wg, wu, wd, gf)


def kernel(x, norm_mix_g, w_in, ssm_a_re, ssm_a_im, ssm_log_dt, ssm_b_re, ssm_b_im, ssm_c_re, ssm_c_im, ssm_d, ssm_w_glu, w_up_ssm, cmp_pos_k, cmp_pos_v, cmp_w1_k, cmp_w2_k, cmp_w1_v, cmp_w2_v, rel_bias, w_up_nsa, w_out, norm_ffn_g, w_ffn_gate, w_ffn_up, w_ffn_down, norm_final_g):
    bsz, s, _ = x.shape
    assert (bsz, s) == (1, SEQ) and w_in.shape[0] == 1
    x2 = x.reshape(s, D_MODEL)
    l = 0
    row = lambda v: v.astype(F32).reshape(1, -1)

    u, qT, kcr, vcr, ks, kw, vsT, vwT, gnT, gb = _inproj(x2, row(norm_mix_g[l]), w_in[l].astype(F32).T)

    b_sg, c_sg, pw, seg = _s5_params(
        ssm_a_re[l], ssm_a_im[l], ssm_log_dt[l], ssm_b_re[l], ssm_b_im[l], ssm_c_re[l], ssm_c_im[l])
    ya = _s5(u, b_sg, c_sg, pw, seg, row(ssm_d[l]), ssm_w_glu[l].astype(BF16), w_up_ssm[l].astype(BF16))

    kc, vcT = _compress(kcr, vcr,
                        cmp_w1_k[l].astype(BF16), cmp_w2_k[l].astype(BF16), row(cmp_pos_k[l]),
                        cmp_w1_v[l].astype(BF16), cmp_w2_v[l].T.astype(BF16), row(cmp_pos_v[l]))

    bias_tile, bias_cmp = _bias_tiles(rel_bias)
    oT = _nsa(qT, kc, vcT, ks, vsT, kw, vwT, bias_tile, bias_cmp, _overlap_matrix(), gnT)

    out = _tail(x2, ya, oT, gb, w_up_nsa[l].astype(BF16), w_out[l].astype(BF16),
                row(norm_ffn_g[l]), w_ffn_gate[l].astype(BF16), w_ffn_up[l].astype(BF16),
                w_ffn_down[l].astype(BF16), row(norm_final_g))
    return out.reshape(bsz, s, D_MODEL)
```

```python
import functools
import math

import numpy as np
import jax
import jax.numpy as jnp
from jax import lax
from jax.experimental import pallas as pl
from jax.experimental.pallas import tpu as pltpu

F32 = jnp.float32
BF16 = jnp.bfloat16

D_MODEL = 1024
SEQ = 16384
EPS = 1e-6
SSM_WIDTH = 512
SSM_GROUP = 16
SSM_GROUPS = SSM_WIDTH // SSM_GROUP
SSM_STATE = 64
SSM_LANES = SSM_GROUPS * SSM_STATE
NSA_HEADS = 8
NSA_KV_HEADS = 2
GQA = NSA_HEADS // NSA_KV_HEADS
HEAD_DIM = 64
NSA_WIDTH = NSA_HEADS * HEAD_DIM
KV_WIDTH = NSA_KV_HEADS * HEAD_DIM
CMP_LEN = 32
CMP_STRIDE = 16
CMP_HIDDEN = 256
SLC_LEN = 64
N_SEL = 16
N_LOCAL = 2
WINDOW = 512
BIG = 1e4
REL_BUCKETS = 32
REL_MAX_DIST = 128
D_FF = 2816

N_CHUNK = SEQ // CMP_STRIDE
N_CMP = (SEQ - CMP_LEN) // CMP_STRIDE + 1
N_SLC = SEQ // SLC_LEN
TQ = 256
QL = GQA * TQ
NEAR = 128
assert TQ > NEAR and TQ % NEAR == 0
CMP_PAD = 16
CMP_ROW_STEP = 128
N_CMP_PAD = -(-(CMP_PAD + N_CHUNK) // CMP_ROW_STEP) * CMP_ROW_STEP
CMP_BAND = 16 + TQ // CMP_STRIDE
NEG = -1e30
M_FLOOR = -1e29
SCAN_ROWS = 8
S5_TT = 256
SEG = S5_TT // SCAN_ROWS
SSM_SUPER = 2
SG_CH = SSM_WIDTH // SSM_SUPER
SG_LANES = SSM_LANES // SSM_SUPER
LOG2E = math.log2(math.e)
V_ROWS = HEAD_DIM + 16
FAR_KEYS = 512
FAR_BLOCKS = FAR_KEYS // SLC_LEN
K_COLS = HEAD_DIM + 16
CMP_EXTENTS = tuple(range(2 * CMP_ROW_STEP, N_CMP_PAD + 1, CMP_ROW_STEP))

VMEM_LIMIT = 56 * 1024 * 1024

COL_U = 0
COL_Q = 512
COL_KC = 1024
COL_VC = 1152
COL_KS = 1280
COL_VS = 1408
COL_KW = 1536
COL_VW = 1664
COL_GN = 1792
COL_GB = 1816
COL_END = 3864
GATE_COLS = 128
GATE_ROWS = 32
GATES_PER_KV_HEAD = GQA * 3
INPROJ_ROW_RANGES = ((COL_U, COL_Q), (COL_KC, COL_VS), (COL_KW, COL_VW), (COL_GB, COL_END))
INPROJ_COL_RANGES = ((COL_Q, COL_KC), (COL_VS, COL_KW), (COL_VW, COL_GN), (COL_GN, COL_GN + GATE_COLS))
INPROJ_PIECE = 512


def _rms(x, g):
    return x * lax.rsqrt(jnp.mean(x * x, axis=-1, keepdims=True) + EPS) * g


def _dot(a, b):
    return jnp.dot(a, b, preferred_element_type=F32)


def _dot_nt(a, b):
    return lax.dot_general(a, b, (((1,), (1,)), ((), ())), preferred_element_type=F32)


def _inproj_kernel(x_ref, g_ref, w_ref,
                   u_ref, qT_ref, kcr_ref, vcr_ref, ks_ref, kw_ref, vsT_ref, vwT_ref, gnT_ref, gb_ref,
                   wrow_ref, wcol_ref, kstage_ref, vstage_ref):
    @pl.when(pl.program_id(0) == 0)
    def _():
        dst = 0
        for lo, hi in INPROJ_ROW_RANGES:
            for c in range(lo, hi, INPROJ_PIECE):
                n = min(INPROJ_PIECE, hi - c)
                wrow_ref[:, dst:dst + n] = w_ref[c:c + n, :].T.astype(BF16)
                dst += n
        dst = 0
        for lo, hi in INPROJ_COL_RANGES:
            wcol_ref[dst:dst + hi - lo, :] = w_ref[lo:hi, :].astype(BF16)
            dst += hi - lo

    h = _rms(x_ref[...], g_ref[...]).astype(BF16)
    tm = h.shape[0]
    n_main = COL_Q - COL_U + COL_VS - COL_KC + COL_VW - COL_KW
    main = _dot(h, wrow_ref[:, :n_main])
    u_ref[...] = main[:, :COL_Q - COL_U]
    pm = main[:, COL_Q - COL_U:COL_Q - COL_U + COL_VS - COL_KC]
    pm_kw = main[:, n_main - (COL_VW - COL_KW):]
    projT = lambda lo, hi: _dot_nt(wcol_ref[lo:hi, :], h)
    n_q, n_v = COL_KC - COL_Q, COL_KW - COL_VS
    qT_ref[...] = (projT(0, n_q) * (HEAD_DIM ** -0.5 * LOG2E)).astype(BF16)
    vsT = projT(n_q, n_q + n_v)
    vwT = projT(n_q + n_v, n_q + 2 * n_v)
    ones = jnp.ones((V_ROWS - HEAD_DIM, tm), F32)
    tok = pl.program_id(0) * tm + lax.broadcasted_iota(jnp.int32, (tm, K_COLS - HEAD_DIM), 0)
    col = lax.broadcasted_iota(jnp.int32, (tm, K_COLS - HEAD_DIM), 1)
    blk_onehot = jnp.where((tok // SLC_LEN) % FAR_BLOCKS == col, 1.0, 0.0)
    for g in range(NSA_KV_HEADS):
        lo = g * HEAD_DIM
        ks_ref[g] = jnp.concatenate(
            [pm[:, COL_KS - COL_KC + lo:COL_KS - COL_KC + lo + HEAD_DIM], blk_onehot], axis=1).astype(BF16)
        kw_ref[g] = pm_kw[:, lo:lo + HEAD_DIM].astype(BF16)
        vsT_ref[g] = jnp.concatenate([vsT[lo:lo + HEAD_DIM], ones], axis=0).astype(BF16)
        vwT_ref[g] = jnp.concatenate([vwT[lo:lo + HEAD_DIM], ones], axis=0).astype(BF16)
    for src_col, stage_ref, dst_ref in ((COL_KC, kstage_ref, kcr_ref), (COL_VC, vstage_ref, vcr_ref)):
        stage_ref[...] = pm[:, src_col - COL_KC:src_col - COL_KC + KV_WIDTH]
        for t in range(CMP_STRIDE):
            rows = stage_ref[pl.ds(t, tm // CMP_STRIDE, stride=CMP_STRIDE), :]
            for g in range(NSA_KV_HEADS):
                dst_ref[g, :, t * HEAD_DIM:(t + 1) * HEAD_DIM] = rows[:, g * HEAD_DIM:(g + 1) * HEAD_DIM]
    gnT_ref[...] = jax.nn.sigmoid(projT(n_q + 2 * n_v, n_q + 2 * n_v + GATE_ROWS))
    gb_ref[...] = jax.nn.sigmoid(_dot(h, wrow_ref[:, n_main:])).astype(BF16)


def _inproj(x2, g, w, tm=512):
    s = x2.shape[0]
    const = lambda i: (0, 0)
    row = lambda i: (i, 0)
    col = lambda i: (0, i)
    return pl.pallas_call(
        _inproj_kernel,
        grid=(s // tm,),
        in_specs=[
            pl.BlockSpec((tm, D_MODEL), row),
            pl.BlockSpec((1, D_MODEL), const),
            pl.BlockSpec(w.shape, const),
        ],
        out_specs=[
            pl.BlockSpec((tm, SSM_WIDTH), row),
            pl.BlockSpec((NSA_WIDTH, tm), col),
            pl.BlockSpec((NSA_KV_HEADS, tm // CMP_STRIDE, CMP_STRIDE * HEAD_DIM), lambda i: (0, i, 0)),
            pl.BlockSpec((NSA_KV_HEADS, tm // CMP_STRIDE, CMP_STRIDE * HEAD_DIM), lambda i: (0, i, 0)),
            pl.BlockSpec((NSA_KV_HEADS, tm, K_COLS), lambda i: (0, i, 0)),
            pl.BlockSpec((NSA_KV_HEADS, tm, HEAD_DIM), lambda i: (0, i, 0)),
            pl.BlockSpec((NSA_KV_HEADS, V_ROWS, tm), lambda i: (0, 0, i)),
            pl.BlockSpec((NSA_KV_HEADS, V_ROWS, tm), lambda i: (0, 0, i)),
            pl.BlockSpec((GATE_ROWS, tm), col),
            pl.BlockSpec((tm, 2 * D_MODEL), row),
        ],
        out_shape=[
            jax.ShapeDtypeStruct((s, SSM_WIDTH), F32),
            jax.ShapeDtypeStruct((NSA_WIDTH, s), BF16),
            jax.ShapeDtypeStruct((NSA_KV_HEADS, s // CMP_STRIDE, CMP_STRIDE * HEAD_DIM), F32),
            jax.ShapeDtypeStruct((NSA_KV_HEADS, s // CMP_STRIDE, CMP_STRIDE * HEAD_DIM), F32),
            jax.ShapeDtypeStruct((NSA_KV_HEADS, s, K_COLS), BF16),
            jax.ShapeDtypeStruct((NSA_KV_HEADS, s, HEAD_DIM), BF16),
            jax.ShapeDtypeStruct((NSA_KV_HEADS, V_ROWS, s), BF16),
            jax.ShapeDtypeStruct((NSA_KV_HEADS, V_ROWS, s), BF16),
            jax.ShapeDtypeStruct((GATE_ROWS, s), F32),
            jax.ShapeDtypeStruct((s, 2 * D_MODEL), BF16),
        ],
        scratch_shapes=[
            pltpu.VMEM((D_MODEL, sum(hi - lo for lo, hi in INPROJ_ROW_RANGES)), BF16),
            pltpu.VMEM((sum(hi - lo for lo, hi in INPROJ_COL_RANGES), D_MODEL), BF16),
            pltpu.VMEM((tm, KV_WIDTH), F32),
            pltpu.VMEM((tm, KV_WIDTH), F32),
        ],
        compiler_params=pltpu.CompilerParams(
            dimension_semantics=("arbitrary",), vmem_limit_bytes=VMEM_LIMIT),
        name="inproj",
    )(x2, g, w)


def _s5_kernel(u_ref, perm_ref, permT_ref, b_ref, c_ref, pw_ref, seg_ref, d_ref, wglu_ref, wup_ref,
               ya_ref, xre_ref, xim_ref, st_ref, cre_s, cim_s):
    @pl.when(pl.program_id(0) == 0)
    def _():
        cre_s[...] = jnp.zeros_like(cre_s)
        cim_s[...] = jnp.zeros_like(cim_s)

    u = u_ref[...]
    ub = _dot(perm_ref[...], u.astype(BF16)).astype(BF16)
    for sg in range(SSM_SUPER):
        bu = _dot(ub[:, sg * SG_CH:(sg + 1) * SG_CH], b_ref[sg])
        xre_ref[:, sg * SG_LANES:(sg + 1) * SG_LANES] = bu[:, :SG_LANES]
        xim_ref[:, sg * SG_LANES:(sg + 1) * SG_LANES] = bu[:, SG_LANES:]

    def cmul_add(re, im, are, aim, sre, sim):
        return re + are * sre - aim * sim, im + are * sim + aim * sre

    for sg in range(SSM_SUPER):
        lanes = slice(sg * SG_LANES, (sg + 1) * SG_LANES)

        lam_re, lam_im = pw_ref[0, 0:SCAN_ROWS, lanes], pw_ref[1, 0:SCAN_ROWS, lanes]

        def local(j, carry):
            r0 = pl.multiple_of(j * SCAN_ROWS, SCAN_ROWS)
            re, im = cmul_add(xre_ref[pl.ds(r0, SCAN_ROWS), lanes], xim_ref[pl.ds(r0, SCAN_ROWS), lanes],
                              lam_re, lam_im, *carry)
            xre_ref[pl.ds(r0, SCAN_ROWS), lanes] = re
            xim_ref[pl.ds(r0, SCAN_ROWS), lanes] = im
            return re, im

        zero = jnp.zeros((SCAN_ROWS, SG_LANES), F32)
        re, im = lax.fori_loop(0, SEG, local, (zero, zero), unroll=True)

        for k, shift in enumerate((1, 2, 4)):
            re, im = cmul_add(re, im, seg_ref[2 * k, :, lanes], seg_ref[2 * k + 1, :, lanes],
                              pltpu.roll(re, shift, 0), pltpu.roll(im, shift, 0))
        cin_re, cin_im = cre_s[:, lanes], cim_s[:, lanes]
        re, im = cmul_add(re, im, seg_ref[6, :, lanes], seg_ref[7, :, lanes], cin_re, cin_im)
        cre_s[:, lanes] = re[SCAN_ROWS - 1:SCAN_ROWS, :]
        cim_s[:, lanes] = im[SCAN_ROWS - 1:SCAN_ROWS, :]
        first = lax.broadcasted_iota(jnp.int32, (SCAN_ROWS, SG_LANES), 0) == 0
        start_re = jnp.where(first, cin_re, pltpu.roll(re, 1, 0))
        start_im = jnp.where(first, cin_im, pltpu.roll(im, 1, 0))
        start_re = jnp.concatenate([start_re, start_re], axis=0)
        start_im = jnp.concatenate([start_im, start_im], axis=0)

        def fix(jj, carry):
            r0 = pl.multiple_of(jj * 2 * SCAN_ROWS, 2 * SCAN_ROWS)
            rows = pl.ds(r0, 2 * SCAN_ROWS)
            re, im = cmul_add(xre_ref[rows, lanes], xim_ref[rows, lanes],
                              pw_ref[0, rows, lanes], pw_ref[1, rows, lanes], start_re, start_im)
            st_ref[rows, 2 * sg * SG_LANES:(2 * sg + 1) * SG_LANES] = re.astype(BF16)
            st_ref[rows, (2 * sg + 1) * SG_LANES:(2 * sg + 2) * SG_LANES] = im.astype(BF16)
            return carry

        lax.fori_loop(0, SEG // 2, fix, 0, unroll=True)
    y_perm = jnp.concatenate(
        [_dot(st_ref[:, 2 * sg * SG_LANES:(2 * sg + 2) * SG_LANES], c_ref[sg])
         for sg in range(SSM_SUPER)], axis=1)
    y_hi = y_perm.astype(BF16)
    y_lo = (y_perm - y_hi.astype(F32)).astype(BF16)
    y = _dot(permT_ref[...], y_hi) + _dot(permT_ref[...], y_lo) + d_ref[...] * u
    z = jax.nn.gelu(y)
    z = z * jax.nn.sigmoid(_dot(z.astype(BF16), wglu_ref[...]))
    ya_ref[...] = _dot(z.astype(BF16), wup_ref[...])


def _s5(u, b, c, pw, seg, d, wglu, wup):
    s = u.shape[0]
    t = np.arange(S5_TT)
    perm = np.zeros((S5_TT, S5_TT), np.float32)
    perm[SCAN_ROWS * (t % SEG) + t // SEG, t] = 1.0
    permT = jnp.asarray(perm.T, BF16)
    perm = jnp.asarray(perm, BF16)
    const2 = lambda i: (0, 0)
    const3 = lambda i: (0, 0, 0)
    return pl.pallas_call(
        _s5_kernel,
        grid=(s // S5_TT,),
        in_specs=[
            pl.BlockSpec((S5_TT, SSM_WIDTH), lambda i: (i, 0)),
            pl.BlockSpec(perm.shape, const2),
            pl.BlockSpec(permT.shape, const2),
            pl.BlockSpec(b.shape, const3),
            pl.BlockSpec(c.shape, const3),
            pl.BlockSpec(pw.shape, const3),
            pl.BlockSpec(seg.shape, const3),
            pl.BlockSpec((1, SSM_WIDTH), const2),
            pl.BlockSpec(wglu.shape, const2),
            pl.BlockSpec(wup.shape, const2),
        ],
        out_specs=pl.BlockSpec((S5_TT, D_MODEL), lambda i: (i, 0)),
        out_shape=jax.ShapeDtypeStruct((s, D_MODEL), F32),
        scratch_shapes=[
            pltpu.VMEM((S5_TT, SSM_LANES), F32),
            pltpu.VMEM((S5_TT, SSM_LANES), F32),
            pltpu.VMEM((S5_TT, 2 * SSM_LANES), BF16),
            pltpu.VMEM((1, SSM_LANES), F32),
            pltpu.VMEM((1, SSM_LANES), F32),
        ],
        compiler_params=pltpu.CompilerParams(
            dimension_semantics=("arbitrary",), vmem_limit_bytes=VMEM_LIMIT),
        name="s5",
    )(u, perm, permT, b, c, pw, seg, d, wglu, wup)


def _s5_params(a_re, a_im, log_dt, b_re, b_im, c_re, c_im):
    dt = jnp.exp(log_dt.astype(F32))[:, None]
    ar, ai = a_re.astype(F32), a_im.astype(F32)
    zr, zi = ar * dt, ai * dt

    def power(n):
        mag = jnp.exp(n * zr)
        return mag * jnp.cos(n * zi), mag * jnp.sin(n * zi)

    lr, li = power(1.0)
    den = ar * ar + ai * ai
    kr = ((lr - 1.0) * ar + li * ai) / den
    ki = (li * ar - (lr - 1.0) * ai) / den
    br, bi = b_re.astype(F32), b_im.astype(F32)
    bbr = kr[..., None] * br - ki[..., None] * bi
    bbi = kr[..., None] * bi + ki[..., None] * br
    groups_per_super = SSM_GROUPS // SSM_SUPER

    def super_blocks(w, per_group_rows, per_group_cols):
        rows = w.reshape(SSM_SUPER, groups_per_super * per_group_rows, per_group_cols)
        tiled = jnp.tile(rows, (1, 1, groups_per_super))
        same = (lax.broadcasted_iota(jnp.int32, tiled.shape, 1) // per_group_rows
                == lax.broadcasted_iota(jnp.int32, tiled.shape, 2) // per_group_cols)
        return jnp.where(same, tiled, 0.0)

    b_in = lambda b: super_blocks(jnp.transpose(b, (0, 2, 1)), SSM_GROUP, SSM_STATE)
    c_out = lambda c: super_blocks(jnp.transpose(c, (0, 2, 1)), SSM_STATE, SSM_GROUP)
    zr, zi = zr.reshape(1, SSM_LANES), zi.reshape(1, SSM_LANES)
    pw = jnp.stack(power(jnp.arange(1, SEG + 1, dtype=F32)[:, None]))
    pw = jnp.broadcast_to(pw[:, :, None, :], (2, SEG, SCAN_ROWS, SSM_LANES)).reshape(2, S5_TT, SSM_LANES)
    row = jnp.arange(SCAN_ROWS)[:, None]
    seg = []
    for shift in (1, 2, 4):
        pr, pi = power(float(SEG * shift))
        seg += [jnp.where(row >= shift, pr, 0.0), jnp.where(row >= shift, pi, 0.0)]
    seg += list(power(SEG * (row + 1).astype(F32)))
    b_sg = jnp.concatenate([b_in(bbr), b_in(bbi)], axis=2)
    c_sg = jnp.concatenate([c_out(c_re.astype(F32)), -c_out(c_im.astype(F32))], axis=1)
    return b_sg.astype(BF16), c_sg.astype(BF16), pw, jnp.stack(seg)


def _compress_kernel(xk_ref, xv_ref, w1k_ref, w2k_ref, pk_ref, w1v_ref, w2vT_ref, pv_ref, kc_ref, vcT_ref):
    half = CMP_STRIDE * HEAD_DIM

    def hidden(x_ref, w1_ref, pos_ref):
        x = x_ref[0].astype(BF16)
        first = _dot(x, w1_ref[:half, :])
        second = _dot(x, w1_ref[half:, :])
        bias = _dot(jnp.broadcast_to(pos_ref[...], (8, 2 * half)).astype(BF16), w1_ref[...])[:1]
        pre = first + pltpu.roll(second, N_CHUNK - 1, 0) + bias
        return jax.nn.gelu(pre).astype(BF16)

    hk = hidden(xk_ref, w1k_ref, pk_ref)
    kc = _dot(hk, w2k_ref[...])
    tail_pad = N_CMP_PAD - CMP_PAD - N_CHUNK
    kc_ref[0] = jnp.concatenate([jnp.zeros((CMP_PAD, HEAD_DIM), F32), kc,
                                 jnp.zeros((tail_pad, HEAD_DIM), F32)], axis=0).astype(BF16)
    hv = hidden(xv_ref, w1v_ref, pv_ref)
    vcT = _dot_nt(w2vT_ref[...], hv)
    vcT_ref[0] = jnp.concatenate([jnp.zeros((HEAD_DIM, CMP_PAD), F32), vcT,
                                  jnp.zeros((HEAD_DIM, tail_pad), F32)], axis=1).astype(BF16)


def _compress(xk, xv, w1k, w2k, pk, w1v, w2vT, pv):
    const2 = lambda g: (0, 0)
    head = lambda g: (g, 0, 0)
    return pl.pallas_call(
        _compress_kernel,
        grid=(NSA_KV_HEADS,),
        in_specs=[
            pl.BlockSpec((1, N_CHUNK, CMP_STRIDE * HEAD_DIM), head),
            pl.BlockSpec((1, N_CHUNK, CMP_STRIDE * HEAD_DIM), head),
            pl.BlockSpec(w1k.shape, const2),
            pl.BlockSpec(w2k.shape, const2),
            pl.BlockSpec(pk.shape, const2),
            pl.BlockSpec(w1v.shape, const2),
            pl.BlockSpec(w2vT.shape, const2),
            pl.BlockSpec(pv.shape, const2),
        ],
        out_specs=[
            pl.BlockSpec((1, N_CMP_PAD, HEAD_DIM), head),
            pl.BlockSpec((1, HEAD_DIM, N_CMP_PAD), head),
        ],
        out_shape=[
            jax.ShapeDtypeStruct((NSA_KV_HEADS, N_CMP_PAD, HEAD_DIM), BF16),
            jax.ShapeDtypeStruct((NSA_KV_HEADS, HEAD_DIM, N_CMP_PAD), BF16),
        ],
        compiler_params=pltpu.CompilerParams(
            dimension_semantics=("arbitrary",), vmem_limit_bytes=VMEM_LIMIT),
        name="compress",
    )(xk, xv, w1k, w2k, pk, w1v, w2vT, pv)


def _nsa_kernel(qT_ref, kc_ref, vcT_ref, ks_ref, vsT_ref, kw_ref, vwT_ref, bias_ref, bc_ref, mt_ref, gT_ref,
                oT_ref, sc_ref, neg_ref, negfar_ref, m_ref, acc_ref, tot_ref, sbuf0_ref, sbuf1_ref, mloc_ref, pslc_ref):
    i = pl.program_id(1)
    s0 = i * TQ
    qT = jnp.concatenate([qT_ref[r * HEAD_DIM:(r + 1) * HEAD_DIM, :] for r in range(GQA)], axis=1)

    qT_nomask = jnp.concatenate([qT, jnp.zeros((K_COLS - HEAD_DIM, QL), BF16)], axis=0)

    gate0 = pl.program_id(0) * GATES_PER_KV_HEAD

    def gate_row(branch):
        return jnp.concatenate([gT_ref[pl.ds(gate0 + r * 3 + branch, 1), :] for r in range(GQA)], axis=1)

    def reset():
        m_ref[...] = jnp.full_like(m_ref, M_FLOOR)
        acc_ref[...] = jnp.zeros_like(acc_ref)

    def attend(k_ref, vT_ref, start, size, add):
        k = k_ref[0, pl.ds(start, size), :]
        s = _dot(k, qT if k.shape[1] == HEAD_DIM else qT_nomask) + add
        m_prev = m_ref[...]
        m_new = jnp.maximum(m_prev, jnp.max(s, axis=0, keepdims=True))
        alpha = jnp.exp2(m_prev - m_new)
        p = jnp.exp2(s - m_new).astype(BF16)
        acc_ref[...] = alpha * acc_ref[...] + _dot(vT_ref[0, :, pl.ds(start, size)], p)
        m_ref[...] = m_new

    def finish(branch):
        acc = acc_ref[...]
        scale = gate_row(branch) / jnp.maximum(acc[HEAD_DIM:HEAD_DIM + 1, :], 1e-30)
        tot_ref[...] += acc[:HEAD_DIM, :] * scale

    n0 = i * (TQ // CMP_STRIDE)
    band0 = pl.multiple_of(n0 + CMP_PAD - 16, 8)

    def cmp_branch(nrows):
        row = lax.broadcasted_iota(jnp.int32, (nrows, QL), 0)
        live = (row >= CMP_PAD) & (row < band0 + CMP_BAND)
        sc_ref[0:nrows, :] = jnp.where(live, _dot(kc_ref[0, 0:nrows, :], qT), NEG)
        sc_ref[pl.ds(band0, CMP_BAND), :] += bc_ref[0]
        sc = sc_ref[0:nrows, :]
        mc = jnp.maximum(jnp.max(sc, axis=0, keepdims=True), M_FLOOR)
        pc = jnp.exp2(sc - mc)
        pc = pc * (1.0 / jnp.maximum(jnp.sum(pc, axis=0, keepdims=True), 1e-30))
        tot_ref[...] = _dot(vcT_ref[0, :, 0:nrows], pc.astype(BF16)) * gate_row(0)
        imp = pc[:, 0:TQ]
        for r in range(1, GQA):
            imp = imp + pc[:, r * TQ:(r + 1) * TQ]
        mt = mt_ref[:, 0:nrows]
        p_slc = jnp.zeros((N_SLC, TQ), F32)
        rem = imp
        for _ in range(2):
            piece = rem.astype(BF16)
            p_slc = p_slc + _dot(mt, piece)
            rem = rem - piece.astype(F32)
        pslc_ref[...] = p_slc

    prev_rows = 0
    for nrows in CMP_EXTENTS:
        lo, hi = prev_rows, nrows
        pl.when((band0 + CMP_BAND > lo) & (band0 + CMP_BAND <= hi))(functools.partial(cmp_branch, nrows))
        prev_rows = nrows

    reset()
    win_start = jnp.maximum(s0 - WINDOW, 0)
    win_rows = pl.ds(pl.multiple_of(WINDOW - (s0 - win_start), NEAR), WINDOW + TQ)
    attend(kw_ref, vwT_ref, pl.multiple_of(win_start, NEAR), WINDOW + TQ, bias_ref[0, win_rows, :])
    finish(2)

    p_slc = pslc_ref[...]
    blk = lax.broadcasted_iota(jnp.int32, (N_SLC, TQ), 0)
    cur = (s0 + lax.broadcasted_iota(jnp.int32, (N_SLC, TQ), 1)) // SLC_LEN
    valid = blk <= cur
    forced = valid & ((blk == 0) | (blk >= cur - (N_LOCAL - 1)))
    score = jnp.where(forced, -jnp.inf, jnp.where(valid, p_slc, -BIG))
    blk_f = blk.astype(F32)
    for _ in range(N_SEL - (N_LOCAL + 1)):
        best = jnp.max(score, axis=0, keepdims=True)
        first = jnp.min(jnp.where(score == best, blk_f, float(N_SLC)), axis=0, keepdims=True)
        score = jnp.where(blk_f == first, -jnp.inf, score)
    neg = jnp.where(score == -jnp.inf, 0.0, NEG)
    neg_ref[...] = jnp.concatenate([neg] * GQA, axis=1)
    near_blk = (s0 - NEAR) // SLC_LEN
    negfar_ref[...] = jnp.concatenate([jnp.where(blk >= near_blk, NEG, neg)] * GQA, axis=1)

    def block_mask(ref, j0, nblk):
        return jnp.concatenate(
            [jnp.broadcast_to(ref[pl.ds(j0 + b, 1), :], (SLC_LEN, QL)) for b in range(nblk)], axis=0)

    reset()
    n_far = (s0 - NEAR + FAR_KEYS - 1) // FAR_KEYS

    sbufs = (sbuf0_ref, sbuf1_ref)

    def far_logits(c, slot):
        start = pl.multiple_of(c * FAR_KEYS, FAR_KEYS)
        mask_rows = negfar_ref[pl.ds(pl.multiple_of(c * FAR_BLOCKS, FAR_BLOCKS), FAR_BLOCKS), :]
        extra = jnp.concatenate([mask_rows, jnp.zeros((K_COLS - HEAD_DIM - FAR_BLOCKS, QL), F32)], axis=0)
        q_masked = jnp.concatenate([qT, extra.astype(BF16)], axis=0)
        s = _dot(ks_ref[0, pl.ds(start, FAR_KEYS), :], q_masked)
        sbufs[slot][...] = s
        mloc_ref[slot] = jnp.max(s, axis=0, keepdims=True)

    def far_consume(c, slot):
        start = pl.multiple_of(c * FAR_KEYS, FAR_KEYS)
        m_prev = m_ref[...]
        m_new = jnp.maximum(m_prev, mloc_ref[slot])
        alpha = jnp.exp2(m_prev - m_new)
        p = jnp.exp2(sbufs[slot][...] - m_new).astype(BF16)
        acc_ref[...] = alpha * acc_ref[...] + _dot(vsT_ref[0, :, pl.ds(start, FAR_KEYS)], p)
        m_ref[...] = m_new

    @pl.when(i >= 1)
    def _():
        n_pairs = (n_far + 1) // 2
        far_logits(0, 0)
        attend(ks_ref, vsT_ref, pl.multiple_of(s0 - NEAR, NEAR), NEAR + TQ,
               bias_ref[0, WINDOW - NEAR:WINDOW + TQ, :] + block_mask(neg_ref, near_blk, (NEAR + TQ) // SLC_LEN))

        def far_body(p, carry):
            far_logits(2 * p + 1, 1)
            far_consume(2 * p, 0)
            far_logits(2 * p + 2, 0)
            far_consume(2 * p + 1, 1)
            return carry

        lax.fori_loop(0, n_pairs - 1, far_body, 0)
        last = 2 * (n_pairs - 1)

        @pl.when(n_far % 2 == 0)
        def _():
            far_logits(last + 1, 1)
            far_consume(last, 0)
            far_consume(last + 1, 1)

        @pl.when(n_far % 2 == 1)
        def _():
            far_consume(last, 0)

    @pl.when(i == 0)
    def _():
        attend(ks_ref, vsT_ref, 0, TQ, bias_ref[0, WINDOW:WINDOW + TQ, :] + block_mask(neg_ref, 0, TQ // SLC_LEN))

    finish(1)

    tot = tot_ref[...]
    for r in range(GQA):
        oT_ref[r * HEAD_DIM:(r + 1) * HEAD_DIM, :] = tot[:, r * TQ:(r + 1) * TQ]


def _nsa(qT, kc, vcT, ks, vsT, kw, vwT, bias_tile, bias_cmp, mt, gnT):
    s = qT.shape[1]
    head3 = lambda g, i: (g, 0, 0)
    return pl.pallas_call(
        _nsa_kernel,
        grid=(NSA_KV_HEADS, s // TQ),
        in_specs=[
            pl.BlockSpec((GQA * HEAD_DIM, TQ), lambda g, i: (g, i)),
            pl.BlockSpec((1, N_CMP_PAD, HEAD_DIM), head3),
            pl.BlockSpec((1, HEAD_DIM, N_CMP_PAD), head3),
            pl.BlockSpec((1, s, K_COLS), head3),
            pl.BlockSpec((1, V_ROWS, s), head3),
            pl.BlockSpec((1, s, HEAD_DIM), head3),
            pl.BlockSpec((1, V_ROWS, s), head3),
            pl.BlockSpec((1, 2 * WINDOW + TQ, QL), head3, pipeline_mode=pl.Buffered(1)),
            pl.BlockSpec((1, CMP_BAND, QL), head3),
            pl.BlockSpec(mt.shape, lambda g, i: (0, 0)),
            pl.BlockSpec((GATE_ROWS, TQ), lambda g, i: (0, i)),
        ],
        out_specs=pl.BlockSpec((GQA * HEAD_DIM, TQ), lambda g, i: (g, i)),
        out_shape=jax.ShapeDtypeStruct((NSA_WIDTH, s), F32),
        scratch_shapes=[
            pltpu.VMEM((N_CMP_PAD, QL), F32),
            pltpu.VMEM((N_SLC, QL), F32),
            pltpu.VMEM((N_SLC, QL), F32),
            pltpu.VMEM((1, QL), F32),
            pltpu.VMEM((V_ROWS, QL), F32),
            pltpu.VMEM((HEAD_DIM, QL), F32),
            pltpu.VMEM((FAR_KEYS, QL), F32),
            pltpu.VMEM((FAR_KEYS, QL), F32),
            pltpu.VMEM((2, 1, QL), F32),
            pltpu.VMEM((N_SLC, TQ), F32),
        ],
        compiler_params=pltpu.CompilerParams(
            dimension_semantics=("arbitrary", "arbitrary"), vmem_limit_bytes=VMEM_LIMIT),
        name="nsa",
    )(qT, kc, vcT, ks, vsT, kw, vwT, bias_tile, bias_cmp, mt, gnT)


def _t5_bucket(dist):
    n = jnp.maximum(dist, 0)
    max_exact = REL_BUCKETS // 2
    nf = jnp.maximum(n, 1).astype(F32)
    large = max_exact + (jnp.log(nf / max_exact) / math.log(REL_MAX_DIST / max_exact)
                         * (REL_BUCKETS - max_exact)).astype(jnp.int32)
    large = jnp.minimum(large, REL_BUCKETS - 1)
    return jnp.where(n < max_exact, n, large)


def _bias_tiles(rel_bias):
    tab = rel_bias.astype(F32)
    tab = (tab[_t5_bucket(jnp.arange(NEAR))] - tab[REL_BUCKETS - 1]).T * LOG2E
    tab = jnp.concatenate([tab, jnp.zeros((NSA_HEADS, 1), F32)], axis=1)

    def by_distance(d):
        return jnp.where(d >= 0, tab[:, jnp.clip(d, 0, NEAR)], NEG)

    def toeplitz(c, nk, nq, step=1):
        n = step * (nk - 1) + nq
        w = by_distance(jnp.arange(n) + c - step * (nk - 1))
        reps = -(-nk * (n + step) // n)
        return jnp.tile(w, (1, reps))[:, :nk * (n + step)].reshape(NSA_HEADS, nk, n + step)[:, ::-1, :nq]

    def per_kv_head(a):
        a = a.reshape(NSA_KV_HEADS, GQA, a.shape[1], TQ)
        return jnp.transpose(a, (0, 2, 1, 3)).reshape(NSA_KV_HEADS, a.shape[2], QL)

    k2, q2 = np.arange(NEAR)[:, None], np.arange(NEAR)[None, :]
    const = lambda a: jnp.broadcast_to(jnp.asarray(a, F32), (NSA_HEADS, NEAR, NEAR))
    edge = WINDOW // NEAR
    nb = TQ // NEAR
    blocks, slabs = {}, {}

    def block(d):
        d = min(max(d, -1), edge + 1)
        if d not in blocks:
            if d < 0:
                b = const(NEG)
            elif d < 2:
                b = toeplitz(d * NEAR, NEAR, NEAR)
            elif d < edge:
                b = const(0.0)
            else:
                b = const(np.where(k2 > q2, 0.0, NEG) if d == edge else NEG)
            blocks[d] = jnp.transpose(b.reshape(NSA_KV_HEADS, GQA, NEAR, NEAR), (0, 2, 1, 3))
        return blocks[d]

    def slab(dd):
        dd = min(max(dd, -nb), edge + 1)
        if dd not in slabs:
            s = jnp.stack([block(dd + b) for b in range(nb)], axis=3)
            slabs[dd] = s.reshape(NSA_KV_HEADS, NEAR, QL)
        return slabs[dd]

    tile = jnp.concatenate([slab(edge - a) for a in range((2 * WINDOW + TQ) // NEAR)], axis=1)
    cmp_band = toeplitz(16 * CMP_STRIDE - (CMP_LEN - 1), CMP_BAND, TQ, step=CMP_STRIDE)
    return tile, per_kv_head(cmp_band)


def _overlap_matrix():
    ratio = SLC_LEN // CMP_STRIDE
    front = CMP_LEN // CMP_STRIDE - 1
    w_ov = np.convolve(np.ones(ratio), np.ones(CMP_LEN // CMP_STRIDE))
    mt = np.zeros((N_SLC, N_CMP_PAD), np.float32)
    for j in range(N_SLC):
        for o, w in enumerate(w_ov):
            n = ratio * j + o - front
            if 0 <= n < N_CMP:
                mt[j, CMP_PAD + n] = w
    return jnp.asarray(mt, BF16)


def _tail_kernel(x_ref, ya_ref, oT_ref, gb_ref, wup_ref, wout_ref, g_ref, wg_ref, wu_ref, wd_ref, gf_ref, o_ref):
    yb = _dot(oT_ref[...].T.astype(BF16), wup_ref[...])
    mix = gb_ref[:, :D_MODEL].astype(F32) * ya_ref[...] + gb_ref[:, D_MODEL:].astype(F32) * yb
    x = x_ref[...] + _dot(mix.astype(BF16), wout_ref[...])
    h = _rms(x, g_ref[...]).astype(BF16)
    f = jax.nn.silu(_dot(h, wg_ref[...])) * _dot(h, wu_ref[...])
    x = x + _dot(f.astype(BF16), wd_ref[...])
    o_ref[...] = _rms(x, gf_ref[...])


def _tail(x2, ya, oT, gb, wup, wout, g, wg, wu, wd, gf, tm=256):
    s = x2.shape[0]
    row = lambda i: (i, 0)
    const = lambda i: (0, 0)
    return pl.pallas_call(
        _tail_kernel,
        grid=(s // tm,),
        in_specs=[
            pl.BlockSpec((tm, D_MODEL), row),
            pl.BlockSpec((tm, D_MODEL), row),
            pl.BlockSpec((NSA_WIDTH, tm), lambda i: (0, i)),
            pl.BlockSpec((tm, 2 * D_MODEL), row),
            pl.BlockSpec(wup.shape, const),
            pl.BlockSpec(wout.shape, const),
            pl.BlockSpec((1, D_MODEL), const),
            pl.BlockSpec(wg.shape, const),
            pl.BlockSpec(wu.shape, const),
            pl.BlockSpec(wd.shape, const),
            pl.BlockSpec((1, D_MODEL), const),
        ],
        out_specs=pl.BlockSpec((tm, D_MODEL), row),
        out_shape=jax.ShapeDtypeStruct((s, D_MODEL), F32),
        compiler_params=pltpu.CompilerParams(
            dimension_semantics=("arbitrary",), vmem_limit_bytes=VMEM_LIMIT),
        name="tail",
    )(x2, ya, oT, gb, wup, wout, g, wg, wu, wd, gf)


def kernel(x, norm_mix_g, w_in, ssm_a_re, ssm_a_im, ssm_log_dt, ssm_b_re, ssm_b_im, ssm_c_re, ssm_c_im, ssm_d, ssm_w_glu, w_up_ssm, cmp_pos_k, cmp_pos_v, cmp_w1_k, cmp_w2_k, cmp_w1_v, cmp_w2_v, rel_bias, w_up_nsa, w_out, norm_ffn_g, w_ffn_gate, w_ffn_up, w_ffn_down, norm_final_g):
    bsz, s, _ = x.shape
    assert (bsz, s) == (1, SEQ) and w_in.shape[0] == 1
    x2 = x.reshape(s, D_MODEL)
    l = 0
    row = lambda v: v.astype(F32).reshape(1, -1)

    u, qT, kcr, vcr, ks, kw, vsT, vwT, gnT, gb = _inproj(x2, row(norm_mix_g[l]), w_in[l].astype(F32).T)

    b_sg, c_sg, pw, seg = _s5_params(
        ssm_a_re[l], ssm_a_im[l], ssm_log_dt[l], ssm_b_re[l], ssm_b_im[l], ssm_c_re[l], ssm_c_im[l])
    ya = _s5(u, b_sg, c_sg, pw, seg, row(ssm_d[l]), ssm_w_glu[l].astype(BF16), w_up_ssm[l].astype(BF16))

    kc, vcT = _compress(kcr, vcr,
                        cmp_w1_k[l].astype(BF16), cmp_w2_k[l].astype(BF16), row(cmp_pos_k[l]),
                        cmp_w1_v[l].astype(BF16), cmp_w2_v[l].T.astype(BF16), row(cmp_pos_v[l]))

    bias_tile, bias_cmp = _bias_tiles(rel_bias)
    oT = _nsa(qT, kc, vcT, ks, vsT, kw, vwT, bias_tile, bias_cmp, _overlap_matrix(), gnT)

    out = _tail(x2, ya, oT, gb, w_up_nsa[l].astype(BF16), w_out[l].astype(BF16),
                row(norm_ffn_g[l]), w_ffn_gate[l].astype(BF16), w_ffn_up[l].astype(BF16),
                w_ffn_down[l].astype(BF16), row(norm_final_g))
    return out.reshape(bsz, s, D_MODEL)
```

```python
import functools
import math

import numpy as np
import jax
import jax.numpy as jnp
from jax import lax
from jax.experimental import pallas as pl
from jax.experimental.pallas import tpu as pltpu

F32 = jnp.float32
BF16 = jnp.bfloat16

D_MODEL = 1024
SEQ = 16384
EPS = 1e-6
SSM_WIDTH = 512
SSM_GROUP = 16
SSM_GROUPS = SSM_WIDTH // SSM_GROUP
SSM_STATE = 64
SSM_LANES = SSM_GROUPS * SSM_STATE
NSA_HEADS = 8
NSA_KV_HEADS = 2
GQA = NSA_HEADS // NSA_KV_HEADS
HEAD_DIM = 64
NSA_WIDTH = NSA_HEADS * HEAD_DIM
KV_WIDTH = NSA_KV_HEADS * HEAD_DIM
CMP_LEN = 32
CMP_STRIDE = 16
CMP_HIDDEN = 256
SLC_LEN = 64
N_SEL = 16
N_LOCAL = 2
WINDOW = 512
BIG = 1e4
REL_BUCKETS = 32
REL_MAX_DIST = 128
D_FF = 2816

N_CHUNK = SEQ // CMP_STRIDE
N_CMP = (SEQ - CMP_LEN) // CMP_STRIDE + 1
N_SLC = SEQ // SLC_LEN
TQ = 256
QL = GQA * TQ
NEAR = 128
assert TQ > NEAR and TQ % NEAR == 0
CMP_PAD = 16
CMP_ROW_STEP = 128
N_CMP_PAD = -(-(CMP_PAD + N_CHUNK) // CMP_ROW_STEP) * CMP_ROW_STEP
CMP_BAND = 16 + TQ // CMP_STRIDE
NEG = -1e30
M_FLOOR = -1e29
SCAN_ROWS = 8
S5_TT = 256
SEG = S5_TT // SCAN_ROWS
SSM_SUPER = 2
SG_CH = SSM_WIDTH // SSM_SUPER
SG_LANES = SSM_LANES // SSM_SUPER
LOG2E = math.log2(math.e)
V_ROWS = HEAD_DIM + 16
FAR_KEYS = 512
FAR_BLOCKS = FAR_KEYS // SLC_LEN
K_COLS = HEAD_DIM + 16
CMP_EXTENTS = tuple(range(2 * CMP_ROW_STEP, N_CMP_PAD + 1, CMP_ROW_STEP))

VMEM_LIMIT = 56 * 1024 * 1024

COL_U = 0
COL_Q = 512
COL_KC = 1024
COL_VC = 1152
COL_KS = 1280
COL_VS = 1408
COL_KW = 1536
COL_VW = 1664
COL_GN = 1792
COL_GB = 1816
COL_END = 3864
GATE_COLS = 128
GATE_ROWS = 32
GATES_PER_KV_HEAD = GQA * 3
INPROJ_ROW_RANGES = ((COL_U, COL_Q), (COL_KC, COL_VS), (COL_KW, COL_VW), (COL_GB, COL_END))
INPROJ_COL_RANGES = ((COL_Q, COL_KC), (COL_VS, COL_KW), (COL_VW, COL_GN), (COL_GN, COL_GN + GATE_COLS))
INPROJ_PIECE = 512


def _rms(x, g):
    return x * lax.rsqrt(jnp.mean(x * x, axis=-1, keepdims=True) + EPS) * g


def _dot(a, b):
    return jnp.dot(a, b, preferred_element_type=F32)


def _dot_nt(a, b):
    return lax.dot_general(a, b, (((1,), (1,)), ((), ())), preferred_element_type=F32)


def _inproj_kernel(x_ref, g_ref, w_ref,
                   u_ref, qT_ref, kcr_ref, vcr_ref, ks_ref, kw_ref, vsT_ref, vwT_ref, gnT_ref, gb_ref,
                   wrow_ref, wcol_ref, kstage_ref, vstage_ref):
    @pl.when(pl.program_id(0) == 0)
    def _():
        dst = 0
        for lo, hi in INPROJ_ROW_RANGES:
            for c in range(lo, hi, INPROJ_PIECE):
                n = min(INPROJ_PIECE, hi - c)
                wrow_ref[:, dst:dst + n] = w_ref[c:c + n, :].T.astype(BF16)
                dst += n
        dst = 0
        for lo, hi in INPROJ_COL_RANGES:
            wcol_ref[dst:dst + hi - lo, :] = w_ref[lo:hi, :].astype(BF16)
            dst += hi - lo

    h = _rms(x_ref[...], g_ref[...]).astype(BF16)
    tm = h.shape[0]
    n_main = COL_Q - COL_U + COL_VS - COL_KC + COL_VW - COL_KW
    main = _dot(h, wrow_ref[:, :n_main])
    u_ref[...] = main[:, :COL_Q - COL_U]
    pm = main[:, COL_Q - COL_U:COL_Q - COL_U + COL_VS - COL_KC]
    pm_kw = main[:, n_main - (COL_VW - COL_KW):]
    projT = lambda lo, hi: _dot_nt(wcol_ref[lo:hi, :], h)
    n_q, n_v = COL_KC - COL_Q, COL_KW - COL_VS
    qT_ref[...] = (projT(0, n_q) * (HEAD_DIM ** -0.5 * LOG2E)).astype(BF16)
    vsT = projT(n_q, n_q + n_v)
    vwT = projT(n_q + n_v, n_q + 2 * n_v)
    ones = jnp.ones((V_ROWS - HEAD_DIM, tm), F32)
    tok = pl.program_id(0) * tm + lax.broadcasted_iota(jnp.int32, (tm, K_COLS - HEAD_DIM), 0)
    col = lax.broadcasted_iota(jnp.int32, (tm, K_COLS - HEAD_DIM), 1)
    blk_onehot = jnp.where((tok // SLC_LEN) % FAR_BLOCKS == col, 1.0, 0.0)
    for g in range(NSA_KV_HEADS):
        lo = g * HEAD_DIM
        ks_ref[g] = jnp.concatenate(
            [pm[:, COL_KS - COL_KC + lo:COL_KS - COL_KC + lo + HEAD_DIM], blk_onehot], axis=1).astype(BF16)
        kw_ref[g] = pm_kw[:, lo:lo + HEAD_DIM].astype(BF16)
        vsT_ref[g] = jnp.concatenate([vsT[lo:lo + HEAD_DIM], ones], axis=0).astype(BF16)
        vwT_ref[g] = jnp.concatenate([vwT[lo:lo + HEAD_DIM], ones], axis=0).astype(BF16)
    for src_col, stage_ref, dst_ref in ((COL_KC, kstage_ref, kcr_ref), (COL_VC, vstage_ref, vcr_ref)):
        stage_ref[...] = pm[:, src_col - COL_KC:src_col - COL_KC + KV_WIDTH]
        for t in range(CMP_STRIDE):
            rows = stage_ref[pl.ds(t, tm // CMP_STRIDE, stride=CMP_STRIDE), :]
            for g in range(NSA_KV_HEADS):
                dst_ref[g, :, t * HEAD_DIM:(t + 1) * HEAD_DIM] = rows[:, g * HEAD_DIM:(g + 1) * HEAD_DIM]
    gnT_ref[...] = jax.nn.sigmoid(projT(n_q + 2 * n_v, n_q + 2 * n_v + GATE_ROWS))
    gb_ref[...] = jax.nn.sigmoid(_dot(h, wrow_ref[:, n_main:])).astype(BF16)


def _inproj(x2, g, w, tm=512):
    s = x2.shape[0]
    const = lambda i: (0, 0)
    row = lambda i: (i, 0)
    col = lambda i: (0, i)
    return pl.pallas_call(
        _inproj_kernel,
        grid=(s // tm,),
        in_specs=[
            pl.BlockSpec((tm, D_MODEL), row),
            pl.BlockSpec((1, D_MODEL), const),
            pl.BlockSpec(w.shape, const),
        ],
        out_specs=[
            pl.BlockSpec((tm, SSM_WIDTH), row),
            pl.BlockSpec((NSA_WIDTH, tm), col),
            pl.BlockSpec((NSA_KV_HEADS, tm // CMP_STRIDE, CMP_STRIDE * HEAD_DIM), lambda i: (0, i, 0)),
            pl.BlockSpec((NSA_KV_HEADS, tm // CMP_STRIDE, CMP_STRIDE * HEAD_DIM), lambda i: (0, i, 0)),
            pl.BlockSpec((NSA_KV_HEADS, tm, K_COLS), lambda i: (0, i, 0)),
            pl.BlockSpec((NSA_KV_HEADS, tm, HEAD_DIM), lambda i: (0, i, 0)),
            pl.BlockSpec((NSA_KV_HEADS, V_ROWS, tm), lambda i: (0, 0, i)),
            pl.BlockSpec((NSA_KV_HEADS, V_ROWS, tm), lambda i: (0, 0, i)),
            pl.BlockSpec((GATE_ROWS, tm), col),
            pl.BlockSpec((tm, 2 * D_MODEL), row),
        ],
        out_shape=[
            jax.ShapeDtypeStruct((s, SSM_WIDTH), F32),
            jax.ShapeDtypeStruct((NSA_WIDTH, s), BF16),
            jax.ShapeDtypeStruct((NSA_KV_HEADS, s // CMP_STRIDE, CMP_STRIDE * HEAD_DIM), F32),
            jax.ShapeDtypeStruct((NSA_KV_HEADS, s // CMP_STRIDE, CMP_STRIDE * HEAD_DIM), F32),
            jax.ShapeDtypeStruct((NSA_KV_HEADS, s, K_COLS), BF16),
            jax.ShapeDtypeStruct((NSA_KV_HEADS, s, HEAD_DIM), BF16),
            jax.ShapeDtypeStruct((NSA_KV_HEADS, V_ROWS, s), BF16),
            jax.ShapeDtypeStruct((NSA_KV_HEADS, V_ROWS, s), BF16),
            jax.ShapeDtypeStruct((GATE_ROWS, s), F32),
            jax.ShapeDtypeStruct((s, 2 * D_MODEL), BF16),
        ],
        scratch_shapes=[
            pltpu.VMEM((D_MODEL, sum(hi - lo for lo, hi in INPROJ_ROW_RANGES)), BF16),
            pltpu.VMEM((sum(hi - lo for lo, hi in INPROJ_COL_RANGES), D_MODEL), BF16),
            pltpu.VMEM((tm, KV_WIDTH), F32),
            pltpu.VMEM((tm, KV_WIDTH), F32),
        ],
        compiler_params=pltpu.CompilerParams(
            dimension_semantics=("arbitrary",), vmem_limit_bytes=VMEM_LIMIT),
        name="inproj",
    )(x2, g, w)


def _s5_kernel(u_ref, perm_ref, permT_ref, b_ref, c_ref, pw_ref, seg_ref, d_ref, wglu_ref, wup_ref,
               ya_ref, xre_ref, xim_ref, st_ref, cre_s, cim_s):
    @pl.when(pl.program_id(0) == 0)
    def _():
        cre_s[...] = jnp.zeros_like(cre_s)
        cim_s[...] = jnp.zeros_like(cim_s)

    u = u_ref[...]
    ub = _dot(perm_ref[...], u.astype(BF16)).astype(BF16)
    for sg in range(SSM_SUPER):
        bu = _dot(ub[:, sg * SG_CH:(sg + 1) * SG_CH], b_ref[sg])
        xre_ref[:, sg * SG_LANES:(sg + 1) * SG_LANES] = bu[:, :SG_LANES]
        xim_ref[:, sg * SG_LANES:(sg + 1) * SG_LANES] = bu[:, SG_LANES:]

    def cmul_add(re, im, are, aim, sre, sim):
        return re + are * sre - aim * sim, im + are * sim + aim * sre

    for sg in range(SSM_SUPER):
        lanes = slice(sg * SG_LANES, (sg + 1) * SG_LANES)

        lam_re, lam_im = pw_ref[0, 0:SCAN_ROWS, lanes], pw_ref[1, 0:SCAN_ROWS, lanes]

        def local(j, carry):
            r0 = pl.multiple_of(j * SCAN_ROWS, SCAN_ROWS)
            re, im = cmul_add(xre_ref[pl.ds(r0, SCAN_ROWS), lanes], xim_ref[pl.ds(r0, SCAN_ROWS), lanes],
                              lam_re, lam_im, *carry)
            xre_ref[pl.ds(r0, SCAN_ROWS), lanes] = re
            xim_ref[pl.ds(r0, SCAN_ROWS), lanes] = im
            return re, im

        zero = jnp.zeros((SCAN_ROWS, SG_LANES), F32)
        re, im = lax.fori_loop(0, SEG, local, (zero, zero), unroll=True)

        for k, shift in enumerate((1, 2, 4)):
            re, im = cmul_add(re, im, seg_ref[2 * k, :, lanes], seg_ref[2 * k + 1, :, lanes],
                              pltpu.roll(re, shift, 0), pltpu.roll(im, shift, 0))
        cin_re, cin_im = cre_s[:, lanes], cim_s[:, lanes]
        re, im = cmul_add(re, im, seg_ref[6, :, lanes], seg_ref[7, :, lanes], cin_re, cin_im)
        cre_s[:, lanes] = re[SCAN_ROWS - 1:SCAN_ROWS, :]
        cim_s[:, lanes] = im[SCAN_ROWS - 1:SCAN_ROWS, :]
        first = lax.broadcasted_iota(jnp.int32, (SCAN_ROWS, SG_LANES), 0) == 0
        start_re = jnp.where(first, cin_re, pltpu.roll(re, 1, 0))
        start_im = jnp.where(first, cin_im, pltpu.roll(im, 1, 0))
        start_re = jnp.concatenate([start_re, start_re], axis=0)
        start_im = jnp.concatenate([start_im, start_im], axis=0)

        def fix(jj, carry):
            r0 = pl.multiple_of(jj * 2 * SCAN_ROWS, 2 * SCAN_ROWS)
            rows = pl.ds(r0, 2 * SCAN_ROWS)
            re, im = cmul_add(xre_ref[rows, lanes], xim_ref[rows, lanes],
                              pw_ref[0, rows, lanes], pw_ref[1, rows, lanes], start_re, start_im)
            st_ref[rows, 2 * sg * SG_LANES:(2 * sg + 1) * SG_LANES] = re.astype(BF16)
            st_ref[rows, (2 * sg + 1) * SG_LANES:(2 * sg + 2) * SG_LANES] = im.astype(BF16)
            return carry

        lax.fori_loop(0, SEG // 2, fix, 0, unroll=True)
    y_perm = jnp.concatenate(
        [_dot(st_ref[:, 2 * sg * SG_LANES:(2 * sg + 2) * SG_LANES], c_ref[sg])
         for sg in range(SSM_SUPER)], axis=1)
    y_hi = y_perm.astype(BF16)
    y_lo = (y_perm - y_hi.astype(F32)).astype(BF16)
    y = _dot(permT_ref[...], y_hi) + _dot(permT_ref[...], y_lo) + d_ref[...] * u
    z = jax.nn.gelu(y)
    z = z * jax.nn.sigmoid(_dot(z.astype(BF16), wglu_ref[...]))
    ya_ref[...] = _dot(z.astype(BF16), wup_ref[...])


def _s5(u, b, c, pw, seg, d, wglu, wup):
    s = u.shape[0]
    t = np.arange(S5_TT)
    perm = np.zeros((S5_TT, S5_TT), np.float32)
    perm[SCAN_ROWS * (t % SEG) + t // SEG, t] = 1.0
    permT = jnp.asarray(perm.T, BF16)
    perm = jnp.asarray(perm, BF16)
    const2 = lambda i: (0, 0)
    const3 = lambda i: (0, 0, 0)
    return pl.pallas_call(
        _s5_kernel,
        grid=(s // S5_TT,),
        in_specs=[
            pl.BlockSpec((S5_TT, SSM_WIDTH), lambda i: (i, 0)),
            pl.BlockSpec(perm.shape, const2),
            pl.BlockSpec(permT.shape, const2),
            pl.BlockSpec(b.shape, const3),
            pl.BlockSpec(c.shape, const3),
            pl.BlockSpec(pw.shape, const3),
            pl.BlockSpec(seg.shape, const3),
            pl.BlockSpec((1, SSM_WIDTH), const2),
            pl.BlockSpec(wglu.shape, const2),
            pl.BlockSpec(wup.shape, const2),
        ],
        out_specs=pl.BlockSpec((S5_TT, D_MODEL), lambda i: (i, 0)),
        out_shape=jax.ShapeDtypeStruct((s, D_MODEL), F32),
        scratch_shapes=[
            pltpu.VMEM((S5_TT, SSM_LANES), F32),
            pltpu.VMEM((S5_TT, SSM_LANES), F32),
            pltpu.VMEM((S5_TT, 2 * SSM_LANES), BF16),
            pltpu.VMEM((1, SSM_LANES), F32),
            pltpu.VMEM((1, SSM_LANES), F32),
        ],
        compiler_params=pltpu.CompilerParams(
            dimension_semantics=("arbitrary",), vmem_limit_bytes=VMEM_LIMIT),
        name="s5",
    )(u, perm, permT, b, c, pw, seg, d, wglu, wup)


def _s5_params(a_re, a_im, log_dt, b_re, b_im, c_re, c_im):
    dt = jnp.exp(log_dt.astype(F32))[:, None]
    ar, ai = a_re.astype(F32), a_im.astype(F32)
    zr, zi = ar * dt, ai * dt

    def power(n):
        mag = jnp.exp(n * zr)
        return mag * jnp.cos(n * zi), mag * jnp.sin(n * zi)

    lr, li = power(1.0)
    den = ar * ar + ai * ai
    kr = ((lr - 1.0) * ar + li * ai) / den
    ki = (li * ar - (lr - 1.0) * ai) / den
    br, bi = b_re.astype(F32), b_im.astype(F32)
    bbr = kr[..., None] * br - ki[..., None] * bi
    bbi = kr[..., None] * bi + ki[..., None] * br
    groups_per_super = SSM_GROUPS // SSM_SUPER

    def super_blocks(w, per_group_rows, per_group_cols):
        rows = w.reshape(SSM_SUPER, groups_per_super * per_group_rows, per_group_cols)
        tiled = jnp.tile(rows, (1, 1, groups_per_super))
        same = (lax.broadcasted_iota(jnp.int32, tiled.shape, 1) // per_group_rows
                == lax.broadcasted_iota(jnp.int32, tiled.shape, 2) // per_group_cols)
        return jnp.where(same, tiled, 0.0)

    b_in = lambda b: super_blocks(jnp.transpose(b, (0, 2, 1)), SSM_GROUP, SSM_STATE)
    c_out = lambda c: super_blocks(jnp.transpose(c, (0, 2, 1)), SSM_STATE, SSM_GROUP)
    zr, zi = zr.reshape(1, SSM_LANES), zi.reshape(1, SSM_LANES)
    pw = jnp.stack(power(jnp.arange(1, SEG + 1, dtype=F32)[:, None]))
    pw = jnp.broadcast_to(pw[:, :, None, :], (2, SEG, SCAN_ROWS, SSM_LANES)).reshape(2, S5_TT, SSM_LANES)
    row = jnp.arange(SCAN_ROWS)[:, None]
    seg = []
    for shift in (1, 2, 4):
        pr, pi = power(float(SEG * shift))
        seg += [jnp.where(row >= shift, pr, 0.0), jnp.where(row >= shift, pi, 0.0)]
    seg += list(power(SEG * (row + 1).astype(F32)))
    b_sg = jnp.concatenate([b_in(bbr), b_in(bbi)], axis=2)
    c_sg = jnp.concatenate([c_out(c_re.astype(F32)), -c_out(c_im.astype(F32))], axis=1)
    return b_sg.astype(BF16), c_sg.astype(BF16), pw, jnp.stack(seg)


def _compress_kernel(xk_ref, xv_ref, w1k_ref, w2k_ref, pk_ref, w1v_ref, w2vT_ref, pv_ref, kc_ref, vcT_ref):
    half = CMP_STRIDE * HEAD_DIM

    def hidden(x_ref, w1_ref, pos_ref):
        x = x_ref[0].astype(BF16)
        first = _dot(x, w1_ref[:half, :])
        second = _dot(x, w1_ref[half:, :])
        bias = _dot(jnp.broadcast_to(pos_ref[...], (8, 2 * half)).astype(BF16), w1_ref[...])[:1]
        pre = first + pltpu.roll(second, N_CHUNK - 1, 0) + bias
        return jax.nn.gelu(pre).astype(BF16)

    hk = hidden(xk_ref, w1k_ref, pk_ref)
    kc = _dot(hk, w2k_ref[...])
    tail_pad = N_CMP_PAD - CMP_PAD - N_CHUNK
    kc_ref[0] = jnp.concatenate([jnp.zeros((CMP_PAD, HEAD_DIM), F32), kc,
                                 jnp.zeros((tail_pad, HEAD_DIM), F32)], axis=0).astype(BF16)
    hv = hidden(xv_ref, w1v_ref, pv_ref)
    vcT = _dot_nt(w2vT_ref[...], hv)
    vcT_ref[0] = jnp.concatenate([jnp.zeros((HEAD_DIM, CMP_PAD), F32), vcT,
                                  jnp.zeros((HEAD_DIM, tail_pad), F32)], axis=1).astype(BF16)


def _compress(xk, xv, w1k, w2k, pk, w1v, w2vT, pv):
    const2 = lambda g: (0, 0)
    head = lambda g: (g, 0, 0)
    return pl.pallas_call(
        _compress_kernel,
        grid=(NSA_KV_HEADS,),
        in_specs=[
            pl.BlockSpec((1, N_CHUNK, CMP_STRIDE * HEAD_DIM), head),
            pl.BlockSpec((1, N_CHUNK, CMP_STRIDE * HEAD_DIM), head),
            pl.BlockSpec(w1k.shape, const2),
            pl.BlockSpec(w2k.shape, const2),
            pl.BlockSpec(pk.shape, const2),
            pl.BlockSpec(w1v.shape, const2),
            pl.BlockSpec(w2vT.shape, const2),
            pl.BlockSpec(pv.shape, const2),
        ],
        out_specs=[
            pl.BlockSpec((1, N_CMP_PAD, HEAD_DIM), head),
            pl.BlockSpec((1, HEAD_DIM, N_CMP_PAD), head),
        ],
        out_shape=[
            jax.ShapeDtypeStruct((NSA_KV_HEADS, N_CMP_PAD, HEAD_DIM), BF16),
            jax.ShapeDtypeStruct((NSA_KV_HEADS, HEAD_DIM, N_CMP_PAD), BF16),
        ],
        compiler_params=pltpu.CompilerParams(
            dimension_semantics=("arbitrary",), vmem_limit_bytes=VMEM_LIMIT),
        name="compress",
    )(xk, xv, w1k, w2k, pk, w1v, w2vT, pv)


def _nsa_kernel(qT_ref, kc_ref, vcT_ref, ks_ref, vsT_ref, kw_ref, vwT_ref, bias_ref, bc_ref, mt_ref, gT_ref,
                oT_ref, sc_ref, neg_ref, negfar_ref, m_ref, acc_ref, tot_ref, sbuf0_ref, sbuf1_ref, mloc_ref, pslc_ref):
    i = pl.program_id(1)
    s0 = i * TQ
    qT = jnp.concatenate([qT_ref[r * HEAD_DIM:(r + 1) * HEAD_DIM, :] for r in range(GQA)], axis=1)

    qT_nomask = jnp.concatenate([qT, jnp.zeros((K_COLS - HEAD_DIM, QL), BF16)], axis=0)

    gate0 = pl.program_id(0) * GATES_PER_KV_HEAD

    def gate_row(branch):
        return jnp.concatenate([gT_ref[pl.ds(gate0 + r * 3 + branch, 1), :] for r in range(GQA)], axis=1)

    def reset():
        m_ref[...] = jnp.full_like(m_ref, M_FLOOR)
        acc_ref[...] = jnp.zeros_like(acc_ref)

    def attend(k_ref, vT_ref, start, size, add):
        k = k_ref[0, pl.ds(start, size), :]
        s = _dot(k, qT if k.shape[1] == HEAD_DIM else qT_nomask) + add
        m_prev = m_ref[...]
        m_new = jnp.maximum(m_prev, jnp.max(s, axis=0, keepdims=True))
        alpha = jnp.exp2(m_prev - m_new)
        p = jnp.exp2(s - m_new).astype(BF16)
        acc_ref[...] = alpha * acc_ref[...] + _dot(vT_ref[0, :, pl.ds(start, size)], p)
        m_ref[...] = m_new

    def finish(branch):
        acc = acc_ref[...]
        scale = gate_row(branch) / jnp.maximum(acc[HEAD_DIM:HEAD_DIM + 1, :], 1e-30)
        tot_ref[...] += acc[:HEAD_DIM, :] * scale

    n0 = i * (TQ // CMP_STRIDE)
    band0 = pl.multiple_of(n0 + CMP_PAD - 16, 8)

    def cmp_branch(nrows):
        row = lax.broadcasted_iota(jnp.int32, (nrows, QL), 0)
        live = (row >= CMP_PAD) & (row < band0 + CMP_BAND)
        sc_ref[0:nrows, :] = jnp.where(live, _dot(kc_ref[0, 0:nrows, :], qT), NEG)
        sc_ref[pl.ds(band0, CMP_BAND), :] += bc_ref[0]
        sc = sc_ref[0:nrows, :]
        mc = jnp.maximum(jnp.max(sc, axis=0, keepdims=True), M_FLOOR)
        pc = jnp.exp2(sc - mc)
        pc = pc * (1.0 / jnp.maximum(jnp.sum(pc, axis=0, keepdims=True), 1e-30))
        tot_ref[...] = _dot(vcT_ref[0, :, 0:nrows], pc.astype(BF16)) * gate_row(0)
        imp = pc[:, 0:TQ]
        for r in range(1, GQA):
            imp = imp + pc[:, r * TQ:(r + 1) * TQ]
        mt = mt_ref[:, 0:nrows]
        p_slc = jnp.zeros((N_SLC, TQ), F32)
        rem = imp
        for _ in range(2):
            piece = rem.astype(BF16)
            p_slc = p_slc + _dot(mt, piece)
            rem = rem - piece.astype(F32)
        pslc_ref[...] = p_slc

    prev_rows = 0
    for nrows in CMP_EXTENTS:
        lo, hi = prev_rows, nrows
        pl.when((band0 + CMP_BAND > lo) & (band0 + CMP_BAND <= hi))(functools.partial(cmp_branch, nrows))
        prev_rows = nrows

    reset()
    win_start = jnp.maximum(s0 - WINDOW, 0)
    win_rows = pl.ds(pl.multiple_of(WINDOW - (s0 - win_start), NEAR), WINDOW + TQ)
    attend(kw_ref, vwT_ref, pl.multiple_of(win_start, NEAR), WINDOW + TQ, bias_ref[0, win_rows, :])
    finish(2)

    p_slc = pslc_ref[...]
    blk = lax.broadcasted_iota(jnp.int32, (N_SLC, TQ), 0)
    cur = (s0 + lax.broadcasted_iota(jnp.int32, (N_SLC, TQ), 1)) // SLC_LEN
    valid = blk <= cur
    forced = valid & ((blk == 0) | (blk >= cur - (N_LOCAL - 1)))
    n_pick = N_SEL - (N_LOCAL + 1)
    start_score = jnp.where(forced | ~valid, -jnp.inf, p_slc)
    near_blk = (s0 - NEAR) // SLC_LEN

    def publish(score):
        neg = jnp.where(score == -jnp.inf, 0.0, NEG)
        neg_ref[...] = jnp.concatenate([neg] * GQA, axis=1)
        negfar_ref[...] = jnp.concatenate([jnp.where(blk >= near_blk, NEG, neg)] * GQA, axis=1)

    score = start_score
    for _ in range(n_pick):
        score = jnp.where(score == jnp.max(score, axis=0, keepdims=True), -jnp.inf, score)
    marks = jnp.sum(jnp.where(score == -jnp.inf, 1.0, 0.0), axis=0, keepdims=True)
    cur_q = cur[0:1, :]
    n_forced = jnp.minimum(cur_q + 1, N_LOCAL + 1)
    expected = (N_SLC - 1 - cur_q) + n_forced + jnp.minimum(n_pick, cur_q + 1 - n_forced)
    tied = jnp.max(jnp.abs(marks - expected.astype(F32))) > 0.5
    fast_score = score

    @pl.when(jnp.logical_not(tied))
    def _():
        publish(fast_score)

    @pl.when(tied)
    def _():
        score = jnp.where(forced | ~valid, -jnp.inf, pslc_ref[...])
        blk_f = blk.astype(F32)
        for _ in range(n_pick):
            best = jnp.max(score, axis=0, keepdims=True)
            first = jnp.min(jnp.where(score == best, blk_f, float(N_SLC)), axis=0, keepdims=True)
            score = jnp.where(blk_f == first, -jnp.inf, score)
        publish(score)

    def block_mask(ref, j0, nblk):
        return jnp.concatenate(
            [jnp.broadcast_to(ref[pl.ds(j0 + b, 1), :], (SLC_LEN, QL)) for b in range(nblk)], axis=0)

    reset()
    n_far = (s0 - NEAR + FAR_KEYS - 1) // FAR_KEYS

    sbufs = (sbuf0_ref, sbuf1_ref)

    def far_logits(c, slot):
        start = pl.multiple_of(c * FAR_KEYS, FAR_KEYS)
        mask_rows = negfar_ref[pl.ds(pl.multiple_of(c * FAR_BLOCKS, FAR_BLOCKS), FAR_BLOCKS), :]
        extra = jnp.concatenate([mask_rows, jnp.zeros((K_COLS - HEAD_DIM - FAR_BLOCKS, QL), F32)], axis=0)
        q_masked = jnp.concatenate([qT, extra.astype(BF16)], axis=0)
        s = _dot(ks_ref[0, pl.ds(start, FAR_KEYS), :], q_masked)
        sbufs[slot][...] = s
        mloc_ref[slot] = jnp.max(s, axis=0, keepdims=True)

    def far_consume(c, slot):
        start = pl.multiple_of(c * FAR_KEYS, FAR_KEYS)
        m_prev = m_ref[...]
        m_new = jnp.maximum(m_prev, mloc_ref[slot])
        alpha = jnp.exp2(m_prev - m_new)
        p = jnp.exp2(sbufs[slot][...] - m_new).astype(BF16)
        acc_ref[...] = alpha * acc_ref[...] + _dot(vsT_ref[0, :, pl.ds(start, FAR_KEYS)], p)
        m_ref[...] = m_new

    @pl.when(i >= 1)
    def _():
        n_pairs = (n_far + 1) // 2
        far_logits(0, 0)
        attend(ks_ref, vsT_ref, pl.multiple_of(s0 - NEAR, NEAR), NEAR + TQ,
               bias_ref[0, WINDOW - NEAR:WINDOW + TQ, :] + block_mask(neg_ref, near_blk, (NEAR + TQ) // SLC_LEN))

        def far_body(p, carry):
            far_logits(2 * p + 1, 1)
            far_consume(2 * p, 0)
            far_logits(2 * p + 2, 0)
            far_consume(2 * p + 1, 1)
            return carry

        lax.fori_loop(0, n_pairs - 1, far_body, 0)
        last = 2 * (n_pairs - 1)

        @pl.when(n_far % 2 == 0)
        def _():
            far_logits(last + 1, 1)
            far_consume(last, 0)
            far_consume(last + 1, 1)

        @pl.when(n_far % 2 == 1)
        def _():
            far_consume(last, 0)

    @pl.when(i == 0)
    def _():
        attend(ks_ref, vsT_ref, 0, TQ, bias_ref[0, WINDOW:WINDOW + TQ, :] + block_mask(neg_ref, 0, TQ // SLC_LEN))

    finish(1)

    tot = tot_ref[...]
    for r in range(GQA):
        oT_ref[r * HEAD_DIM:(r + 1) * HEAD_DIM, :] = tot[:, r * TQ:(r + 1) * TQ]


def _nsa(qT, kc, vcT, ks, vsT, kw, vwT, bias_tile, bias_cmp, mt, gnT):
    s = qT.shape[1]
    head3 = lambda g, i: (g, 0, 0)
    return pl.pallas_call(
        _nsa_kernel,
        grid=(NSA_KV_HEADS, s // TQ),
        in_specs=[
            pl.BlockSpec((GQA * HEAD_DIM, TQ), lambda g, i: (g, i)),
            pl.BlockSpec((1, N_CMP_PAD, HEAD_DIM), head3),
            pl.BlockSpec((1, HEAD_DIM, N_CMP_PAD), head3),
            pl.BlockSpec((1, s, K_COLS), head3),
            pl.BlockSpec((1, V_ROWS, s), head3),
            pl.BlockSpec((1, s, HEAD_DIM), head3),
            pl.BlockSpec((1, V_ROWS, s), head3),
            pl.BlockSpec((1, 2 * WINDOW + TQ, QL), head3, pipeline_mode=pl.Buffered(1)),
            pl.BlockSpec((1, CMP_BAND, QL), head3),
            pl.BlockSpec(mt.shape, lambda g, i: (0, 0)),
            pl.BlockSpec((GATE_ROWS, TQ), lambda g, i: (0, i)),
        ],
        out_specs=pl.BlockSpec((GQA * HEAD_DIM, TQ), lambda g, i: (g, i)),
        out_shape=jax.ShapeDtypeStruct((NSA_WIDTH, s), F32),
        scratch_shapes=[
            pltpu.VMEM((N_CMP_PAD, QL), F32),
            pltpu.VMEM((N_SLC, QL), F32),
            pltpu.VMEM((N_SLC, QL), F32),
            pltpu.VMEM((1, QL), F32),
            pltpu.VMEM((V_ROWS, QL), F32),
            pltpu.VMEM((HEAD_DIM, QL), F32),
            pltpu.VMEM((FAR_KEYS, QL), F32),
            pltpu.VMEM((FAR_KEYS, QL), F32),
            pltpu.VMEM((2, 1, QL), F32),
            pltpu.VMEM((N_SLC, TQ), F32),
        ],
        compiler_params=pltpu.CompilerParams(
            dimension_semantics=("arbitrary", "arbitrary"), vmem_limit_bytes=VMEM_LIMIT),
        name="nsa",
    )(qT, kc, vcT, ks, vsT, kw, vwT, bias_tile, bias_cmp, mt, gnT)


def _t5_bucket(dist):
    n = jnp.maximum(dist, 0)
    max_exact = REL_BUCKETS // 2
    nf = jnp.maximum(n, 1).astype(F32)
    large = max_exact + (jnp.log(nf / max_exact) / math.log(REL_MAX_DIST / max_exact)
                         * (REL_BUCKETS - max_exact)).astype(jnp.int32)
    large = jnp.minimum(large, REL_BUCKETS - 1)
    return jnp.where(n < max_exact, n, large)


def _bias_tiles(rel_bias):
    tab = rel_bias.astype(F32)
    tab = (tab[_t5_bucket(jnp.arange(NEAR))] - tab[REL_BUCKETS - 1]).T * LOG2E
    tab = jnp.concatenate([tab, jnp.zeros((NSA_HEADS, 1), F32)], axis=1)

    def by_distance(d):
        return jnp.where(d >= 0, tab[:, jnp.clip(d, 0, NEAR)], NEG)

    def toeplitz(c, nk, nq, step=1):
        n = step * (nk - 1) + nq
        w = by_distance(jnp.arange(n) + c - step * (nk - 1))
        reps = -(-nk * (n + step) // n)
        return jnp.tile(w, (1, reps))[:, :nk * (n + step)].reshape(NSA_HEADS, nk, n + step)[:, ::-1, :nq]

    def per_kv_head(a):
        a = a.reshape(NSA_KV_HEADS, GQA, a.shape[1], TQ)
        return jnp.transpose(a, (0, 2, 1, 3)).reshape(NSA_KV_HEADS, a.shape[2], QL)

    k2, q2 = np.arange(NEAR)[:, None], np.arange(NEAR)[None, :]
    const = lambda a: jnp.broadcast_to(jnp.asarray(a, F32), (NSA_HEADS, NEAR, NEAR))
    edge = WINDOW // NEAR
    nb = TQ // NEAR
    blocks, slabs = {}, {}

    def block(d):
        d = min(max(d, -1), edge + 1)
        if d not in blocks:
            if d < 0:
                b = const(NEG)
            elif d < 2:
                b = toeplitz(d * NEAR, NEAR, NEAR)
            elif d < edge:
                b = const(0.0)
            else:
                b = const(np.where(k2 > q2, 0.0, NEG) if d == edge else NEG)
            blocks[d] = jnp.transpose(b.reshape(NSA_KV_HEADS, GQA, NEAR, NEAR), (0, 2, 1, 3))
        return blocks[d]

    def slab(dd):
        dd = min(max(dd, -nb), edge + 1)
        if dd not in slabs:
            s = jnp.stack([block(dd + b) for b in range(nb)], axis=3)
            slabs[dd] = s.reshape(NSA_KV_HEADS, NEAR, QL)
        return slabs[dd]

    tile = jnp.concatenate([slab(edge - a) for a in range((2 * WINDOW + TQ) // NEAR)], axis=1)
    cmp_band = toeplitz(16 * CMP_STRIDE - (CMP_LEN - 1), CMP_BAND, TQ, step=CMP_STRIDE)
    return tile, per_kv_head(cmp_band)


def _overlap_matrix():
    ratio = SLC_LEN // CMP_STRIDE
    front = CMP_LEN // CMP_STRIDE - 1
    w_ov = np.convolve(np.ones(ratio), np.ones(CMP_LEN // CMP_STRIDE))
    mt = np.zeros((N_SLC, N_CMP_PAD), np.float32)
    for j in range(N_SLC):
        for o, w in enumerate(w_ov):
            n = ratio * j + o - front
            if 0 <= n < N_CMP:
                mt[j, CMP_PAD + n] = w
    return jnp.asarray(mt, BF16)


def _tail_kernel(x_ref, ya_ref, oT_ref, gb_ref, wup_ref, wout_ref, g_ref, wg_ref, wu_ref, wd_ref, gf_ref, o_ref):
    yb = _dot(oT_ref[...].T.astype(BF16), wup_ref[...])
    mix = gb_ref[:, :D_MODEL].astype(F32) * ya_ref[...] + gb_ref[:, D_MODEL:].astype(F32) * yb
    x = x_ref[...] + _dot(mix.astype(BF16), wout_ref[...])
    h = _rms(x, g_ref[...]).astype(BF16)
    f = jax.nn.silu(_dot(h, wg_ref[...])) * _dot(h, wu_ref[...])
    x = x + _dot(f.astype(BF16), wd_ref[...])
    o_ref[...] = _rms(x, gf_ref[...])


def _tail(x2, ya, oT, gb, wup, wout, g, wg, wu, wd, gf, tm=256):
    s = x2.shape[0]
    row = lambda i: (i, 0)
    const = lambda i: (0, 0)
    return pl.pallas_call(
        _tail_kernel,
        grid=(s // tm,),
        in_specs=[
            pl.BlockSpec((tm, D_MODEL), row),
            pl.BlockSpec((tm, D_MODEL), row),
            pl.BlockSpec((NSA_WIDTH, tm), lambda i: (0, i)),
            pl.BlockSpec((tm, 2 * D_MODEL), row),
            pl.BlockSpec(wup.shape, const),
            pl.BlockSpec(wout.shape, const),
            pl.BlockSpec((1, D_MODEL), const),
            pl.BlockSpec(wg.shape, const),
            pl.BlockSpec(wu.shape, const),
            pl.BlockSpec(wd.shape, const),
            pl.BlockSpec((1, D_MODEL), const),
        ],
        out_specs=pl.BlockSpec((tm, D_MODEL), row),
        out_shape=jax.ShapeDtypeStruct((s, D_MODEL), F32),
        compiler_params=pltpu.CompilerParams(
            dimension_semantics=("arbitrary",), vmem_limit_bytes=VMEM_LIMIT),
        name="tail",
    )(x2, ya, oT, gb, wup, wout, g, wg, wu, wd, gf)


def kernel(x, norm_mix_g, w_in, ssm_a_re, ssm_a_im, ssm_log_dt, ssm_b_re, ssm_b_im, ssm_c_re, ssm_c_im, ssm_d, ssm_w_glu, w_up_ssm, cmp_pos_k, cmp_pos_v, cmp_w1_k, cmp_w2_k, cmp_w1_v, cmp_w2_v, rel_bias, w_up_nsa, w_out, norm_ffn_g, w_ffn_gate, w_ffn_up, w_ffn_down, norm_final_g):
    bsz, s, _ = x.shape
    assert (bsz, s) == (1, SEQ) and w_in.shape[0] == 1
    x2 = x.reshape(s, D_MODEL)
    l = 0
    row = lambda v: v.astype(F32).reshape(1, -1)

    u, qT, kcr, vcr, ks, kw, vsT, vwT, gnT, gb = _inproj(x2, row(norm_mix_g[l]), w_in[l].astype(F32).T)

    b_sg, c_sg, pw, seg = _s5_params(
        ssm_a_re[l], ssm_a_im[l], ssm_log_dt[l], ssm_b_re[l], ssm_b_im[l], ssm_c_re[l], ssm_c_im[l])
    ya = _s5(u, b_sg, c_sg, pw, seg, row(ssm_d[l]), ssm_w_glu[l].astype(BF16), w_up_ssm[l].astype(BF16))

    kc, vcT = _compress(kcr, vcr,
                        cmp_w1_k[l].astype(BF16), cmp_w2_k[l].astype(BF16), row(cmp_pos_k[l]),
                        cmp_w1_v[l].astype(BF16), cmp_w2_v[l].T.astype(BF16), row(cmp_pos_v[l]))

    bias_tile, bias_cmp = _bias_tiles(rel_bias)
    oT = _nsa(qT, kc, vcT, ks, vsT, kw, vwT, bias_tile, bias_cmp, _overlap_matrix(), gnT)

    out = _tail(x2, ya, oT, gb, w_up_nsa[l].astype(BF16), w_out[l].astype(BF16),
                row(norm_ffn_g[l]), w_ffn_gate[l].astype(BF16), w_ffn_up[l].astype(BF16),
                w_ffn_down[l].astype(BF16), row(norm_final_g))
    return out.reshape(bsz, s, D_MODEL)
```

```python
import functools
import math

import numpy as np
import jax
import jax.numpy as jnp
from jax import lax
from jax.experimental import pallas as pl
from jax.experimental.pallas import tpu as pltpu

F32 = jnp.float32
BF16 = jnp.bfloat16

D_MODEL = 1024
SEQ = 16384
EPS = 1e-6
SSM_WIDTH = 512
SSM_GROUP = 16
SSM_GROUPS = SSM_WIDTH // SSM_GROUP
SSM_STATE = 64
SSM_LANES = SSM_GROUPS * SSM_STATE
NSA_HEADS = 8
NSA_KV_HEADS = 2
GQA = NSA_HEADS // NSA_KV_HEADS
HEAD_DIM = 64
NSA_WIDTH = NSA_HEADS * HEAD_DIM
KV_WIDTH = NSA_KV_HEADS * HEAD_DIM
CMP_LEN = 32
CMP_STRIDE = 16
SLC_LEN = 64
N_SEL = 16
N_LOCAL = 2
WINDOW = 512
BIG = 1e4
REL_BUCKETS = 32
REL_MAX_DIST = 128

N_CHUNK = SEQ // CMP_STRIDE
N_CMP = (SEQ - CMP_LEN) // CMP_STRIDE + 1
N_SLC = SEQ // SLC_LEN
TQ = 256
QL = GQA * TQ
NEAR = 128
assert TQ > NEAR and TQ % NEAR == 0
CMP_PAD = 16
CMP_ROW_STEP = 128
N_CMP_PAD = -(-(CMP_PAD + N_CHUNK) // CMP_ROW_STEP) * CMP_ROW_STEP
CMP_BAND = 16 + TQ // CMP_STRIDE
NEG = -1e30
M_FLOOR = -1e29
SCAN_ROWS = 8
S5_TT = 256
SEG = S5_TT // SCAN_ROWS
SSM_SUPER = 2
SG_CH = SSM_WIDTH // SSM_SUPER
SG_LANES = SSM_LANES // SSM_SUPER
LOG2E = math.log2(math.e)
V_ROWS = HEAD_DIM + 16
FAR_KEYS = 512
FAR_BLOCKS = FAR_KEYS // SLC_LEN
K_COLS = HEAD_DIM + 16
CMP_EXTENTS = tuple(range(2 * CMP_ROW_STEP, N_CMP_PAD + 1, CMP_ROW_STEP))

VMEM_LIMIT = 56 * 1024 * 1024

COL_U = 0
COL_Q = 512
COL_KC = 1024
COL_VC = 1152
COL_KS = 1280
COL_VS = 1408
COL_KW = 1536
COL_VW = 1664
COL_GN = 1792
COL_GB = 1816
COL_END = 3864
GATE_COLS = 128
GATE_ROWS = 32
GATES_PER_KV_HEAD = GQA * 3
INPROJ_PIECE = 512


def _rms(x, g):
    return x * lax.rsqrt(jnp.mean(x * x, axis=-1, keepdims=True) + EPS) * g


def _dot(a, b):
    return jnp.dot(a, b, preferred_element_type=F32)


def _dot_nt(a, b):
    return lax.dot_general(a, b, (((1,), (1,)), ((), ())), preferred_element_type=F32)


def _inproj_kernel(x_ref, g_ref, w_ref,
                   u_ref, qT_ref, kcr_ref, vcr_ref, ks_ref, kw_ref, vsT_ref, vwT_ref, gnT_ref, gb_ref,
                   wm_ref, wb_ref, kstage_ref, vstage_ref):
    @pl.when(pl.program_id(0) == 0)
    def _():
        for dst_ref, lo, hi in ((wm_ref, 0, COL_GN + GATE_COLS), (wb_ref, COL_GB, COL_END)):
            for c in range(lo, hi, INPROJ_PIECE):
                n = min(INPROJ_PIECE, hi - c)
                dst_ref[:, c - lo:c - lo + n] = w_ref[c:c + n, :].T.astype(BF16)

    h = _rms(x_ref[...], g_ref[...]).astype(BF16)
    tm = h.shape[0]
    pm = _dot(h, wm_ref[:, :COL_GN])
    u_ref[...] = pm[:, COL_U:COL_Q]
    qT_ref[...] = (pm[:, COL_Q:COL_KC] * (HEAD_DIM ** -0.5 * LOG2E)).T.astype(BF16)
    vsT = pm[:, COL_VS:COL_KW].T
    vwT = pm[:, COL_VW:COL_GN].T
    ones = jnp.ones((V_ROWS - HEAD_DIM, tm), F32)
    tok = pl.program_id(0) * tm + lax.broadcasted_iota(jnp.int32, (tm, K_COLS - HEAD_DIM), 0)
    col = lax.broadcasted_iota(jnp.int32, (tm, K_COLS - HEAD_DIM), 1)
    blk_onehot = jnp.where((tok // SLC_LEN) % FAR_BLOCKS == col, 1.0, 0.0)
    for g in range(NSA_KV_HEADS):
        lo = g * HEAD_DIM
        ks_ref[g] = jnp.concatenate(
            [pm[:, COL_KS + lo:COL_KS + lo + HEAD_DIM], blk_onehot], axis=1).astype(BF16)
        kw_ref[g] = pm[:, COL_KW + lo:COL_KW + lo + HEAD_DIM].astype(BF16)
        vsT_ref[g] = jnp.concatenate([vsT[lo:lo + HEAD_DIM], ones], axis=0).astype(BF16)
        vwT_ref[g] = jnp.concatenate([vwT[lo:lo + HEAD_DIM], ones], axis=0).astype(BF16)
    for src_col, stage_ref, dst_ref in ((COL_KC, kstage_ref, kcr_ref), (COL_VC, vstage_ref, vcr_ref)):
        stage_ref[...] = pm[:, src_col:src_col + KV_WIDTH]
        for t in range(CMP_STRIDE):
            rows = stage_ref[pl.ds(t, tm // CMP_STRIDE, stride=CMP_STRIDE), :]
            for g in range(NSA_KV_HEADS):
                dst_ref[g, :, t * HEAD_DIM:(t + 1) * HEAD_DIM] = rows[:, g * HEAD_DIM:(g + 1) * HEAD_DIM]
    gn = jax.nn.sigmoid(_dot(h, wm_ref[:, COL_GN:]))
    gnT_ref[...] = gn.T[:GATE_ROWS, :]
    gb_ref[...] = jax.nn.sigmoid(_dot(h, wb_ref[...])).astype(BF16)


def _inproj(x2, g, w, tm=512):
    s = x2.shape[0]
    const = lambda i: (0, 0)
    row = lambda i: (i, 0)
    col = lambda i: (0, i)
    return pl.pallas_call(
        _inproj_kernel,
        grid=(s // tm,),
        in_specs=[
            pl.BlockSpec((tm, D_MODEL), row),
            pl.BlockSpec((1, D_MODEL), const),
            pl.BlockSpec(w.shape, const),
        ],
        out_specs=[
            pl.BlockSpec((tm, SSM_WIDTH), row),
            pl.BlockSpec((NSA_WIDTH, tm), col),
            pl.BlockSpec((NSA_KV_HEADS, tm // CMP_STRIDE, CMP_STRIDE * HEAD_DIM), lambda i: (0, i, 0)),
            pl.BlockSpec((NSA_KV_HEADS, tm // CMP_STRIDE, CMP_STRIDE * HEAD_DIM), lambda i: (0, i, 0)),
            pl.BlockSpec((NSA_KV_HEADS, tm, K_COLS), lambda i: (0, i, 0)),
            pl.BlockSpec((NSA_KV_HEADS, tm, HEAD_DIM), lambda i: (0, i, 0)),
            pl.BlockSpec((NSA_KV_HEADS, V_ROWS, tm), lambda i: (0, 0, i)),
            pl.BlockSpec((NSA_KV_HEADS, V_ROWS, tm), lambda i: (0, 0, i)),
            pl.BlockSpec((GATE_ROWS, tm), col),
            pl.BlockSpec((tm, 2 * D_MODEL), row),
        ],
        out_shape=[
            jax.ShapeDtypeStruct((s, SSM_WIDTH), F32),
            jax.ShapeDtypeStruct((NSA_WIDTH, s), BF16),
            jax.ShapeDtypeStruct((NSA_KV_HEADS, s // CMP_STRIDE, CMP_STRIDE * HEAD_DIM), F32),
            jax.ShapeDtypeStruct((NSA_KV_HEADS, s // CMP_STRIDE, CMP_STRIDE * HEAD_DIM), F32),
            jax.ShapeDtypeStruct((NSA_KV_HEADS, s, K_COLS), BF16),
            jax.ShapeDtypeStruct((NSA_KV_HEADS, s, HEAD_DIM), BF16),
            jax.ShapeDtypeStruct((NSA_KV_HEADS, V_ROWS, s), BF16),
            jax.ShapeDtypeStruct((NSA_KV_HEADS, V_ROWS, s), BF16),
            jax.ShapeDtypeStruct((GATE_ROWS, s), F32),
            jax.ShapeDtypeStruct((s, 2 * D_MODEL), BF16),
        ],
        scratch_shapes=[
            pltpu.VMEM((D_MODEL, COL_GN + GATE_COLS), BF16),
            pltpu.VMEM((D_MODEL, COL_END - COL_GB), BF16),
            pltpu.VMEM((tm, KV_WIDTH), F32),
            pltpu.VMEM((tm, KV_WIDTH), F32),
        ],
        compiler_params=pltpu.CompilerParams(
            dimension_semantics=("arbitrary",), vmem_limit_bytes=VMEM_LIMIT),
        name="inproj",
    )(x2, g, w)


def _s5_kernel(u_ref, perm_ref, permT_ref, b_ref, c_ref, pw_ref, seg_ref, d_ref, wglu_ref, wup_ref,
               ya_ref, xre_ref, xim_ref, st_ref, cre_s, cim_s):
    @pl.when(pl.program_id(0) == 0)
    def _():
        cre_s[...] = jnp.zeros_like(cre_s)
        cim_s[...] = jnp.zeros_like(cim_s)

    u = u_ref[...]
    ub = _dot(perm_ref[...], u.astype(BF16)).astype(BF16)
    for sg in range(SSM_SUPER):
        bu = _dot(ub[:, sg * SG_CH:(sg + 1) * SG_CH], b_ref[sg])
        xre_ref[:, sg * SG_LANES:(sg + 1) * SG_LANES] = bu[:, :SG_LANES]
        xim_ref[:, sg * SG_LANES:(sg + 1) * SG_LANES] = bu[:, SG_LANES:]

    def cmul_add(re, im, are, aim, sre, sim):
        return re + are * sre - aim * sim, im + are * sim + aim * sre

    for sg in range(SSM_SUPER):
        lanes = slice(sg * SG_LANES, (sg + 1) * SG_LANES)

        lam_re, lam_im = pw_ref[0, 0:SCAN_ROWS, lanes], pw_ref[1, 0:SCAN_ROWS, lanes]

        def local(j, carry):
            r0 = pl.multiple_of(j * SCAN_ROWS, SCAN_ROWS)
            re, im = cmul_add(xre_ref[pl.ds(r0, SCAN_ROWS), lanes], xim_ref[pl.ds(r0, SCAN_ROWS), lanes],
                              lam_re, lam_im, *carry)
            xre_ref[pl.ds(r0, SCAN_ROWS), lanes] = re
            xim_ref[pl.ds(r0, SCAN_ROWS), lanes] = im
            return re, im

        zero = jnp.zeros((SCAN_ROWS, SG_LANES), F32)
        re, im = lax.fori_loop(0, SEG, local, (zero, zero), unroll=True)

        for k, shift in enumerate((1, 2, 4)):
            re, im = cmul_add(re, im, seg_ref[2 * k, :, lanes], seg_ref[2 * k + 1, :, lanes],
                              pltpu.roll(re, shift, 0), pltpu.roll(im, shift, 0))
        cin_re, cin_im = cre_s[:, lanes], cim_s[:, lanes]
        re, im = cmul_add(re, im, seg_ref[6, :, lanes], seg_ref[7, :, lanes], cin_re, cin_im)
        cre_s[:, lanes] = re[SCAN_ROWS - 1:SCAN_ROWS, :]
        cim_s[:, lanes] = im[SCAN_ROWS - 1:SCAN_ROWS, :]
        first = lax.broadcasted_iota(jnp.int32, (SCAN_ROWS, SG_LANES), 0) == 0
        start_re = jnp.where(first, cin_re, pltpu.roll(re, 1, 0))
        start_im = jnp.where(first, cin_im, pltpu.roll(im, 1, 0))
        start_re = jnp.concatenate([start_re, start_re], axis=0)
        start_im = jnp.concatenate([start_im, start_im], axis=0)

        def fix(jj, carry):
            r0 = pl.multiple_of(jj * 2 * SCAN_ROWS, 2 * SCAN_ROWS)
            rows = pl.ds(r0, 2 * SCAN_ROWS)
            re, im = cmul_add(xre_ref[rows, lanes], xim_ref[rows, lanes],
                              pw_ref[0, rows, lanes], pw_ref[1, rows, lanes], start_re, start_im)
            st_ref[rows, 2 * sg * SG_LANES:(2 * sg + 1) * SG_LANES] = re.astype(BF16)
            st_ref[rows, (2 * sg + 1) * SG_LANES:(2 * sg + 2) * SG_LANES] = im.astype(BF16)
            return carry

        lax.fori_loop(0, SEG // 2, fix, 0, unroll=True)
    y_perm = jnp.concatenate(
        [_dot(st_ref[:, 2 * sg * SG_LANES:(2 * sg + 2) * SG_LANES], c_ref[sg])
         for sg in range(SSM_SUPER)], axis=1)
    y_hi = y_perm.astype(BF16)
    y_lo = (y_perm - y_hi.astype(F32)).astype(BF16)
    y = _dot(permT_ref[...], y_hi) + _dot(permT_ref[...], y_lo) + d_ref[...] * u
    z = jax.nn.gelu(y)
    z = z * jax.nn.sigmoid(_dot(z.astype(BF16), wglu_ref[...]))
    ya_ref[...] = _dot(z.astype(BF16), wup_ref[...])


def _s5(u, b, c, pw, seg, d, wglu, wup):
    s = u.shape[0]
    t = np.arange(S5_TT)
    perm = np.zeros((S5_TT, S5_TT), np.float32)
    perm[SCAN_ROWS * (t % SEG) + t // SEG, t] = 1.0
    permT = jnp.asarray(perm.T, BF16)
    perm = jnp.asarray(perm, BF16)
    const2 = lambda i: (0, 0)
    const3 = lambda i: (0, 0, 0)
    return pl.pallas_call(
        _s5_kernel,
        grid=(s // S5_TT,),
        in_specs=[
            pl.BlockSpec((S5_TT, SSM_WIDTH), lambda i: (i, 0)),
            pl.BlockSpec(perm.shape, const2),
            pl.BlockSpec(permT.shape, const2),
            pl.BlockSpec(b.shape, const3),
            pl.BlockSpec(c.shape, const3),
            pl.BlockSpec(pw.shape, const3),
            pl.BlockSpec(seg.shape, const3),
            pl.BlockSpec((1, SSM_WIDTH), const2),
            pl.BlockSpec(wglu.shape, const2),
            pl.BlockSpec(wup.shape, const2),
        ],
        out_specs=pl.BlockSpec((S5_TT, D_MODEL), lambda i: (i, 0)),
        out_shape=jax.ShapeDtypeStruct((s, D_MODEL), F32),
        scratch_shapes=[
            pltpu.VMEM((S5_TT, SSM_LANES), F32),
            pltpu.VMEM((S5_TT, SSM_LANES), F32),
            pltpu.VMEM((S5_TT, 2 * SSM_LANES), BF16),
            pltpu.VMEM((1, SSM_LANES), F32),
            pltpu.VMEM((1, SSM_LANES), F32),
        ],
        compiler_params=pltpu.CompilerParams(
            dimension_semantics=("arbitrary",), vmem_limit_bytes=VMEM_LIMIT),
        name="s5",
    )(u, perm, permT, b, c, pw, seg, d, wglu, wup)


def _s5_params(a_re, a_im, log_dt, b_re, b_im, c_re, c_im):
    dt = jnp.exp(log_dt.astype(F32))[:, None]
    ar, ai = a_re.astype(F32), a_im.astype(F32)
    zr, zi = ar * dt, ai * dt

    def power(n):
        mag = jnp.exp(n * zr)
        return mag * jnp.cos(n * zi), mag * jnp.sin(n * zi)

    lr, li = power(1.0)
    den = ar * ar + ai * ai
    kr = ((lr - 1.0) * ar + li * ai) / den
    ki = (li * ar - (lr - 1.0) * ai) / den
    br, bi = b_re.astype(F32), b_im.astype(F32)
    bbr = kr[..., None] * br - ki[..., None] * bi
    bbi = kr[..., None] * bi + ki[..., None] * br
    groups_per_super = SSM_GROUPS // SSM_SUPER

    def super_blocks(w, per_group_rows, per_group_cols):
        rows = w.reshape(SSM_SUPER, groups_per_super * per_group_rows, per_group_cols)
        tiled = jnp.tile(rows, (1, 1, groups_per_super))
        same = (lax.broadcasted_iota(jnp.int32, tiled.shape, 1) // per_group_rows
                == lax.broadcasted_iota(jnp.int32, tiled.shape, 2) // per_group_cols)
        return jnp.where(same, tiled, 0.0)

    b_in = lambda b: super_blocks(jnp.transpose(b, (0, 2, 1)), SSM_GROUP, SSM_STATE)
    c_out = lambda c: super_blocks(jnp.transpose(c, (0, 2, 1)), SSM_STATE, SSM_GROUP)
    zr, zi = zr.reshape(1, SSM_LANES), zi.reshape(1, SSM_LANES)
    pw = jnp.stack(power(jnp.arange(1, SEG + 1, dtype=F32)[:, None]))
    pw = jnp.broadcast_to(pw[:, :, None, :], (2, SEG, SCAN_ROWS, SSM_LANES)).reshape(2, S5_TT, SSM_LANES)
    row = jnp.arange(SCAN_ROWS)[:, None]
    seg = []
    for shift in (1, 2, 4):
        pr, pi = power(float(SEG * shift))
        seg += [jnp.where(row >= shift, pr, 0.0), jnp.where(row >= shift, pi, 0.0)]
    seg += list(power(SEG * (row + 1).astype(F32)))
    b_sg = jnp.concatenate([b_in(bbr), b_in(bbi)], axis=2)
    c_sg = jnp.concatenate([c_out(c_re.astype(F32)), -c_out(c_im.astype(F32))], axis=1)
    return b_sg.astype(BF16), c_sg.astype(BF16), pw, jnp.stack(seg)


def _compress_kernel(xk_ref, xv_ref, w1k_ref, w2k_ref, pk_ref, w1v_ref, w2vT_ref, pv_ref, kc_ref, vcT_ref):
    half = CMP_STRIDE * HEAD_DIM

    def hidden(x_ref, w1_ref, pos_ref):
        x = x_ref[0].astype(BF16)
        first = _dot(x, w1_ref[:half, :])
        second = _dot(x, w1_ref[half:, :])
        bias = _dot(jnp.broadcast_to(pos_ref[...], (8, 2 * half)).astype(BF16), w1_ref[...])[:1]
        pre = first + pltpu.roll(second, N_CHUNK - 1, 0) + bias
        return jax.nn.gelu(pre).astype(BF16)

    hk = hidden(xk_ref, w1k_ref, pk_ref)
    kc = _dot(hk, w2k_ref[...])
    tail_pad = N_CMP_PAD - CMP_PAD - N_CHUNK
    kc_ref[0] = jnp.concatenate([jnp.zeros((CMP_PAD, HEAD_DIM), F32), kc,
                                 jnp.zeros((tail_pad, HEAD_DIM), F32)], axis=0).astype(BF16)
    hv = hidden(xv_ref, w1v_ref, pv_ref)
    vcT = _dot_nt(w2vT_ref[...], hv)
    vcT_ref[0] = jnp.concatenate([jnp.zeros((HEAD_DIM, CMP_PAD), F32), vcT,
                                  jnp.zeros((HEAD_DIM, tail_pad), F32)], axis=1).astype(BF16)


def _compress(xk, xv, w1k, w2k, pk, w1v, w2vT, pv):
    const2 = lambda g: (0, 0)
    head = lambda g: (g, 0, 0)
    return pl.pallas_call(
        _compress_kernel,
        grid=(NSA_KV_HEADS,),
        in_specs=[
            pl.BlockSpec((1, N_CHUNK, CMP_STRIDE * HEAD_DIM), head),
            pl.BlockSpec((1, N_CHUNK, CMP_STRIDE * HEAD_DIM), head),
            pl.BlockSpec(w1k.shape, const2),
            pl.BlockSpec(w2k.shape, const2),
            pl.BlockSpec(pk.shape, const2),
            pl.BlockSpec(w1v.shape, const2),
            pl.BlockSpec(w2vT.shape, const2),
            pl.BlockSpec(pv.shape, const2),
        ],
        out_specs=[
            pl.BlockSpec((1, N_CMP_PAD, HEAD_DIM), head),
            pl.BlockSpec((1, HEAD_DIM, N_CMP_PAD), head),
        ],
        out_shape=[
            jax.ShapeDtypeStruct((NSA_KV_HEADS, N_CMP_PAD, HEAD_DIM), BF16),
            jax.ShapeDtypeStruct((NSA_KV_HEADS, HEAD_DIM, N_CMP_PAD), BF16),
        ],
        compiler_params=pltpu.CompilerParams(
            dimension_semantics=("arbitrary",), vmem_limit_bytes=VMEM_LIMIT),
        name="compress",
    )(xk, xv, w1k, w2k, pk, w1v, w2vT, pv)


def _nsa_kernel(qT_ref, kc_ref, vcT_ref, ks_ref, vsT_ref, kw_ref, vwT_ref, bias_ref, bc_ref, mt_ref, gT_ref,
                oT_ref, sc_ref, neg_ref, negfar_ref, m_ref, acc_ref, tot_ref, sbuf0_ref, sbuf1_ref, mloc_ref, pslc_ref):
    i = pl.program_id(1)
    s0 = i * TQ
    qT = jnp.concatenate([qT_ref[r * HEAD_DIM:(r + 1) * HEAD_DIM, :] for r in range(GQA)], axis=1)

    qT_nomask = jnp.concatenate([qT, jnp.zeros((K_COLS - HEAD_DIM, QL), BF16)], axis=0)

    gate0 = pl.program_id(0) * GATES_PER_KV_HEAD

    def gate_row(branch):
        return jnp.concatenate([gT_ref[pl.ds(gate0 + r * 3 + branch, 1), :] for r in range(GQA)], axis=1)

    def reset():
        m_ref[...] = jnp.full_like(m_ref, M_FLOOR)
        acc_ref[...] = jnp.zeros_like(acc_ref)

    def attend(k_ref, vT_ref, start, size, add):
        k = k_ref[0, pl.ds(start, size), :]
        s = _dot(k, qT if k.shape[1] == HEAD_DIM else qT_nomask) + add
        m_prev = m_ref[...]
        m_new = jnp.maximum(m_prev, jnp.max(s, axis=0, keepdims=True))
        alpha = jnp.exp2(m_prev - m_new)
        p = jnp.exp2(s - m_new).astype(BF16)
        acc_ref[...] = alpha * acc_ref[...] + _dot(vT_ref[0, :, pl.ds(start, size)], p)
        m_ref[...] = m_new

    def finish(branch):
        acc = acc_ref[...]
        scale = gate_row(branch) / jnp.maximum(acc[HEAD_DIM:HEAD_DIM + 1, :], 1e-30)
        tot_ref[...] += acc[:HEAD_DIM, :] * scale

    n0 = i * (TQ // CMP_STRIDE)
    band0 = pl.multiple_of(n0 + CMP_PAD - 16, 8)

    def cmp_branch(nrows):
        row = lax.broadcasted_iota(jnp.int32, (nrows, QL), 0)
        live = (row >= CMP_PAD) & (row < band0 + CMP_BAND)
        sc_ref[0:nrows, :] = jnp.where(live, _dot(kc_ref[0, 0:nrows, :], qT), NEG)
        sc_ref[pl.ds(band0, CMP_BAND), :] += bc_ref[0]
        sc = sc_ref[0:nrows, :]
        mc = jnp.maximum(jnp.max(sc, axis=0, keepdims=True), M_FLOOR)
        pc = jnp.exp2(sc - mc)
        pc = pc * (1.0 / jnp.maximum(jnp.sum(pc, axis=0, keepdims=True), 1e-30))
        tot_ref[...] = _dot(vcT_ref[0, :, 0:nrows], pc.astype(BF16)) * gate_row(0)
        imp = pc[:, 0:TQ]
        for r in range(1, GQA):
            imp = imp + pc[:, r * TQ:(r + 1) * TQ]
        mt = mt_ref[:, 0:nrows]
        p_slc = jnp.zeros((N_SLC, TQ), F32)
        rem = imp
        for _ in range(2):
            piece = rem.astype(BF16)
            p_slc = p_slc + _dot(mt, piece)
            rem = rem - piece.astype(F32)
        pslc_ref[...] = p_slc

    prev_rows = 0
    for nrows in CMP_EXTENTS:
        lo, hi = prev_rows, nrows
        pl.when((band0 + CMP_BAND > lo) & (band0 + CMP_BAND <= hi))(functools.partial(cmp_branch, nrows))
        prev_rows = nrows

    reset()
    win_start = jnp.maximum(s0 - WINDOW, 0)
    win_rows = pl.ds(pl.multiple_of(WINDOW - (s0 - win_start), NEAR), WINDOW + TQ)
    attend(kw_ref, vwT_ref, pl.multiple_of(win_start, NEAR), WINDOW + TQ, bias_ref[0, win_rows, :])
    finish(2)

    p_slc = pslc_ref[...]
    blk = lax.broadcasted_iota(jnp.int32, (N_SLC, TQ), 0)
    cur = (s0 + lax.broadcasted_iota(jnp.int32, (N_SLC, TQ), 1)) // SLC_LEN
    valid = blk <= cur
    forced = valid & ((blk == 0) | (blk >= cur - (N_LOCAL - 1)))
    n_pick = N_SEL - (N_LOCAL + 1)
    start_score = jnp.where(forced | ~valid, -jnp.inf, p_slc)
    near_blk = (s0 - NEAR) // SLC_LEN

    def publish(score):
        neg = jnp.where(score == -jnp.inf, 0.0, NEG)
        neg_ref[...] = jnp.concatenate([neg] * GQA, axis=1)
        negfar_ref[...] = jnp.concatenate([jnp.where(blk >= near_blk, NEG, neg)] * GQA, axis=1)

    score = start_score
    for _ in range(n_pick):
        score = jnp.where(score == jnp.max(score, axis=0, keepdims=True), -jnp.inf, score)
    marks = jnp.sum(jnp.where(score == -jnp.inf, 1.0, 0.0), axis=0, keepdims=True)
    cur_q = cur[0:1, :]
    n_forced = jnp.minimum(cur_q + 1, N_LOCAL + 1)
    expected = (N_SLC - 1 - cur_q) + n_forced + jnp.minimum(n_pick, cur_q + 1 - n_forced)
    tied = jnp.max(jnp.abs(marks - expected.astype(F32))) > 0.5
    fast_score = score

    @pl.when(jnp.logical_not(tied))
    def _():
        publish(fast_score)

    @pl.when(tied)
    def _():
        score = jnp.where(forced | ~valid, -jnp.inf, pslc_ref[...])
        blk_f = blk.astype(F32)
        for _ in range(n_pick):
            best = jnp.max(score, axis=0, keepdims=True)
            first = jnp.min(jnp.where(score == best, blk_f, float(N_SLC)), axis=0, keepdims=True)
            score = jnp.where(blk_f == first, -jnp.inf, score)
        publish(score)

    def block_mask(ref, j0, nblk):
        return jnp.concatenate(
            [jnp.broadcast_to(ref[pl.ds(j0 + b, 1), :], (SLC_LEN, QL)) for b in range(nblk)], axis=0)

    reset()
    n_far = (s0 - NEAR + FAR_KEYS - 1) // FAR_KEYS

    sbufs = (sbuf0_ref, sbuf1_ref)

    def far_logits(c, slot):
        start = pl.multiple_of(c * FAR_KEYS, FAR_KEYS)
        mask_rows = negfar_ref[pl.ds(pl.multiple_of(c * FAR_BLOCKS, FAR_BLOCKS), FAR_BLOCKS), :]
        extra = jnp.concatenate([mask_rows, jnp.zeros((K_COLS - HEAD_DIM - FAR_BLOCKS, QL), F32)], axis=0)
        q_masked = jnp.concatenate([qT, extra.astype(BF16)], axis=0)
        s = _dot(ks_ref[0, pl.ds(start, FAR_KEYS), :], q_masked)
        sbufs[slot][...] = s
        mloc_ref[slot] = jnp.max(s, axis=0, keepdims=True)

    def far_consume(c, slot):
        start = pl.multiple_of(c * FAR_KEYS, FAR_KEYS)
        m_prev = m_ref[...]
        m_new = jnp.maximum(m_prev, mloc_ref[slot])
        alpha = jnp.exp2(m_prev - m_new)
        p = jnp.exp2(sbufs[slot][...] - m_new).astype(BF16)
        acc_ref[...] = alpha * acc_ref[...] + _dot(vsT_ref[0, :, pl.ds(start, FAR_KEYS)], p)
        m_ref[...] = m_new

    @pl.when(i >= 1)
    def _():
        n_pairs = (n_far + 1) // 2
        far_logits(0, 0)
        attend(ks_ref, vsT_ref, pl.multiple_of(s0 - NEAR, NEAR), NEAR + TQ,
               bias_ref[0, WINDOW - NEAR:WINDOW + TQ, :] + block_mask(neg_ref, near_blk, (NEAR + TQ) // SLC_LEN))

        def far_body(p, carry):
            far_logits(2 * p + 1, 1)
            far_consume(2 * p, 0)
            far_logits(2 * p + 2, 0)
            far_consume(2 * p + 1, 1)
            return carry

        lax.fori_loop(0, n_pairs - 1, far_body, 0)
        last = 2 * (n_pairs - 1)

        @pl.when(n_far % 2 == 0)
        def _():
            far_logits(last + 1, 1)
            far_consume(last, 0)
            far_consume(last + 1, 1)

        @pl.when(n_far % 2 == 1)
        def _():
            far_consume(last, 0)

    @pl.when(i == 0)
    def _():
        attend(ks_ref, vsT_ref, 0, TQ, bias_ref[0, WINDOW:WINDOW + TQ, :] + block_mask(neg_ref, 0, TQ // SLC_LEN))

    finish(1)

    tot = tot_ref[...]
    for r in range(GQA):
        oT_ref[r * HEAD_DIM:(r + 1) * HEAD_DIM, :] = tot[:, r * TQ:(r + 1) * TQ]


def _nsa(qT, kc, vcT, ks, vsT, kw, vwT, bias_tile, bias_cmp, mt, gnT):
    s = qT.shape[1]
    head3 = lambda g, i: (g, 0, 0)
    return pl.pallas_call(
        _nsa_kernel,
        grid=(NSA_KV_HEADS, s // TQ),
        in_specs=[
            pl.BlockSpec((GQA * HEAD_DIM, TQ), lambda g, i: (g, i)),
            pl.BlockSpec((1, N_CMP_PAD, HEAD_DIM), head3),
            pl.BlockSpec((1, HEAD_DIM, N_CMP_PAD), head3),
            pl.BlockSpec((1, s, K_COLS), head3),
            pl.BlockSpec((1, V_ROWS, s), head3),
            pl.BlockSpec((1, s, HEAD_DIM), head3),
            pl.BlockSpec((1, V_ROWS, s), head3),
            pl.BlockSpec((1, 2 * WINDOW + TQ, QL), head3, pipeline_mode=pl.Buffered(1)),
            pl.BlockSpec((1, CMP_BAND, QL), head3),
            pl.BlockSpec(mt.shape, lambda g, i: (0, 0)),
            pl.BlockSpec((GATE_ROWS, TQ), lambda g, i: (0, i)),
        ],
        out_specs=pl.BlockSpec((GQA * HEAD_DIM, TQ), lambda g, i: (g, i)),
        out_shape=jax.ShapeDtypeStruct((NSA_WIDTH, s), F32),
        scratch_shapes=[
            pltpu.VMEM((N_CMP_PAD, QL), F32),
            pltpu.VMEM((N_SLC, QL), F32),
            pltpu.VMEM((N_SLC, QL), F32),
            pltpu.VMEM((1, QL), F32),
            pltpu.VMEM((V_ROWS, QL), F32),
            pltpu.VMEM((HEAD_DIM, QL), F32),
            pltpu.VMEM((FAR_KEYS, QL), F32),
            pltpu.VMEM((FAR_KEYS, QL), F32),
            pltpu.VMEM((2, 1, QL), F32),
            pltpu.VMEM((N_SLC, TQ), F32),
        ],
        compiler_params=pltpu.CompilerParams(
            dimension_semantics=("arbitrary", "arbitrary"), vmem_limit_bytes=VMEM_LIMIT),
        name="nsa",
    )(qT, kc, vcT, ks, vsT, kw, vwT, bias_tile, bias_cmp, mt, gnT)


def _t5_bucket(dist):
    n = jnp.maximum(dist, 0)
    max_exact = REL_BUCKETS // 2
    nf = jnp.maximum(n, 1).astype(F32)
    large = max_exact + (jnp.log(nf / max_exact) / math.log(REL_MAX_DIST / max_exact)
                         * (REL_BUCKETS - max_exact)).astype(jnp.int32)
    large = jnp.minimum(large, REL_BUCKETS - 1)
    return jnp.where(n < max_exact, n, large)


def _bias_tiles(rel_bias):
    tab = rel_bias.astype(F32)
    tab = (tab[_t5_bucket(jnp.arange(NEAR))] - tab[REL_BUCKETS - 1]).T * LOG2E
    tab = jnp.concatenate([tab, jnp.zeros((NSA_HEADS, 1), F32)], axis=1)

    def by_distance(d):
        return jnp.where(d >= 0, tab[:, jnp.clip(d, 0, NEAR)], NEG)

    def toeplitz(c, nk, nq, step=1):
        n = step * (nk - 1) + nq
        w = by_distance(jnp.arange(n) + c - step * (nk - 1))
        reps = -(-nk * (n + step) // n)
        return jnp.tile(w, (1, reps))[:, :nk * (n + step)].reshape(NSA_HEADS, nk, n + step)[:, ::-1, :nq]

    def per_kv_head(a):
        a = a.reshape(NSA_KV_HEADS, GQA, a.shape[1], TQ)
        return jnp.transpose(a, (0, 2, 1, 3)).reshape(NSA_KV_HEADS, a.shape[2], QL)

    k2, q2 = np.arange(NEAR)[:, None], np.arange(NEAR)[None, :]
    const = lambda a: jnp.broadcast_to(jnp.asarray(a, F32), (NSA_HEADS, NEAR, NEAR))
    edge = WINDOW // NEAR
    nb = TQ // NEAR
    blocks, slabs = {}, {}

    def block(d):
        d = min(max(d, -1), edge + 1)
        if d not in blocks:
            if d < 0:
                b = const(NEG)
            elif d < 2:
                b = toeplitz(d * NEAR, NEAR, NEAR)
            elif d < edge:
                b = const(0.0)
            else:
                b = const(np.where(k2 > q2, 0.0, NEG) if d == edge else NEG)
            blocks[d] = jnp.transpose(b.reshape(NSA_KV_HEADS, GQA, NEAR, NEAR), (0, 2, 1, 3))
        return blocks[d]

    def slab(dd):
        dd = min(max(dd, -nb), edge + 1)
        if dd not in slabs:
            s = jnp.stack([block(dd + b) for b in range(nb)], axis=3)
            slabs[dd] = s.reshape(NSA_KV_HEADS, NEAR, QL)
        return slabs[dd]

    tile = jnp.concatenate([slab(edge - a) for a in range((2 * WINDOW + TQ) // NEAR)], axis=1)
    cmp_band = toeplitz(16 * CMP_STRIDE - (CMP_LEN - 1), CMP_BAND, TQ, step=CMP_STRIDE)
    return tile, per_kv_head(cmp_band)


def _overlap_matrix():
    ratio = SLC_LEN // CMP_STRIDE
    front = CMP_LEN // CMP_STRIDE - 1
    w_ov = np.convolve(np.ones(ratio), np.ones(CMP_LEN // CMP_STRIDE))
    mt = np.zeros((N_SLC, N_CMP_PAD), np.float32)
    for j in range(N_SLC):
        for o, w in enumerate(w_ov):
            n = ratio * j + o - front
            if 0 <= n < N_CMP:
                mt[j, CMP_PAD + n] = w
    return jnp.asarray(mt, BF16)


def _tail_kernel(x_ref, ya_ref, oT_ref, gb_ref, wup_ref, wout_ref, g_ref, wg_ref, wu_ref, wd_ref, gf_ref, o_ref):
    yb = _dot(oT_ref[...].T.astype(BF16), wup_ref[...])
    mix = gb_ref[:, :D_MODEL].astype(F32) * ya_ref[...] + gb_ref[:, D_MODEL:].astype(F32) * yb
    x = x_ref[...] + _dot(mix.astype(BF16), wout_ref[...])
    h = _rms(x, g_ref[...]).astype(BF16)
    f = jax.nn.silu(_dot(h, wg_ref[...])) * _dot(h, wu_ref[...])
    x = x + _dot(f.astype(BF16), wd_ref[...])
    o_ref[...] = _rms(x, gf_ref[...])


def _tail(x2, ya, oT, gb, wup, wout, g, wg, wu, wd, gf, tm=256):
    s = x2.shape[0]
    row = lambda i: (i, 0)
    const = lambda i: (0, 0)
    return pl.pallas_call(
        _tail_kernel,
        grid=(s // tm,),
        in_specs=[
            pl.BlockSpec((tm, D_MODEL), row),
            pl.BlockSpec((tm, D_MODEL), row),
            pl.BlockSpec((NSA_WIDTH, tm), lambda i: (0, i)),
            pl.BlockSpec((tm, 2 * D_MODEL), row),
            pl.BlockSpec(wup.shape, const),
            pl.BlockSpec(wout.shape, const),
            pl.BlockSpec((1, D_MODEL), const),
            pl.BlockSpec(wg.shape, const),
            pl.BlockSpec(wu.shape, const),
            pl.BlockSpec(wd.shape, const),
            pl.BlockSpec((1, D_MODEL), const),
        ],
        out_specs=pl.BlockSpec((tm, D_MODEL), row),
        out_shape=jax.ShapeDtypeStruct((s, D_MODEL), F32),
        compiler_params=pltpu.CompilerParams(
            dimension_semantics=("arbitrary",), vmem_limit_bytes=VMEM_LIMIT),
        name="tail",
    )(x2, ya, oT, gb, wup, wout, g, wg, wu, wd, gf)


def kernel(x, norm_mix_g, w_in, ssm_a_re, ssm_a_im, ssm_log_dt, ssm_b_re, ssm_b_im, ssm_c_re, ssm_c_im, ssm_d, ssm_w_glu, w_up_ssm, cmp_pos_k, cmp_pos_v, cmp_w1_k, cmp_w2_k, cmp_w1_v, cmp_w2_v, rel_bias, w_up_nsa, w_out, norm_ffn_g, w_ffn_gate, w_ffn_up, w_ffn_down, norm_final_g):
    bsz, s, _ = x.shape
    assert (bsz, s) == (1, SEQ) and w_in.shape[0] == 1
    x2 = x.reshape(s, D_MODEL)
    l = 0
    row = lambda v: v.astype(F32).reshape(1, -1)

    u, qT, kcr, vcr, ks, kw, vsT, vwT, gnT, gb = _inproj(x2, row(norm_mix_g[l]), w_in[l].astype(F32).T)

    b_sg, c_sg, pw, seg = _s5_params(
        ssm_a_re[l], ssm_a_im[l], ssm_log_dt[l], ssm_b_re[l], ssm_b_im[l], ssm_c_re[l], ssm_c_im[l])
    ya = _s5(u, b_sg, c_sg, pw, seg, row(ssm_d[l]), ssm_w_glu[l].astype(BF16), w_up_ssm[l].astype(BF16))

    kc, vcT = _compress(kcr, vcr,
                        cmp_w1_k[l].astype(BF16), cmp_w2_k[l].astype(BF16), row(cmp_pos_k[l]),
                        cmp_w1_v[l].astype(BF16), cmp_w2_v[l].T.astype(BF16), row(cmp_pos_v[l]))

    bias_tile, bias_cmp = _bias_tiles(rel_bias)
    oT = _nsa(qT, kc, vcT, ks, vsT, kw, vwT, bias_tile, bias_cmp, _overlap_matrix(), gnT)

    out = _tail(x2, ya, oT, gb, w_up_nsa[l].astype(BF16), w_out[l].astype(BF16),
                row(norm_ffn_g[l]), w_ffn_gate[l].astype(BF16), w_ffn_up[l].astype(BF16),
                w_ffn_down[l].astype(BF16), row(norm_final_g))
    return out.reshape(bsz, s, D_MODEL)
```

```python
import functools
import math

import numpy as np
import jax
import jax.numpy as jnp
from jax import lax
from jax.experimental import pallas as pl
from jax.experimental.pallas import tpu as pltpu

F32 = jnp.float32
BF16 = jnp.bfloat16

D_MODEL = 1024
SEQ = 16384
EPS = 1e-6
SSM_WIDTH = 512
SSM_GROUP = 16
SSM_GROUPS = SSM_WIDTH // SSM_GROUP
SSM_STATE = 64
SSM_LANES = SSM_GROUPS * SSM_STATE
NSA_HEADS = 8
NSA_KV_HEADS = 2
GQA = NSA_HEADS // NSA_KV_HEADS
HEAD_DIM = 64
NSA_WIDTH = NSA_HEADS * HEAD_DIM
KV_WIDTH = NSA_KV_HEADS * HEAD_DIM
CMP_LEN = 32
CMP_STRIDE = 16
SLC_LEN = 64
N_SEL = 16
N_LOCAL = 2
WINDOW = 512
BIG = 1e4
REL_BUCKETS = 32
REL_MAX_DIST = 128

N_CHUNK = SEQ // CMP_STRIDE
N_CMP = (SEQ - CMP_LEN) // CMP_STRIDE + 1
N_SLC = SEQ // SLC_LEN
TQ = 256
QL = GQA * TQ
NEAR = 128
assert TQ > NEAR and TQ % NEAR == 0
WIN_PIECES = 2
CMP_PAD = 16
CMP_ROW_STEP = 128
N_CMP_PAD = -(-(CMP_PAD + N_CHUNK) // CMP_ROW_STEP) * CMP_ROW_STEP
CMP_BAND = 16 + TQ // CMP_STRIDE
NEG = -1e30
M_FLOOR = -1e29
SCAN_ROWS = 8
S5_TT = 256
SEG = S5_TT // SCAN_ROWS
SSM_SUPER = 2
SG_CH = SSM_WIDTH // SSM_SUPER
SG_LANES = SSM_LANES // SSM_SUPER
LOG2E = math.log2(math.e)
V_ROWS = HEAD_DIM + 16
FAR_KEYS = 512
FAR_BLOCKS = FAR_KEYS // SLC_LEN
K_COLS = HEAD_DIM + 16
CMP_EXTENTS = tuple(range(2 * CMP_ROW_STEP, N_CMP_PAD + 1, CMP_ROW_STEP))

VMEM_LIMIT = 56 * 1024 * 1024

COL_U = 0
COL_Q = 512
COL_KC = 1024
COL_VC = 1152
COL_KS = 1280
COL_VS = 1408
COL_KW = 1536
COL_VW = 1664
COL_GN = 1792
COL_GB = 1816
COL_END = 3864
GATE_COLS = 128
GATE_ROWS = 32
GATES_PER_KV_HEAD = GQA * 3
INPROJ_PIECE = 512


def _rms(x, g):
    return x * lax.rsqrt(jnp.mean(x * x, axis=-1, keepdims=True) + EPS) * g


def _dot(a, b):
    return jnp.dot(a, b, preferred_element_type=F32)


def _dot_nt(a, b):
    return lax.dot_general(a, b, (((1,), (1,)), ((), ())), preferred_element_type=F32)


def _inproj_kernel(x_ref, g_ref, w_ref,
                   u_ref, qT_ref, kcr_ref, vcr_ref, ks_ref, kw_ref, vsT_ref, vwT_ref, gnT_ref, gb_ref,
                   wm_ref, wb_ref, kstage_ref, vstage_ref):
    @pl.when(pl.program_id(0) == 0)
    def _():
        for dst_ref, lo, hi in ((wm_ref, 0, COL_GN + GATE_COLS), (wb_ref, COL_GB, COL_END)):
            for c in range(lo, hi, INPROJ_PIECE):
                n = min(INPROJ_PIECE, hi - c)
                dst_ref[:, c - lo:c - lo + n] = w_ref[c:c + n, :].T.astype(BF16)

    h = _rms(x_ref[...], g_ref[...]).astype(BF16)
    tm = h.shape[0]
    pm = _dot(h, wm_ref[:, :COL_GN])
    u_ref[...] = pm[:, COL_U:COL_Q]
    qT_ref[...] = (pm[:, COL_Q:COL_KC] * (HEAD_DIM ** -0.5 * LOG2E)).T.astype(BF16)
    vsT = pm[:, COL_VS:COL_KW].T
    vwT = pm[:, COL_VW:COL_GN].T
    ones = jnp.ones((V_ROWS - HEAD_DIM, tm), F32)
    tok = pl.program_id(0) * tm + lax.broadcasted_iota(jnp.int32, (tm, K_COLS - HEAD_DIM), 0)
    col = lax.broadcasted_iota(jnp.int32, (tm, K_COLS - HEAD_DIM), 1)
    blk_onehot = jnp.where((tok // SLC_LEN) % FAR_BLOCKS == col, 1.0, 0.0)
    for g in range(NSA_KV_HEADS):
        lo = g * HEAD_DIM
        ks_ref[g] = jnp.concatenate(
            [pm[:, COL_KS + lo:COL_KS + lo + HEAD_DIM], blk_onehot], axis=1).astype(BF16)
        kw_ref[g] = pm[:, COL_KW + lo:COL_KW + lo + HEAD_DIM].astype(BF16)
        vsT_ref[g] = jnp.concatenate([vsT[lo:lo + HEAD_DIM], ones], axis=0).astype(BF16)
        vwT_ref[g] = jnp.concatenate([vwT[lo:lo + HEAD_DIM], ones], axis=0).astype(BF16)
    for src_col, stage_ref, dst_ref in ((COL_KC, kstage_ref, kcr_ref), (COL_VC, vstage_ref, vcr_ref)):
        stage_ref[...] = pm[:, src_col:src_col + KV_WIDTH]
        for t in range(CMP_STRIDE):
            rows = stage_ref[pl.ds(t, tm // CMP_STRIDE, stride=CMP_STRIDE), :]
            for g in range(NSA_KV_HEADS):
                dst_ref[g, :, t * HEAD_DIM:(t + 1) * HEAD_DIM] = rows[:, g * HEAD_DIM:(g + 1) * HEAD_DIM]
    gn = jax.nn.sigmoid(_dot(h, wm_ref[:, COL_GN:]))
    gnT_ref[...] = gn.T[:GATE_ROWS, :]
    gb_ref[...] = jax.nn.sigmoid(_dot(h, wb_ref[...])).astype(BF16)


def _inproj(x2, g, w, tm=512):
    s = x2.shape[0]
    const = lambda i: (0, 0)
    row = lambda i: (i, 0)
    col = lambda i: (0, i)
    return pl.pallas_call(
        _inproj_kernel,
        grid=(s // tm,),
        in_specs=[
            pl.BlockSpec((tm, D_MODEL), row),
            pl.BlockSpec((1, D_MODEL), const),
            pl.BlockSpec(w.shape, const),
        ],
        out_specs=[
            pl.BlockSpec((tm, SSM_WIDTH), row),
            pl.BlockSpec((NSA_WIDTH, tm), col),
            pl.BlockSpec((NSA_KV_HEADS, tm // CMP_STRIDE, CMP_STRIDE * HEAD_DIM), lambda i: (0, i, 0)),
            pl.BlockSpec((NSA_KV_HEADS, tm // CMP_STRIDE, CMP_STRIDE * HEAD_DIM), lambda i: (0, i, 0)),
            pl.BlockSpec((NSA_KV_HEADS, tm, K_COLS), lambda i: (0, i, 0)),
            pl.BlockSpec((NSA_KV_HEADS, tm, HEAD_DIM), lambda i: (0, i, 0)),
            pl.BlockSpec((NSA_KV_HEADS, V_ROWS, tm), lambda i: (0, 0, i)),
            pl.BlockSpec((NSA_KV_HEADS, V_ROWS, tm), lambda i: (0, 0, i)),
            pl.BlockSpec((GATE_ROWS, tm), col),
            pl.BlockSpec((tm, 2 * D_MODEL), row),
        ],
        out_shape=[
            jax.ShapeDtypeStruct((s, SSM_WIDTH), F32),
            jax.ShapeDtypeStruct((NSA_WIDTH, s), BF16),
            jax.ShapeDtypeStruct((NSA_KV_HEADS, s // CMP_STRIDE, CMP_STRIDE * HEAD_DIM), F32),
            jax.ShapeDtypeStruct((NSA_KV_HEADS, s // CMP_STRIDE, CMP_STRIDE * HEAD_DIM), F32),
            jax.ShapeDtypeStruct((NSA_KV_HEADS, s, K_COLS), BF16),
            jax.ShapeDtypeStruct((NSA_KV_HEADS, s, HEAD_DIM), BF16),
            jax.ShapeDtypeStruct((NSA_KV_HEADS, V_ROWS, s), BF16),
            jax.ShapeDtypeStruct((NSA_KV_HEADS, V_ROWS, s), BF16),
            jax.ShapeDtypeStruct((GATE_ROWS, s), F32),
            jax.ShapeDtypeStruct((s, 2 * D_MODEL), BF16),
        ],
        scratch_shapes=[
            pltpu.VMEM((D_MODEL, COL_GN + GATE_COLS), BF16),
            pltpu.VMEM((D_MODEL, COL_END - COL_GB), BF16),
            pltpu.VMEM((tm, KV_WIDTH), F32),
            pltpu.VMEM((tm, KV_WIDTH), F32),
        ],
        compiler_params=pltpu.CompilerParams(
            dimension_semantics=("arbitrary",), vmem_limit_bytes=VMEM_LIMIT),
        name="inproj",
    )(x2, g, w)


def _s5_kernel(u_ref, perm_ref, permT_ref, b_ref, c_ref, pw_ref, seg_ref, d_ref, wglu_ref, wup_ref,
               ya_ref, xre_ref, xim_ref, st_ref, cre_s, cim_s):
    @pl.when(pl.program_id(0) == 0)
    def _():
        cre_s[...] = jnp.zeros_like(cre_s)
        cim_s[...] = jnp.zeros_like(cim_s)

    u = u_ref[...]
    ub = _dot(perm_ref[...], u.astype(BF16)).astype(BF16)
    for sg in range(SSM_SUPER):
        bu = _dot(ub[:, sg * SG_CH:(sg + 1) * SG_CH], b_ref[sg])
        xre_ref[:, sg * SG_LANES:(sg + 1) * SG_LANES] = bu[:, :SG_LANES]
        xim_ref[:, sg * SG_LANES:(sg + 1) * SG_LANES] = bu[:, SG_LANES:]

    def cmul_add(re, im, are, aim, sre, sim):
        return re + are * sre - aim * sim, im + are * sim + aim * sre

    for sg in range(SSM_SUPER):
        lanes = slice(sg * SG_LANES, (sg + 1) * SG_LANES)

        lam_re, lam_im = pw_ref[0, 0:SCAN_ROWS, lanes], pw_ref[1, 0:SCAN_ROWS, lanes]

        def local(j, carry):
            r0 = pl.multiple_of(j * SCAN_ROWS, SCAN_ROWS)
            re, im = cmul_add(xre_ref[pl.ds(r0, SCAN_ROWS), lanes], xim_ref[pl.ds(r0, SCAN_ROWS), lanes],
                              lam_re, lam_im, *carry)
            xre_ref[pl.ds(r0, SCAN_ROWS), lanes] = re
            xim_ref[pl.ds(r0, SCAN_ROWS), lanes] = im
            return re, im

        zero = jnp.zeros((SCAN_ROWS, SG_LANES), F32)
        re, im = lax.fori_loop(0, SEG, local, (zero, zero), unroll=True)

        for k, shift in enumerate((1, 2, 4)):
            re, im = cmul_add(re, im, seg_ref[2 * k, :, lanes], seg_ref[2 * k + 1, :, lanes],
                              pltpu.roll(re, shift, 0), pltpu.roll(im, shift, 0))
        cin_re, cin_im = cre_s[:, lanes], cim_s[:, lanes]
        re, im = cmul_add(re, im, seg_ref[6, :, lanes], seg_ref[7, :, lanes], cin_re, cin_im)
        cre_s[:, lanes] = re[SCAN_ROWS - 1:SCAN_ROWS, :]
        cim_s[:, lanes] = im[SCAN_ROWS - 1:SCAN_ROWS, :]
        first = lax.broadcasted_iota(jnp.int32, (SCAN_ROWS, SG_LANES), 0) == 0
        start_re = jnp.where(first, cin_re, pltpu.roll(re, 1, 0))
        start_im = jnp.where(first, cin_im, pltpu.roll(im, 1, 0))
        start_re = jnp.concatenate([start_re, start_re], axis=0)
        start_im = jnp.concatenate([start_im, start_im], axis=0)

        def fix(jj, carry):
            r0 = pl.multiple_of(jj * 2 * SCAN_ROWS, 2 * SCAN_ROWS)
            rows = pl.ds(r0, 2 * SCAN_ROWS)
            re, im = cmul_add(xre_ref[rows, lanes], xim_ref[rows, lanes],
                              pw_ref[0, rows, lanes], pw_ref[1, rows, lanes], start_re, start_im)
            st_ref[rows, 2 * sg * SG_LANES:(2 * sg + 1) * SG_LANES] = re.astype(BF16)
            st_ref[rows, (2 * sg + 1) * SG_LANES:(2 * sg + 2) * SG_LANES] = im.astype(BF16)
            return carry

        lax.fori_loop(0, SEG // 2, fix, 0, unroll=True)
    y_perm = jnp.concatenate(
        [_dot(st_ref[:, 2 * sg * SG_LANES:(2 * sg + 2) * SG_LANES], c_ref[sg])
         for sg in range(SSM_SUPER)], axis=1)
    y_hi = y_perm.astype(BF16)
    y_lo = (y_perm - y_hi.astype(F32)).astype(BF16)
    y = _dot(permT_ref[...], y_hi) + _dot(permT_ref[...], y_lo) + d_ref[...] * u
    z = jax.nn.gelu(y)
    z = z * jax.nn.sigmoid(_dot(z.astype(BF16), wglu_ref[...]))
    ya_ref[...] = _dot(z.astype(BF16), wup_ref[...])


def _s5(u, b, c, pw, seg, d, wglu, wup):
    s = u.shape[0]
    t = np.arange(S5_TT)
    perm = np.zeros((S5_TT, S5_TT), np.float32)
    perm[SCAN_ROWS * (t % SEG) + t // SEG, t] = 1.0
    permT = jnp.asarray(perm.T, BF16)
    perm = jnp.asarray(perm, BF16)
    const2 = lambda i: (0, 0)
    const3 = lambda i: (0, 0, 0)
    return pl.pallas_call(
        _s5_kernel,
        grid=(s // S5_TT,),
        in_specs=[
            pl.BlockSpec((S5_TT, SSM_WIDTH), lambda i: (i, 0)),
            pl.BlockSpec(perm.shape, const2),
            pl.BlockSpec(permT.shape, const2),
            pl.BlockSpec(b.shape, const3),
            pl.BlockSpec(c.shape, const3),
            pl.BlockSpec(pw.shape, const3),
            pl.BlockSpec(seg.shape, const3),
            pl.BlockSpec((1, SSM_WIDTH), const2),
            pl.BlockSpec(wglu.shape, const2),
            pl.BlockSpec(wup.shape, const2),
        ],
        out_specs=pl.BlockSpec((S5_TT, D_MODEL), lambda i: (i, 0)),
        out_shape=jax.ShapeDtypeStruct((s, D_MODEL), F32),
        scratch_shapes=[
            pltpu.VMEM((S5_TT, SSM_LANES), F32),
            pltpu.VMEM((S5_TT, SSM_LANES), F32),
            pltpu.VMEM((S5_TT, 2 * SSM_LANES), BF16),
            pltpu.VMEM((1, SSM_LANES), F32),
            pltpu.VMEM((1, SSM_LANES), F32),
        ],
        compiler_params=pltpu.CompilerParams(
            dimension_semantics=("arbitrary",), vmem_limit_bytes=VMEM_LIMIT),
        name="s5",
    )(u, perm, permT, b, c, pw, seg, d, wglu, wup)


def _s5_params(a_re, a_im, log_dt, b_re, b_im, c_re, c_im):
    dt = jnp.exp(log_dt.astype(F32))[:, None]
    ar, ai = a_re.astype(F32), a_im.astype(F32)
    zr, zi = ar * dt, ai * dt

    def power(n):
        mag = jnp.exp(n * zr)
        return mag * jnp.cos(n * zi), mag * jnp.sin(n * zi)

    lr, li = power(1.0)
    den = ar * ar + ai * ai
    kr = ((lr - 1.0) * ar + li * ai) / den
    ki = (li * ar - (lr - 1.0) * ai) / den
    br, bi = b_re.astype(F32), b_im.astype(F32)
    bbr = kr[..., None] * br - ki[..., None] * bi
    bbi = kr[..., None] * bi + ki[..., None] * br
    groups_per_super = SSM_GROUPS // SSM_SUPER

    def super_blocks(w, per_group_rows, per_group_cols):
        rows = w.reshape(SSM_SUPER, groups_per_super * per_group_rows, per_group_cols)
        tiled = jnp.tile(rows, (1, 1, groups_per_super))
        same = (lax.broadcasted_iota(jnp.int32, tiled.shape, 1) // per_group_rows
                == lax.broadcasted_iota(jnp.int32, tiled.shape, 2) // per_group_cols)
        return jnp.where(same, tiled, 0.0)

    b_in = lambda b: super_blocks(jnp.transpose(b, (0, 2, 1)), SSM_GROUP, SSM_STATE)
    c_out = lambda c: super_blocks(jnp.transpose(c, (0, 2, 1)), SSM_STATE, SSM_GROUP)
    zr, zi = zr.reshape(1, SSM_LANES), zi.reshape(1, SSM_LANES)
    pw = jnp.stack(power(jnp.arange(1, SEG + 1, dtype=F32)[:, None]))
    pw = jnp.broadcast_to(pw[:, :, None, :], (2, SEG, SCAN_ROWS, SSM_LANES)).reshape(2, S5_TT, SSM_LANES)
    row = jnp.arange(SCAN_ROWS)[:, None]
    seg = []
    for shift in (1, 2, 4):
        pr, pi = power(float(SEG * shift))
        seg += [jnp.where(row >= shift, pr, 0.0), jnp.where(row >= shift, pi, 0.0)]
    seg += list(power(SEG * (row + 1).astype(F32)))
    b_sg = jnp.concatenate([b_in(bbr), b_in(bbi)], axis=2)
    c_sg = jnp.concatenate([c_out(c_re.astype(F32)), -c_out(c_im.astype(F32))], axis=1)
    return b_sg.astype(BF16), c_sg.astype(BF16), pw, jnp.stack(seg)


def _compress_kernel(xk_ref, xv_ref, w1k_ref, w2k_ref, pk_ref, w1v_ref, w2vT_ref, pv_ref, kc_ref, vcT_ref):
    half = CMP_STRIDE * HEAD_DIM

    def hidden(x_ref, w1_ref, pos_ref):
        x = x_ref[0].astype(BF16)
        first = _dot(x, w1_ref[:half, :])
        second = _dot(x, w1_ref[half:, :])
        bias = _dot(jnp.broadcast_to(pos_ref[...], (8, 2 * half)).astype(BF16), w1_ref[...])[:1]
        pre = first + pltpu.roll(second, N_CHUNK - 1, 0) + bias
        return jax.nn.gelu(pre).astype(BF16)

    hk = hidden(xk_ref, w1k_ref, pk_ref)
    kc = _dot(hk, w2k_ref[...])
    tail_pad = N_CMP_PAD - CMP_PAD - N_CHUNK
    kc_ref[0] = jnp.concatenate([jnp.zeros((CMP_PAD, HEAD_DIM), F32), kc,
                                 jnp.zeros((tail_pad, HEAD_DIM), F32)], axis=0).astype(BF16)
    hv = hidden(xv_ref, w1v_ref, pv_ref)
    vcT = _dot_nt(w2vT_ref[...], hv)
    vcT_ref[0] = jnp.concatenate([jnp.zeros((HEAD_DIM, CMP_PAD), F32), vcT,
                                  jnp.zeros((HEAD_DIM, tail_pad), F32)], axis=1).astype(BF16)


def _compress(xk, xv, w1k, w2k, pk, w1v, w2vT, pv):
    const2 = lambda g: (0, 0)
    head = lambda g: (g, 0, 0)
    return pl.pallas_call(
        _compress_kernel,
        grid=(NSA_KV_HEADS,),
        in_specs=[
            pl.BlockSpec((1, N_CHUNK, CMP_STRIDE * HEAD_DIM), head),
            pl.BlockSpec((1, N_CHUNK, CMP_STRIDE * HEAD_DIM), head),
            pl.BlockSpec(w1k.shape, const2),
            pl.BlockSpec(w2k.shape, const2),
            pl.BlockSpec(pk.shape, const2),
            pl.BlockSpec(w1v.shape, const2),
            pl.BlockSpec(w2vT.shape, const2),
            pl.BlockSpec(pv.shape, const2),
        ],
        out_specs=[
            pl.BlockSpec((1, N_CMP_PAD, HEAD_DIM), head),
            pl.BlockSpec((1, HEAD_DIM, N_CMP_PAD), head),
        ],
        out_shape=[
            jax.ShapeDtypeStruct((NSA_KV_HEADS, N_CMP_PAD, HEAD_DIM), BF16),
            jax.ShapeDtypeStruct((NSA_KV_HEADS, HEAD_DIM, N_CMP_PAD), BF16),
        ],
        compiler_params=pltpu.CompilerParams(
            dimension_semantics=("arbitrary",), vmem_limit_bytes=VMEM_LIMIT),
        name="compress",
    )(xk, xv, w1k, w2k, pk, w1v, w2vT, pv)


def _nsa_kernel(qT_ref, kc_ref, vcT_ref, ks_ref, vsT_ref, kw_ref, vwT_ref, bias_ref, bc_ref, mt_ref, gT_ref,
                oT_ref, sc_ref, neg_ref, negfar_ref, m_ref, acc_ref, tot_ref, sbuf0_ref, sbuf1_ref, mloc_ref, pslc_ref):
    i = pl.program_id(1)
    s0 = i * TQ
    qT = jnp.concatenate([qT_ref[r * HEAD_DIM:(r + 1) * HEAD_DIM, :] for r in range(GQA)], axis=1)

    qT_nomask = jnp.concatenate([qT, jnp.zeros((K_COLS - HEAD_DIM, QL), BF16)], axis=0)

    gate0 = pl.program_id(0) * GATES_PER_KV_HEAD

    def gate_row(branch):
        return jnp.concatenate([gT_ref[pl.ds(gate0 + r * 3 + branch, 1), :] for r in range(GQA)], axis=1)

    def reset():
        m_ref[...] = jnp.full_like(m_ref, M_FLOOR)
        acc_ref[...] = jnp.zeros_like(acc_ref)

    def attend(k_ref, vT_ref, start, size, add):
        k = k_ref[0, pl.ds(start, size), :]
        s = _dot(k, qT if k.shape[1] == HEAD_DIM else qT_nomask) + add
        m_prev = m_ref[...]
        m_new = jnp.maximum(m_prev, jnp.max(s, axis=0, keepdims=True))
        alpha = jnp.exp2(m_prev - m_new)
        p = jnp.exp2(s - m_new).astype(BF16)
        acc_ref[...] = alpha * acc_ref[...] + _dot(vT_ref[0, :, pl.ds(start, size)], p)
        m_ref[...] = m_new

    def finish(branch):
        acc = acc_ref[...]
        scale = gate_row(branch) / jnp.maximum(acc[HEAD_DIM:HEAD_DIM + 1, :], 1e-30)
        tot_ref[...] += acc[:HEAD_DIM, :] * scale

    n0 = i * (TQ // CMP_STRIDE)
    band0 = pl.multiple_of(n0 + CMP_PAD - 16, 8)

    def cmp_branch(nrows):
        row = lax.broadcasted_iota(jnp.int32, (nrows, QL), 0)
        live = (row >= CMP_PAD) & (row < band0 + CMP_BAND)
        sc_ref[0:nrows, :] = jnp.where(live, _dot(kc_ref[0, 0:nrows, :], qT), NEG)
        sc_ref[pl.ds(band0, CMP_BAND), :] += bc_ref[0]
        sc = sc_ref[0:nrows, :]
        mc = jnp.maximum(jnp.max(sc, axis=0, keepdims=True), M_FLOOR)
        pc = jnp.exp2(sc - mc)
        pc = pc * (1.0 / jnp.maximum(jnp.sum(pc, axis=0, keepdims=True), 1e-30))
        tot_ref[...] = _dot(vcT_ref[0, :, 0:nrows], pc.astype(BF16)) * gate_row(0)
        imp = pc[:, 0:TQ]
        for r in range(1, GQA):
            imp = imp + pc[:, r * TQ:(r + 1) * TQ]
        mt = mt_ref[:, 0:nrows]
        p_slc = jnp.zeros((N_SLC, TQ), F32)
        rem = imp
        for _ in range(2):
            piece = rem.astype(BF16)
            p_slc = p_slc + _dot(mt, piece)
            rem = rem - piece.astype(F32)
        pslc_ref[...] = p_slc

    prev_rows = 0
    for nrows in CMP_EXTENTS:
        lo, hi = prev_rows, nrows
        pl.when((band0 + CMP_BAND > lo) & (band0 + CMP_BAND <= hi))(functools.partial(cmp_branch, nrows))
        prev_rows = nrows

    reset()
    win_start = jnp.maximum(s0 - WINDOW, 0)
    win_bias0 = WINDOW - (s0 - win_start)
    piece_keys = (WINDOW + TQ) // WIN_PIECES
    wbufs = (sbuf0_ref, sbuf1_ref)

    def win_logits(piece):
        start = pl.multiple_of(win_start + piece * piece_keys, NEAR)
        s = (_dot(kw_ref[0, pl.ds(start, piece_keys), :], qT)
             + bias_ref[0, pl.ds(pl.multiple_of(win_bias0 + piece * piece_keys, NEAR), piece_keys), :])
        wbufs[piece % 2][0:piece_keys, :] = s
        mloc_ref[piece % 2] = jnp.max(s, axis=0, keepdims=True)

    def win_consume(piece):
        start = pl.multiple_of(win_start + piece * piece_keys, NEAR)
        m_prev = m_ref[...]
        m_new = jnp.maximum(m_prev, mloc_ref[piece % 2])
        alpha = jnp.exp2(m_prev - m_new)
        p = jnp.exp2(wbufs[piece % 2][0:piece_keys, :] - m_new).astype(BF16)
        acc_ref[...] = alpha * acc_ref[...] + _dot(vwT_ref[0, :, pl.ds(start, piece_keys)], p)
        m_ref[...] = m_new

    win_logits(0)
    for piece in range(WIN_PIECES):
        if piece + 1 < WIN_PIECES:
            win_logits(piece + 1)
        win_consume(piece)
    finish(2)

    p_slc = pslc_ref[...]
    blk = lax.broadcasted_iota(jnp.int32, (N_SLC, TQ), 0)
    cur = (s0 + lax.broadcasted_iota(jnp.int32, (N_SLC, TQ), 1)) // SLC_LEN
    valid = blk <= cur
    forced = valid & ((blk == 0) | (blk >= cur - (N_LOCAL - 1)))
    n_pick = N_SEL - (N_LOCAL + 1)
    start_score = jnp.where(forced | ~valid, -jnp.inf, p_slc)
    near_blk = (s0 - NEAR) // SLC_LEN

    def publish(score):
        neg = jnp.where(score == -jnp.inf, 0.0, NEG)
        neg_ref[...] = jnp.concatenate([neg] * GQA, axis=1)
        negfar_ref[...] = jnp.concatenate([jnp.where(blk >= near_blk, NEG, neg)] * GQA, axis=1)

    score = start_score
    for _ in range(n_pick):
        score = jnp.where(score == jnp.max(score, axis=0, keepdims=True), -jnp.inf, score)
    marks = jnp.sum(jnp.where(score == -jnp.inf, 1.0, 0.0), axis=0, keepdims=True)
    cur_q = cur[0:1, :]
    n_forced = jnp.minimum(cur_q + 1, N_LOCAL + 1)
    expected = (N_SLC - 1 - cur_q) + n_forced + jnp.minimum(n_pick, cur_q + 1 - n_forced)
    tied = jnp.max(jnp.abs(marks - expected.astype(F32))) > 0.5
    fast_score = score

    @pl.when(jnp.logical_not(tied))
    def _():
        publish(fast_score)

    @pl.when(tied)
    def _():
        score = jnp.where(forced | ~valid, -jnp.inf, pslc_ref[...])
        blk_f = blk.astype(F32)
        for _ in range(n_pick):
            best = jnp.max(score, axis=0, keepdims=True)
            first = jnp.min(jnp.where(score == best, blk_f, float(N_SLC)), axis=0, keepdims=True)
            score = jnp.where(blk_f == first, -jnp.inf, score)
        publish(score)

    def block_mask(ref, j0, nblk):
        return jnp.concatenate(
            [jnp.broadcast_to(ref[pl.ds(j0 + b, 1), :], (SLC_LEN, QL)) for b in range(nblk)], axis=0)

    reset()
    n_far = (s0 - NEAR + FAR_KEYS - 1) // FAR_KEYS

    sbufs = (sbuf0_ref, sbuf1_ref)

    def far_logits(c, slot):
        start = pl.multiple_of(c * FAR_KEYS, FAR_KEYS)
        mask_rows = negfar_ref[pl.ds(pl.multiple_of(c * FAR_BLOCKS, FAR_BLOCKS), FAR_BLOCKS), :]
        extra = jnp.concatenate([mask_rows, jnp.zeros((K_COLS - HEAD_DIM - FAR_BLOCKS, QL), F32)], axis=0)
        q_masked = jnp.concatenate([qT, extra.astype(BF16)], axis=0)
        s = _dot(ks_ref[0, pl.ds(start, FAR_KEYS), :], q_masked)
        sbufs[slot][...] = s
        mloc_ref[slot] = jnp.max(s, axis=0, keepdims=True)

    def far_consume(c, slot):
        start = pl.multiple_of(c * FAR_KEYS, FAR_KEYS)
        m_prev = m_ref[...]
        m_new = jnp.maximum(m_prev, mloc_ref[slot])
        alpha = jnp.exp2(m_prev - m_new)
        p = jnp.exp2(sbufs[slot][...] - m_new).astype(BF16)
        acc_ref[...] = alpha * acc_ref[...] + _dot(vsT_ref[0, :, pl.ds(start, FAR_KEYS)], p)
        m_ref[...] = m_new

    @pl.when(i >= 1)
    def _():
        n_pairs = (n_far + 1) // 2
        far_logits(0, 0)
        attend(ks_ref, vsT_ref, pl.multiple_of(s0 - NEAR, NEAR), NEAR + TQ,
               bias_ref[0, WINDOW - NEAR:WINDOW + TQ, :] + block_mask(neg_ref, near_blk, (NEAR + TQ) // SLC_LEN))

        def far_body(p, carry):
            far_logits(2 * p + 1, 1)
            far_consume(2 * p, 0)
            far_logits(2 * p + 2, 0)
            far_consume(2 * p + 1, 1)
            return carry

        lax.fori_loop(0, n_pairs - 1, far_body, 0)
        last = 2 * (n_pairs - 1)

        @pl.when(n_far % 2 == 0)
        def _():
            far_logits(last + 1, 1)
            far_consume(last, 0)
            far_consume(last + 1, 1)

        @pl.when(n_far % 2 == 1)
        def _():
            far_consume(last, 0)

    @pl.when(i == 0)
    def _():
        attend(ks_ref, vsT_ref, 0, TQ, bias_ref[0, WINDOW:WINDOW + TQ, :] + block_mask(neg_ref, 0, TQ // SLC_LEN))

    finish(1)

    tot = tot_ref[...]
    for r in range(GQA):
        oT_ref[r * HEAD_DIM:(r + 1) * HEAD_DIM, :] = tot[:, r * TQ:(r + 1) * TQ]


def _nsa(qT, kc, vcT, ks, vsT, kw, vwT, bias_tile, bias_cmp, mt, gnT):
    s = qT.shape[1]
    head3 = lambda g, i: (g, 0, 0)
    return pl.pallas_call(
        _nsa_kernel,
        grid=(NSA_KV_HEADS, s // TQ),
        in_specs=[
            pl.BlockSpec((GQA * HEAD_DIM, TQ), lambda g, i: (g, i)),
            pl.BlockSpec((1, N_CMP_PAD, HEAD_DIM), head3),
            pl.BlockSpec((1, HEAD_DIM, N_CMP_PAD), head3),
            pl.BlockSpec((1, s, K_COLS), head3),
            pl.BlockSpec((1, V_ROWS, s), head3),
            pl.BlockSpec((1, s, HEAD_DIM), head3),
            pl.BlockSpec((1, V_ROWS, s), head3),
            pl.BlockSpec((1, 2 * WINDOW + TQ, QL), head3, pipeline_mode=pl.Buffered(1)),
            pl.BlockSpec((1, CMP_BAND, QL), head3),
            pl.BlockSpec(mt.shape, lambda g, i: (0, 0)),
            pl.BlockSpec((GATE_ROWS, TQ), lambda g, i: (0, i)),
        ],
        out_specs=pl.BlockSpec((GQA * HEAD_DIM, TQ), lambda g, i: (g, i)),
        out_shape=jax.ShapeDtypeStruct((NSA_WIDTH, s), F32),
        scratch_shapes=[
            pltpu.VMEM((N_CMP_PAD, QL), F32),
            pltpu.VMEM((N_SLC, QL), F32),
            pltpu.VMEM((N_SLC, QL), F32),
            pltpu.VMEM((1, QL), F32),
            pltpu.VMEM((V_ROWS, QL), F32),
            pltpu.VMEM((HEAD_DIM, QL), F32),
            pltpu.VMEM((FAR_KEYS, QL), F32),
            pltpu.VMEM((FAR_KEYS, QL), F32),
            pltpu.VMEM((2, 1, QL), F32),
            pltpu.VMEM((N_SLC, TQ), F32),
        ],
        compiler_params=pltpu.CompilerParams(
            dimension_semantics=("arbitrary", "arbitrary"), vmem_limit_bytes=VMEM_LIMIT),
        name="nsa",
    )(qT, kc, vcT, ks, vsT, kw, vwT, bias_tile, bias_cmp, mt, gnT)


def _t5_bucket(dist):
    n = jnp.maximum(dist, 0)
    max_exact = REL_BUCKETS // 2
    nf = jnp.maximum(n, 1).astype(F32)
    large = max_exact + (jnp.log(nf / max_exact) / math.log(REL_MAX_DIST / max_exact)
                         * (REL_BUCKETS - max_exact)).astype(jnp.int32)
    large = jnp.minimum(large, REL_BUCKETS - 1)
    return jnp.where(n < max_exact, n, large)


def _bias_tiles(rel_bias):
    tab = rel_bias.astype(F32)
    tab = (tab[_t5_bucket(jnp.arange(NEAR))] - tab[REL_BUCKETS - 1]).T * LOG2E
    tab = jnp.concatenate([tab, jnp.zeros((NSA_HEADS, 1), F32)], axis=1)

    def by_distance(d):
        return jnp.where(d >= 0, tab[:, jnp.clip(d, 0, NEAR)], NEG)

    def toeplitz(c, nk, nq, step=1):
        n = step * (nk - 1) + nq
        w = by_distance(jnp.arange(n) + c - step * (nk - 1))
        reps = -(-nk * (n + step) // n)
        return jnp.tile(w, (1, reps))[:, :nk * (n + step)].reshape(NSA_HEADS, nk, n + step)[:, ::-1, :nq]

    def per_kv_head(a):
        a = a.reshape(NSA_KV_HEADS, GQA, a.shape[1], TQ)
        return jnp.transpose(a, (0, 2, 1, 3)).reshape(NSA_KV_HEADS, a.shape[2], QL)

    k2, q2 = np.arange(NEAR)[:, None], np.arange(NEAR)[None, :]
    const = lambda a: jnp.broadcast_to(jnp.asarray(a, F32), (NSA_HEADS, NEAR, NEAR))
    edge = WINDOW // NEAR
    nb = TQ // NEAR
    blocks, slabs = {}, {}

    def block(d):
        d = min(max(d, -1), edge + 1)
        if d not in blocks:
            if d < 0:
                b = const(NEG)
            elif d < 2:
                b = toeplitz(d * NEAR, NEAR, NEAR)
            elif d < edge:
                b = const(0.0)
            else:
                b = const(np.where(k2 > q2, 0.0, NEG) if d == edge else NEG)
            blocks[d] = jnp.transpose(b.reshape(NSA_KV_HEADS, GQA, NEAR, NEAR), (0, 2, 1, 3))
        return blocks[d]

    def slab(dd):
        dd = min(max(dd, -nb), edge + 1)
        if dd not in slabs:
            s = jnp.stack([block(dd + b) for b in range(nb)], axis=3)
            slabs[dd] = s.reshape(NSA_KV_HEADS, NEAR, QL)
        return slabs[dd]

    tile = jnp.concatenate([slab(edge - a) for a in range((2 * WINDOW + TQ) // NEAR)], axis=1)
    cmp_band = toeplitz(16 * CMP_STRIDE - (CMP_LEN - 1), CMP_BAND, TQ, step=CMP_STRIDE)
    return tile, per_kv_head(cmp_band)


def _overlap_matrix():
    ratio = SLC_LEN // CMP_STRIDE
    front = CMP_LEN // CMP_STRIDE - 1
    w_ov = np.convolve(np.ones(ratio), np.ones(CMP_LEN // CMP_STRIDE))
    mt = np.zeros((N_SLC, N_CMP_PAD), np.float32)
    for j in range(N_SLC):
        for o, w in enumerate(w_ov):
            n = ratio * j + o - front
            if 0 <= n < N_CMP:
                mt[j, CMP_PAD + n] = w
    return jnp.asarray(mt, BF16)


def _tail_kernel(x_ref, ya_ref, oT_ref, gb_ref, wup_ref, wout_ref, g_ref, wg_ref, wu_ref, wd_ref, gf_ref, o_ref):
    yb = _dot(oT_ref[...].T.astype(BF16), wup_ref[...])
    mix = gb_ref[:, :D_MODEL].astype(F32) * ya_ref[...] + gb_ref[:, D_MODEL:].astype(F32) * yb
    x = x_ref[...] + _dot(mix.astype(BF16), wout_ref[...])
    h = _rms(x, g_ref[...]).astype(BF16)
    f = jax.nn.silu(_dot(h, wg_ref[...])) * _dot(h, wu_ref[...])
    x = x + _dot(f.astype(BF16), wd_ref[...])
    o_ref[...] = _rms(x, gf_ref[...])


def _tail(x2, ya, oT, gb, wup, wout, g, wg, wu, wd, gf, tm=256):
    s = x2.shape[0]
    row = lambda i: (i, 0)
    const = lambda i: (0, 0)
    return pl.pallas_call(
        _tail_kernel,
        grid=(s // tm,),
        in_specs=[
            pl.BlockSpec((tm, D_MODEL), row),
            pl.BlockSpec((tm, D_MODEL), row),
            pl.BlockSpec((NSA_WIDTH, tm), lambda i: (0, i)),
            pl.BlockSpec((tm, 2 * D_MODEL), row),
            pl.BlockSpec(wup.shape, const),
            pl.BlockSpec(wout.shape, const),
            pl.BlockSpec((1, D_MODEL), const),
            pl.BlockSpec(wg.shape, const),
            pl.BlockSpec(wu.shape, const),
            pl.BlockSpec(wd.shape, const),
            pl.BlockSpec((1, D_MODEL), const),
        ],
        out_specs=pl.BlockSpec((tm, D_MODEL), row),
        out_shape=jax.ShapeDtypeStruct((s, D_MODEL), F32),
        compiler_params=pltpu.CompilerParams(
            dimension_semantics=("arbitrary",), vmem_limit_bytes=VMEM_LIMIT),
        name="tail",
    )(x2, ya, oT, gb, wup, wout, g, wg, wu, wd, gf)


def kernel(x, norm_mix_g, w_in, ssm_a_re, ssm_a_im, ssm_log_dt, ssm_b_re, ssm_b_im, ssm_c_re, ssm_c_im, ssm_d, ssm_w_glu, w_up_ssm, cmp_pos_k, cmp_pos_v, cmp_w1_k, cmp_w2_k, cmp_w1_v, cmp_w2_v, rel_bias, w_up_nsa, w_out, norm_ffn_g, w_ffn_gate, w_ffn_up, w_ffn_down, norm_final_g):
    bsz, s, _ = x.shape
    assert (bsz, s) == (1, SEQ) and w_in.shape[0] == 1
    x2 = x.reshape(s, D_MODEL)
    l = 0
    row = lambda v: v.astype(F32).reshape(1, -1)

    u, qT, kcr, vcr, ks, kw, vsT, vwT, gnT, gb = _inproj(x2, row(norm_mix_g[l]), w_in[l].astype(F32).T)

    b_sg, c_sg, pw, seg = _s5_params(
        ssm_a_re[l], ssm_a_im[l], ssm_log_dt[l], ssm_b_re[l], ssm_b_im[l], ssm_c_re[l], ssm_c_im[l])
    ya = _s5(u, b_sg, c_sg, pw, seg, row(ssm_d[l]), ssm_w_glu[l].astype(BF16), w_up_ssm[l].astype(BF16))

    kc, vcT = _compress(kcr, vcr,
                        cmp_w1_k[l].astype(BF16), cmp_w2_k[l].astype(BF16), row(cmp_pos_k[l]),
                        cmp_w1_v[l].astype(BF16), cmp_w2_v[l].T.astype(BF16), row(cmp_pos_v[l]))

    bias_tile, bias_cmp = _bias_tiles(rel_bias)
    oT = _nsa(qT, kc, vcT, ks, vsT, kw, vwT, bias_tile, bias_cmp, _overlap_matrix(), gnT)

    out = _tail(x2, ya, oT, gb, w_up_nsa[l].astype(BF16), w_out[l].astype(BF16),
                row(norm_ffn_g[l]), w_ffn_gate[l].astype(BF16), w_ffn_up[l].astype(BF16),
                w_ffn_down[l].astype(BF16), row(norm_final_g))
    return out.reshape(bsz, s, D_MODEL)
```

```python
import functools
import math

import numpy as np
import jax
import jax.numpy as jnp
from jax import lax
from jax.experimental import pallas as pl
from jax.experimental.pallas import tpu as pltpu

F32 = jnp.float32
BF16 = jnp.bfloat16

D_MODEL = 1024
SEQ = 16384
EPS = 1e-6
SSM_WIDTH = 512
SSM_GROUP = 16
SSM_GROUPS = SSM_WIDTH // SSM_GROUP
SSM_STATE = 64
SSM_LANES = SSM_GROUPS * SSM_STATE
NSA_HEADS = 8
NSA_KV_HEADS = 2
GQA = NSA_HEADS // NSA_KV_HEADS
HEAD_DIM = 64
NSA_WIDTH = NSA_HEADS * HEAD_DIM
KV_WIDTH = NSA_KV_HEADS * HEAD_DIM
CMP_LEN = 32
CMP_STRIDE = 16
SLC_LEN = 64
N_SEL = 16
N_LOCAL = 2
WINDOW = 512
BIG = 1e4
REL_BUCKETS = 32
REL_MAX_DIST = 128

N_CHUNK = SEQ // CMP_STRIDE
N_CMP = (SEQ - CMP_LEN) // CMP_STRIDE + 1
N_SLC = SEQ // SLC_LEN
TQ = 256
QL = GQA * TQ
NEAR = 128
assert TQ > NEAR and TQ % NEAR == 0
WIN_PIECES = 2
CMP_PAD = 16
CMP_ROW_STEP = 128
CMP_CHUNK = CMP_ROW_STEP
CMP_TAIL_ZEROS = 2 * CMP_CHUNK
N_CMP_PAD = -(-(CMP_PAD + N_CHUNK) // CMP_ROW_STEP) * CMP_ROW_STEP
CMP_BAND = 16 + TQ // CMP_STRIDE
NEG = -1e30
M_FLOOR = -1e29
SCAN_ROWS = 8
S5_TT = 256
SEG = S5_TT // SCAN_ROWS
SSM_SUPER = 2
SG_CH = SSM_WIDTH // SSM_SUPER
SG_LANES = SSM_LANES // SSM_SUPER
LOG2E = math.log2(math.e)
V_ROWS = HEAD_DIM + 16
FAR_KEYS = 512
FAR_BLOCKS = FAR_KEYS // SLC_LEN
K_COLS = HEAD_DIM + 16
CMP_EXTENTS = tuple(range(2 * CMP_ROW_STEP, N_CMP_PAD + 1, CMP_ROW_STEP))

VMEM_LIMIT = 56 * 1024 * 1024

COL_U = 0
COL_Q = 512
COL_KC = 1024
COL_VC = 1152
COL_KS = 1280
COL_VS = 1408
COL_KW = 1536
COL_VW = 1664
COL_GN = 1792
COL_GB = 1816
COL_END = 3864
GATE_COLS = 128
GATE_ROWS = 32
GATES_PER_KV_HEAD = GQA * 3
INPROJ_PIECE = 512


def _rms(x, g):
    return x * lax.rsqrt(jnp.mean(x * x, axis=-1, keepdims=True) + EPS) * g


def _dot(a, b):
    return jnp.dot(a, b, preferred_element_type=F32)


def _dot_nt(a, b):
    return lax.dot_general(a, b, (((1,), (1,)), ((), ())), preferred_element_type=F32)


def _inproj_kernel(x_ref, g_ref, w_ref,
                   u_ref, qT_ref, kcr_ref, vcr_ref, ks_ref, kw_ref, vsT_ref, vwT_ref, gnT_ref, gb_ref,
                   wm_ref, wb_ref, kstage_ref, vstage_ref):
    @pl.when(pl.program_id(0) == 0)
    def _():
        for dst_ref, lo, hi in ((wm_ref, 0, COL_GN + GATE_COLS), (wb_ref, COL_GB, COL_END)):
            for c in range(lo, hi, INPROJ_PIECE):
                n = min(INPROJ_PIECE, hi - c)
                dst_ref[:, c - lo:c - lo + n] = w_ref[c:c + n, :].T.astype(BF16)

    h = _rms(x_ref[...], g_ref[...]).astype(BF16)
    tm = h.shape[0]
    pm = _dot(h, wm_ref[:, :COL_GN])
    u_ref[...] = pm[:, COL_U:COL_Q]
    qT_ref[...] = (pm[:, COL_Q:COL_KC] * (HEAD_DIM ** -0.5 * LOG2E)).T.astype(BF16)
    vsT = pm[:, COL_VS:COL_KW].T
    vwT = pm[:, COL_VW:COL_GN].T
    ones = jnp.ones((V_ROWS - HEAD_DIM, tm), F32)
    tok = pl.program_id(0) * tm + lax.broadcasted_iota(jnp.int32, (tm, K_COLS - HEAD_DIM), 0)
    col = lax.broadcasted_iota(jnp.int32, (tm, K_COLS - HEAD_DIM), 1)
    blk_onehot = jnp.where((tok // SLC_LEN) % FAR_BLOCKS == col, 1.0, 0.0)
    for g in range(NSA_KV_HEADS):
        lo = g * HEAD_DIM
        ks_ref[g] = jnp.concatenate(
            [pm[:, COL_KS + lo:COL_KS + lo + HEAD_DIM], blk_onehot], axis=1).astype(BF16)
        kw_ref[g] = pm[:, COL_KW + lo:COL_KW + lo + HEAD_DIM].astype(BF16)
        vsT_ref[g] = jnp.concatenate([vsT[lo:lo + HEAD_DIM], ones], axis=0).astype(BF16)
        vwT_ref[g] = jnp.concatenate([vwT[lo:lo + HEAD_DIM], ones], axis=0).astype(BF16)
    for src_col, stage_ref, dst_ref in ((COL_KC, kstage_ref, kcr_ref), (COL_VC, vstage_ref, vcr_ref)):
        stage_ref[...] = pm[:, src_col:src_col + KV_WIDTH]
        for t in range(CMP_STRIDE):
            rows = stage_ref[pl.ds(t, tm // CMP_STRIDE, stride=CMP_STRIDE), :]
            for g in range(NSA_KV_HEADS):
                dst_ref[g, :, t * HEAD_DIM:(t + 1) * HEAD_DIM] = rows[:, g * HEAD_DIM:(g + 1) * HEAD_DIM]
    gn = jax.nn.sigmoid(_dot(h, wm_ref[:, COL_GN:]))
    gnT_ref[...] = gn.T[:GATE_ROWS, :]
    gb_ref[...] = jax.nn.sigmoid(_dot(h, wb_ref[...])).astype(BF16)


def _inproj(x2, g, w, tm=512):
    s = x2.shape[0]
    const = lambda i: (0, 0)
    row = lambda i: (i, 0)
    col = lambda i: (0, i)
    return pl.pallas_call(
        _inproj_kernel,
        grid=(s // tm,),
        in_specs=[
            pl.BlockSpec((tm, D_MODEL), row),
            pl.BlockSpec((1, D_MODEL), const),
            pl.BlockSpec(w.shape, const),
        ],
        out_specs=[
            pl.BlockSpec((tm, SSM_WIDTH), row),
            pl.BlockSpec((NSA_WIDTH, tm), col),
            pl.BlockSpec((NSA_KV_HEADS, tm // CMP_STRIDE, CMP_STRIDE * HEAD_DIM), lambda i: (0, i, 0)),
            pl.BlockSpec((NSA_KV_HEADS, tm // CMP_STRIDE, CMP_STRIDE * HEAD_DIM), lambda i: (0, i, 0)),
            pl.BlockSpec((NSA_KV_HEADS, tm, K_COLS), lambda i: (0, i, 0)),
            pl.BlockSpec((NSA_KV_HEADS, tm, HEAD_DIM), lambda i: (0, i, 0)),
            pl.BlockSpec((NSA_KV_HEADS, V_ROWS, tm), lambda i: (0, 0, i)),
            pl.BlockSpec((NSA_KV_HEADS, V_ROWS, tm), lambda i: (0, 0, i)),
            pl.BlockSpec((GATE_ROWS, tm), col),
            pl.BlockSpec((tm, 2 * D_MODEL), row),
        ],
        out_shape=[
            jax.ShapeDtypeStruct((s, SSM_WIDTH), F32),
            jax.ShapeDtypeStruct((NSA_WIDTH, s), BF16),
            jax.ShapeDtypeStruct((NSA_KV_HEADS, s // CMP_STRIDE, CMP_STRIDE * HEAD_DIM), F32),
            jax.ShapeDtypeStruct((NSA_KV_HEADS, s // CMP_STRIDE, CMP_STRIDE * HEAD_DIM), F32),
            jax.ShapeDtypeStruct((NSA_KV_HEADS, s, K_COLS), BF16),
            jax.ShapeDtypeStruct((NSA_KV_HEADS, s, HEAD_DIM), BF16),
            jax.ShapeDtypeStruct((NSA_KV_HEADS, V_ROWS, s), BF16),
            jax.ShapeDtypeStruct((NSA_KV_HEADS, V_ROWS, s), BF16),
            jax.ShapeDtypeStruct((GATE_ROWS, s), F32),
            jax.ShapeDtypeStruct((s, 2 * D_MODEL), BF16),
        ],
        scratch_shapes=[
            pltpu.VMEM((D_MODEL, COL_GN + GATE_COLS), BF16),
            pltpu.VMEM((D_MODEL, COL_END - COL_GB), BF16),
            pltpu.VMEM((tm, KV_WIDTH), F32),
            pltpu.VMEM((tm, KV_WIDTH), F32),
        ],
        compiler_params=pltpu.CompilerParams(
            dimension_semantics=("arbitrary",), vmem_limit_bytes=VMEM_LIMIT),
        name="inproj",
    )(x2, g, w)


def _s5_kernel(u_ref, perm_ref, permT_ref, b_ref, c_ref, pw_ref, seg_ref, d_ref, wglu_ref, wup_ref,
               ya_ref, xre_ref, xim_ref, st_ref, cre_s, cim_s):
    @pl.when(pl.program_id(0) == 0)
    def _():
        cre_s[...] = jnp.zeros_like(cre_s)
        cim_s[...] = jnp.zeros_like(cim_s)

    u = u_ref[...]
    ub = _dot(perm_ref[...], u.astype(BF16)).astype(BF16)
    for sg in range(SSM_SUPER):
        bu = _dot(ub[:, sg * SG_CH:(sg + 1) * SG_CH], b_ref[sg])
        xre_ref[:, sg * SG_LANES:(sg + 1) * SG_LANES] = bu[:, :SG_LANES]
        xim_ref[:, sg * SG_LANES:(sg + 1) * SG_LANES] = bu[:, SG_LANES:]

    def cmul_add(re, im, are, aim, sre, sim):
        return re + are * sre - aim * sim, im + are * sim + aim * sre

    for sg in range(SSM_SUPER):
        lanes = slice(sg * SG_LANES, (sg + 1) * SG_LANES)

        lam_re, lam_im = pw_ref[0, 0:SCAN_ROWS, lanes], pw_ref[1, 0:SCAN_ROWS, lanes]

        def local(j, carry):
            r0 = pl.multiple_of(j * SCAN_ROWS, SCAN_ROWS)
            re, im = cmul_add(xre_ref[pl.ds(r0, SCAN_ROWS), lanes], xim_ref[pl.ds(r0, SCAN_ROWS), lanes],
                              lam_re, lam_im, *carry)
            xre_ref[pl.ds(r0, SCAN_ROWS), lanes] = re
            xim_ref[pl.ds(r0, SCAN_ROWS), lanes] = im
            return re, im

        zero = jnp.zeros((SCAN_ROWS, SG_LANES), F32)
        re, im = lax.fori_loop(0, SEG, local, (zero, zero), unroll=True)

        for k, shift in enumerate((1, 2, 4)):
            re, im = cmul_add(re, im, seg_ref[2 * k, :, lanes], seg_ref[2 * k + 1, :, lanes],
                              pltpu.roll(re, shift, 0), pltpu.roll(im, shift, 0))
        cin_re, cin_im = cre_s[:, lanes], cim_s[:, lanes]
        re, im = cmul_add(re, im, seg_ref[6, :, lanes], seg_ref[7, :, lanes], cin_re, cin_im)
        cre_s[:, lanes] = re[SCAN_ROWS - 1:SCAN_ROWS, :]
        cim_s[:, lanes] = im[SCAN_ROWS - 1:SCAN_ROWS, :]
        first = lax.broadcasted_iota(jnp.int32, (SCAN_ROWS, SG_LANES), 0) == 0
        start_re = jnp.where(first, cin_re, pltpu.roll(re, 1, 0))
        start_im = jnp.where(first, cin_im, pltpu.roll(im, 1, 0))
        start_re = jnp.concatenate([start_re, start_re], axis=0)
        start_im = jnp.concatenate([start_im, start_im], axis=0)

        def fix(jj, carry):
            r0 = pl.multiple_of(jj * 2 * SCAN_ROWS, 2 * SCAN_ROWS)
            rows = pl.ds(r0, 2 * SCAN_ROWS)
            re, im = cmul_add(xre_ref[rows, lanes], xim_ref[rows, lanes],
                              pw_ref[0, rows, lanes], pw_ref[1, rows, lanes], start_re, start_im)
            st_ref[rows, 2 * sg * SG_LANES:(2 * sg + 1) * SG_LANES] = re.astype(BF16)
            st_ref[rows, (2 * sg + 1) * SG_LANES:(2 * sg + 2) * SG_LANES] = im.astype(BF16)
            return carry

        lax.fori_loop(0, SEG // 2, fix, 0, unroll=True)
    y_perm = jnp.concatenate(
        [_dot(st_ref[:, 2 * sg * SG_LANES:(2 * sg + 2) * SG_LANES], c_ref[sg])
         for sg in range(SSM_SUPER)], axis=1)
    y_hi = y_perm.astype(BF16)
    y_lo = (y_perm - y_hi.astype(F32)).astype(BF16)
    y = _dot(permT_ref[...], y_hi) + _dot(permT_ref[...], y_lo) + d_ref[...] * u
    z = jax.nn.gelu(y)
    z = z * jax.nn.sigmoid(_dot(z.astype(BF16), wglu_ref[...]))
    ya_ref[...] = _dot(z.astype(BF16), wup_ref[...])


def _s5(u, b, c, pw, seg, d, wglu, wup):
    s = u.shape[0]
    t = np.arange(S5_TT)
    perm = np.zeros((S5_TT, S5_TT), np.float32)
    perm[SCAN_ROWS * (t % SEG) + t // SEG, t] = 1.0
    permT = jnp.asarray(perm.T, BF16)
    perm = jnp.asarray(perm, BF16)
    const2 = lambda i: (0, 0)
    const3 = lambda i: (0, 0, 0)
    return pl.pallas_call(
        _s5_kernel,
        grid=(s // S5_TT,),
        in_specs=[
            pl.BlockSpec((S5_TT, SSM_WIDTH), lambda i: (i, 0)),
            pl.BlockSpec(perm.shape, const2),
            pl.BlockSpec(permT.shape, const2),
            pl.BlockSpec(b.shape, const3),
            pl.BlockSpec(c.shape, const3),
            pl.BlockSpec(pw.shape, const3),
            pl.BlockSpec(seg.shape, const3),
            pl.BlockSpec((1, SSM_WIDTH), const2),
            pl.BlockSpec(wglu.shape, const2),
            pl.BlockSpec(wup.shape, const2),
        ],
        out_specs=pl.BlockSpec((S5_TT, D_MODEL), lambda i: (i, 0)),
        out_shape=jax.ShapeDtypeStruct((s, D_MODEL), F32),
        scratch_shapes=[
            pltpu.VMEM((S5_TT, SSM_LANES), F32),
            pltpu.VMEM((S5_TT, SSM_LANES), F32),
            pltpu.VMEM((S5_TT, 2 * SSM_LANES), BF16),
            pltpu.VMEM((1, SSM_LANES), F32),
            pltpu.VMEM((1, SSM_LANES), F32),
        ],
        compiler_params=pltpu.CompilerParams(
            dimension_semantics=("arbitrary",), vmem_limit_bytes=VMEM_LIMIT),
        name="s5",
    )(u, perm, permT, b, c, pw, seg, d, wglu, wup)


def _s5_params(a_re, a_im, log_dt, b_re, b_im, c_re, c_im):
    dt = jnp.exp(log_dt.astype(F32))[:, None]
    ar, ai = a_re.astype(F32), a_im.astype(F32)
    zr, zi = ar * dt, ai * dt

    def power(n):
        mag = jnp.exp(n * zr)
        return mag * jnp.cos(n * zi), mag * jnp.sin(n * zi)

    lr, li = power(1.0)
    den = ar * ar + ai * ai
    kr = ((lr - 1.0) * ar + li * ai) / den
    ki = (li * ar - (lr - 1.0) * ai) / den
    br, bi = b_re.astype(F32), b_im.astype(F32)
    bbr = kr[..., None] * br - ki[..., None] * bi
    bbi = kr[..., None] * bi + ki[..., None] * br
    groups_per_super = SSM_GROUPS // SSM_SUPER

    def super_blocks(w, per_group_rows, per_group_cols):
        rows = w.reshape(SSM_SUPER, groups_per_super * per_group_rows, per_group_cols)
        tiled = jnp.tile(rows, (1, 1, groups_per_super))
        same = (lax.broadcasted_iota(jnp.int32, tiled.shape, 1) // per_group_rows
                == lax.broadcasted_iota(jnp.int32, tiled.shape, 2) // per_group_cols)
        return jnp.where(same, tiled, 0.0)

    b_in = lambda b: super_blocks(jnp.transpose(b, (0, 2, 1)), SSM_GROUP, SSM_STATE)
    c_out = lambda c: super_blocks(jnp.transpose(c, (0, 2, 1)), SSM_STATE, SSM_GROUP)
    zr, zi = zr.reshape(1, SSM_LANES), zi.reshape(1, SSM_LANES)
    pw = jnp.stack(power(jnp.arange(1, SEG + 1, dtype=F32)[:, None]))
    pw = jnp.broadcast_to(pw[:, :, None, :], (2, SEG, SCAN_ROWS, SSM_LANES)).reshape(2, S5_TT, SSM_LANES)
    row = jnp.arange(SCAN_ROWS)[:, None]
    seg = []
    for shift in (1, 2, 4):
        pr, pi = power(float(SEG * shift))
        seg += [jnp.where(row >= shift, pr, 0.0), jnp.where(row >= shift, pi, 0.0)]
    seg += list(power(SEG * (row + 1).astype(F32)))
    b_sg = jnp.concatenate([b_in(bbr), b_in(bbi)], axis=2)
    c_sg = jnp.concatenate([c_out(c_re.astype(F32)), -c_out(c_im.astype(F32))], axis=1)
    return b_sg.astype(BF16), c_sg.astype(BF16), pw, jnp.stack(seg)


def _compress_kernel(xk_ref, xv_ref, w1k_ref, w2k_ref, pk_ref, w1v_ref, w2vT_ref, pv_ref, kc_ref, vcT_ref):
    half = CMP_STRIDE * HEAD_DIM

    def hidden(x_ref, w1_ref, pos_ref):
        x = x_ref[0].astype(BF16)
        first = _dot(x, w1_ref[:half, :])
        second = _dot(x, w1_ref[half:, :])
        bias = _dot(jnp.broadcast_to(pos_ref[...], (8, 2 * half)).astype(BF16), w1_ref[...])[:1]
        pre = first + pltpu.roll(second, N_CHUNK - 1, 0) + bias
        return jax.nn.gelu(pre).astype(BF16)

    hk = hidden(xk_ref, w1k_ref, pk_ref)
    kc = _dot(hk, w2k_ref[...])
    tail_pad = N_CMP_PAD - CMP_PAD - N_CHUNK
    kc_ref[0] = jnp.concatenate([jnp.zeros((CMP_PAD, HEAD_DIM), F32), kc,
                                 jnp.zeros((tail_pad, HEAD_DIM), F32)], axis=0).astype(BF16)
    hv = hidden(xv_ref, w1v_ref, pv_ref)
    vcT = _dot_nt(w2vT_ref[...], hv)
    vcT_ref[0] = jnp.concatenate([jnp.zeros((HEAD_DIM, CMP_PAD), F32), vcT,
                                  jnp.zeros((HEAD_DIM, tail_pad), F32)], axis=1).astype(BF16)


def _compress(xk, xv, w1k, w2k, pk, w1v, w2vT, pv):
    const2 = lambda g: (0, 0)
    head = lambda g: (g, 0, 0)
    return pl.pallas_call(
        _compress_kernel,
        grid=(NSA_KV_HEADS,),
        in_specs=[
            pl.BlockSpec((1, N_CHUNK, CMP_STRIDE * HEAD_DIM), head),
            pl.BlockSpec((1, N_CHUNK, CMP_STRIDE * HEAD_DIM), head),
            pl.BlockSpec(w1k.shape, const2),
            pl.BlockSpec(w2k.shape, const2),
            pl.BlockSpec(pk.shape, const2),
            pl.BlockSpec(w1v.shape, const2),
            pl.BlockSpec(w2vT.shape, const2),
            pl.BlockSpec(pv.shape, const2),
        ],
        out_specs=[
            pl.BlockSpec((1, N_CMP_PAD, HEAD_DIM), head),
            pl.BlockSpec((1, HEAD_DIM, N_CMP_PAD), head),
        ],
        out_shape=[
            jax.ShapeDtypeStruct((NSA_KV_HEADS, N_CMP_PAD, HEAD_DIM), BF16),
            jax.ShapeDtypeStruct((NSA_KV_HEADS, HEAD_DIM, N_CMP_PAD), BF16),
        ],
        compiler_params=pltpu.CompilerParams(
            dimension_semantics=("arbitrary",), vmem_limit_bytes=VMEM_LIMIT),
        name="compress",
    )(xk, xv, w1k, w2k, pk, w1v, w2vT, pv)


def _nsa_kernel(qT_ref, kc_ref, vcT_ref, ks_ref, vsT_ref, kw_ref, vwT_ref, bias_ref, bct_ref, mt_ref, gT_ref,
                oT_ref, sc_ref, neg_ref, negfar_ref, m_ref, acc_ref, tot_ref, sbuf0_ref, sbuf1_ref, mloc_ref, pslc_ref,
                cm_ref, cl_ref):
    i = pl.program_id(1)
    s0 = i * TQ
    qT = jnp.concatenate([qT_ref[r * HEAD_DIM:(r + 1) * HEAD_DIM, :] for r in range(GQA)], axis=1)

    qT_nomask = jnp.concatenate([qT, jnp.zeros((K_COLS - HEAD_DIM, QL), BF16)], axis=0)

    gate0 = pl.program_id(0) * GATES_PER_KV_HEAD

    def gate_row(branch):
        return jnp.concatenate([gT_ref[pl.ds(gate0 + r * 3 + branch, 1), :] for r in range(GQA)], axis=1)

    def reset():
        m_ref[...] = jnp.full_like(m_ref, M_FLOOR)
        acc_ref[...] = jnp.zeros_like(acc_ref)

    def attend(k_ref, vT_ref, start, size, add):
        k = k_ref[0, pl.ds(start, size), :]
        s = _dot(k, qT if k.shape[1] == HEAD_DIM else qT_nomask) + add
        m_prev = m_ref[...]
        m_new = jnp.maximum(m_prev, jnp.max(s, axis=0, keepdims=True))
        alpha = jnp.exp2(m_prev - m_new)
        p = jnp.exp2(s - m_new).astype(BF16)
        acc_ref[...] = alpha * acc_ref[...] + _dot(vT_ref[0, :, pl.ds(start, size)], p)
        m_ref[...] = m_new

    def finish(branch):
        acc = acc_ref[...]
        scale = gate_row(branch) / jnp.maximum(acc[HEAD_DIM:HEAD_DIM + 1, :], 1e-30)
        tot_ref[...] += acc[:HEAD_DIM, :] * scale

    n0 = i * (TQ // CMP_STRIDE)
    band0 = pl.multiple_of(n0 + CMP_PAD - 16, 8)

    def cmp_branch(nrows):
        n_chunks = nrows // CMP_CHUNK
        chunk_rows = lambda c: slice(c * CMP_CHUNK, (c + 1) * CMP_CHUNK)
        for c in range(n_chunks):
            s = _dot(kc_ref[0, chunk_rows(c), :], qT)
            if c == 0:
                pad = lax.broadcasted_iota(jnp.int32, (CMP_CHUNK, QL), 0) < CMP_PAD
                s = jnp.where(pad, NEG, s)
            if c >= n_chunks - 2:
                tail0 = pl.multiple_of(c * CMP_CHUNK - band0 + CMP_TAIL_ZEROS, 8)
                s = s + bct_ref[0, pl.ds(tail0, CMP_CHUNK), :]
            sc_ref[chunk_rows(c), :] = s
            cm_ref[c] = jnp.max(s, axis=0, keepdims=True)
        mc = cm_ref[0]
        for c in range(1, n_chunks):
            mc = jnp.maximum(mc, cm_ref[c])
        mc = jnp.maximum(mc, M_FLOOR)
        for c in range(n_chunks):
            e = jnp.exp2(sc_ref[chunk_rows(c), :] - mc)
            sc_ref[chunk_rows(c), :] = e
            cl_ref[c] = jnp.sum(e, axis=0, keepdims=True)
            pv = _dot(vcT_ref[0, :, chunk_rows(c)], e.astype(BF16))
            if c == 0:
                tot_ref[...] = pv
            else:
                tot_ref[...] += pv
        total = cl_ref[0]
        for c in range(1, n_chunks):
            total = total + cl_ref[c]
        rinv = 1.0 / jnp.maximum(total, 1e-30)
        tot_ref[...] = tot_ref[...] * (rinv * gate_row(0))
        mt = mt_ref[:, 0:nrows]
        p_slc = jnp.zeros((N_SLC, TQ), F32)
        for c in range(n_chunks):
            pc = sc_ref[chunk_rows(c), :] * rinv
            imp = pc[:, 0:TQ]
            for r in range(1, GQA):
                imp = imp + pc[:, r * TQ:(r + 1) * TQ]
            sc_ref[chunk_rows(c), 0:TQ] = imp
        rem = sc_ref[0:nrows, 0:TQ]
        for _ in range(2):
            piece = rem.astype(BF16)
            p_slc = p_slc + _dot(mt, piece)
            rem = rem - piece.astype(F32)
        pslc_ref[...] = p_slc

    prev_rows = 0
    for nrows in CMP_EXTENTS:
        lo, hi = prev_rows, nrows
        pl.when((band0 + CMP_BAND > lo) & (band0 + CMP_BAND <= hi))(functools.partial(cmp_branch, nrows))
        prev_rows = nrows

    reset()
    win_start = jnp.maximum(s0 - WINDOW, 0)
    win_bias0 = WINDOW - (s0 - win_start)
    piece_keys = (WINDOW + TQ) // WIN_PIECES
    wbufs = (sbuf0_ref, sbuf1_ref)

    def win_logits(piece):
        start = pl.multiple_of(win_start + piece * piece_keys, NEAR)
        s = (_dot(kw_ref[0, pl.ds(start, piece_keys), :], qT)
             + bias_ref[0, pl.ds(pl.multiple_of(win_bias0 + piece * piece_keys, NEAR), piece_keys), :])
        wbufs[piece % 2][0:piece_keys, :] = s
        mloc_ref[piece % 2] = jnp.max(s, axis=0, keepdims=True)

    def win_consume(piece):
        start = pl.multiple_of(win_start + piece * piece_keys, NEAR)
        m_prev = m_ref[...]
        m_new = jnp.maximum(m_prev, mloc_ref[piece % 2])
        alpha = jnp.exp2(m_prev - m_new)
        p = jnp.exp2(wbufs[piece % 2][0:piece_keys, :] - m_new).astype(BF16)
        acc_ref[...] = alpha * acc_ref[...] + _dot(vwT_ref[0, :, pl.ds(start, piece_keys)], p)
        m_ref[...] = m_new

    win_logits(0)
    for piece in range(WIN_PIECES):
        if piece + 1 < WIN_PIECES:
            win_logits(piece + 1)
        win_consume(piece)
    finish(2)

    p_slc = pslc_ref[...]
    blk = lax.broadcasted_iota(jnp.int32, (N_SLC, TQ), 0)
    cur = (s0 + lax.broadcasted_iota(jnp.int32, (N_SLC, TQ), 1)) // SLC_LEN
    valid = blk <= cur
    forced = valid & ((blk == 0) | (blk >= cur - (N_LOCAL - 1)))
    n_pick = N_SEL - (N_LOCAL + 1)
    start_score = jnp.where(forced | ~valid, -jnp.inf, p_slc)
    near_blk = (s0 - NEAR) // SLC_LEN

    def publish(score):
        neg = jnp.where(score == -jnp.inf, 0.0, NEG)
        neg_ref[...] = jnp.concatenate([neg] * GQA, axis=1)
        negfar_ref[...] = jnp.concatenate([jnp.where(blk >= near_blk, NEG, neg)] * GQA, axis=1)

    score = start_score
    for _ in range(n_pick):
        score = jnp.where(score == jnp.max(score, axis=0, keepdims=True), -jnp.inf, score)
    marks = jnp.sum(jnp.where(score == -jnp.inf, 1.0, 0.0), axis=0, keepdims=True)
    cur_q = cur[0:1, :]
    n_forced = jnp.minimum(cur_q + 1, N_LOCAL + 1)
    expected = (N_SLC - 1 - cur_q) + n_forced + jnp.minimum(n_pick, cur_q + 1 - n_forced)
    tied = jnp.max(jnp.abs(marks - expected.astype(F32))) > 0.5
    fast_score = score

    @pl.when(jnp.logical_not(tied))
    def _():
        publish(fast_score)

    @pl.when(tied)
    def _():
        score = jnp.where(forced | ~valid, -jnp.inf, pslc_ref[...])
        blk_f = blk.astype(F32)
        for _ in range(n_pick):
            best = jnp.max(score, axis=0, keepdims=True)
            first = jnp.min(jnp.where(score == best, blk_f, float(N_SLC)), axis=0, keepdims=True)
            score = jnp.where(blk_f == first, -jnp.inf, score)
        publish(score)

    def block_mask(ref, j0, nblk):
        return jnp.concatenate(
            [jnp.broadcast_to(ref[pl.ds(j0 + b, 1), :], (SLC_LEN, QL)) for b in range(nblk)], axis=0)

    reset()
    n_far = (s0 - NEAR + FAR_KEYS - 1) // FAR_KEYS

    sbufs = (sbuf0_ref, sbuf1_ref)

    def far_logits(c, slot):
        start = pl.multiple_of(c * FAR_KEYS, FAR_KEYS)
        mask_rows = negfar_ref[pl.ds(pl.multiple_of(c * FAR_BLOCKS, FAR_BLOCKS), FAR_BLOCKS), :]
        extra = jnp.concatenate([mask_rows, jnp.zeros((K_COLS - HEAD_DIM - FAR_BLOCKS, QL), F32)], axis=0)
        q_masked = jnp.concatenate([qT, extra.astype(BF16)], axis=0)
        s = _dot(ks_ref[0, pl.ds(start, FAR_KEYS), :], q_masked)
        sbufs[slot][...] = s
        mloc_ref[slot] = jnp.max(s, axis=0, keepdims=True)

    def far_consume(c, slot):
        start = pl.multiple_of(c * FAR_KEYS, FAR_KEYS)
        m_prev = m_ref[...]
        m_new = jnp.maximum(m_prev, mloc_ref[slot])
        alpha = jnp.exp2(m_prev - m_new)
        p = jnp.exp2(sbufs[slot][...] - m_new).astype(BF16)
        acc_ref[...] = alpha * acc_ref[...] + _dot(vsT_ref[0, :, pl.ds(start, FAR_KEYS)], p)
        m_ref[...] = m_new

    @pl.when(i >= 1)
    def _():
        n_pairs = (n_far + 1) // 2
        far_logits(0, 0)
        attend(ks_ref, vsT_ref, pl.multiple_of(s0 - NEAR, NEAR), NEAR + TQ,
               bias_ref[0, WINDOW - NEAR:WINDOW + TQ, :] + block_mask(neg_ref, near_blk, (NEAR + TQ) // SLC_LEN))

        def far_body(p, carry):
            far_logits(2 * p + 1, 1)
            far_consume(2 * p, 0)
            far_logits(2 * p + 2, 0)
            far_consume(2 * p + 1, 1)
            return carry

        lax.fori_loop(0, n_pairs - 1, far_body, 0)
        last = 2 * (n_pairs - 1)

        @pl.when(n_far % 2 == 0)
        def _():
            far_logits(last + 1, 1)
            far_consume(last, 0)
            far_consume(last + 1, 1)

        @pl.when(n_far % 2 == 1)
        def _():
            far_consume(last, 0)

    @pl.when(i == 0)
    def _():
        attend(ks_ref, vsT_ref, 0, TQ, bias_ref[0, WINDOW:WINDOW + TQ, :] + block_mask(neg_ref, 0, TQ // SLC_LEN))

    finish(1)

    tot = tot_ref[...]
    for r in range(GQA):
        oT_ref[r * HEAD_DIM:(r + 1) * HEAD_DIM, :] = tot[:, r * TQ:(r + 1) * TQ]


def _nsa(qT, kc, vcT, ks, vsT, kw, vwT, bias_tile, bias_cmp, mt, gnT):
    s = qT.shape[1]
    head3 = lambda g, i: (g, 0, 0)
    return pl.pallas_call(
        _nsa_kernel,
        grid=(NSA_KV_HEADS, s // TQ),
        in_specs=[
            pl.BlockSpec((GQA * HEAD_DIM, TQ), lambda g, i: (g, i)),
            pl.BlockSpec((1, N_CMP_PAD, HEAD_DIM), head3),
            pl.BlockSpec((1, HEAD_DIM, N_CMP_PAD), head3),
            pl.BlockSpec((1, s, K_COLS), head3),
            pl.BlockSpec((1, V_ROWS, s), head3),
            pl.BlockSpec((1, s, HEAD_DIM), head3),
            pl.BlockSpec((1, V_ROWS, s), head3),
            pl.BlockSpec((1, 2 * WINDOW + TQ, QL), head3, pipeline_mode=pl.Buffered(1)),
            pl.BlockSpec((1,) + bias_cmp.shape[1:], head3, pipeline_mode=pl.Buffered(1)),
            pl.BlockSpec(mt.shape, lambda g, i: (0, 0)),
            pl.BlockSpec((GATE_ROWS, TQ), lambda g, i: (0, i)),
        ],
        out_specs=pl.BlockSpec((GQA * HEAD_DIM, TQ), lambda g, i: (g, i)),
        out_shape=jax.ShapeDtypeStruct((NSA_WIDTH, s), F32),
        scratch_shapes=[
            pltpu.VMEM((N_CMP_PAD, QL), F32),
            pltpu.VMEM((N_SLC, QL), F32),
            pltpu.VMEM((N_SLC, QL), F32),
            pltpu.VMEM((1, QL), F32),
            pltpu.VMEM((V_ROWS, QL), F32),
            pltpu.VMEM((HEAD_DIM, QL), F32),
            pltpu.VMEM((FAR_KEYS, QL), F32),
            pltpu.VMEM((FAR_KEYS, QL), F32),
            pltpu.VMEM((2, 1, QL), F32),
            pltpu.VMEM((N_SLC, TQ), F32),
            pltpu.VMEM((N_CMP_PAD // CMP_CHUNK, 1, QL), F32),
            pltpu.VMEM((N_CMP_PAD // CMP_CHUNK, 1, QL), F32),
        ],
        compiler_params=pltpu.CompilerParams(
            dimension_semantics=("arbitrary", "arbitrary"), vmem_limit_bytes=VMEM_LIMIT),
        name="nsa",
    )(qT, kc, vcT, ks, vsT, kw, vwT, bias_tile, bias_cmp, mt, gnT)


def _t5_bucket(dist):
    n = jnp.maximum(dist, 0)
    max_exact = REL_BUCKETS // 2
    nf = jnp.maximum(n, 1).astype(F32)
    large = max_exact + (jnp.log(nf / max_exact) / math.log(REL_MAX_DIST / max_exact)
                         * (REL_BUCKETS - max_exact)).astype(jnp.int32)
    large = jnp.minimum(large, REL_BUCKETS - 1)
    return jnp.where(n < max_exact, n, large)


def _bias_tiles(rel_bias):
    tab = rel_bias.astype(F32)
    tab = (tab[_t5_bucket(jnp.arange(NEAR))] - tab[REL_BUCKETS - 1]).T * LOG2E
    tab = jnp.concatenate([tab, jnp.zeros((NSA_HEADS, 1), F32)], axis=1)

    def by_distance(d):
        return jnp.where(d >= 0, tab[:, jnp.clip(d, 0, NEAR)], NEG)

    def toeplitz(c, nk, nq, step=1):
        n = step * (nk - 1) + nq
        w = by_distance(jnp.arange(n) + c - step * (nk - 1))
        reps = -(-nk * (n + step) // n)
        return jnp.tile(w, (1, reps))[:, :nk * (n + step)].reshape(NSA_HEADS, nk, n + step)[:, ::-1, :nq]

    def per_kv_head(a):
        a = a.reshape(NSA_KV_HEADS, GQA, a.shape[1], TQ)
        return jnp.transpose(a, (0, 2, 1, 3)).reshape(NSA_KV_HEADS, a.shape[2], QL)

    k2, q2 = np.arange(NEAR)[:, None], np.arange(NEAR)[None, :]
    const = lambda a: jnp.broadcast_to(jnp.asarray(a, F32), (NSA_HEADS, NEAR, NEAR))
    edge = WINDOW // NEAR
    nb = TQ // NEAR
    blocks, slabs = {}, {}

    def block(d):
        d = min(max(d, -1), edge + 1)
        if d not in blocks:
            if d < 0:
                b = const(NEG)
            elif d < 2:
                b = toeplitz(d * NEAR, NEAR, NEAR)
            elif d < edge:
                b = const(0.0)
            else:
                b = const(np.where(k2 > q2, 0.0, NEG) if d == edge else NEG)
            blocks[d] = jnp.transpose(b.reshape(NSA_KV_HEADS, GQA, NEAR, NEAR), (0, 2, 1, 3))
        return blocks[d]

    def slab(dd):
        dd = min(max(dd, -nb), edge + 1)
        if dd not in slabs:
            s = jnp.stack([block(dd + b) for b in range(nb)], axis=3)
            slabs[dd] = s.reshape(NSA_KV_HEADS, NEAR, QL)
        return slabs[dd]

    tile = jnp.concatenate([slab(edge - a) for a in range((2 * WINDOW + TQ) // NEAR)], axis=1)
    cmp_band = per_kv_head(toeplitz(16 * CMP_STRIDE - (CMP_LEN - 1), CMP_BAND, TQ, step=CMP_STRIDE))
    cmp_tail = jnp.concatenate([
        jnp.zeros((NSA_KV_HEADS, CMP_TAIL_ZEROS, QL), F32), cmp_band,
        jnp.full((NSA_KV_HEADS, 2 * CMP_CHUNK, QL), NEG, F32)], axis=1)
    return tile, cmp_tail


def _overlap_matrix():
    ratio = SLC_LEN // CMP_STRIDE
    front = CMP_LEN // CMP_STRIDE - 1
    w_ov = np.convolve(np.ones(ratio), np.ones(CMP_LEN // CMP_STRIDE))
    mt = np.zeros((N_SLC, N_CMP_PAD), np.float32)
    for j in range(N_SLC):
        for o, w in enumerate(w_ov):
            n = ratio * j + o - front
            if 0 <= n < N_CMP:
                mt[j, CMP_PAD + n] = w
    return jnp.asarray(mt, BF16)


def _tail_kernel(x_ref, ya_ref, oT_ref, gb_ref, wup_ref, wout_ref, g_ref, wg_ref, wu_ref, wd_ref, gf_ref, o_ref):
    yb = _dot(oT_ref[...].T.astype(BF16), wup_ref[...])
    mix = gb_ref[:, :D_MODEL].astype(F32) * ya_ref[...] + gb_ref[:, D_MODEL:].astype(F32) * yb
    x = x_ref[...] + _dot(mix.astype(BF16), wout_ref[...])
    h = _rms(x, g_ref[...]).astype(BF16)
    f = jax.nn.silu(_dot(h, wg_ref[...])) * _dot(h, wu_ref[...])
    x = x + _dot(f.astype(BF16), wd_ref[...])
    o_ref[...] = _rms(x, gf_ref[...])


def _tail(x2, ya, oT, gb, wup, wout, g, wg, wu, wd, gf, tm=256):
    s = x2.shape[0]
    row = lambda i: (i, 0)
    const = lambda i: (0, 0)
    return pl.pallas_call(
        _tail_kernel,
        grid=(s // tm,),
        in_specs=[
            pl.BlockSpec((tm, D_MODEL), row),
            pl.BlockSpec((tm, D_MODEL), row),
            pl.BlockSpec((NSA_WIDTH, tm), lambda i: (0, i)),
            pl.BlockSpec((tm, 2 * D_MODEL), row),
            pl.BlockSpec(wup.shape, const),
            pl.BlockSpec(wout.shape, const),
            pl.BlockSpec((1, D_MODEL), const),
            pl.BlockSpec(wg.shape, const),
            pl.BlockSpec(wu.shape, const),
            pl.BlockSpec(wd.shape, const),
            pl.BlockSpec((1, D_MODEL), const),
        ],
        out_specs=pl.BlockSpec((tm, D_MODEL), row),
        out_shape=jax.ShapeDtypeStruct((s, D_MODEL), F32),
        compiler_params=pltpu.CompilerParams(
            dimension_semantics=("arbitrary",), vmem_limit_bytes=VMEM_LIMIT),
        name="tail",
    )(x2, ya, oT, gb, wup, wout, g, wg, wu, wd, gf)


def kernel(x, norm_mix_g, w_in, ssm_a_re, ssm_a_im, ssm_log_dt, ssm_b_re, ssm_b_im, ssm_c_re, ssm_c_im, ssm_d, ssm_w_glu, w_up_ssm, cmp_pos_k, cmp_pos_v, cmp_w1_k, cmp_w2_k, cmp_w1_v, cmp_w2_v, rel_bias, w_up_nsa, w_out, norm_ffn_g, w_ffn_gate, w_ffn_up, w_ffn_down, norm_final_g):
    bsz, s, _ = x.shape
    assert (bsz, s) == (1, SEQ) and w_in.shape[0] == 1
    x2 = x.reshape(s, D_MODEL)
    l = 0
    row = lambda v: v.astype(F32).reshape(1, -1)

    u, qT, kcr, vcr, ks, kw, vsT, vwT, gnT, gb = _inproj(x2, row(norm_mix_g[l]), w_in[l].astype(F32).T)

    b_sg, c_sg, pw, seg = _s5_params(
        ssm_a_re[l], ssm_a_im[l], ssm_log_dt[l], ssm_b_re[l], ssm_b_im[l], ssm_c_re[l], ssm_c_im[l])
    ya = _s5(u, b_sg, c_sg, pw, seg, row(ssm_d[l]), ssm_w_glu[l].astype(BF16), w_up_ssm[l].astype(BF16))

    kc, vcT = _compress(kcr, vcr,
                        cmp_w1_k[l].astype(BF16), cmp_w2_k[l].astype(BF16), row(cmp_pos_k[l]),
                        cmp_w1_v[l].astype(BF16), cmp_w2_v[l].T.astype(BF16), row(cmp_pos_v[l]))

    bias_tile, bias_cmp = _bias_tiles(rel_bias)
    oT = _nsa(qT, kc, vcT, ks, vsT, kw, vwT, bias_tile, bias_cmp, _overlap_matrix(), gnT)

    out = _tail(x2, ya, oT, gb, w_up_nsa[l].astype(BF16), w_out[l].astype(BF16),
                row(norm_ffn_g[l]), w_ffn_gate[l].astype(BF16), w_ffn_up[l].astype(BF16),
                w_ffn_down[l].astype(BF16), row(norm_final_g))
    return out.reshape(bsz, s, D_MODEL)
```

```python
import functools
import math

import numpy as np
import jax
import jax.numpy as jnp
from jax import lax
from jax.experimental import pallas as pl
from jax.experimental.pallas import tpu as pltpu

F32 = jnp.float32
BF16 = jnp.bfloat16

D_MODEL = 1024
SEQ = 16384
EPS = 1e-6
SSM_WIDTH = 512
SSM_GROUP = 16
SSM_GROUPS = SSM_WIDTH // SSM_GROUP
SSM_STATE = 64
SSM_LANES = SSM_GROUPS * SSM_STATE
NSA_HEADS = 8
NSA_KV_HEADS = 2
GQA = NSA_HEADS // NSA_KV_HEADS
HEAD_DIM = 64
NSA_WIDTH = NSA_HEADS * HEAD_DIM
KV_WIDTH = NSA_KV_HEADS * HEAD_DIM
CMP_LEN = 32
CMP_STRIDE = 16
SLC_LEN = 64
N_SEL = 16
N_LOCAL = 2
WINDOW = 512
BIG = 1e4
REL_BUCKETS = 32
REL_MAX_DIST = 128

N_CHUNK = SEQ // CMP_STRIDE
N_CMP = (SEQ - CMP_LEN) // CMP_STRIDE + 1
N_SLC = SEQ // SLC_LEN
TQ = 256
QL = GQA * TQ
NEAR = 128
assert TQ > NEAR and TQ % NEAR == 0
WIN_PIECES = 2
CMP_PAD = 16
CMP_ROW_STEP = 128
CMP_CHUNK = CMP_ROW_STEP
CMP_TAIL_ZEROS = 2 * CMP_CHUNK
N_CMP_PAD = -(-(CMP_PAD + N_CHUNK) // CMP_ROW_STEP) * CMP_ROW_STEP
CMP_BAND = 16 + TQ // CMP_STRIDE
NEG = -1e30
M_FLOOR = -1e29
SCAN_ROWS = 8
S5_TT = 256
SEG = S5_TT // SCAN_ROWS
SSM_SUPER = 2
SG_CH = SSM_WIDTH // SSM_SUPER
SG_LANES = SSM_LANES // SSM_SUPER
LOG2E = math.log2(math.e)
V_ROWS = HEAD_DIM + 16
FAR_KEYS = 512
FAR_BLOCKS = FAR_KEYS // SLC_LEN
K_COLS = HEAD_DIM + 16
CMP_EXTENTS = tuple(range(2 * CMP_ROW_STEP, N_CMP_PAD + 1, CMP_ROW_STEP))

VMEM_LIMIT = 56 * 1024 * 1024

COL_U = 0
COL_Q = 512
COL_KC = 1024
COL_VC = 1152
COL_KS = 1280
COL_VS = 1408
COL_KW = 1536
COL_VW = 1664
COL_GN = 1792
COL_GB = 1816
COL_END = 3864
GATE_COLS = 128
GATE_ROWS = 32
GATES_PER_KV_HEAD = GQA * 3
INPROJ_PIECE = 512


def _rms(x, g):
    return x * lax.rsqrt(jnp.mean(x * x, axis=-1, keepdims=True) + EPS) * g


def _dot(a, b):
    return jnp.dot(a, b, preferred_element_type=F32)


def _dot_nt(a, b):
    return lax.dot_general(a, b, (((1,), (1,)), ((), ())), preferred_element_type=F32)


def _inproj_kernel(x_ref, g_ref, w_ref,
                   u_ref, qT_ref, kcr_ref, vcr_ref, ks_ref, kw_ref, vsT_ref, vwT_ref, gnT_ref, gb_ref,
                   wm_ref, wb_ref, kstage_ref, vstage_ref):
    @pl.when(pl.program_id(0) == 0)
    def _():
        for dst_ref, lo, hi in ((wm_ref, 0, COL_GN + GATE_COLS), (wb_ref, COL_GB, COL_END)):
            for c in range(lo, hi, INPROJ_PIECE):
                n = min(INPROJ_PIECE, hi - c)
                dst_ref[:, c - lo:c - lo + n] = w_ref[c:c + n, :].T.astype(BF16)

    h = _rms(x_ref[...], g_ref[...]).astype(BF16)
    tm = h.shape[0]
    pm = _dot(h, wm_ref[:, :COL_GN])
    u_ref[...] = pm[:, COL_U:COL_Q]
    qT_ref[...] = (pm[:, COL_Q:COL_KC] * (HEAD_DIM ** -0.5 * LOG2E)).T.astype(BF16)
    vsT = pm[:, COL_VS:COL_KW].T
    vwT = pm[:, COL_VW:COL_GN].T
    ones = jnp.ones((V_ROWS - HEAD_DIM, tm), F32)
    tok = pl.program_id(0) * tm + lax.broadcasted_iota(jnp.int32, (tm, K_COLS - HEAD_DIM), 0)
    col = lax.broadcasted_iota(jnp.int32, (tm, K_COLS - HEAD_DIM), 1)
    blk_onehot = jnp.where((tok // SLC_LEN) % FAR_BLOCKS == col, 1.0, 0.0)
    for g in range(NSA_KV_HEADS):
        lo = g * HEAD_DIM
        ks_ref[g] = jnp.concatenate(
            [pm[:, COL_KS + lo:COL_KS + lo + HEAD_DIM], blk_onehot], axis=1).astype(BF16)
        kw_ref[g] = pm[:, COL_KW + lo:COL_KW + lo + HEAD_DIM].astype(BF16)
        vsT_ref[g] = jnp.concatenate([vsT[lo:lo + HEAD_DIM], ones], axis=0).astype(BF16)
        vwT_ref[g] = jnp.concatenate([vwT[lo:lo + HEAD_DIM], ones], axis=0).astype(BF16)
    for src_col, stage_ref, dst_ref in ((COL_KC, kstage_ref, kcr_ref), (COL_VC, vstage_ref, vcr_ref)):
        stage_ref[...] = pm[:, src_col:src_col + KV_WIDTH]
        for t in range(CMP_STRIDE):
            rows = stage_ref[pl.ds(t, tm // CMP_STRIDE, stride=CMP_STRIDE), :]
            for g in range(NSA_KV_HEADS):
                dst_ref[g, :, t * HEAD_DIM:(t + 1) * HEAD_DIM] = rows[:, g * HEAD_DIM:(g + 1) * HEAD_DIM]
    gn = jax.nn.sigmoid(_dot(h, wm_ref[:, COL_GN:]))
    gnT_ref[...] = gn.T[:GATE_ROWS, :]
    gb_ref[...] = jax.nn.sigmoid(_dot(h, wb_ref[...])).astype(BF16)


def _inproj(x2, g, w, tm=512):
    s = x2.shape[0]
    const = lambda i: (0, 0)
    row = lambda i: (i, 0)
    col = lambda i: (0, i)
    return pl.pallas_call(
        _inproj_kernel,
        grid=(s // tm,),
        in_specs=[
            pl.BlockSpec((tm, D_MODEL), row),
            pl.BlockSpec((1, D_MODEL), const),
            pl.BlockSpec(w.shape, const),
        ],
        out_specs=[
            pl.BlockSpec((tm, SSM_WIDTH), row),
            pl.BlockSpec((NSA_WIDTH, tm), col),
            pl.BlockSpec((NSA_KV_HEADS, tm // CMP_STRIDE, CMP_STRIDE * HEAD_DIM), lambda i: (0, i, 0)),
            pl.BlockSpec((NSA_KV_HEADS, tm // CMP_STRIDE, CMP_STRIDE * HEAD_DIM), lambda i: (0, i, 0)),
            pl.BlockSpec((NSA_KV_HEADS, tm, K_COLS), lambda i: (0, i, 0)),
            pl.BlockSpec((NSA_KV_HEADS, tm, HEAD_DIM), lambda i: (0, i, 0)),
            pl.BlockSpec((NSA_KV_HEADS, V_ROWS, tm), lambda i: (0, 0, i)),
            pl.BlockSpec((NSA_KV_HEADS, V_ROWS, tm), lambda i: (0, 0, i)),
            pl.BlockSpec((GATE_ROWS, tm), col),
            pl.BlockSpec((tm, 2 * D_MODEL), row),
        ],
        out_shape=[
            jax.ShapeDtypeStruct((s, SSM_WIDTH), F32),
            jax.ShapeDtypeStruct((NSA_WIDTH, s), BF16),
            jax.ShapeDtypeStruct((NSA_KV_HEADS, s // CMP_STRIDE, CMP_STRIDE * HEAD_DIM), F32),
            jax.ShapeDtypeStruct((NSA_KV_HEADS, s // CMP_STRIDE, CMP_STRIDE * HEAD_DIM), F32),
            jax.ShapeDtypeStruct((NSA_KV_HEADS, s, K_COLS), BF16),
            jax.ShapeDtypeStruct((NSA_KV_HEADS, s, HEAD_DIM), BF16),
            jax.ShapeDtypeStruct((NSA_KV_HEADS, V_ROWS, s), BF16),
            jax.ShapeDtypeStruct((NSA_KV_HEADS, V_ROWS, s), BF16),
            jax.ShapeDtypeStruct((GATE_ROWS, s), F32),
            jax.ShapeDtypeStruct((s, 2 * D_MODEL), BF16),
        ],
        scratch_shapes=[
            pltpu.VMEM((D_MODEL, COL_GN + GATE_COLS), BF16),
            pltpu.VMEM((D_MODEL, COL_END - COL_GB), BF16),
            pltpu.VMEM((tm, KV_WIDTH), F32),
            pltpu.VMEM((tm, KV_WIDTH), F32),
        ],
        compiler_params=pltpu.CompilerParams(
            dimension_semantics=("arbitrary",), vmem_limit_bytes=VMEM_LIMIT),
        name="inproj",
    )(x2, g, w)


def _s5_kernel(u_ref, perm_ref, permT_ref, b_ref, c_ref, pw_ref, seg_ref, d_ref, wglu_ref, wup_ref,
               ya_ref, xre_ref, xim_ref, st_ref, cre_s, cim_s):
    @pl.when(pl.program_id(0) == 0)
    def _():
        cre_s[...] = jnp.zeros_like(cre_s)
        cim_s[...] = jnp.zeros_like(cim_s)

    u = u_ref[...]
    ub = _dot(perm_ref[...], u.astype(BF16)).astype(BF16)
    for sg in range(SSM_SUPER):
        bu = _dot(ub[:, sg * SG_CH:(sg + 1) * SG_CH], b_ref[sg])
        xre_ref[:, sg * SG_LANES:(sg + 1) * SG_LANES] = bu[:, :SG_LANES]
        xim_ref[:, sg * SG_LANES:(sg + 1) * SG_LANES] = bu[:, SG_LANES:]

    def cmul_add(re, im, are, aim, sre, sim):
        return re + are * sre - aim * sim, im + are * sim + aim * sre

    for sg in range(SSM_SUPER):
        lanes = slice(sg * SG_LANES, (sg + 1) * SG_LANES)

        lam_re, lam_im = pw_ref[0, 0:SCAN_ROWS, lanes], pw_ref[1, 0:SCAN_ROWS, lanes]

        def local(j, carry):
            r0 = pl.multiple_of(j * SCAN_ROWS, SCAN_ROWS)
            re, im = cmul_add(xre_ref[pl.ds(r0, SCAN_ROWS), lanes], xim_ref[pl.ds(r0, SCAN_ROWS), lanes],
                              lam_re, lam_im, *carry)
            xre_ref[pl.ds(r0, SCAN_ROWS), lanes] = re
            xim_ref[pl.ds(r0, SCAN_ROWS), lanes] = im
            return re, im

        zero = jnp.zeros((SCAN_ROWS, SG_LANES), F32)
        re, im = lax.fori_loop(0, SEG, local, (zero, zero), unroll=True)

        for k, shift in enumerate((1, 2, 4)):
            re, im = cmul_add(re, im, seg_ref[2 * k, :, lanes], seg_ref[2 * k + 1, :, lanes],
                              pltpu.roll(re, shift, 0), pltpu.roll(im, shift, 0))
        cin_re, cin_im = cre_s[:, lanes], cim_s[:, lanes]
        re, im = cmul_add(re, im, seg_ref[6, :, lanes], seg_ref[7, :, lanes], cin_re, cin_im)
        cre_s[:, lanes] = re[SCAN_ROWS - 1:SCAN_ROWS, :]
        cim_s[:, lanes] = im[SCAN_ROWS - 1:SCAN_ROWS, :]
        first = lax.broadcasted_iota(jnp.int32, (SCAN_ROWS, SG_LANES), 0) == 0
        start_re = jnp.where(first, cin_re, pltpu.roll(re, 1, 0))
        start_im = jnp.where(first, cin_im, pltpu.roll(im, 1, 0))
        start_re = jnp.concatenate([start_re, start_re], axis=0)
        start_im = jnp.concatenate([start_im, start_im], axis=0)

        def fix(jj, carry):
            r0 = pl.multiple_of(jj * 2 * SCAN_ROWS, 2 * SCAN_ROWS)
            rows = pl.ds(r0, 2 * SCAN_ROWS)
            re, im = cmul_add(xre_ref[rows, lanes], xim_ref[rows, lanes],
                              pw_ref[0, rows, lanes], pw_ref[1, rows, lanes], start_re, start_im)
            st_ref[rows, 2 * sg * SG_LANES:(2 * sg + 1) * SG_LANES] = re.astype(BF16)
            st_ref[rows, (2 * sg + 1) * SG_LANES:(2 * sg + 2) * SG_LANES] = im.astype(BF16)
            return carry

        lax.fori_loop(0, SEG // 2, fix, 0, unroll=True)
    y_perm = jnp.concatenate(
        [_dot(st_ref[:, 2 * sg * SG_LANES:(2 * sg + 2) * SG_LANES], c_ref[sg])
         for sg in range(SSM_SUPER)], axis=1)
    y_hi = y_perm.astype(BF16)
    y_lo = (y_perm - y_hi.astype(F32)).astype(BF16)
    y = _dot(permT_ref[...], y_hi) + _dot(permT_ref[...], y_lo) + d_ref[...] * u
    z = jax.nn.gelu(y)
    z = z * jax.nn.sigmoid(_dot(z.astype(BF16), wglu_ref[...]))
    ya_ref[...] = _dot(z.astype(BF16), wup_ref[...])


def _s5(u, b, c, pw, seg, d, wglu, wup):
    s = u.shape[0]
    t = np.arange(S5_TT)
    perm = np.zeros((S5_TT, S5_TT), np.float32)
    perm[SCAN_ROWS * (t % SEG) + t // SEG, t] = 1.0
    permT = jnp.asarray(perm.T, BF16)
    perm = jnp.asarray(perm, BF16)
    const2 = lambda i: (0, 0)
    const3 = lambda i: (0, 0, 0)
    return pl.pallas_call(
        _s5_kernel,
        grid=(s // S5_TT,),
        in_specs=[
            pl.BlockSpec((S5_TT, SSM_WIDTH), lambda i: (i, 0)),
            pl.BlockSpec(perm.shape, const2),
            pl.BlockSpec(permT.shape, const2),
            pl.BlockSpec(b.shape, const3),
            pl.BlockSpec(c.shape, const3),
            pl.BlockSpec(pw.shape, const3),
            pl.BlockSpec(seg.shape, const3),
            pl.BlockSpec((1, SSM_WIDTH), const2),
            pl.BlockSpec(wglu.shape, const2),
            pl.BlockSpec(wup.shape, const2),
        ],
        out_specs=pl.BlockSpec((S5_TT, D_MODEL), lambda i: (i, 0)),
        out_shape=jax.ShapeDtypeStruct((s, D_MODEL), F32),
        scratch_shapes=[
            pltpu.VMEM((S5_TT, SSM_LANES), F32),
            pltpu.VMEM((S5_TT, SSM_LANES), F32),
            pltpu.VMEM((S5_TT, 2 * SSM_LANES), BF16),
            pltpu.VMEM((1, SSM_LANES), F32),
            pltpu.VMEM((1, SSM_LANES), F32),
        ],
        compiler_params=pltpu.CompilerParams(
            dimension_semantics=("arbitrary",), vmem_limit_bytes=VMEM_LIMIT),
        name="s5",
    )(u, perm, permT, b, c, pw, seg, d, wglu, wup)


def _s5_params(a_re, a_im, log_dt, b_re, b_im, c_re, c_im):
    dt = jnp.exp(log_dt.astype(F32))[:, None]
    ar, ai = a_re.astype(F32), a_im.astype(F32)
    zr, zi = ar * dt, ai * dt

    def power(n):
        mag = jnp.exp(n * zr)
        return mag * jnp.cos(n * zi), mag * jnp.sin(n * zi)

    lr, li = power(1.0)
    den = ar * ar + ai * ai
    kr = ((lr - 1.0) * ar + li * ai) / den
    ki = (li * ar - (lr - 1.0) * ai) / den
    br, bi = b_re.astype(F32), b_im.astype(F32)
    bbr = kr[..., None] * br - ki[..., None] * bi
    bbi = kr[..., None] * bi + ki[..., None] * br
    groups_per_super = SSM_GROUPS // SSM_SUPER

    def super_blocks(w, per_group_rows, per_group_cols):
        rows = w.reshape(SSM_SUPER, groups_per_super * per_group_rows, per_group_cols)
        tiled = jnp.tile(rows, (1, 1, groups_per_super))
        same = (lax.broadcasted_iota(jnp.int32, tiled.shape, 1) // per_group_rows
                == lax.broadcasted_iota(jnp.int32, tiled.shape, 2) // per_group_cols)
        return jnp.where(same, tiled, 0.0)

    b_in = lambda b: super_blocks(jnp.transpose(b, (0, 2, 1)), SSM_GROUP, SSM_STATE)
    c_out = lambda c: super_blocks(jnp.transpose(c, (0, 2, 1)), SSM_STATE, SSM_GROUP)
    zr, zi = zr.reshape(1, SSM_LANES), zi.reshape(1, SSM_LANES)
    pw = jnp.stack(power(jnp.arange(1, SEG + 1, dtype=F32)[:, None]))
    pw = jnp.broadcast_to(pw[:, :, None, :], (2, SEG, SCAN_ROWS, SSM_LANES)).reshape(2, S5_TT, SSM_LANES)
    row = jnp.arange(SCAN_ROWS)[:, None]
    seg = []
    for shift in (1, 2, 4):
        pr, pi = power(float(SEG * shift))
        seg += [jnp.where(row >= shift, pr, 0.0), jnp.where(row >= shift, pi, 0.0)]
    seg += list(power(SEG * (row + 1).astype(F32)))
    b_sg = jnp.concatenate([b_in(bbr), b_in(bbi)], axis=2)
    c_sg = jnp.concatenate([c_out(c_re.astype(F32)), -c_out(c_im.astype(F32))], axis=1)
    return b_sg.astype(BF16), c_sg.astype(BF16), pw, jnp.stack(seg)


def _compress_kernel(xk_ref, xv_ref, w1k_ref, w2k_ref, pk_ref, w1v_ref, w2vT_ref, pv_ref, kc_ref, vcT_ref):
    half = CMP_STRIDE * HEAD_DIM

    def hidden(x_ref, w1_ref, pos_ref):
        x = x_ref[0].astype(BF16)
        first = _dot(x, w1_ref[:half, :])
        second = _dot(x, w1_ref[half:, :])
        bias = _dot(jnp.broadcast_to(pos_ref[...], (8, 2 * half)).astype(BF16), w1_ref[...])[:1]
        pre = first + pltpu.roll(second, N_CHUNK - 1, 0) + bias
        return jax.nn.gelu(pre).astype(BF16)

    hk = hidden(xk_ref, w1k_ref, pk_ref)
    kc = _dot(hk, w2k_ref[...])
    tail_pad = N_CMP_PAD - CMP_PAD - N_CHUNK
    kc_ref[0] = jnp.concatenate([jnp.zeros((CMP_PAD, HEAD_DIM), F32), kc,
                                 jnp.zeros((tail_pad, HEAD_DIM), F32)], axis=0).astype(BF16)
    hv = hidden(xv_ref, w1v_ref, pv_ref)
    vcT = _dot_nt(w2vT_ref[...], hv)
    vcT_ref[0] = jnp.concatenate([jnp.zeros((HEAD_DIM, CMP_PAD), F32), vcT,
                                  jnp.zeros((HEAD_DIM, tail_pad), F32)], axis=1).astype(BF16)


def _compress(xk, xv, w1k, w2k, pk, w1v, w2vT, pv):
    const2 = lambda g: (0, 0)
    head = lambda g: (g, 0, 0)
    return pl.pallas_call(
        _compress_kernel,
        grid=(NSA_KV_HEADS,),
        in_specs=[
            pl.BlockSpec((1, N_CHUNK, CMP_STRIDE * HEAD_DIM), head),
            pl.BlockSpec((1, N_CHUNK, CMP_STRIDE * HEAD_DIM), head),
            pl.BlockSpec(w1k.shape, const2),
            pl.BlockSpec(w2k.shape, const2),
            pl.BlockSpec(pk.shape, const2),
            pl.BlockSpec(w1v.shape, const2),
            pl.BlockSpec(w2vT.shape, const2),
            pl.BlockSpec(pv.shape, const2),
        ],
        out_specs=[
            pl.BlockSpec((1, N_CMP_PAD, HEAD_DIM), head),
            pl.BlockSpec((1, HEAD_DIM, N_CMP_PAD), head),
        ],
        out_shape=[
            jax.ShapeDtypeStruct((NSA_KV_HEADS, N_CMP_PAD, HEAD_DIM), BF16),
            jax.ShapeDtypeStruct((NSA_KV_HEADS, HEAD_DIM, N_CMP_PAD), BF16),
        ],
        compiler_params=pltpu.CompilerParams(
            dimension_semantics=("arbitrary",), vmem_limit_bytes=VMEM_LIMIT),
        name="compress",
    )(xk, xv, w1k, w2k, pk, w1v, w2vT, pv)


def _nsa_kernel(qT_ref, kc_ref, vcT_ref, ks_ref, vsT_ref, kw_ref, vwT_ref, bias_ref, bct_ref, mt_ref, gT_ref,
                oT_ref, sc_ref, neg_ref, negfar_ref, m_ref, acc_ref, tot_ref, sbuf0_ref, sbuf1_ref, mloc_ref, pslc_ref,
                cm_ref, cl_ref):
    i = pl.program_id(1)
    s0 = i * TQ
    qT = jnp.concatenate([qT_ref[r * HEAD_DIM:(r + 1) * HEAD_DIM, :] for r in range(GQA)], axis=1)

    qT_nomask = jnp.concatenate([qT, jnp.zeros((K_COLS - HEAD_DIM, QL), BF16)], axis=0)

    gate0 = pl.program_id(0) * GATES_PER_KV_HEAD

    def gate_row(branch):
        return jnp.concatenate([gT_ref[pl.ds(gate0 + r * 3 + branch, 1), :] for r in range(GQA)], axis=1)

    def reset():
        m_ref[...] = jnp.full_like(m_ref, M_FLOOR)
        acc_ref[...] = jnp.zeros_like(acc_ref)

    def attend(k_ref, vT_ref, start, size, add):
        k = k_ref[0, pl.ds(start, size), :]
        s = _dot(k, qT if k.shape[1] == HEAD_DIM else qT_nomask) + add
        m_prev = m_ref[...]
        m_new = jnp.maximum(m_prev, jnp.max(s, axis=0, keepdims=True))
        alpha = jnp.exp2(m_prev - m_new)
        p = jnp.exp2(s - m_new).astype(BF16)
        acc_ref[...] = alpha * acc_ref[...] + _dot(vT_ref[0, :, pl.ds(start, size)], p)
        m_ref[...] = m_new

    def finish(branch):
        acc = acc_ref[...]
        scale = gate_row(branch) / jnp.maximum(acc[HEAD_DIM:HEAD_DIM + 1, :], 1e-30)
        tot_ref[...] += acc[:HEAD_DIM, :] * scale

    n0 = i * (TQ // CMP_STRIDE)
    band0 = pl.multiple_of(n0 + CMP_PAD - 16, 8)

    def cmp_branch(nrows):
        n_chunks = nrows // CMP_CHUNK
        chunk_rows = lambda c: slice(c * CMP_CHUNK, (c + 1) * CMP_CHUNK)
        for c in range(n_chunks):
            s = _dot(kc_ref[0, chunk_rows(c), :], qT)
            if c == 0:
                pad = lax.broadcasted_iota(jnp.int32, (CMP_CHUNK, QL), 0) < CMP_PAD
                s = jnp.where(pad, NEG, s)
            if c >= n_chunks - 2:
                tail0 = pl.multiple_of(c * CMP_CHUNK - band0 + CMP_TAIL_ZEROS, 8)
                s = s + bct_ref[0, pl.ds(tail0, CMP_CHUNK), :]
            sc_ref[chunk_rows(c), :] = s
            cm_ref[c] = jnp.max(s, axis=0, keepdims=True)
        mc = cm_ref[0]
        for c in range(1, n_chunks):
            mc = jnp.maximum(mc, cm_ref[c])
        mc = jnp.maximum(mc, M_FLOOR)
        for c in range(n_chunks):
            e = jnp.exp2(sc_ref[chunk_rows(c), :] - mc)
            sc_ref[chunk_rows(c), :] = e
            cl_ref[c] = jnp.sum(e, axis=0, keepdims=True)
            pv = _dot(vcT_ref[0, :, chunk_rows(c)], e.astype(BF16))
            if c == 0:
                tot_ref[...] = pv
            else:
                tot_ref[...] += pv
        total = cl_ref[0]
        for c in range(1, n_chunks):
            total = total + cl_ref[c]
        rinv = 1.0 / jnp.maximum(total, 1e-30)
        tot_ref[...] = tot_ref[...] * (rinv * gate_row(0))
        mt = mt_ref[:, 0:nrows]
        p_slc = jnp.zeros((N_SLC, TQ), F32)
        for c in range(n_chunks):
            pc = sc_ref[chunk_rows(c), :] * rinv
            imp = pc[:, 0:TQ]
            for r in range(1, GQA):
                imp = imp + pc[:, r * TQ:(r + 1) * TQ]
            sc_ref[chunk_rows(c), 0:TQ] = imp
        rem = sc_ref[0:nrows, 0:TQ]
        for _ in range(2):
            piece = rem.astype(BF16)
            p_slc = p_slc + _dot(mt, piece)
            rem = rem - piece.astype(F32)
        pslc_ref[...] = p_slc

    prev_rows = 0
    for nrows in CMP_EXTENTS:
        lo, hi = prev_rows, nrows
        pl.when((band0 + CMP_BAND > lo) & (band0 + CMP_BAND <= hi))(functools.partial(cmp_branch, nrows))
        prev_rows = nrows

    reset()
    win_start = jnp.maximum(s0 - WINDOW, 0)
    win_bias0 = WINDOW - (s0 - win_start)
    piece_keys = (WINDOW + TQ) // WIN_PIECES
    wbufs = (sbuf0_ref, sbuf1_ref)

    def win_logits(piece):
        start = pl.multiple_of(win_start + piece * piece_keys, NEAR)
        s = (_dot(kw_ref[0, pl.ds(start, piece_keys), :], qT)
             + bias_ref[0, pl.ds(pl.multiple_of(win_bias0 + piece * piece_keys, NEAR), piece_keys), :])
        wbufs[piece % 2][0:piece_keys, :] = s
        mloc_ref[piece % 2] = jnp.max(s, axis=0, keepdims=True)

    def win_consume(piece):
        start = pl.multiple_of(win_start + piece * piece_keys, NEAR)
        m_prev = m_ref[...]
        m_new = jnp.maximum(m_prev, mloc_ref[piece % 2])
        alpha = jnp.exp2(m_prev - m_new)
        p = jnp.exp2(wbufs[piece % 2][0:piece_keys, :] - m_new).astype(BF16)
        acc_ref[...] = alpha * acc_ref[...] + _dot(vwT_ref[0, :, pl.ds(start, piece_keys)], p)
        m_ref[...] = m_new

    win_logits(0)
    for piece in range(WIN_PIECES):
        if piece + 1 < WIN_PIECES:
            win_logits(piece + 1)
        win_consume(piece)
    finish(2)

    p_slc = pslc_ref[...]
    blk = lax.broadcasted_iota(jnp.int32, (N_SLC, TQ), 0)
    cur = (s0 + lax.broadcasted_iota(jnp.int32, (N_SLC, TQ), 1)) // SLC_LEN
    valid = blk <= cur
    forced = valid & ((blk == 0) | (blk >= cur - (N_LOCAL - 1)))
    n_pick = N_SEL - (N_LOCAL + 1)
    start_score = jnp.where(forced | ~valid, -jnp.inf, p_slc)
    near_blk = (s0 - NEAR) // SLC_LEN

    def publish(score):
        neg = jnp.where(score == -jnp.inf, 0.0, NEG)
        neg_ref[...] = neg
        negfar_ref[...] = jnp.where(blk >= near_blk, NEG, neg)

    score = start_score
    for _ in range(n_pick):
        score = jnp.where(score == jnp.max(score, axis=0, keepdims=True), -jnp.inf, score)
    marks = jnp.sum(jnp.where(score == -jnp.inf, 1.0, 0.0), axis=0, keepdims=True)
    cur_q = cur[0:1, :]
    n_forced = jnp.minimum(cur_q + 1, N_LOCAL + 1)
    expected = (N_SLC - 1 - cur_q) + n_forced + jnp.minimum(n_pick, cur_q + 1 - n_forced)
    tied = jnp.max(jnp.abs(marks - expected.astype(F32))) > 0.5
    fast_score = score

    @pl.when(jnp.logical_not(tied))
    def _():
        publish(fast_score)

    @pl.when(tied)
    def _():
        score = jnp.where(forced | ~valid, -jnp.inf, pslc_ref[...])
        blk_f = blk.astype(F32)
        for _ in range(n_pick):
            best = jnp.max(score, axis=0, keepdims=True)
            first = jnp.min(jnp.where(score == best, blk_f, float(N_SLC)), axis=0, keepdims=True)
            score = jnp.where(blk_f == first, -jnp.inf, score)
        publish(score)

    per_head = lambda rows: jnp.concatenate([rows] * GQA, axis=1)

    def block_mask(ref, j0, nblk):
        return jnp.concatenate(
            [jnp.broadcast_to(per_head(ref[pl.ds(j0 + b, 1), :]), (SLC_LEN, QL)) for b in range(nblk)], axis=0)

    reset()
    n_far = (s0 - NEAR + FAR_KEYS - 1) // FAR_KEYS

    sbufs = (sbuf0_ref, sbuf1_ref)

    def far_logits(c, slot):
        start = pl.multiple_of(c * FAR_KEYS, FAR_KEYS)
        mask_rows = per_head(negfar_ref[pl.ds(pl.multiple_of(c * FAR_BLOCKS, FAR_BLOCKS), FAR_BLOCKS), :])
        extra = jnp.concatenate([mask_rows, jnp.zeros((K_COLS - HEAD_DIM - FAR_BLOCKS, QL), F32)], axis=0)
        q_masked = jnp.concatenate([qT, extra.astype(BF16)], axis=0)
        s = _dot(ks_ref[0, pl.ds(start, FAR_KEYS), :], q_masked)
        sbufs[slot][...] = s
        mloc_ref[slot] = jnp.max(s, axis=0, keepdims=True)

    def far_consume(c, slot):
        start = pl.multiple_of(c * FAR_KEYS, FAR_KEYS)
        m_prev = m_ref[...]
        m_new = jnp.maximum(m_prev, mloc_ref[slot])
        alpha = jnp.exp2(m_prev - m_new)
        p = jnp.exp2(sbufs[slot][...] - m_new).astype(BF16)
        acc_ref[...] = alpha * acc_ref[...] + _dot(vsT_ref[0, :, pl.ds(start, FAR_KEYS)], p)
        m_ref[...] = m_new

    @pl.when(i >= 1)
    def _():
        n_pairs = (n_far + 1) // 2
        far_logits(0, 0)
        attend(ks_ref, vsT_ref, pl.multiple_of(s0 - NEAR, NEAR), NEAR + TQ,
               bias_ref[0, WINDOW - NEAR:WINDOW + TQ, :] + block_mask(neg_ref, near_blk, (NEAR + TQ) // SLC_LEN))

        def far_body(p, carry):
            far_logits(2 * p + 1, 1)
            far_consume(2 * p, 0)
            far_logits(2 * p + 2, 0)
            far_consume(2 * p + 1, 1)
            return carry

        lax.fori_loop(0, n_pairs - 1, far_body, 0)
        last = 2 * (n_pairs - 1)

        @pl.when(n_far % 2 == 0)
        def _():
            far_logits(last + 1, 1)
            far_consume(last, 0)
            far_consume(last + 1, 1)

        @pl.when(n_far % 2 == 1)
        def _():
            far_consume(last, 0)

    @pl.when(i == 0)
    def _():
        attend(ks_ref, vsT_ref, 0, TQ, bias_ref[0, WINDOW:WINDOW + TQ, :] + block_mask(neg_ref, 0, TQ // SLC_LEN))

    finish(1)

    tot = tot_ref[...]
    for r in range(GQA):
        oT_ref[r * HEAD_DIM:(r + 1) * HEAD_DIM, :] = tot[:, r * TQ:(r + 1) * TQ]


def _nsa(qT, kc, vcT, ks, vsT, kw, vwT, bias_tile, bias_cmp, mt, gnT):
    s = qT.shape[1]
    head3 = lambda g, i: (g, 0, 0)
    return pl.pallas_call(
        _nsa_kernel,
        grid=(NSA_KV_HEADS, s // TQ),
        in_specs=[
            pl.BlockSpec((GQA * HEAD_DIM, TQ), lambda g, i: (g, i)),
            pl.BlockSpec((1, N_CMP_PAD, HEAD_DIM), head3),
            pl.BlockSpec((1, HEAD_DIM, N_CMP_PAD), head3),
            pl.BlockSpec((1, s, K_COLS), head3),
            pl.BlockSpec((1, V_ROWS, s), head3),
            pl.BlockSpec((1, s, HEAD_DIM), head3),
            pl.BlockSpec((1, V_ROWS, s), head3),
            pl.BlockSpec((1, 2 * WINDOW + TQ, QL), head3, pipeline_mode=pl.Buffered(1)),
            pl.BlockSpec((1,) + bias_cmp.shape[1:], head3, pipeline_mode=pl.Buffered(1)),
            pl.BlockSpec(mt.shape, lambda g, i: (0, 0)),
            pl.BlockSpec((GATE_ROWS, TQ), lambda g, i: (0, i)),
        ],
        out_specs=pl.BlockSpec((GQA * HEAD_DIM, TQ), lambda g, i: (g, i)),
        out_shape=jax.ShapeDtypeStruct((NSA_WIDTH, s), F32),
        scratch_shapes=[
            pltpu.VMEM((N_CMP_PAD, QL), F32),
            pltpu.VMEM((N_SLC, TQ), F32),
            pltpu.VMEM((N_SLC, TQ), F32),
            pltpu.VMEM((1, QL), F32),
            pltpu.VMEM((V_ROWS, QL), F32),
            pltpu.VMEM((HEAD_DIM, QL), F32),
            pltpu.VMEM((FAR_KEYS, QL), F32),
            pltpu.VMEM((FAR_KEYS, QL), F32),
            pltpu.VMEM((2, 1, QL), F32),
            pltpu.VMEM((N_SLC, TQ), F32),
            pltpu.VMEM((N_CMP_PAD // CMP_CHUNK, 1, QL), F32),
            pltpu.VMEM((N_CMP_PAD // CMP_CHUNK, 1, QL), F32),
        ],
        compiler_params=pltpu.CompilerParams(
            dimension_semantics=("arbitrary", "arbitrary"), vmem_limit_bytes=VMEM_LIMIT),
        name="nsa",
    )(qT, kc, vcT, ks, vsT, kw, vwT, bias_tile, bias_cmp, mt, gnT)


def _t5_bucket(dist):
    n = jnp.maximum(dist, 0)
    max_exact = REL_BUCKETS // 2
    nf = jnp.maximum(n, 1).astype(F32)
    large = max_exact + (jnp.log(nf / max_exact) / math.log(REL_MAX_DIST / max_exact)
                         * (REL_BUCKETS - max_exact)).astype(jnp.int32)
    large = jnp.minimum(large, REL_BUCKETS - 1)
    return jnp.where(n < max_exact, n, large)


def _bias_tiles(rel_bias):
    tab = rel_bias.astype(F32)
    tab = (tab[_t5_bucket(jnp.arange(NEAR))] - tab[REL_BUCKETS - 1]).T * LOG2E
    tab = jnp.concatenate([tab, jnp.zeros((NSA_HEADS, 1), F32)], axis=1)

    def by_distance(d):
        return jnp.where(d >= 0, tab[:, jnp.clip(d, 0, NEAR)], NEG)

    def toeplitz(c, nk, nq, step=1):
        n = step * (nk - 1) + nq
        w = by_distance(jnp.arange(n) + c - step * (nk - 1))
        reps = -(-nk * (n + step) // n)
        return jnp.tile(w, (1, reps))[:, :nk * (n + step)].reshape(NSA_HEADS, nk, n + step)[:, ::-1, :nq]

    def per_kv_head(a):
        a = a.reshape(NSA_KV_HEADS, GQA, a.shape[1], TQ)
        return jnp.transpose(a, (0, 2, 1, 3)).reshape(NSA_KV_HEADS, a.shape[2], QL)

    k2, q2 = np.arange(NEAR)[:, None], np.arange(NEAR)[None, :]
    const = lambda a: jnp.broadcast_to(jnp.asarray(a, F32), (NSA_HEADS, NEAR, NEAR))
    edge = WINDOW // NEAR
    nb = TQ // NEAR
    blocks, slabs = {}, {}

    def block(d):
        d = min(max(d, -1), edge + 1)
        if d not in blocks:
            if d < 0:
                b = const(NEG)
            elif d < 2:
                b = toeplitz(d * NEAR, NEAR, NEAR)
            elif d < edge:
                b = const(0.0)
            else:
                b = const(np.where(k2 > q2, 0.0, NEG) if d == edge else NEG)
            blocks[d] = jnp.transpose(b.reshape(NSA_KV_HEADS, GQA, NEAR, NEAR), (0, 2, 1, 3))
        return blocks[d]

    def slab(dd):
        dd = min(max(dd, -nb), edge + 1)
        if dd not in slabs:
            s = jnp.stack([block(dd + b) for b in range(nb)], axis=3)
            slabs[dd] = s.reshape(NSA_KV_HEADS, NEAR, QL)
        return slabs[dd]

    tile = jnp.concatenate([slab(edge - a) for a in range((2 * WINDOW + TQ) // NEAR)], axis=1)
    cmp_band = per_kv_head(toeplitz(16 * CMP_STRIDE - (CMP_LEN - 1), CMP_BAND, TQ, step=CMP_STRIDE))
    cmp_tail = jnp.concatenate([
        jnp.zeros((NSA_KV_HEADS, CMP_TAIL_ZEROS, QL), F32), cmp_band,
        jnp.full((NSA_KV_HEADS, 2 * CMP_CHUNK, QL), NEG, F32)], axis=1)
    return tile, cmp_tail


def _overlap_matrix():
    ratio = SLC_LEN // CMP_STRIDE
    front = CMP_LEN // CMP_STRIDE - 1
    w_ov = np.convolve(np.ones(ratio), np.ones(CMP_LEN // CMP_STRIDE))
    mt = np.zeros((N_SLC, N_CMP_PAD), np.float32)
    for j in range(N_SLC):
        for o, w in enumerate(w_ov):
            n = ratio * j + o - front
            if 0 <= n < N_CMP:
                mt[j, CMP_PAD + n] = w
    return jnp.asarray(mt, BF16)


def _tail_kernel(x_ref, ya_ref, oT_ref, gb_ref, wup_ref, wout_ref, g_ref, wg_ref, wu_ref, wd_ref, gf_ref, o_ref):
    yb = _dot(oT_ref[...].T.astype(BF16), wup_ref[...])
    mix = gb_ref[:, :D_MODEL].astype(F32) * ya_ref[...] + gb_ref[:, D_MODEL:].astype(F32) * yb
    x = x_ref[...] + _dot(mix.astype(BF16), wout_ref[...])
    h = _rms(x, g_ref[...]).astype(BF16)
    f = jax.nn.silu(_dot(h, wg_ref[...])) * _dot(h, wu_ref[...])
    x = x + _dot(f.astype(BF16), wd_ref[...])
    o_ref[...] = _rms(x, gf_ref[...])


def _tail(x2, ya, oT, gb, wup, wout, g, wg, wu, wd, gf, tm=256):
    s = x2.shape[0]
    row = lambda i: (i, 0)
    const = lambda i: (0, 0)
    return pl.pallas_call(
        _tail_kernel,
        grid=(s // tm,),
        in_specs=[
            pl.BlockSpec((tm, D_MODEL), row),
            pl.BlockSpec((tm, D_MODEL), row),
            pl.BlockSpec((NSA_WIDTH, tm), lambda i: (0, i)),
            pl.BlockSpec((tm, 2 * D_MODEL), row),
            pl.BlockSpec(wup.shape, const),
            pl.BlockSpec(wout.shape, const),
            pl.BlockSpec((1, D_MODEL), const),
            pl.BlockSpec(wg.shape, const),
            pl.BlockSpec(wu.shape, const),
            pl.BlockSpec(wd.shape, const),
            pl.BlockSpec((1, D_MODEL), const),
        ],
        out_specs=pl.BlockSpec((tm, D_MODEL), row),
        out_shape=jax.ShapeDtypeStruct((s, D_MODEL), F32),
        compiler_params=pltpu.CompilerParams(
            dimension_semantics=("arbitrary",), vmem_limit_bytes=VMEM_LIMIT),
        name="tail",
    )(x2, ya, oT, gb, wup, wout, g, wg, wu, wd, gf)


def kernel(x, norm_mix_g, w_in, ssm_a_re, ssm_a_im, ssm_log_dt, ssm_b_re, ssm_b_im, ssm_c_re, ssm_c_im, ssm_d, ssm_w_glu, w_up_ssm, cmp_pos_k, cmp_pos_v, cmp_w1_k, cmp_w2_k, cmp_w1_v, cmp_w2_v, rel_bias, w_up_nsa, w_out, norm_ffn_g, w_ffn_gate, w_ffn_up, w_ffn_down, norm_final_g):
    bsz, s, _ = x.shape
    assert (bsz, s) == (1, SEQ) and w_in.shape[0] == 1
    x2 = x.reshape(s, D_MODEL)
    l = 0
    row = lambda v: v.astype(F32).reshape(1, -1)

    u, qT, kcr, vcr, ks, kw, vsT, vwT, gnT, gb = _inproj(x2, row(norm_mix_g[l]), w_in[l].astype(F32).T)

    b_sg, c_sg, pw, seg = _s5_params(
        ssm_a_re[l], ssm_a_im[l], ssm_log_dt[l], ssm_b_re[l], ssm_b_im[l], ssm_c_re[l], ssm_c_im[l])
    ya = _s5(u, b_sg, c_sg, pw, seg, row(ssm_d[l]), ssm_w_glu[l].astype(BF16), w_up_ssm[l].astype(BF16))

    kc, vcT = _compress(kcr, vcr,
                        cmp_w1_k[l].astype(BF16), cmp_w2_k[l].astype(BF16), row(cmp_pos_k[l]),
                        cmp_w1_v[l].astype(BF16), cmp_w2_v[l].T.astype(BF16), row(cmp_pos_v[l]))

    bias_tile, bias_cmp = _bias_tiles(rel_bias)
    oT = _nsa(qT, kc, vcT, ks, vsT, kw, vwT, bias_tile, bias_cmp, _overlap_matrix(), gnT)

    out = _tail(x2, ya, oT, gb, w_up_nsa[l].astype(BF16), w_out[l].astype(BF16),
                row(norm_ffn_g[l]), w_ffn_gate[l].astype(BF16), w_ffn_up[l].astype(BF16),
                w_ffn_down[l].astype(BF16), row(norm_final_g))
    return out.reshape(bsz, s, D_MODEL)
```
